```python
import jax
import jax.numpy as jnp
from jax import lax
import numpy as np

D_MODEL = 2048
BATCH = 2
SEQ = 4096
DEPTH = 2
DEC_BATCH = 8
DEC_SEQ = 1
PAST_LEN = 16384
PAGE_SIZE = 128

N_BRANCH = 4
MIX_W = D_MODEL // 4
FOX_H = 4
FOX_HD = MIX_W // FOX_H
GLA_H = 4
GLA_DK = MIX_W // (2 * GLA_H)
GLA_DV = MIX_W // GLA_H
GLA_RANK = 16
GLA_TAU = 16.0
NSA_H = 4
NSA_KVH = 1
NSA_HD = MIX_W // NSA_H
NSA_BLOCK = 64
NSA_TOPK = 16
NSA_WINDOW = 512
DN_H = 4
DN_DK = MIX_W // DN_H
DN_DV = MIX_W // DN_H
DN_CONV = 4
DN_QKV = DN_H * (2 * DN_DK + DN_DV)
ROPE_DIM = NSA_HD // 4
ROPE_THETA = 500000.0
D_FF = 4 * D_MODEL
CHUNK = 64
Q_BLOCK = 128
RMS_EPS = 1e-6
SPLIT_SIZES = (
    FOX_H * FOX_HD, FOX_H * FOX_HD, FOX_H * FOX_HD, FOX_H,
    GLA_H * GLA_DK, GLA_H * GLA_DK, GLA_H * GLA_DV, GLA_RANK, GLA_H * GLA_DV,
    NSA_H * NSA_HD, 2 * NSA_KVH * NSA_HD, 2 * NSA_KVH * NSA_HD, 2 * NSA_KVH * NSA_HD, 3 * NSA_H,
    DN_QKV, DN_H, DN_H, DN_H * DN_DV,
    N_BRANCH * D_MODEL,
)
D_IN = sum(SPLIT_SIZES)

kernel_name = 'hybrid_fox_gla_nsa_gdn_decoder_step'


def split_proj(p):
    idx = np.cumsum(np.array(SPLIT_SIZES))[:-1].tolist()
    return jnp.split(p, idx, axis=-1)


def rmsnorm(x, g):
    xf = x.astype(jnp.float32)
    y = xf * lax.rsqrt(jnp.mean(xf * xf, axis=-1, keepdims=True) + RMS_EPS)
    return (y * g.astype(jnp.float32)).astype(x.dtype)


def l2norm(x):
    return x * lax.rsqrt(jnp.sum(x * x, axis=-1, keepdims=True) + 1e-6)


def rope(x, pos):
    half = ROPE_DIM // 2
    inv = ROPE_THETA ** (-jnp.arange(half, dtype=jnp.float32) / half)
    ang = pos.astype(jnp.float32)[:, None] * inv[None, :]
    cos = jnp.cos(ang)[None, :, None, :]
    sin = jnp.sin(ang)[None, :, None, :]
    xf = x.astype(jnp.float32)
    x1, x2 = xf[..., :half], xf[..., half:ROPE_DIM]
    out = jnp.concatenate([x1 * cos - x2 * sin, x2 * cos + x1 * sin, xf[..., ROPE_DIM:]], axis=-1)
    return out.astype(x.dtype)


def masked_softmax(s, mask):
    s = jnp.where(mask, s.astype(jnp.float32), -jnp.inf)
    m = jnp.max(s, axis=-1, keepdims=True)
    m = jnp.where(jnp.isfinite(m), m, 0.0)
    p = jnp.where(mask, jnp.exp(s - m), 0.0)
    return p / jnp.maximum(jnp.sum(p, axis=-1, keepdims=True), 1e-30)


def map_query_blocks(fn, arrays, pos):
    B, T = arrays[0].shape[:2]
    n = T // Q_BLOCK
    xs = tuple(jnp.swapaxes(a.reshape((B, n, Q_BLOCK) + a.shape[2:]), 0, 1) for a in arrays)
    out = lax.map(lambda args: fn(*args), xs + (pos.reshape(n, Q_BLOCK),))
    return jnp.swapaxes(out, 0, 1).reshape((B, T) + out.shape[3:])


def fox_attend(q, cq, qpos, k, v, ck, kpos):
    s = jnp.einsum('bqhd,bkhd->bhqk', q.astype(jnp.float32), k.astype(jnp.float32)) * (FOX_HD ** -0.5)
    s = s + jnp.swapaxes(cq, 1, 2)[..., :, None] - jnp.swapaxes(ck, 1, 2)[..., None, :]
    p = masked_softmax(s, kpos[None, :] <= qpos[:, None])
    return jnp.einsum('bhqk,bkhd->bqhd', p, v.astype(jnp.float32))


def nsa_block(q, g, qpos, kc, vc, ks, vs, kw, vw, kwpos):
    B, Tq = q.shape[:2]
    G = NSA_H // NSA_KVH
    qg = q.reshape(B, Tq, NSA_KVH, G, NSA_HD).astype(jnp.float32) * (NSA_HD ** -0.5)
    nb = kc.shape[1]
    blk = jnp.arange(nb)
    cmask = (blk[None, :] + 1) * NSA_BLOCK <= qpos[:, None] + 1
    p_c = masked_softmax(jnp.einsum('bqhgd,bnhd->bhgqn', qg, kc), cmask)
    o_c = jnp.einsum('bhgqn,bnhd->bqhgd', p_c, vc)
    cur = qpos // NSA_BLOCK
    imp = jnp.sum(p_c, axis=2)
    score = jnp.where(blk[None, :] == cur[:, None], jnp.inf, jnp.where(cmask, imp, -jnp.inf))
    n_sel = min(NSA_TOPK, nb)
    _, idx = lax.top_k(score, n_sel)
    tok = idx[..., None] * NSA_BLOCK + jnp.arange(NSA_BLOCK)
    smask = (idx[..., None] <= cur[:, None, None]) & (tok <= qpos[:, None, None])
    tok = tok.reshape(B, NSA_KVH, Tq, n_sel * NSA_BLOCK)
    smask = smask.reshape(B, NSA_KVH, Tq, n_sel * NSA_BLOCK)
    take = jax.vmap(jax.vmap(lambda rows, ids: rows[ids]))
    kg = take(ks, tok).astype(jnp.float32)
    vg = take(vs, tok).astype(jnp.float32)
    p_s = masked_softmax(jnp.einsum('bqhgd,bhqsd->bhgqs', qg, kg), smask[:, :, None])
    o_s = jnp.einsum('bhgqs,bhqsd->bqhgd', p_s, vg)
    dist = qpos[:, None] - kwpos[None, :]
    wmask = (dist >= 0) & (dist < NSA_WINDOW) & (kwpos[None, :] >= 0)
    p_w = masked_softmax(jnp.einsum('bqhgd,bkhd->bhgqk', qg, kw.astype(jnp.float32)), wmask)
    o_w = jnp.einsum('bhgqk,bkhd->bqhgd', p_w, vw.astype(jnp.float32))
    g = g.reshape(B, Tq, NSA_KVH, G, 3)
    o = g[..., 0:1] * o_c + g[..., 1:2] * o_s + g[..., 2:3] * o_w
    return o.reshape(B, Tq, NSA_H, NSA_HD)


def to_chunks(a, n):
    B, T = a.shape[:2]
    a = jnp.pad(a, [(0, 0), (0, n * CHUNK - T)] + [(0, 0)] * (a.ndim - 2))
    a = jnp.moveaxis(a.reshape((B, n, CHUNK) + a.shape[2:]), 1, 0)
    return jnp.swapaxes(a, 2, 3)


def from_chunks(o, T):
    n, B, H, C = o.shape[:4]
    return jnp.transpose(o, (1, 0, 3, 2, 4)).reshape(B, n * C, H, o.shape[-1])[:, :T]


def gla_chunked(q, k, v, loga, s0):
    f32 = jnp.float32
    T = q.shape[1]
    n = -(-T // CHUNK)
    qc, kc, vc, ac = (to_chunks(a.astype(f32), n) for a in (q, k, v, loga))
    causal = jnp.tril(jnp.ones((CHUNK, CHUNK), bool))[..., None]

    def step(s, inp):
        qi, ki, vi, ai = inp
        b = jnp.cumsum(ai, axis=2)
        diff = b[:, :, :, None, :] - b[:, :, None, :, :]
        decay = jnp.where(causal, jnp.exp(jnp.where(causal, diff, 0.0)), 0.0)
        att = jnp.einsum('bhid,bhjd,bhijd->bhij', qi, ki, decay)
        o = jnp.einsum('bhij,bhjv->bhiv', att, vi) + jnp.einsum('bhid,bhdv->bhiv', qi * jnp.exp(b), s)
        b_end = b[:, :, -1:, :]
        s = jnp.exp(b_end[:, :, 0, :])[..., None] * s + jnp.einsum('bhjd,bhjv->bhdv', ki * jnp.exp(b_end - b), vi)
        return s, o

    s, o = lax.scan(step, s0.astype(f32), (qc, kc, vc, ac))
    return from_chunks(o, T), s


def gdn_chunked(q, k, v, g, beta, s0):
    f32 = jnp.float32
    T = q.shape[1]
    n = -(-T // CHUNK)
    qc, kc, vc = (to_chunks(a.astype(f32), n) for a in (q, k, v))
    gc, bc = (to_chunks(a.astype(f32), n) for a in (g, beta))
    gam = jnp.cumsum(gc, axis=-1)
    diff = gam[..., :, None] - gam[..., None, :]
    incl = jnp.tril(jnp.ones((CHUNK, CHUNK), bool))
    strict = jnp.tril(jnp.ones((CHUNK, CHUNK), bool), -1)
    dec_incl = jnp.where(incl, jnp.exp(jnp.where(incl, diff, 0.0)), 0.0)
    dec_strict = jnp.where(strict, dec_incl, 0.0)
    kk = jnp.einsum('nbhid,nbhjd->nbhij', kc, kc)
    a = jnp.eye(CHUNK, dtype=f32) + bc[..., :, None] * kk * dec_strict
    w_v = lax.linalg.triangular_solve(a, bc[..., None] * vc, left_side=True, lower=True, unit_diagonal=True)
    w_k = lax.linalg.triangular_solve(a, (bc * jnp.exp(gam))[..., None] * kc, left_side=True, lower=True, unit_diagonal=True)
    qk = jnp.einsum('nbhid,nbhjd->nbhij', qc, kc) * dec_incl

    def step(s, inp):
        qi, ki, wv, wk, qki, gi = inp
        u = wv - jnp.einsum('bhid,bhdv->bhiv', wk, s)
        o = jnp.einsum('bhid,bhdv->bhiv', qi * jnp.exp(gi)[..., None], s) + jnp.einsum('bhij,bhjv->bhiv', qki, u)
        g_end = gi[..., -1:]
        s = jnp.exp(g_end)[..., None] * s + jnp.einsum('bhjd,bhjv->bhdv', ki * jnp.exp(g_end - gi)[..., None], u)
        return s, o

    s, o = lax.scan(step, s0.astype(f32), (qc, kc, w_v, w_k, qk, gam))
    return from_chunks(o, T), s


def token_mixer(h, past, w_in, fox_b_f, gla_w_a2, gla_b_a, gla_norm, dn_conv_w, dn_a_log, dn_dt_bias, dn_norm, w_branch, w_out):
    f32 = jnp.float32
    B, T, _ = h.shape
    prompt = past is None
    if prompt:
        past_len = 0
    else:
        p_fox_kv, p_fox_logf, p_cmp, p_slc, p_win, p_gla, p_dn, p_conv = past
        past_len = p_fox_kv.shape[1]
    pos = past_len + jnp.arange(T)
    (fq, fk, fv, ff, gq, gk, gv, ga, gr, nq, nkc, nks, nkw, ng, dqkv, da, dbeta, dz, mg) = split_proj(h @ w_in)

    fq = fq.reshape(B, T, FOX_H, FOX_HD)
    fox_kv = jnp.stack([fk.reshape(B, T, FOX_H, FOX_HD), fv.reshape(B, T, FOX_H, FOX_HD)], axis=2)
    fox_logf = jax.nn.log_sigmoid(ff.astype(f32) + fox_b_f.astype(f32))
    if prompt:
        kv_all, logf_all = fox_kv, fox_logf
    else:
        kv_all = jnp.concatenate([p_fox_kv, fox_kv.astype(p_fox_kv.dtype)], axis=1)
        logf_all = jnp.concatenate([p_fox_logf.astype(f32), fox_logf], axis=1)
    c_all = jnp.cumsum(logf_all, axis=1)
    k_all, v_all = kv_all[:, :, 0], kv_all[:, :, 1]
    kpos = jnp.arange(past_len + T)
    fox_fn = lambda qb, cqb, qpos: fox_attend(qb, cqb, qpos, k_all, v_all, c_all, kpos)
    cq = c_all[:, past_len:]
    o_fox = map_query_blocks(fox_fn, (fq, cq), pos) if prompt else fox_fn(fq, cq, pos)
    o_fox = o_fox.reshape(B, T, MIX_W)

    q_g = gq.astype(f32).reshape(B, T, GLA_H, GLA_DK) * (GLA_DK ** -0.5)
    k_g = gk.astype(f32).reshape(B, T, GLA_H, GLA_DK)
    v_g = gv.astype(f32).reshape(B, T, GLA_H, GLA_DV)
    loga = jax.nn.log_sigmoid((ga @ gla_w_a2 + gla_b_a).astype(f32)).reshape(B, T, GLA_H, GLA_DK) / GLA_TAU
    s0_gla = jnp.zeros((B, GLA_H, GLA_DK, GLA_DV), f32) if prompt else p_gla
    o_g, s_gla = gla_chunked(q_g, k_g, v_g, loga, s0_gla)
    o_gla = (rmsnorm(o_g, gla_norm) * jax.nn.silu(gr.astype(f32).reshape(B, T, GLA_H, GLA_DV))).reshape(B, T, MIX_W)

    q_n = rope(nq.reshape(B, T, NSA_H, NSA_HD), pos)

    def kv_rows(a):
        a = a.reshape(B, T, 2, NSA_KVH, NSA_HD)
        return jnp.stack([rope(a[:, :, 0], pos), a[:, :, 1]], axis=2)

    cmp_kv, slc_kv, win_kv = kv_rows(nkc), kv_rows(nks), kv_rows(nkw)
    ngate = jax.nn.sigmoid(ng.astype(f32)).reshape(B, T, NSA_H, 3)
    if prompt:
        cmp_all, slc_all = cmp_kv, slc_kv
        win_src = jnp.pad(win_kv, [(0, 0), (NSA_WINDOW, 0), (0, 0), (0, 0), (0, 0)])
        src_pos0 = -NSA_WINDOW
    else:
        cmp_all = jnp.concatenate([p_cmp, cmp_kv.astype(p_cmp.dtype)], axis=1)
        slc_all = jnp.concatenate([p_slc, slc_kv.astype(p_slc.dtype)], axis=1)
        win_src = jnp.concatenate([p_win, win_kv.astype(p_win.dtype)], axis=1)
        src_pos0 = past_len - p_win.shape[1]
    L = past_len + T
    nb = -(-L // NSA_BLOCK)
    pad = [(0, 0), (0, nb * NSA_BLOCK - L), (0, 0), (0, 0), (0, 0)]
    cmp_mean = jnp.mean(jnp.pad(cmp_all.astype(f32), pad).reshape(B, nb, NSA_BLOCK, 2, NSA_KVH, NSA_HD), axis=2)
    kc, vc = cmp_mean[:, :, 0], cmp_mean[:, :, 1]
    slc_t = jnp.moveaxis(jnp.pad(slc_all, pad), 1, 3)
    ks, vs = slc_t[:, 0], slc_t[:, 1]

    def nsa_fn(qb, gb, qpos):
        if prompt:
            st = qpos[0] - src_pos0 - NSA_WINDOW
            kw = lax.dynamic_slice_in_dim(win_src, st, Q_BLOCK + NSA_WINDOW, axis=1)
            kwpos = src_pos0 + st + jnp.arange(Q_BLOCK + NSA_WINDOW)
        else:
            kw = win_src
            kwpos = src_pos0 + jnp.arange(win_src.shape[1])
        return nsa_block(qb, gb, qpos, kc, vc, ks, vs, kw[:, :, 0], kw[:, :, 1], kwpos)

    o_nsa = map_query_blocks(nsa_fn, (q_n, ngate), pos) if prompt else nsa_fn(q_n, ngate, pos)
    o_nsa = o_nsa.reshape(B, T, MIX_W)
    if prompt:
        new_win = win_kv[:, T - min(NSA_WINDOW, T):]
    else:
        new_win = win_src[:, win_src.shape[1] - min(NSA_WINDOW, win_src.shape[1]):]

    conv_prev = jnp.zeros((B, DN_CONV - 1, DN_QKV), dqkv.dtype) if prompt else p_conv.astype(dqkv.dtype)
    xc = jnp.concatenate([conv_prev, dqkv], axis=1)
    conv = xc[:, 0:T] * dn_conv_w[0]
    for j in range(1, DN_CONV):
        conv = conv + xc[:, j:j + T] * dn_conv_w[j]
    new_conv = xc[:, xc.shape[1] - (DN_CONV - 1):]
    u = jax.nn.silu(conv.astype(f32))
    q_d, k_d, v_d = jnp.split(u, [DN_H * DN_DK, 2 * DN_H * DN_DK], axis=-1)
    q_d = l2norm(q_d.reshape(B, T, DN_H, DN_DK)) * (DN_DK ** -0.5)
    k_d = l2norm(k_d.reshape(B, T, DN_H, DN_DK))
    v_d = v_d.reshape(B, T, DN_H, DN_DV)
    g_d = -jnp.exp(dn_a_log.astype(f32)) * jax.nn.softplus(da.astype(f32) + dn_dt_bias.astype(f32))
    beta_d = jax.nn.sigmoid(dbeta.astype(f32))
    s0_dn = jnp.zeros((B, DN_H, DN_DK, DN_DV), f32) if prompt else p_dn
    o_d, s_dn = gdn_chunked(q_d, k_d, v_d, g_d, beta_d, s0_dn)
    o_dn = (rmsnorm(o_d, dn_norm) * jax.nn.silu(dz.astype(f32).reshape(B, T, DN_H, DN_DV))).reshape(B, T, MIX_W)

    branches = jnp.stack([o_fox, o_gla, o_nsa, o_dn], axis=2)
    y_br = jnp.einsum('btnw,nwd->btnd', branches, w_branch)
    gates = jax.nn.sigmoid(mg.astype(f32).reshape(B, T, N_BRANCH, D_MODEL))
    mix = (jnp.sum(gates * y_br, axis=2) @ w_out).astype(h.dtype)
    state = (fox_kv, fox_logf, cmp_kv, slc_kv, new_win, s_gla, s_dn, new_conv)
    return mix, state


def trunk_layer(x, past, lp):
    (g_pre_mix, g_post_mix, g_pre_mlp, g_post_mlp, w_in, fox_b_f, gla_w_a2, gla_b_a, gla_norm,
     dn_conv_w, dn_a_log, dn_dt_bias, dn_norm, w_branch, w_out, w_up, w_down) = lp
    mix, state = token_mixer(rmsnorm(x, g_pre_mix), past, w_in, fox_b_f, gla_w_a2, gla_b_a, gla_norm,
                             dn_conv_w, dn_a_log, dn_dt_bias, dn_norm, w_branch, w_out)
    x = (x + rmsnorm(mix, g_post_mix)).astype(x.dtype)
    hid = jnp.square(jax.nn.relu(rmsnorm(x, g_pre_mlp) @ w_up))
    x = (x + rmsnorm(hid @ w_down, g_post_mlp)).astype(x.dtype)
    return x, state


def setup_inputs(seed: int = 0) -> dict:
    key = jax.random.key(seed)
    keys = iter(jax.random.split(key, 40))

    def nrm(shape, scale=1.0):
        return jax.random.normal(next(keys), shape, jnp.float32) * scale

    def unif(shape):
        return jax.random.uniform(next(keys), shape, jnp.float32)

    n_pages = PAST_LEN // PAGE_SIZE
    n_used = DEC_BATCH * n_pages
    n_pool = n_used + (n_used + 3) // 4
    win_len = min(NSA_WINDOW, PAST_LEN)
    page_table = jax.random.permutation(next(keys), n_pool)[:n_used].reshape(DEC_BATCH, n_pages).astype(jnp.int32)
    dt = jnp.exp(unif((DEPTH, DN_H)) * (jnp.log(0.1) - jnp.log(0.001)) + jnp.log(0.001))
    inp = {}
    inp['x_prompt'] = nrm((BATCH, SEQ, D_MODEL))
    inp['x_sample'] = nrm((DEC_BATCH, DEC_SEQ, D_MODEL))
    inp['cache_fox_kv'] = nrm((DEPTH, n_pool, PAGE_SIZE, 2, FOX_H, FOX_HD))
    inp['cache_fox_logf'] = jax.nn.log_sigmoid(2.0 + 3.0 * unif((DEPTH, n_pool, PAGE_SIZE, FOX_H)) + nrm((DEPTH, n_pool, PAGE_SIZE, FOX_H)))
    inp['cache_nsa_cmp_kv'] = nrm((DEPTH, n_pool, PAGE_SIZE, 2, NSA_KVH, NSA_HD))
    inp['cache_nsa_slc_kv'] = nrm((DEPTH, n_pool, PAGE_SIZE, 2, NSA_KVH, NSA_HD))
    inp['cache_nsa_win_kv'] = nrm((DEPTH, DEC_BATCH, win_len, 2, NSA_KVH, NSA_HD))
    inp['state_gla'] = nrm((DEPTH, DEC_BATCH, GLA_H, GLA_DK, GLA_DV))
    inp['state_dn'] = nrm((DEPTH, DEC_BATCH, DN_H, DN_DK, DN_DV), 0.1)
    inp['state_dn_conv'] = nrm((DEPTH, DEC_BATCH, DN_CONV - 1, DN_QKV))
    inp['page_table'] = page_table
    inp['norm_pre_mix'] = 1.0 + nrm((DEPTH, D_MODEL), 0.05)
    inp['norm_post_mix'] = 1.0 + nrm((DEPTH, D_MODEL), 0.05)
    inp['norm_pre_mlp'] = 1.0 + nrm((DEPTH, D_MODEL), 0.05)
    inp['norm_post_mlp'] = 1.0 + nrm((DEPTH, D_MODEL), 0.05)
    inp['w_in'] = nrm((DEPTH, D_MODEL, D_IN), D_MODEL ** -0.5)
    inp['fox_b_f'] = 2.0 + 3.0 * unif((DEPTH, FOX_H))
    inp['gla_w_a2'] = nrm((DEPTH, GLA_RANK, GLA_H * GLA_DK), GLA_RANK ** -0.5)
    inp['gla_b_a'] = nrm((DEPTH, GLA_H * GLA_DK), 0.1)
    inp['gla_norm'] = 1.0 + nrm((DEPTH, GLA_DV), 0.05)
    inp['dn_conv_w'] = nrm((DEPTH, DN_CONV, DN_QKV), DN_CONV ** -0.5)
    inp['dn_a_log'] = jnp.log(1.0 + 15.0 * unif((DEPTH, DN_H)))
    inp['dn_dt_bias'] = dt + jnp.log(-jnp.expm1(-dt))
    inp['dn_norm'] = 1.0 + nrm((DEPTH, DN_DV), 0.05)
    inp['w_branch'] = nrm((DEPTH, N_BRANCH, MIX_W, D_MODEL), MIX_W ** -0.5)
    inp['w_out'] = nrm((DEPTH, D_MODEL, D_MODEL), D_MODEL ** -0.5)
    inp['w_up'] = nrm((DEPTH, D_MODEL, D_FF), D_MODEL ** -0.5)
    inp['w_down'] = nrm((DEPTH, D_FF, D_MODEL), D_FF ** -0.5)
    return inp


def reference(x_prompt, x_sample, cache_fox_kv, cache_fox_logf, cache_nsa_cmp_kv, cache_nsa_slc_kv,
              cache_nsa_win_kv, state_gla, state_dn, state_dn_conv, page_table, norm_pre_mix, norm_post_mix,
              norm_pre_mlp, norm_post_mlp, w_in, fox_b_f, gla_w_a2, gla_b_a, gla_norm, dn_conv_w, dn_a_log,
              dn_dt_bias, dn_norm, w_branch, w_out, w_up, w_down):
    def gather_pages(pool):
        rows = pool[page_table]
        return rows.reshape((rows.shape[0], rows.shape[1] * rows.shape[2]) + rows.shape[3:])

    y_p, y_s = x_prompt, x_sample
    new_p = [[] for _ in range(8)]
    new_s = [[] for _ in range(8)]
    for l in range(DEPTH):
        lp = (norm_pre_mix[l], norm_post_mix[l], norm_pre_mlp[l], norm_post_mlp[l], w_in[l], fox_b_f[l],
              gla_w_a2[l], gla_b_a[l], gla_norm[l], dn_conv_w[l], dn_a_log[l], dn_dt_bias[l], dn_norm[l],
              w_branch[l], w_out[l], w_up[l], w_down[l])
        past = (gather_pages(cache_fox_kv[l]), gather_pages(cache_fox_logf[l]), gather_pages(cache_nsa_cmp_kv[l]),
                gather_pages(cache_nsa_slc_kv[l]), cache_nsa_win_kv[l], state_gla[l], state_dn[l], state_dn_conv[l])
        y_p, st_p = trunk_layer(y_p, None, lp)
        y_s, st_s = trunk_layer(y_s, past, lp)
        for i in range(8):
            new_p[i].append(st_p[i])
            new_s[i].append(st_s[i])
    fox_kv_p, fox_logf_p, cmp_kv_p, slc_kv_p, win_kv_p, gla_p, dn_p, conv_p = [jnp.stack(a) for a in new_p]
    fox_kv_s, fox_logf_s, cmp_kv_s, slc_kv_s, win_kv_s, gla_s, dn_s, conv_s = [jnp.stack(a) for a in new_s]
    return (y_p, y_s, fox_kv_p, fox_kv_s, fox_logf_p, fox_logf_s, cmp_kv_p, cmp_kv_s, slc_kv_p, slc_kv_s,
            win_kv_p, win_kv_s, gla_p, gla_s, dn_p, dn_s, conv_p, conv_s)
```

```python
import functools

import jax
import jax.numpy as jnp
from jax import lax
from jax.experimental import pallas as pl
from jax.experimental.pallas import tpu as pltpu

F32 = jnp.float32
BF16 = jnp.bfloat16
HI = lax.Precision.HIGHEST
NT = (((1,), (1,)), ((), ()))
TN = (((0,), (0,)), ((), ()))
NEG = -1e30

D_MODEL = 2048
DEPTH = 2
PAGE_SIZE = 128
N_BRANCH = 4
MIX_W = D_MODEL // 4
FOX_H = 4
FOX_HD = MIX_W // FOX_H
GLA_H = 4
GLA_DK = MIX_W // (2 * GLA_H)
GLA_DV = MIX_W // GLA_H
GLA_RANK = 16
GLA_TAU = 16.0
NSA_H = 4
NSA_HD = MIX_W // NSA_H
NSA_BLOCK = 64
NSA_TOPK = 16
NSA_WINDOW = 512
DN_H = 4
DN_DK = MIX_W // DN_H
DN_DV = MIX_W // DN_H
DN_CONV = 4
DN_QKV = DN_H * (2 * DN_DK + DN_DV)
ROPE_DIM = NSA_HD // 4
ROPE_THETA = 500000.0
D_FF = 4 * D_MODEL
CHUNK = 64
SUB = 16
RMS_EPS = 1e-6
LANES = 128
SPLIT_SIZES = (
    FOX_H * FOX_HD, FOX_H * FOX_HD, FOX_H * FOX_HD, FOX_H,
    GLA_H * GLA_DK, GLA_H * GLA_DK, GLA_H * GLA_DV, GLA_RANK, GLA_H * GLA_DV,
    NSA_H * NSA_HD, 2 * NSA_HD, 2 * NSA_HD, 2 * NSA_HD, 3 * NSA_H,
    DN_QKV, DN_H, DN_H, DN_H * DN_DV,
    N_BRANCH * D_MODEL,
)
SM_FF, SM_GA, SM_NG, SM_DA, SM_DB = 0, 4, 20, 32, 36
VMEM_LIMIT = 56 * 1024 * 1024


def _cp(*sem):
    return pltpu.CompilerParams(dimension_semantics=sem, vmem_limit_bytes=VMEM_LIMIT)


def _pick(n, cap):
    if n <= cap:
        return n
    best = None
    for t in range(LANES, cap + 1, LANES):
        if n % t == 0:
            best = t
    assert best is not None, n
    return best


def _log_sigmoid(z):
    return jnp.minimum(z, 0.0) - jnp.log1p(jnp.exp(-jnp.abs(z)))


def _iota(shape, dim):
    return lax.broadcasted_iota(jnp.int32, shape, dim)


def _rmsnorm_cast_kernel(x_ref, g_ref, o_ref):
    x = x_ref[...]
    y = x * lax.rsqrt(jnp.mean(x * x, axis=-1, keepdims=True) + RMS_EPS)
    o_ref[...] = (y * g_ref[...]).astype(o_ref.dtype)


def _rmsnorm_cast(x, g):
    m, d = x.shape
    tm = min(m, 512)
    return pl.pallas_call(
        _rmsnorm_cast_kernel,
        grid=(m // tm,),
        in_specs=[pl.BlockSpec((tm, d), lambda i: (i, 0)), pl.BlockSpec((1, d), lambda i: (0, 0))],
        out_specs=pl.BlockSpec((tm, d), lambda i: (i, 0)),
        out_shape=jax.ShapeDtypeStruct((m, d), BF16),
        compiler_params=_cp("parallel"),
        name="rmsnorm_cast",
    )(x, g.reshape(1, d))


def _mm_kernel(a_ref, w_ref, o_ref, *, act):
    y = jnp.dot(a_ref[...], w_ref[...], preferred_element_type=F32)
    if act == "relu2":
        y = jnp.square(jnp.maximum(y, 0.0))
    o_ref[...] = y.astype(o_ref.dtype)


def _mm(a, w, out_dtype=F32, act=None):
    m, k = a.shape
    n = w.shape[1]
    tm = min(m, 512)
    tn = _pick(n, 1024)
    return pl.pallas_call(
        functools.partial(_mm_kernel, act=act),
        grid=(n // tn, m // tm),
        in_specs=[pl.BlockSpec((tm, k), lambda j, i: (i, 0)), pl.BlockSpec((k, tn), lambda j, i: (0, j))],
        out_specs=pl.BlockSpec((tm, tn), lambda j, i: (i, j)),
        out_shape=jax.ShapeDtypeStruct((m, n), out_dtype),
        compiler_params=_cp("parallel", "parallel"),
        name="mm",
    )(a, w)


def _mm_norm_res_kernel(a_ref, w_ref, g_ref, x_ref, o_ref, acc_ref):
    k = pl.program_id(1)

    @pl.when(k == 0)
    def _():
        acc_ref[...] = jnp.zeros_like(acc_ref)

    acc_ref[...] += jnp.dot(a_ref[...], w_ref[...], preferred_element_type=F32)

    @pl.when(k == pl.num_programs(1) - 1)
    def _():
        y = acc_ref[...]
        y = y * lax.rsqrt(jnp.mean(y * y, axis=-1, keepdims=True) + RMS_EPS)
        o_ref[...] = x_ref[...] + y * g_ref[...]


def _mm_norm_res(a, w, g, x):
    m, k = a.shape
    n = w.shape[1]
    tm = min(m, 512)
    tk = _pick(k, 512)
    return pl.pallas_call(
        _mm_norm_res_kernel,
        grid=(m // tm, k // tk),
        in_specs=[
            pl.BlockSpec((tm, tk), lambda i, kk: (i, kk)),
            pl.BlockSpec((tk, n), lambda i, kk: (kk, 0)),
            pl.BlockSpec((1, n), lambda i, kk: (0, 0)),
            pl.BlockSpec((tm, n), lambda i, kk: (i, 0)),
        ],
        out_specs=pl.BlockSpec((tm, n), lambda i, kk: (i, 0)),
        out_shape=jax.ShapeDtypeStruct((m, n), F32),
        scratch_shapes=[pltpu.VMEM((tm, n), F32)],
        compiler_params=_cp("parallel", "arbitrary"),
        name="mm_norm_res",
    )(a, w, g.reshape(1, n), x)


def _branch_gate_kernel(o0, o1, o2, o3, wb_ref, m0, m1, m2, m3, out_ref):
    acc = None
    for n, (o_n, m_n) in enumerate(((o0, m0), (o1, m1), (o2, m2), (o3, m3))):
        y = jnp.dot(o_n[...], wb_ref[n], preferred_element_type=F32)
        t = jax.nn.sigmoid(m_n[...]) * y
        acc = t if acc is None else acc + t
    out_ref[...] = acc.astype(out_ref.dtype)


def _branch_gate(branches, wb, mg):
    m = mg.shape[0]
    tm = min(m, 512)
    tn = 1024
    nj = D_MODEL // tn
    o_spec = pl.BlockSpec((tm, MIX_W), lambda j, i: (i, 0))
    m_specs = [pl.BlockSpec((tm, tn), functools.partial(lambda j, i, n: (i, n * nj + j), n=n)) for n in range(N_BRANCH)]
    return pl.pallas_call(
        _branch_gate_kernel,
        grid=(nj, m // tm),
        in_specs=[o_spec] * 4 + [pl.BlockSpec((N_BRANCH, MIX_W, tn), lambda j, i: (0, 0, j))] + m_specs,
        out_specs=pl.BlockSpec((tm, tn), lambda j, i: (i, j)),
        out_shape=jax.ShapeDtypeStruct((m, D_MODEL), BF16),
        compiler_params=_cp("parallel", "parallel"),
        name="branch_gate",
    )(*branches, wb, mg, mg, mg, mg)


def _fox_prep_kernel(s_ref, b_ref, lf_ref, c_ref, ct_ref, carry_ref, *, tb):
    @pl.when(pl.program_id(1) == 0)
    def _():
        carry_ref[...] = jnp.zeros_like(carry_ref)

    lf = _log_sigmoid(s_ref[...] + b_ref[...])
    tri = (_iota((tb, tb), 1) <= _iota((tb, tb), 0)).astype(F32)
    c = jnp.dot(tri, lf, precision=HI, preferred_element_type=F32) + carry_ref[...]
    lf_ref[...] = lf
    c_ref[...] = c
    ct_ref[...] = c.T[:8]
    carry_ref[...] = c[tb - 1:tb]


def _fox_prep(small, bias_row):
    b, t, _ = small.shape
    tb = min(t, 256)
    blk = pl.BlockSpec((None, tb, LANES), lambda bi, i: (bi, i, 0))
    return pl.pallas_call(
        functools.partial(_fox_prep_kernel, tb=tb),
        grid=(b, t // tb),
        in_specs=[blk, pl.BlockSpec((1, LANES), lambda bi, i: (0, 0))],
        out_specs=[blk, blk, pl.BlockSpec((None, 8, tb), lambda bi, i: (bi, 0, i))],
        out_shape=[jax.ShapeDtypeStruct((b, t, LANES), F32)] * 2 + [jax.ShapeDtypeStruct((b, 8, t), F32)],
        scratch_shapes=[pltpu.VMEM((1, LANES), F32)],
        compiler_params=_cp("parallel", "arbitrary"),
        name="fox_prep",
    )(small, bias_row)


def _fox_attn_kernel(q_ref, k_ref, v_ref, cq_ref, ck_ref, o_ref, m_sc, l_sc, acc_sc, *, tq, tk):
    qi = pl.program_id(1)
    ki = pl.program_id(2)

    @pl.when(ki == 0)
    def _():
        m_sc[...] = jnp.full_like(m_sc, NEG)
        l_sc[...] = jnp.zeros_like(l_sc)
        acc_sc[...] = jnp.zeros_like(acc_sc)

    @pl.when(ki * tk <= qi * tq + tq - 1)
    def _():
        mask = (ki * tk + _iota((tq, tk), 1)) <= (qi * tq + _iota((tq, tk), 0))
        for h in range(FOX_H):
            hs = slice(h * FOX_HD, (h + 1) * FOX_HD)
            q = (q_ref[:, hs] * (FOX_HD ** -0.5)).astype(BF16)
            s = lax.dot_general(q, k_ref[:, hs].astype(BF16), NT, preferred_element_type=F32)
            s = s + cq_ref[:, h:h + 1] - ck_ref[h:h + 1, :]
            s = jnp.where(mask, s, NEG)
            m_prev = m_sc[h]
            m_new = jnp.maximum(m_prev, jnp.max(s, axis=-1, keepdims=True))
            alpha = jnp.exp(m_prev - m_new)
            p = jnp.where(mask, jnp.exp(s - m_new), 0.0)
            l_sc[h] = alpha * l_sc[h] + jnp.sum(p, axis=-1, keepdims=True)
            acc_sc[h] = alpha * acc_sc[h] + jnp.dot(p.astype(BF16), v_ref[:, hs].astype(BF16),
                                                    preferred_element_type=F32)
            m_sc[h] = m_new

    @pl.when(ki == pl.num_programs(2) - 1)
    def _():
        for h in range(FOX_H):
            o_ref[:, h * FOX_HD:(h + 1) * FOX_HD] = (acc_sc[h] / l_sc[h]).astype(o_ref.dtype)


def _fox_attn(fox, c, ct):
    b, t, _ = fox.shape
    tq = tk = min(t, 256)

    def kmap(col):
        return lambda bi, qi, ki: (bi, jnp.minimum(ki, (qi * tq + tq - 1) // tk), col)

    return pl.pallas_call(
        functools.partial(_fox_attn_kernel, tq=tq, tk=tk),
        grid=(b, t // tq, t // tk),
        in_specs=[
            pl.BlockSpec((None, tq, MIX_W), lambda bi, qi, ki: (bi, qi, 0)),
            pl.BlockSpec((None, tk, MIX_W), kmap(1)),
            pl.BlockSpec((None, tk, MIX_W), kmap(2)),
            pl.BlockSpec((None, tq, LANES), lambda bi, qi, ki: (bi, qi, 0)),
            pl.BlockSpec((None, 8, tk), lambda bi, qi, ki: (bi, 0, jnp.minimum(ki, (qi * tq + tq - 1) // tk))),
        ],
        out_specs=pl.BlockSpec((None, tq, MIX_W), lambda bi, qi, ki: (bi, qi, 0)),
        out_shape=jax.ShapeDtypeStruct((b, t, MIX_W), BF16),
        scratch_shapes=[pltpu.VMEM((FOX_H, tq, 1), F32), pltpu.VMEM((FOX_H, tq, 1), F32),
                        pltpu.VMEM((FOX_H, tq, FOX_HD), F32)],
        compiler_params=_cp("parallel", "parallel", "arbitrary"),
        name="fox_attn",
    )(fox, fox, fox, c, ct)


def _gla_kernel(g_ref, sm_ref, w2_ref, ba_ref, gn_ref, s0_ref, o_ref, sout_ref, s_sc, *, C):
    c = pl.program_id(1)

    @pl.when(c == 0)
    def _():
        s_sc[...] = s0_ref[...]

    lane = _iota((C, LANES), 1)
    ga = jnp.where((lane >= SM_GA) & (lane < SM_GA + GLA_RANK), sm_ref[...], 0.0)
    pre = jnp.dot(ga, w2_ref[...], precision=HI, preferred_element_type=F32) + ba_ref[...]
    loga = _log_sigmoid(pre) * (1.0 / GLA_TAU)
    tri = (_iota((C, C), 1) <= _iota((C, C), 0)).astype(F32)
    b_all = jnp.dot(tri, loga, precision=HI, preferred_element_type=F32)
    mask3 = _iota((SUB, SUB, GLA_DK), 1) <= _iota((SUB, SUB, GLA_DK), 0)
    eye = (_iota((GLA_DK, GLA_DK), 0) == _iota((GLA_DK, GLA_DK), 1)).astype(F32)
    for h in range(GLA_H):
        q = g_ref[:, h * GLA_DK:(h + 1) * GLA_DK] * (GLA_DK ** -0.5)
        k = g_ref[:, GLA_H * GLA_DK + h * GLA_DK:GLA_H * GLA_DK + (h + 1) * GLA_DK]
        v0 = 2 * GLA_H * GLA_DK + h * GLA_DV
        vb = g_ref[:, v0:v0 + GLA_DV].astype(BF16)
        r0 = v0 + GLA_H * GLA_DV
        gr = g_ref[:, r0:r0 + GLA_DV]
        bh = b_all[:, h * GLA_DK:(h + 1) * GLA_DK]
        s = s_sc[h]
        sb = s.astype(BF16)
        rows = []
        for ib in range(C // SUB):
            a0 = ib * SUB
            bi, qi, ki = bh[a0:a0 + SUB], q[a0:a0 + SUB], k[a0:a0 + SUB]
            o_i = jnp.dot((qi * jnp.exp(bi)).astype(BF16), sb, preferred_element_type=F32)
            diff = bi[:, None, :] - bi[None, :, :]
            e = jnp.where(mask3, jnp.exp(jnp.where(mask3, diff, 0.0)), 0.0)
            d = jnp.sum(qi[:, None, :] * ki[None, :, :] * e, axis=-1)
            o_i = o_i + jnp.dot(d.astype(BF16), vb[a0:a0 + SUB], preferred_element_type=F32)
            if ib > 0:
                r = bi[0:1]
                qe = (qi * jnp.exp(bi - r)).astype(BF16)
                ke = (k[:a0] * jnp.exp(r - bh[:a0])).astype(BF16)
                att = lax.dot_general(qe, ke, NT, preferred_element_type=F32)
                o_i = o_i + jnp.dot(att.astype(BF16), vb[:a0], preferred_element_type=F32)
            rows.append(o_i)
        o = jnp.concatenate(rows, axis=0)
        bend = bh[C - 1:C]
        kdec = (k * jnp.exp(bend - bh)).astype(BF16)
        dcol = jnp.sum(eye * jnp.exp(bend), axis=1, keepdims=True)
        s_sc[h] = dcol * s + lax.dot_general(kdec, vb, TN, preferred_element_type=F32)
        y = o * lax.rsqrt(jnp.mean(o * o, axis=-1, keepdims=True) + RMS_EPS) * gn_ref[...]
        o_ref[:, h * GLA_DV:(h + 1) * GLA_DV] = (y * jax.nn.silu(gr)).astype(o_ref.dtype)

    @pl.when(c == pl.num_programs(1) - 1)
    def _():
        sout_ref[...] = s_sc[...]


def _gla(gla, small, w2p, ba_row, gn_row, s0):
    b, t, _ = gla.shape
    C = CHUNK
    n_gla = gla.shape[-1]
    return pl.pallas_call(
        functools.partial(_gla_kernel, C=C),
        grid=(b, t // C),
        in_specs=[
            pl.BlockSpec((None, C, n_gla), lambda bi, c: (bi, c, 0)),
            pl.BlockSpec((None, C, LANES), lambda bi, c: (bi, c, 0)),
            pl.BlockSpec(w2p.shape, lambda bi, c: (0, 0)),
            pl.BlockSpec(ba_row.shape, lambda bi, c: (0, 0)),
            pl.BlockSpec(gn_row.shape, lambda bi, c: (0, 0)),
            pl.BlockSpec((None, GLA_H, GLA_DK, GLA_DV), lambda bi, c: (bi, 0, 0, 0)),
        ],
        out_specs=[
            pl.BlockSpec((None, C, MIX_W), lambda bi, c: (bi, c, 0)),
            pl.BlockSpec((None, GLA_H, GLA_DK, GLA_DV), lambda bi, c: (bi, 0, 0, 0)),
        ],
        out_shape=[jax.ShapeDtypeStruct((b, t, MIX_W), BF16), jax.ShapeDtypeStruct(s0.shape, F32)],
        scratch_shapes=[pltpu.VMEM((GLA_H, GLA_DK, GLA_DV), F32)],
        compiler_params=_cp("parallel", "arbitrary"),
        name="gla",
    )(gla, small, w2p, ba_row, gn_row, s0)


def _rope(x, cos, sa, sb):
    half = ROPE_DIM // 2
    return x * cos + pltpu.roll(x, LANES - half, 1) * sa + pltpu.roll(x, half, 1) * sb


def _nsa_prep_kernel(x_ref, cos_ref, sa_ref, sb_ref, q_ref, cmp_ref, slc_ref, win_ref, *mean_ref, tb):
    cos, sa, sb = cos_ref[...], sa_ref[...], sb_ref[...]
    for h in range(NSA_H):
        hs = slice(h * NSA_HD, (h + 1) * NSA_HD)
        q_ref[:, hs] = _rope(x_ref[:, hs], cos, sa, sb) * (NSA_HD ** -0.5)
    base = NSA_H * NSA_HD
    for i, ref in enumerate((cmp_ref, slc_ref, win_ref)):
        k0 = base + i * 2 * NSA_HD
        ref[:, :NSA_HD] = _rope(x_ref[:, k0:k0 + NSA_HD], cos, sa, sb)
        ref[:, NSA_HD:] = x_ref[:, k0 + NSA_HD:k0 + 2 * NSA_HD]
    if mean_ref:
        kv = cmp_ref[...]
        mean_ref[0][...] = jnp.mean(kv.reshape(tb // NSA_BLOCK, NSA_BLOCK, 2 * NSA_HD), axis=1)


def _nsa_prep(nsa, cos, sa, sb, with_means):
    b, t, n = nsa.shape
    tb = min(t, 512)
    tab = pl.BlockSpec((tb, LANES), lambda bi, i: (i, 0))
    kv = pl.BlockSpec((None, tb, 2 * NSA_HD), lambda bi, i: (bi, i, 0))
    out_specs = [pl.BlockSpec((None, tb, MIX_W), lambda bi, i: (bi, i, 0)), kv, kv, kv]
    out_shape = [jax.ShapeDtypeStruct((b, t, MIX_W), F32)] + [jax.ShapeDtypeStruct((b, t, 2 * NSA_HD), F32)] * 3
    if with_means:
        out_specs.append(pl.BlockSpec((None, tb // NSA_BLOCK, 2 * NSA_HD), lambda bi, i: (bi, i, 0)))
        out_shape.append(jax.ShapeDtypeStruct((b, t // NSA_BLOCK, 2 * NSA_HD), F32))
    return pl.pallas_call(
        functools.partial(_nsa_prep_kernel, tb=tb),
        grid=(b, t // tb),
        in_specs=[pl.BlockSpec((None, tb, n), lambda bi, i: (bi, i, 0)), tab, tab, tab],
        out_specs=out_specs,
        out_shape=out_shape,
        compiler_params=_cp("parallel", "parallel"),
        name="nsa_prep",
    )(nsa, cos, sa, sb)


def _masked_softmax(s, mask):
    s = jnp.where(mask, s, -jnp.inf)
    m = jnp.max(s, axis=-1, keepdims=True)
    m = jnp.where(m == -jnp.inf, 0.0, m)
    p = jnp.where(mask, jnp.exp(s - m), 0.0)
    return p / jnp.maximum(jnp.sum(p, axis=-1, keepdims=True), 1e-30)


def _topk_mask(score, blk, n_sel):
    nb = score.shape[-1]
    rank = jnp.zeros(score.shape, jnp.int32)
    for m in range(nb):
        sm = score[:, m:m + 1]
        beats = (sm > score) | ((sm == score) & (blk > m))
        rank = rank + beats.astype(jnp.int32)
    return rank < n_sel


def _online_update(carry, s, mask, v):
    m_prev, l_prev, acc = carry
    s = jnp.where(mask, s, NEG)
    m_new = jnp.maximum(m_prev, jnp.max(s, axis=-1, keepdims=True))
    alpha = jnp.exp(m_prev - m_new)
    p = jnp.where(mask, jnp.exp(s - m_new), 0.0)
    l_new = alpha * l_prev + jnp.sum(p, axis=-1, keepdims=True)
    acc = alpha * acc + jnp.dot(p.astype(BF16), v, preferred_element_type=F32)
    return m_new, l_new, acc


def _nsa_attn_kernel(q_ref, sm_ref, cm_ref, slc_ref, win_ref, o_ref, tok_sc, *, tq, nb):
    qi = pl.program_id(1)
    qpos = qi * tq + _iota((tq, 1), 0)
    blk = _iota((tq, nb), 1)
    cmask = (blk + 1) * NSA_BLOCK <= qpos + 1
    kc, vc = cm_ref[:, :NSA_HD], cm_ref[:, NSA_HD:]
    o_c = []
    imp = jnp.zeros((tq, nb), F32)
    for h in range(NSA_H):
        s = lax.dot_general(q_ref[:, h * NSA_HD:(h + 1) * NSA_HD], kc, NT, precision=HI, preferred_element_type=F32)
        p = _masked_softmax(s, cmask)
        imp = imp + p
        o_c.append(jnp.dot(p, vc, precision=HI, preferred_element_type=F32))
    cur = qpos // NSA_BLOCK
    score = jnp.where(blk == cur, jnp.inf, jnp.where(cmask, imp, -jnp.inf))
    sel = _topk_mask(score, blk, min(NSA_TOPK, nb)) & (blk <= cur)
    sel_f = sel.astype(F32)
    per = tq // NSA_BLOCK
    low = _iota((tq, tq), 1)
    for kb in range(tok_sc.shape[0]):
        t = sel_f[:, kb * per:kb * per + 1]
        for j in range(1, per):
            t = jnp.where(low >= j * NSA_BLOCK, sel_f[:, kb * per + j:kb * per + j + 1], t)
        tok_sc[kb] = t
    q_all = jnp.concatenate([q_ref[:, h * NSA_HD:(h + 1) * NSA_HD] for h in range(NSA_H)], axis=0).astype(BF16)
    kcol = _iota((tq, tq), 1)

    def init():
        return (jnp.full((NSA_H * tq, 1), NEG, F32), jnp.zeros((NSA_H * tq, 1), F32),
                jnp.zeros((NSA_H * tq, NSA_HD), F32))

    def slc_body(kb, carry):
        rows = pl.ds(pl.multiple_of(kb * tq, tq), tq)
        k = slc_ref[rows, :NSA_HD].astype(BF16)
        v = slc_ref[rows, NSA_HD:].astype(BF16)
        s = lax.dot_general(q_all, k, NT, preferred_element_type=F32)
        mask = (tok_sc[kb] > 0.5) & (kb * tq + kcol <= qpos)
        return _online_update(carry, s, jnp.concatenate([mask] * NSA_H, axis=0), v)

    _, l_s, acc_s = lax.fori_loop(0, qi + 1, slc_body, init())

    def win_body(kb, carry):
        rows = pl.ds(pl.multiple_of(kb * tq, tq), tq)
        k = win_ref[rows, :NSA_HD].astype(BF16)
        v = win_ref[rows, NSA_HD:].astype(BF16)
        s = lax.dot_general(q_all, k, NT, preferred_element_type=F32)
        dist = qpos - (kb * tq + kcol)
        mask = (dist >= 0) & (dist < NSA_WINDOW)
        return _online_update(carry, s, jnp.concatenate([mask] * NSA_H, axis=0), v)

    _, l_w, acc_w = lax.fori_loop(jnp.maximum(qi - NSA_WINDOW // tq, 0), qi + 1, win_body, init())
    o_s = acc_s / l_s
    o_w = acc_w / l_w
    gate = jax.nn.sigmoid(sm_ref[...])
    for h in range(NSA_H):
        g0 = SM_NG + 3 * h
        rs = slice(h * tq, (h + 1) * tq)
        o = gate[:, g0:g0 + 1] * o_c[h] + gate[:, g0 + 1:g0 + 2] * o_s[rs] + gate[:, g0 + 2:g0 + 3] * o_w[rs]
        o_ref[:, h * NSA_HD:(h + 1) * NSA_HD] = o.astype(o_ref.dtype)


def _nsa_attn(qr, small, cmean, slc, win):
    b, t, _ = qr.shape
    tq = 128
    nb = t // NSA_BLOCK
    whole = lambda n: pl.BlockSpec((None, n, 2 * NSA_HD), lambda bi, qi: (bi, 0, 0))
    return pl.pallas_call(
        functools.partial(_nsa_attn_kernel, tq=tq, nb=nb),
        grid=(b, t // tq),
        in_specs=[
            pl.BlockSpec((None, tq, MIX_W), lambda bi, qi: (bi, qi, 0)),
            pl.BlockSpec((None, tq, LANES), lambda bi, qi: (bi, qi, 0)),
            whole(nb), whole(t), whole(t),
        ],
        out_specs=pl.BlockSpec((None, tq, MIX_W), lambda bi, qi: (bi, qi, 0)),
        out_shape=jax.ShapeDtypeStruct((b, t, MIX_W), BF16),
        scratch_shapes=[pltpu.VMEM((t // tq, tq, tq), F32)],
        compiler_params=_cp("parallel", "arbitrary"),
        name="nsa_attn",
    )(qr, small, cmean, slc, win)


def _dn_prep_kernel(x_ref, halo_ref, prev_ref, cw_ref, sm_ref, alog_ref, dtb_ref,
                    q_ref, k_ref, v_ref, g_ref, bt_ref, *, tb):
    halo = jnp.where(pl.program_id(1) == 0, prev_ref[...], halo_ref[...])
    xcat = jnp.concatenate([halo, x_ref[...]], axis=0)
    conv = x_ref[...] * cw_ref[DN_CONV - 1:DN_CONV]
    for j in range(DN_CONV - 1):
        sh = DN_CONV - 1 - j
        conv = conv + pltpu.roll(xcat, sh, 0)[8:] * cw_ref[j:j + 1]
    u = jax.nn.silu(conv)
    for h in range(DN_H):
        hs = slice(h * DN_DK, (h + 1) * DN_DK)
        uq = u[:, hs]
        q_ref[:, hs] = uq * lax.rsqrt(jnp.sum(uq * uq, axis=-1, keepdims=True) + 1e-6) * (DN_DK ** -0.5)
        uk = u[:, DN_H * DN_DK + h * DN_DK:DN_H * DN_DK + (h + 1) * DN_DK]
        k_ref[:, hs] = uk * lax.rsqrt(jnp.sum(uk * uk, axis=-1, keepdims=True) + 1e-6)
    v_ref[...] = u[:, 2 * DN_H * DN_DK:]
    sm = sm_ref[...]
    g_ref[...] = -jnp.exp(alog_ref[...]) * jax.nn.softplus(sm + dtb_ref[...])
    bt_ref[...] = jax.nn.sigmoid(sm)


def _dn_prep(dn, prev8, conv_w, small, alog_row, dtb_row):
    b, t, _ = dn.shape
    tb = min(t, 256)
    hb = tb // 8
    row = pl.BlockSpec((None, tb, MIX_W), lambda bi, i: (bi, i, 0))
    sm = pl.BlockSpec((None, tb, LANES), lambda bi, i: (bi, i, 0))
    one = lambda shape: pl.BlockSpec(shape, lambda bi, i: (0,) * len(shape))
    return pl.pallas_call(
        functools.partial(_dn_prep_kernel, tb=tb),
        grid=(b, t // tb),
        in_specs=[
            pl.BlockSpec((None, tb, DN_QKV), lambda bi, i: (bi, i, 0)),
            pl.BlockSpec((None, 8, DN_QKV), lambda bi, i: (bi, jnp.maximum(i * hb - 1, 0), 0)),
            pl.BlockSpec((None, 8, DN_QKV), lambda bi, i: (bi, 0, 0)),
            one(conv_w.shape), sm, one(alog_row.shape), one(dtb_row.shape),
        ],
        out_specs=[row, row, row, sm, sm],
        out_shape=[jax.ShapeDtypeStruct((b, t, MIX_W), F32)] * 3 + [jax.ShapeDtypeStruct((b, t, LANES), F32)] * 2,
        compiler_params=_cp("parallel", "parallel"),
        name="dn_prep",
    )(dn, dn, prev8, conv_w, small, alog_row, dtb_row)


def _gdn_kernel(q_ref, k_ref, v_ref, g_ref, bt_ref, z_ref, nrm_ref, s0_ref, o_ref, sout_ref, s_sc, *, C):
    c = pl.program_id(1)

    @pl.when(c == 0)
    def _():
        s_sc[...] = s0_ref[...]

    row, col = _iota((C, C), 0), _iota((C, C), 1)
    incl, strict = col <= row, col < row
    eye = (row == col).astype(F32)
    gam_all = jnp.dot(incl.astype(F32), g_ref[...], precision=HI, preferred_element_type=F32)
    lane = _iota((C, LANES), 1)
    n_sq = C.bit_length() - 2
    for h in range(DN_H):
        hs = slice(h * DN_DK, (h + 1) * DN_DK)
        q, k, v = q_ref[:, hs], k_ref[:, hs], v_ref[:, hs]
        gcol = gam_all[:, SM_DA + h:SM_DA + h + 1]
        bcol = bt_ref[:, SM_DB + h:SM_DB + h + 1]
        onehot = (lane == SM_DA + h).astype(F32)
        grow = lax.dot_general(onehot, gam_all, NT, precision=HI, preferred_element_type=F32)
        diff = gcol - grow
        dec_incl = jnp.where(incl, jnp.exp(jnp.where(incl, diff, 0.0)), 0.0)
        dec_strict = jnp.where(strict, dec_incl, 0.0)
        kk = lax.dot_general(k, k, NT, precision=HI, preferred_element_type=F32)
        x = -(bcol * kk * dec_strict)
        p = eye + x
        for _ in range(n_sq):
            x = jnp.dot(x, x, precision=HI, preferred_element_type=F32)
            p = p + jnp.dot(p, x, precision=HI, preferred_element_type=F32)
        rhs = jnp.concatenate([bcol * v, (bcol * jnp.exp(gcol)) * k], axis=1)
        w = jnp.dot(p, rhs, precision=HI, preferred_element_type=F32)
        wv, wk = w[:, :DN_DV], w[:, DN_DV:]
        qk = lax.dot_general(q.astype(BF16), k.astype(BF16), NT, preferred_element_type=F32) * dec_incl
        s = s_sc[h]
        sb = s.astype(BF16)
        u = wv - jnp.dot(wk.astype(BF16), sb, preferred_element_type=F32)
        ub = u.astype(BF16)
        o = (jnp.dot((q * jnp.exp(gcol)).astype(BF16), sb, preferred_element_type=F32)
             + jnp.dot(qk.astype(BF16), ub, preferred_element_type=F32))
        gend = gcol[C - 1:C]
        kdec = (k * jnp.exp(gend - gcol)).astype(BF16)
        s_sc[h] = jnp.exp(gend) * s + lax.dot_general(kdec, ub, TN, preferred_element_type=F32)
        y = o * lax.rsqrt(jnp.mean(o * o, axis=-1, keepdims=True) + RMS_EPS) * nrm_ref[...]
        o_ref[:, hs] = (y * jax.nn.silu(z_ref[:, hs])).astype(o_ref.dtype)

    @pl.when(c == pl.num_programs(1) - 1)
    def _():
        sout_ref[...] = s_sc[...]


def _gdn(qd, kd, vd, gsm, btsm, dn, nrm_row, s0):
    b, t, _ = qd.shape
    C = CHUNK
    row = pl.BlockSpec((None, C, MIX_W), lambda bi, c: (bi, c, 0))
    sm = pl.BlockSpec((None, C, LANES), lambda bi, c: (bi, c, 0))
    st = pl.BlockSpec((None, DN_H, DN_DK, DN_DV), lambda bi, c: (bi, 0, 0, 0))
    return pl.pallas_call(
        functools.partial(_gdn_kernel, C=C),
        grid=(b, t // C),
        in_specs=[row, row, row, sm, sm,
                  pl.BlockSpec((None, C, MIX_W), lambda bi, c: (bi, c, DN_QKV // MIX_W)),
                  pl.BlockSpec(nrm_row.shape, lambda bi, c: (0, 0)), st],
        out_specs=[row, st],
        out_shape=[jax.ShapeDtypeStruct((b, t, MIX_W), BF16), jax.ShapeDtypeStruct(s0.shape, F32)],
        scratch_shapes=[pltpu.VMEM((DN_H, DN_DK, DN_DV), F32)],
        compiler_params=_cp("parallel", "arbitrary"),
        name="gdn",
    )(qd, kd, vd, gsm, btsm, dn, nrm_row, s0)


def _fox_dec_kernel(pt_ref, fx_ref, sm_ref, b_ref, kv_ref, lf_ref, o_ref, lfo_ref, m_sc, l_sc, acc_sc, car_sc):
    j = pl.program_id(1)
    P = PAGE_SIZE
    qs = [fx_ref[:, h * FOX_HD:(h + 1) * FOX_HD] * (FOX_HD ** -0.5) for h in range(FOX_H)]

    @pl.when(j == 0)
    def _():
        lf_new = _log_sigmoid(sm_ref[...] + b_ref[...])
        lfo_ref[...] = lf_new
        car_sc[...] = lf_new
        for h in range(FOX_H):
            k_new = fx_ref[:, MIX_W + h * FOX_HD:MIX_W + (h + 1) * FOX_HD]
            m_sc[h] = jnp.sum(qs[h] * k_new, axis=-1, keepdims=True)
            l_sc[h] = jnp.ones((1, 1), F32)
            acc_sc[h] = fx_ref[:, 2 * MIX_W + h * FOX_HD:2 * MIX_W + (h + 1) * FOX_HD]

    lf = lf_ref[...]
    upper = (_iota((P, P), 1) > _iota((P, P), 0)).astype(F32)
    bias = jnp.dot(upper, lf, precision=HI, preferred_element_type=F32) + car_sc[:, 0:FOX_H]
    for h in range(FOX_H):
        kh = kv_ref[:, h * FOX_HD:(h + 1) * FOX_HD]
        vh = kv_ref[:, MIX_W + h * FOX_HD:MIX_W + (h + 1) * FOX_HD]
        s = jnp.sum(kh * qs[h], axis=-1, keepdims=True) + bias[:, h:h + 1]
        m_prev = m_sc[h]
        m_new = jnp.maximum(m_prev, jnp.max(s, axis=0, keepdims=True))
        alpha = jnp.exp(m_prev - m_new)
        p = jnp.exp(s - m_new)
        l_sc[h] = alpha * l_sc[h] + jnp.sum(p, axis=0, keepdims=True)
        acc_sc[h] = alpha * acc_sc[h] + jnp.sum(p * vh, axis=0, keepdims=True)
        m_sc[h] = m_new
    car_sc[:, 0:FOX_H] = car_sc[:, 0:FOX_H] + jnp.sum(lf, axis=0, keepdims=True)

    @pl.when(j == pl.num_programs(1) - 1)
    def _():
        for h in range(FOX_H):
            o_ref[:, h * FOX_HD:(h + 1) * FOX_HD] = acc_sc[h] / l_sc[h]


def _fox_decode(pt_flat, n_pages, fox_new, small_new, bias_row, kv_cache, lf_cache):
    b = fox_new.shape[0]
    page = lambda bi, j, pt: (pt[bi * n_pages + n_pages - 1 - j], 0, 0)
    per_b = lambda n: pl.BlockSpec((None, 1, n), lambda bi, j, pt: (bi, 0, 0))
    return pl.pallas_call(
        _fox_dec_kernel,
        grid_spec=pltpu.PrefetchScalarGridSpec(
            num_scalar_prefetch=1,
            grid=(b, n_pages),
            in_specs=[per_b(3 * MIX_W), per_b(LANES), pl.BlockSpec((1, LANES), lambda bi, j, pt: (0, 0)),
                      pl.BlockSpec((None, PAGE_SIZE, 2 * MIX_W), page),
                      pl.BlockSpec((None, PAGE_SIZE, FOX_H), page)],
            out_specs=[per_b(MIX_W), per_b(LANES)],
            scratch_shapes=[pltpu.VMEM((FOX_H, 1, 1), F32), pltpu.VMEM((FOX_H, 1, 1), F32),
                            pltpu.VMEM((FOX_H, 1, FOX_HD), F32), pltpu.VMEM((1, LANES), F32)],
        ),
        out_shape=[jax.ShapeDtypeStruct((b, 1, MIX_W), F32), jax.ShapeDtypeStruct((b, 1, LANES), F32)],
        compiler_params=_cp("parallel", "arbitrary"),
        name="fox_decode",
    )(pt_flat, fox_new, small_new, bias_row, kv_cache, lf_cache)


def _col(row, eye):
    return jnp.sum(eye * row, axis=1, keepdims=True)


def _rec_dec_kernel(g_ref, d_ref, sm_ref, w2_ref, ba_ref, gn_ref, cp_ref, cw_ref, alog_ref, dtb_ref, nrm_ref,
                    sg_ref, sd_ref, og_ref, od_ref, sgo_ref, sdo_ref):
    sm = sm_ref[...]
    lane = _iota((8, LANES), 1)
    ga = jnp.where((lane >= SM_GA) & (lane < SM_GA + GLA_RANK), jnp.broadcast_to(sm, (8, LANES)), 0.0)
    pre = jnp.dot(ga, w2_ref[...], precision=HI, preferred_element_type=F32)[0:1] + ba_ref[...]
    loga = _log_sigmoid(pre) * (1.0 / GLA_TAU)
    eye_k = (_iota((GLA_DK, GLA_DK), 0) == _iota((GLA_DK, GLA_DK), 1)).astype(F32)
    for h in range(GLA_H):
        q = g_ref[:, h * GLA_DK:(h + 1) * GLA_DK] * (GLA_DK ** -0.5)
        k = g_ref[:, GLA_H * GLA_DK + h * GLA_DK:GLA_H * GLA_DK + (h + 1) * GLA_DK]
        v0 = 2 * GLA_H * GLA_DK + h * GLA_DV
        v = g_ref[:, v0:v0 + GLA_DV]
        gr = g_ref[:, v0 + GLA_H * GLA_DV:v0 + GLA_H * GLA_DV + GLA_DV]
        ea = jnp.exp(loga[:, h * GLA_DK:(h + 1) * GLA_DK])
        s0 = sg_ref[h]
        o = jnp.sum(q * k, axis=-1, keepdims=True) * v + jnp.sum(_col(q * ea, eye_k) * s0, axis=0, keepdims=True)
        sgo_ref[h] = _col(ea, eye_k) * s0 + _col(k, eye_k) * v
        y = o * lax.rsqrt(jnp.mean(o * o, axis=-1, keepdims=True) + RMS_EPS) * gn_ref[...]
        og_ref[:, h * GLA_DV:(h + 1) * GLA_DV] = y * jax.nn.silu(gr)
    conv = d_ref[:, :DN_QKV] * cw_ref[DN_CONV - 1:DN_CONV]
    for j in range(DN_CONV - 1):
        conv = conv + cp_ref[j:j + 1] * cw_ref[j:j + 1]
    u = jax.nn.silu(conv)
    gall = -jnp.exp(alog_ref[...]) * jax.nn.softplus(sm + dtb_ref[...])
    ball = jax.nn.sigmoid(sm)
    eye_d = (_iota((DN_DK, DN_DK), 0) == _iota((DN_DK, DN_DK), 1)).astype(F32)
    for h in range(DN_H):
        uq = u[:, h * DN_DK:(h + 1) * DN_DK]
        q = uq * lax.rsqrt(jnp.sum(uq * uq, axis=-1, keepdims=True) + 1e-6) * (DN_DK ** -0.5)
        uk = u[:, DN_H * DN_DK + h * DN_DK:DN_H * DN_DK + (h + 1) * DN_DK]
        k = uk * lax.rsqrt(jnp.sum(uk * uk, axis=-1, keepdims=True) + 1e-6)
        v = u[:, 2 * DN_H * DN_DK + h * DN_DV:2 * DN_H * DN_DK + (h + 1) * DN_DV]
        eg = jnp.exp(gall[:, SM_DA + h:SM_DA + h + 1])
        beta = ball[:, SM_DB + h:SM_DB + h + 1]
        s0 = sd_ref[h]
        kcol = _col(k, eye_d)
        ks = jnp.sum(kcol * s0, axis=0, keepdims=True)
        qs = jnp.sum(_col(q, eye_d) * s0, axis=0, keepdims=True)
        un = beta * (v - eg * ks)
        o = eg * qs + jnp.sum(q * k, axis=-1, keepdims=True) * un
        sdo_ref[h] = eg * s0 + kcol * un
        y = o * lax.rsqrt(jnp.mean(o * o, axis=-1, keepdims=True) + RMS_EPS) * nrm_ref[...]
        z = d_ref[:, DN_QKV + h * DN_DV:DN_QKV + (h + 1) * DN_DV]
        od_ref[:, h * DN_DV:(h + 1) * DN_DV] = y * jax.nn.silu(z)


def _rec_decode(gla_new, dn_new, small_new, w2p, ba_row, gn_row, conv_prev, conv_w, alog_row, dtb_row, nrm_row,
                s_gla, s_dn):
    b = gla_new.shape[0]
    per_b = lambda n: pl.BlockSpec((None, 1, n), lambda bi: (bi, 0, 0))
    one = lambda a: pl.BlockSpec(a.shape, lambda bi: (0,) * a.ndim)
    sg = pl.BlockSpec((None, GLA_H, GLA_DK, GLA_DV), lambda bi: (bi, 0, 0, 0))
    sd = pl.BlockSpec((None, DN_H, DN_DK, DN_DV), lambda bi: (bi, 0, 0, 0))
    return pl.pallas_call(
        _rec_dec_kernel,
        grid=(b,),
        in_specs=[per_b(gla_new.shape[-1]), per_b(dn_new.shape[-1]), per_b(LANES), one(w2p), one(ba_row), one(gn_row),
                  pl.BlockSpec((None, DN_CONV - 1, DN_QKV), lambda bi: (bi, 0, 0)), one(conv_w), one(alog_row),
                  one(dtb_row), one(nrm_row), sg, sd],
        out_specs=[per_b(MIX_W), per_b(MIX_W), sg, sd],
        out_shape=[jax.ShapeDtypeStruct((b, 1, MIX_W), F32)] * 2
        + [jax.ShapeDtypeStruct(s_gla.shape, F32), jax.ShapeDtypeStruct(s_dn.shape, F32)],
        compiler_params=_cp("parallel"),
        name="rec_decode",
    )(gla_new, dn_new, small_new, w2p, ba_row, gn_row, conv_prev, conv_w, alog_row, dtb_row, nrm_row, s_gla, s_dn)


def _heads_to_rows(q_row):
    rows = [q_row[:, h * NSA_HD:(h + 1) * NSA_HD] for h in range(NSA_H)]
    return jnp.concatenate(rows + [jnp.zeros((8 - NSA_H, NSA_HD), F32)], axis=0)


CMP_PAGES = 4


def _nsa_dec_cmp_kernel(pt_ref, q_ref, *refs, nbp):
    page_refs, (mean_ref, oc_ref, sel_ref) = refs[:CMP_PAGES], refs[CMP_PAGES:]
    j = pl.program_id(1)
    per = PAGE_SIZE // NSA_BLOCK
    means = [jnp.mean(r[...].reshape(per, NSA_BLOCK, 2 * NSA_HD), axis=1) for r in page_refs]
    rows = CMP_PAGES * per
    mean_ref[pl.ds(pl.multiple_of(j * rows, rows), rows), :] = jnp.concatenate(means, axis=0)

    @pl.when(j == pl.num_programs(1) - 1)
    def _():
        q8 = _heads_to_rows(q_ref[...])
        kc, vc = mean_ref[:, :NSA_HD], mean_ref[:, NSA_HD:]
        s = lax.dot_general(q8, kc, NT, precision=HI, preferred_element_type=F32)
        p = _masked_softmax(s, jnp.full(s.shape, True))
        oc_ref[...] = jnp.dot(p, vc, precision=HI, preferred_element_type=F32)
        imp = jnp.sum(p[0:NSA_H], axis=0, keepdims=True)
        r_i, c_i = _iota((nbp, nbp), 0), _iota((nbp, nbp), 1)
        imp_col = _col(imp, (r_i == c_i).astype(F32))
        beats = (imp_col > imp) | ((imp_col == imp) & (r_i < c_i))
        rank = jnp.sum(beats.astype(jnp.int32), axis=0, keepdims=True)
        ids = jnp.where(rank == _iota((16, nbp), 0), _iota((16, nbp), 1), 0)
        sel_ref[...] = jnp.broadcast_to(jnp.sum(ids, axis=1, keepdims=True), (16, LANES))


def _nsa_dec_cmp(pt_flat, n_pages, q_new, cmp_cache):
    b = q_new.shape[0]
    nbp = n_pages * (PAGE_SIZE // NSA_BLOCK)
    n_sel = NSA_TOPK - 1
    assert nbp >= n_sel and NSA_TOPK <= 16 and n_pages % CMP_PAGES == 0
    page = lambda r: pl.BlockSpec((None, PAGE_SIZE, 2 * NSA_HD),
                                  lambda bi, j, pt: (pt[bi * n_pages + j * CMP_PAGES + r], 0, 0))
    return pl.pallas_call(
        functools.partial(_nsa_dec_cmp_kernel, nbp=nbp),
        grid_spec=pltpu.PrefetchScalarGridSpec(
            num_scalar_prefetch=1,
            grid=(b, n_pages // CMP_PAGES),
            in_specs=[pl.BlockSpec((None, 1, MIX_W), lambda bi, j, pt: (bi, 0, 0))]
            + [page(r) for r in range(CMP_PAGES)],
            out_specs=[pl.BlockSpec((None, nbp, 2 * NSA_HD), lambda bi, j, pt: (bi, 0, 0)),
                       pl.BlockSpec((None, 8, NSA_HD), lambda bi, j, pt: (bi, 0, 0)),
                       pl.BlockSpec((None, 16, LANES), lambda bi, j, pt: (bi, 0, 0))],
        ),
        out_shape=[jax.ShapeDtypeStruct((b, nbp, 2 * NSA_HD), F32), jax.ShapeDtypeStruct((b, 8, NSA_HD), F32),
                   jax.ShapeDtypeStruct((b, 16, LANES), jnp.int32)],
        compiler_params=_cp("parallel", "arbitrary"),
        name="nsa_dec_cmp",
    )(pt_flat, q_new, *([cmp_cache] * CMP_PAGES))


def _nsa_dec_attn_kernel(pt_ref, sel_ref, q_ref, blk_ref, snew_ref, win_ref, wnew_ref, oc_ref, sm_ref, o_ref,
                         m_sc, l_sc, acc_sc, *, sw):
    s_i = pl.program_id(1)
    q8 = _heads_to_rows(q_ref[...])

    @pl.when(s_i == 0)
    def _():
        m_sc[...] = jnp.sum(q8 * snew_ref[:, :NSA_HD], axis=-1, keepdims=True)
        l_sc[...] = jnp.ones_like(l_sc)
        acc_sc[...] = jnp.broadcast_to(snew_ref[:, NSA_HD:], acc_sc.shape)

    k, v = blk_ref[:, :NSA_HD], blk_ref[:, NSA_HD:]
    s = lax.dot_general(q8, k, NT, precision=HI, preferred_element_type=F32)
    m_prev = m_sc[...]
    m_new = jnp.maximum(m_prev, jnp.max(s, axis=-1, keepdims=True))
    alpha = jnp.exp(m_prev - m_new)
    p = jnp.exp(s - m_new)
    l_sc[...] = alpha * l_sc[...] + jnp.sum(p, axis=-1, keepdims=True)
    acc_sc[...] = alpha * acc_sc[...] + jnp.dot(p, v, precision=HI, preferred_element_type=F32)
    m_sc[...] = m_new

    @pl.when(s_i == pl.num_programs(1) - 1)
    def _():
        o_s = acc_sc[...] / l_sc[...]
        kw, vw = win_ref[:, :NSA_HD], win_ref[:, NSA_HD:]
        sw_ = lax.dot_general(q8, kw, NT, precision=HI, preferred_element_type=F32)
        wmask = (sw - _iota((8, sw), 1)) < NSA_WINDOW
        s_new = jnp.sum(q8 * wnew_ref[:, :NSA_HD], axis=-1, keepdims=True)
        m = jnp.maximum(jnp.max(jnp.where(wmask, sw_, -jnp.inf), axis=-1, keepdims=True), s_new)
        pw = jnp.where(wmask, jnp.exp(sw_ - m), 0.0)
        pn = jnp.exp(s_new - m)
        o_w = (jnp.dot(pw, vw, precision=HI, preferred_element_type=F32) + pn * wnew_ref[:, NSA_HD:]) / (
            jnp.sum(pw, axis=-1, keepdims=True) + pn)
        gate = jax.nn.sigmoid(sm_ref[...])
        o_c = oc_ref[...]
        for h in range(NSA_H):
            g0 = SM_NG + 3 * h
            o_ref[:, h * NSA_HD:(h + 1) * NSA_HD] = (
                gate[:, g0:g0 + 1] * o_c[h:h + 1] + gate[:, g0 + 1:g0 + 2] * o_s[h:h + 1]
                + gate[:, g0 + 2:g0 + 3] * o_w[h:h + 1])


def _nsa_dec_attn(pt_flat, sel_flat, n_pages, n_sel, q_new, slc_cache, slc_new, win_cache, win_new, o_c, small_new):
    b = q_new.shape[0]
    sw = win_cache.shape[1]
    per = PAGE_SIZE // NSA_BLOCK
    half = slc_cache.reshape(slc_cache.shape[0] * per, NSA_BLOCK, 2 * NSA_HD)

    def blk_map(bi, s, pt, sel):
        n = sel[bi * n_sel + s]
        return (pt[bi * n_pages + n // per] * per + n % per, 0, 0)

    per_b = lambda r, n: pl.BlockSpec((None, r, n), lambda bi, s, pt, sel: (bi, 0, 0))
    return pl.pallas_call(
        functools.partial(_nsa_dec_attn_kernel, sw=sw),
        grid_spec=pltpu.PrefetchScalarGridSpec(
            num_scalar_prefetch=2,
            grid=(b, n_sel),
            in_specs=[per_b(1, MIX_W), pl.BlockSpec((None, NSA_BLOCK, 2 * NSA_HD), blk_map), per_b(1, 2 * NSA_HD),
                      per_b(sw, 2 * NSA_HD), per_b(1, 2 * NSA_HD), per_b(8, NSA_HD), per_b(1, LANES)],
            out_specs=per_b(1, MIX_W),
            scratch_shapes=[pltpu.VMEM((8, 1), F32), pltpu.VMEM((8, 1), F32), pltpu.VMEM((8, NSA_HD), F32)],
        ),
        out_shape=jax.ShapeDtypeStruct((b, 1, MIX_W), F32),
        compiler_params=_cp("parallel", "arbitrary"),
        name="nsa_dec_attn",
    )(pt_flat, sel_flat, q_new, half, slc_new, win_cache, win_new, o_c, small_new)


def _lane_row(vals, at):
    return jnp.zeros((1, LANES), F32).at[0, at:at + vals.shape[0]].set(vals.astype(F32))


def _layer_weights(w_in, fox_b_f, gla_w_a2, gla_b_a, gla_norm, dn_conv_w, dn_a_log, dn_dt_bias, dn_norm,
                   w_branch, w_out, w_up, w_down):
    offs = [0]
    for s in SPLIT_SIZES:
        offs.append(offs[-1] + s)
    seg = lambda i: w_in[:, offs[i]:offs[i + 1]]
    (fq, fk, fv, ff, gq, gk, gv, ga, gr, nq, nkc, nks, nkw, ng, dqkv, da, dbeta, dz, mg) = [seg(i) for i in range(19)]
    cat = lambda parts: jnp.concatenate(parts, axis=1).astype(BF16)
    small = [ff, ga, ng, da, dbeta]
    pad = LANES - sum(p.shape[1] for p in small)
    w = dict(
        fox=cat([fq, fk, fv]), gla=cat([gq, gk, gv, gr]), nsa=cat([nq, nkc, nks, nkw]), dn=cat([dqkv, dz]),
        small=cat(small + [jnp.zeros((D_MODEL, pad), F32)]), mg=mg.astype(BF16),
        bf_row=_lane_row(fox_b_f, SM_FF),
        w2p=jnp.zeros((LANES, GLA_H * GLA_DK), F32).at[SM_GA:SM_GA + GLA_RANK].set(gla_w_a2),
        ba_row=gla_b_a.reshape(1, -1), gn_row=gla_norm.reshape(1, -1), conv_w=dn_conv_w,
        alog_row=_lane_row(dn_a_log, SM_DA), dtb_row=_lane_row(dn_dt_bias, SM_DA), dnn_row=dn_norm.reshape(1, -1),
        wb=w_branch.astype(BF16), w_out=w_out.astype(BF16), w_up=w_up.astype(BF16), w_down=w_down.astype(BF16),
    )
    return w


def _rope_tables(pos):
    half = ROPE_DIM // 2
    inv = ROPE_THETA ** (-jnp.arange(half, dtype=F32) / half)
    ang = pos.astype(F32)[:, None] * inv[None, :]
    cos, sin = jnp.cos(ang), jnp.sin(ang)
    t = pos.shape[0]
    z = lambda n: jnp.zeros((t, n), F32)
    return (jnp.concatenate([cos, cos, jnp.ones((t, LANES - ROPE_DIM), F32)], axis=1),
            jnp.concatenate([-sin, z(LANES - half)], axis=1),
            jnp.concatenate([z(half), sin, z(LANES - ROPE_DIM)], axis=1))


def _project(h, w):
    return {name: _mm(h, w[name]) for name in ("fox", "gla", "nsa", "dn", "small", "mg")}


def _mixer_prompt(h, b, t, w):
    p = _project(h, w)
    fox, gla, nsa, dn, small = (p[n].reshape(b, t, -1) for n in ("fox", "gla", "nsa", "dn", "small"))
    lf, c, ct = _fox_prep(small, w["bf_row"])
    o_fox = _fox_attn(fox, c, ct)
    o_gla, s_gla = _gla(gla, small, w["w2p"], w["ba_row"], w["gn_row"], jnp.zeros((b, GLA_H, GLA_DK, GLA_DV), F32))
    qr, cmp_kv, slc_kv, win_kv, cmean = _nsa_prep(nsa, *_rope_tables(jnp.arange(t)), True)
    o_nsa = _nsa_attn(qr, small, cmean, slc_kv, win_kv)
    qd, kd, vd, gsm, btsm = _dn_prep(dn, jnp.zeros((b, 8, DN_QKV), F32), w["conv_w"], small, w["alog_row"],
                                     w["dtb_row"])
    o_dn, s_dn = _gdn(qd, kd, vd, gsm, btsm, dn, w["dnn_row"], jnp.zeros((b, DN_H, DN_DK, DN_DV), F32))
    mix = _branch_gate([o.reshape(b * t, MIX_W) for o in (o_fox, o_gla, o_nsa, o_dn)], w["wb"], p["mg"])
    kv5 = lambda a: a.reshape(b, t, 2, 1, NSA_HD)
    wl = min(NSA_WINDOW, t)
    state = (fox[:, :, MIX_W:].reshape(b, t, 2, FOX_H, FOX_HD), lf[:, :, :FOX_H], kv5(cmp_kv), kv5(slc_kv),
             kv5(win_kv)[:, t - wl:], s_gla, s_dn, dn[:, t - (DN_CONV - 1):, :DN_QKV])
    return mix, state


def _mixer_sample(h, past, pt_flat, n_pages, w):
    fox_kv_c, fox_lf_c, cmp_c, slc_c, win_c, s_gla0, s_dn0, conv0 = past
    b = h.shape[0]
    past_len = n_pages * PAGE_SIZE
    p = _project(h, w)
    fox, gla, dn, small = (p[n].reshape(b, 1, -1) for n in ("fox", "gla", "dn", "small"))
    n_pool = fox_kv_c.shape[0]
    o_fox, lf_new = _fox_decode(pt_flat, n_pages, fox, small, w["bf_row"],
                                fox_kv_c.reshape(n_pool, PAGE_SIZE, 2 * MIX_W), fox_lf_c)
    o_gla, o_dn, s_gla, s_dn = _rec_decode(gla, dn, small, w["w2p"], w["ba_row"], w["gn_row"], conv0, w["conv_w"],
                                           w["alog_row"], w["dtb_row"], w["dnn_row"], s_gla0, s_dn0)
    tabs = _rope_tables(jnp.full((b,), past_len, jnp.int32))
    qr, cmp_new, slc_new, win_new = _nsa_prep(p["nsa"].reshape(1, b, -1), *tabs, False)
    qr, cmp_new, slc_new, win_new = (a.reshape(b, 1, -1) for a in (qr, cmp_new, slc_new, win_new))
    _, o_c, sel = _nsa_dec_cmp(pt_flat, n_pages, qr, cmp_c.reshape(n_pool, PAGE_SIZE, 2 * NSA_HD))
    n_sel = NSA_TOPK - 1
    sel_flat = sel[:, :n_sel, 0].reshape(-1)
    sw = win_c.shape[1]
    win_rows = win_c.reshape(b, sw, 2 * NSA_HD)
    o_nsa = _nsa_dec_attn(pt_flat, sel_flat, n_pages, n_sel, qr, slc_c.reshape(n_pool, PAGE_SIZE, 2 * NSA_HD),
                          slc_new, win_rows, win_new, o_c, small)
    mix = _branch_gate([o.reshape(b, MIX_W).astype(BF16) for o in (o_fox, o_gla, o_nsa, o_dn)], w["wb"], p["mg"])
    kv5 = lambda a: a.reshape(b, 1, 2, 1, NSA_HD)
    win_all = jnp.concatenate([win_rows, win_new], axis=1)
    wl = min(NSA_WINDOW, sw + 1)
    new_win = win_all[:, sw + 1 - wl:].reshape(b, wl, 2, 1, NSA_HD)
    new_conv = jnp.concatenate([conv0, dn[:, :, :DN_QKV]], axis=1)[:, 1:]
    state = (fox[:, :, MIX_W:].reshape(b, 1, 2, FOX_H, FOX_HD), lf_new[:, :, :FOX_H], kv5(cmp_new), kv5(slc_new),
             new_win, s_gla, s_dn, new_conv)
    return mix, state


def _trunk_layer(x, mixer, norms, w):
    g_pre_mix, g_post_mix, g_pre_mlp, g_post_mlp = norms
    mix, state = mixer(_rmsnorm_cast(x, g_pre_mix))
    x = _mm_norm_res(mix, w["w_out"], g_post_mix, x)
    hid = _mm(_rmsnorm_cast(x, g_pre_mlp), w["w_up"], BF16, "relu2")
    x = _mm_norm_res(hid, w["w_down"], g_post_mlp, x)
    return x, state


def kernel(x_prompt, x_sample, cache_fox_kv, cache_fox_logf, cache_nsa_cmp_kv, cache_nsa_slc_kv, cache_nsa_win_kv,
           state_gla, state_dn, state_dn_conv, page_table, norm_pre_mix, norm_post_mix, norm_pre_mlp, norm_post_mlp,
           w_in, fox_b_f, gla_w_a2, gla_b_a, gla_norm, dn_conv_w, dn_a_log, dn_dt_bias, dn_norm, w_branch, w_out,
           w_up, w_down):
    bp, tp, _ = x_prompt.shape
    bs, ts, _ = x_sample.shape
    assert ts == 1
    n_pages = page_table.shape[1]
    pt_flat = page_table.reshape(-1).astype(jnp.int32)
    y_p = x_prompt.reshape(bp * tp, D_MODEL)
    y_s = x_sample.reshape(bs, D_MODEL)
    new_p = [[] for _ in range(8)]
    new_s = [[] for _ in range(8)]
    for l in range(DEPTH):
        w = _layer_weights(w_in[l], fox_b_f[l], gla_w_a2[l], gla_b_a[l], gla_norm[l], dn_conv_w[l], dn_a_log[l],
                           dn_dt_bias[l], dn_norm[l], w_branch[l], w_out[l], w_up[l], w_down[l])
        norms = (norm_pre_mix[l], norm_post_mix[l], norm_pre_mlp[l], norm_post_mlp[l])
        past = (cache_fox_kv[l], cache_fox_logf[l], cache_nsa_cmp_kv[l], cache_nsa_slc_kv[l], cache_nsa_win_kv[l],
                state_gla[l], state_dn[l], state_dn_conv[l])
        y_p, st_p = _trunk_layer(y_p, lambda h: _mixer_prompt(h, bp, tp, w), norms, w)
        y_s, st_s = _trunk_layer(y_s, lambda h: _mixer_sample(h, past, pt_flat, n_pages, w), norms, w)
        for i in range(8):
            new_p[i].append(st_p[i])
            new_s[i].append(st_s[i])
    fox_kv_p, fox_logf_p, cmp_kv_p, slc_kv_p, win_kv_p, gla_p, dn_p, conv_p = [jnp.stack(a) for a in new_p]
    fox_kv_s, fox_logf_s, cmp_kv_s, slc_kv_s, win_kv_s, gla_s, dn_s, conv_s = [jnp.stack(a) for a in new_s]
    return (y_p.reshape(bp, tp, D_MODEL), y_s.reshape(bs, ts, D_MODEL), fox_kv_p, fox_kv_s, fox_logf_p, fox_logf_s,
            cmp_kv_p, cmp_kv_s, slc_kv_p, slc_kv_s, win_kv_p, win_kv_s, gla_p, gla_s, dn_p, dn_s, conv_p, conv_s)
```

```python
import functools

import jax
import jax.numpy as jnp
from jax import lax
from jax.experimental import pallas as pl
from jax.experimental.pallas import tpu as pltpu

F32 = jnp.float32
BF16 = jnp.bfloat16
HI = lax.Precision.HIGHEST
NT = (((1,), (1,)), ((), ()))
TN = (((0,), (0,)), ((), ()))
NEG = -1e30

D_MODEL = 2048
DEPTH = 2
PAGE_SIZE = 128
N_BRANCH = 4
MIX_W = D_MODEL // 4
FOX_H = 4
FOX_HD = MIX_W // FOX_H
GLA_H = 4
GLA_DK = MIX_W // (2 * GLA_H)
GLA_DV = MIX_W // GLA_H
GLA_RANK = 16
GLA_TAU = 16.0
NSA_H = 4
NSA_HD = MIX_W // NSA_H
NSA_BLOCK = 64
NSA_TOPK = 16
NSA_WINDOW = 512
DN_H = 4
DN_DK = MIX_W // DN_H
DN_DV = MIX_W // DN_H
DN_CONV = 4
DN_QKV = DN_H * (2 * DN_DK + DN_DV)
ROPE_DIM = NSA_HD // 4
ROPE_THETA = 500000.0
D_FF = 4 * D_MODEL
CHUNK = 64
SUB = 16
RMS_EPS = 1e-6
LANES = 128
SPLIT_SIZES = (
    FOX_H * FOX_HD, FOX_H * FOX_HD, FOX_H * FOX_HD, FOX_H,
    GLA_H * GLA_DK, GLA_H * GLA_DK, GLA_H * GLA_DV, GLA_RANK, GLA_H * GLA_DV,
    NSA_H * NSA_HD, 2 * NSA_HD, 2 * NSA_HD, 2 * NSA_HD, 3 * NSA_H,
    DN_QKV, DN_H, DN_H, DN_H * DN_DV,
    N_BRANCH * D_MODEL,
)
SM_FF, SM_GA, SM_NG, SM_DA, SM_DB = 0, 4, 20, 32, 36
VMEM_LIMIT = 56 * 1024 * 1024


def _cp(*sem):
    return pltpu.CompilerParams(dimension_semantics=sem, vmem_limit_bytes=VMEM_LIMIT)


def _pick(n, cap):
    if n <= cap:
        return n
    best = None
    for t in range(LANES, cap + 1, LANES):
        if n % t == 0:
            best = t
    assert best is not None, n
    return best


def _log_sigmoid(z):
    return jnp.minimum(z, 0.0) - jnp.log1p(jnp.exp(-jnp.abs(z)))


def _iota(shape, dim):
    return lax.broadcasted_iota(jnp.int32, shape, dim)


def _rmsnorm_cast_kernel(x_ref, g_ref, o_ref):
    x = x_ref[...]
    y = x * lax.rsqrt(jnp.mean(x * x, axis=-1, keepdims=True) + RMS_EPS)
    o_ref[...] = (y * g_ref[...]).astype(o_ref.dtype)


def _rmsnorm_cast(x, g):
    m, d = x.shape
    tm = min(m, 512)
    return pl.pallas_call(
        _rmsnorm_cast_kernel,
        grid=(m // tm,),
        in_specs=[pl.BlockSpec((tm, d), lambda i: (i, 0)), pl.BlockSpec((1, d), lambda i: (0, 0))],
        out_specs=pl.BlockSpec((tm, d), lambda i: (i, 0)),
        out_shape=jax.ShapeDtypeStruct((m, d), BF16),
        compiler_params=_cp("parallel"),
        name="rmsnorm_cast",
    )(x, g.reshape(1, d))


def _mm_kernel(a_ref, w_ref, o_ref, *, act):
    y = jnp.dot(a_ref[...], w_ref[...], preferred_element_type=F32)
    if act == "relu2":
        y = jnp.square(jnp.maximum(y, 0.0))
    o_ref[...] = y.astype(o_ref.dtype)


def _mm(a, w, out_dtype=F32, act=None):
    m, k = a.shape
    n = w.shape[1]
    tm = min(m, 512)
    tn = _pick(n, 1024)
    return pl.pallas_call(
        functools.partial(_mm_kernel, act=act),
        grid=(n // tn, m // tm),
        in_specs=[pl.BlockSpec((tm, k), lambda j, i: (i, 0)), pl.BlockSpec((k, tn), lambda j, i: (0, j))],
        out_specs=pl.BlockSpec((tm, tn), lambda j, i: (i, j)),
        out_shape=jax.ShapeDtypeStruct((m, n), out_dtype),
        compiler_params=_cp("parallel", "parallel"),
        name="mm",
    )(a, w)


def _mm_norm_res_kernel(a_ref, w_ref, g_ref, x_ref, o_ref, acc_ref):
    k = pl.program_id(1)

    @pl.when(k == 0)
    def _():
        acc_ref[...] = jnp.zeros_like(acc_ref)

    acc_ref[...] += jnp.dot(a_ref[...], w_ref[...], preferred_element_type=F32)

    @pl.when(k == pl.num_programs(1) - 1)
    def _():
        y = acc_ref[...]
        y = y * lax.rsqrt(jnp.mean(y * y, axis=-1, keepdims=True) + RMS_EPS)
        o_ref[...] = x_ref[...] + y * g_ref[...]


def _mm_norm_res(a, w, g, x):
    m, k = a.shape
    n = w.shape[1]
    tm = min(m, 512)
    tk = _pick(k, 512)
    return pl.pallas_call(
        _mm_norm_res_kernel,
        grid=(m // tm, k // tk),
        in_specs=[
            pl.BlockSpec((tm, tk), lambda i, kk: (i, kk)),
            pl.BlockSpec((tk, n), lambda i, kk: (kk, 0)),
            pl.BlockSpec((1, n), lambda i, kk: (0, 0)),
            pl.BlockSpec((tm, n), lambda i, kk: (i, 0)),
        ],
        out_specs=pl.BlockSpec((tm, n), lambda i, kk: (i, 0)),
        out_shape=jax.ShapeDtypeStruct((m, n), F32),
        scratch_shapes=[pltpu.VMEM((tm, n), F32)],
        compiler_params=_cp("parallel", "arbitrary"),
        name="mm_norm_res",
    )(a, w, g.reshape(1, n), x)


def _branch_gate_kernel(o0, o1, o2, o3, wb_ref, m0, m1, m2, m3, out_ref):
    acc = None
    for n, (o_n, m_n) in enumerate(((o0, m0), (o1, m1), (o2, m2), (o3, m3))):
        y = jnp.dot(o_n[...], wb_ref[n], preferred_element_type=F32)
        t = jax.nn.sigmoid(m_n[...]) * y
        acc = t if acc is None else acc + t
    out_ref[...] = acc.astype(out_ref.dtype)


def _branch_gate(branches, wb, mg):
    m = mg.shape[0]
    tm = min(m, 512)
    tn = 1024
    nj = D_MODEL // tn
    o_spec = pl.BlockSpec((tm, MIX_W), lambda j, i: (i, 0))
    m_specs = [pl.BlockSpec((tm, tn), functools.partial(lambda j, i, n: (i, n * nj + j), n=n)) for n in range(N_BRANCH)]
    return pl.pallas_call(
        _branch_gate_kernel,
        grid=(nj, m // tm),
        in_specs=[o_spec] * 4 + [pl.BlockSpec((N_BRANCH, MIX_W, tn), lambda j, i: (0, 0, j))] + m_specs,
        out_specs=pl.BlockSpec((tm, tn), lambda j, i: (i, j)),
        out_shape=jax.ShapeDtypeStruct((m, D_MODEL), BF16),
        compiler_params=_cp("parallel", "parallel"),
        name="branch_gate",
    )(*branches, wb, mg, mg, mg, mg)


def _fox_prep_kernel(s_ref, b_ref, lf_ref, c_ref, ct_ref, carry_ref, *, tb):
    @pl.when(pl.program_id(1) == 0)
    def _():
        carry_ref[...] = jnp.zeros_like(carry_ref)

    lf = _log_sigmoid(s_ref[...] + b_ref[...])
    tri = (_iota((tb, tb), 1) <= _iota((tb, tb), 0)).astype(F32)
    c = jnp.dot(tri, lf, precision=HI, preferred_element_type=F32) + carry_ref[...]
    lf_ref[...] = lf
    c_ref[...] = c
    ct_ref[...] = c.T[:8]
    carry_ref[...] = c[tb - 1:tb]


def _fox_prep(small, bias_row):
    b, t, _ = small.shape
    tb = min(t, 256)
    blk = pl.BlockSpec((None, tb, LANES), lambda bi, i: (bi, i, 0))
    return pl.pallas_call(
        functools.partial(_fox_prep_kernel, tb=tb),
        grid=(b, t // tb),
        in_specs=[blk, pl.BlockSpec((1, LANES), lambda bi, i: (0, 0))],
        out_specs=[blk, blk, pl.BlockSpec((None, 8, tb), lambda bi, i: (bi, 0, i))],
        out_shape=[jax.ShapeDtypeStruct((b, t, LANES), F32)] * 2 + [jax.ShapeDtypeStruct((b, 8, t), F32)],
        scratch_shapes=[pltpu.VMEM((1, LANES), F32)],
        compiler_params=_cp("parallel", "arbitrary"),
        name="fox_prep",
    )(small, bias_row)


def _fox_attn_kernel(q_ref, k_ref, v_ref, cq_ref, ck_ref, o_ref, m_sc, l_sc, acc_sc, *, tq, tk):
    qi = pl.program_id(1)
    ki = pl.program_id(2)

    @pl.when(ki == 0)
    def _():
        m_sc[...] = jnp.full_like(m_sc, NEG)
        l_sc[...] = jnp.zeros_like(l_sc)
        acc_sc[...] = jnp.zeros_like(acc_sc)

    def step(diagonal):
        mask = _iota((tq, tk), 1) <= _iota((tq, tk), 0)
        for h in range(FOX_H):
            hs = slice(h * FOX_HD, (h + 1) * FOX_HD)
            q = (q_ref[:, hs] * (FOX_HD ** -0.5)).astype(BF16)
            s = lax.dot_general(q, k_ref[:, hs].astype(BF16), NT, preferred_element_type=F32)
            s = s + cq_ref[:, h:h + 1] - ck_ref[h:h + 1, :]
            if diagonal:
                s = jnp.where(mask, s, NEG)
            m_prev = m_sc[h]
            m_new = jnp.maximum(m_prev, jnp.max(s, axis=-1, keepdims=True))
            alpha = jnp.exp(m_prev - m_new)
            p = jnp.exp(s - m_new)
            if diagonal:
                p = jnp.where(mask, p, 0.0)
            l_sc[h] = alpha * l_sc[h] + jnp.sum(p, axis=-1, keepdims=True)
            acc_sc[h] = alpha * acc_sc[h] + jnp.dot(p.astype(BF16), v_ref[:, hs].astype(BF16),
                                                    preferred_element_type=F32)
            m_sc[h] = m_new

    pl.when(ki < qi)(functools.partial(step, False))
    pl.when(ki == qi)(functools.partial(step, True))

    @pl.when(ki == pl.num_programs(2) - 1)
    def _():
        for h in range(FOX_H):
            o_ref[:, h * FOX_HD:(h + 1) * FOX_HD] = (acc_sc[h] / l_sc[h]).astype(o_ref.dtype)


def _fox_attn(fox, c, ct):
    b, t, _ = fox.shape
    tq = tk = min(t, 512)

    def kmap(col):
        return lambda bi, qi, ki: (bi, jnp.minimum(ki, (qi * tq + tq - 1) // tk), col)

    return pl.pallas_call(
        functools.partial(_fox_attn_kernel, tq=tq, tk=tk),
        grid=(b, t // tq, t // tk),
        in_specs=[
            pl.BlockSpec((None, tq, MIX_W), lambda bi, qi, ki: (bi, qi, 0)),
            pl.BlockSpec((None, tk, MIX_W), kmap(1)),
            pl.BlockSpec((None, tk, MIX_W), kmap(2)),
            pl.BlockSpec((None, tq, LANES), lambda bi, qi, ki: (bi, qi, 0)),
            pl.BlockSpec((None, 8, tk), lambda bi, qi, ki: (bi, 0, jnp.minimum(ki, (qi * tq + tq - 1) // tk))),
        ],
        out_specs=pl.BlockSpec((None, tq, MIX_W), lambda bi, qi, ki: (bi, qi, 0)),
        out_shape=jax.ShapeDtypeStruct((b, t, MIX_W), BF16),
        scratch_shapes=[pltpu.VMEM((FOX_H, tq, 1), F32), pltpu.VMEM((FOX_H, tq, 1), F32),
                        pltpu.VMEM((FOX_H, tq, FOX_HD), F32)],
        compiler_params=_cp("parallel", "parallel", "arbitrary"),
        name="fox_attn",
    )(fox, fox, fox, c, ct)


def _gla_kernel(g_ref, sm_ref, w2_ref, ba_ref, gn_ref, s0_ref, o_ref, sout_ref, s_sc, *, C):
    c = pl.program_id(1)

    @pl.when(c == 0)
    def _():
        s_sc[...] = s0_ref[...]

    lane = _iota((C, LANES), 1)
    ga = jnp.where((lane >= SM_GA) & (lane < SM_GA + GLA_RANK), sm_ref[...], 0.0)
    pre = jnp.dot(ga, w2_ref[...], precision=HI, preferred_element_type=F32) + ba_ref[...]
    loga = _log_sigmoid(pre) * (1.0 / GLA_TAU)
    tri = (_iota((C, C), 1) <= _iota((C, C), 0)).astype(F32)
    b_all = jnp.dot(tri, loga, precision=HI, preferred_element_type=F32)
    mask3 = _iota((SUB, SUB, GLA_DK), 1) <= _iota((SUB, SUB, GLA_DK), 0)
    eye = (_iota((GLA_DK, GLA_DK), 0) == _iota((GLA_DK, GLA_DK), 1)).astype(F32)
    for h in range(GLA_H):
        q = g_ref[:, h * GLA_DK:(h + 1) * GLA_DK] * (GLA_DK ** -0.5)
        k = g_ref[:, GLA_H * GLA_DK + h * GLA_DK:GLA_H * GLA_DK + (h + 1) * GLA_DK]
        v0 = 2 * GLA_H * GLA_DK + h * GLA_DV
        vb = g_ref[:, v0:v0 + GLA_DV].astype(BF16)
        r0 = v0 + GLA_H * GLA_DV
        gr = g_ref[:, r0:r0 + GLA_DV]
        bh = b_all[:, h * GLA_DK:(h + 1) * GLA_DK]
        s = s_sc[h]
        sb = s.astype(BF16)
        rows = []
        for ib in range(C // SUB):
            a0 = ib * SUB
            bi, qi, ki = bh[a0:a0 + SUB], q[a0:a0 + SUB], k[a0:a0 + SUB]
            o_i = jnp.dot((qi * jnp.exp(bi)).astype(BF16), sb, preferred_element_type=F32)
            diff = bi[:, None, :] - bi[None, :, :]
            e = jnp.where(mask3, jnp.exp(jnp.where(mask3, diff, 0.0)), 0.0)
            d = jnp.sum(qi[:, None, :] * ki[None, :, :] * e, axis=-1)
            o_i = o_i + jnp.dot(d.astype(BF16), vb[a0:a0 + SUB], preferred_element_type=F32)
            if ib > 0:
                r = bi[0:1]
                qe = (qi * jnp.exp(bi - r)).astype(BF16)
                ke = (k[:a0] * jnp.exp(r - bh[:a0])).astype(BF16)
                att = lax.dot_general(qe, ke, NT, preferred_element_type=F32)
                o_i = o_i + jnp.dot(att.astype(BF16), vb[:a0], preferred_element_type=F32)
            rows.append(o_i)
        o = jnp.concatenate(rows, axis=0)
        bend = bh[C - 1:C]
        kdec = (k * jnp.exp(bend - bh)).astype(BF16)
        dcol = jnp.sum(eye * jnp.exp(bend), axis=1, keepdims=True)
        s_sc[h] = dcol * s + lax.dot_general(kdec, vb, TN, preferred_element_type=F32)
        y = o * lax.rsqrt(jnp.mean(o * o, axis=-1, keepdims=True) + RMS_EPS) * gn_ref[...]
        o_ref[:, h * GLA_DV:(h + 1) * GLA_DV] = (y * jax.nn.silu(gr)).astype(o_ref.dtype)

    @pl.when(c == pl.num_programs(1) - 1)
    def _():
        sout_ref[...] = s_sc[...]


def _gla(gla, small, w2p, ba_row, gn_row, s0):
    b, t, _ = gla.shape
    C = CHUNK
    n_gla = gla.shape[-1]
    return pl.pallas_call(
        functools.partial(_gla_kernel, C=C),
        grid=(b, t // C),
        in_specs=[
            pl.BlockSpec((None, C, n_gla), lambda bi, c: (bi, c, 0)),
            pl.BlockSpec((None, C, LANES), lambda bi, c: (bi, c, 0)),
            pl.BlockSpec(w2p.shape, lambda bi, c: (0, 0)),
            pl.BlockSpec(ba_row.shape, lambda bi, c: (0, 0)),
            pl.BlockSpec(gn_row.shape, lambda bi, c: (0, 0)),
            pl.BlockSpec((None, GLA_H, GLA_DK, GLA_DV), lambda bi, c: (bi, 0, 0, 0)),
        ],
        out_specs=[
            pl.BlockSpec((None, C, MIX_W), lambda bi, c: (bi, c, 0)),
            pl.BlockSpec((None, GLA_H, GLA_DK, GLA_DV), lambda bi, c: (bi, 0, 0, 0)),
        ],
        out_shape=[jax.ShapeDtypeStruct((b, t, MIX_W), BF16), jax.ShapeDtypeStruct(s0.shape, F32)],
        scratch_shapes=[pltpu.VMEM((GLA_H, GLA_DK, GLA_DV), F32)],
        compiler_params=_cp("parallel", "arbitrary"),
        name="gla",
    )(gla, small, w2p, ba_row, gn_row, s0)


def _rope(x, cos, sa, sb):
    half = ROPE_DIM // 2
    return x * cos + pltpu.roll(x, LANES - half, 1) * sa + pltpu.roll(x, half, 1) * sb


def _nsa_prep_kernel(x_ref, cos_ref, sa_ref, sb_ref, q_ref, cmp_ref, slc_ref, win_ref, *mean_ref, tb):
    cos, sa, sb = cos_ref[...], sa_ref[...], sb_ref[...]
    for h in range(NSA_H):
        hs = slice(h * NSA_HD, (h + 1) * NSA_HD)
        q_ref[:, hs] = _rope(x_ref[:, hs], cos, sa, sb) * (NSA_HD ** -0.5)
    base = NSA_H * NSA_HD
    for i, ref in enumerate((cmp_ref, slc_ref, win_ref)):
        k0 = base + i * 2 * NSA_HD
        ref[:, :NSA_HD] = _rope(x_ref[:, k0:k0 + NSA_HD], cos, sa, sb)
        ref[:, NSA_HD:] = x_ref[:, k0 + NSA_HD:k0 + 2 * NSA_HD]
    if mean_ref:
        kv = cmp_ref[...]
        mean_ref[0][...] = jnp.mean(kv.reshape(tb // NSA_BLOCK, NSA_BLOCK, 2 * NSA_HD), axis=1)


def _nsa_prep(nsa, cos, sa, sb, with_means):
    b, t, n = nsa.shape
    tb = min(t, 512)
    tab = pl.BlockSpec((tb, LANES), lambda bi, i: (i, 0))
    kv = pl.BlockSpec((None, tb, 2 * NSA_HD), lambda bi, i: (bi, i, 0))
    out_specs = [pl.BlockSpec((None, tb, MIX_W), lambda bi, i: (bi, i, 0)), kv, kv, kv]
    out_shape = [jax.ShapeDtypeStruct((b, t, MIX_W), F32)] + [jax.ShapeDtypeStruct((b, t, 2 * NSA_HD), F32)] * 3
    if with_means:
        out_specs.append(pl.BlockSpec((None, tb // NSA_BLOCK, 2 * NSA_HD), lambda bi, i: (bi, i, 0)))
        out_shape.append(jax.ShapeDtypeStruct((b, t // NSA_BLOCK, 2 * NSA_HD), F32))
    return pl.pallas_call(
        functools.partial(_nsa_prep_kernel, tb=tb),
        grid=(b, t // tb),
        in_specs=[pl.BlockSpec((None, tb, n), lambda bi, i: (bi, i, 0)), tab, tab, tab],
        out_specs=out_specs,
        out_shape=out_shape,
        compiler_params=_cp("parallel", "parallel"),
        name="nsa_prep",
    )(nsa, cos, sa, sb)


def _masked_softmax(s, mask):
    s = jnp.where(mask, s, -jnp.inf)
    m = jnp.max(s, axis=-1, keepdims=True)
    m = jnp.where(m == -jnp.inf, 0.0, m)
    p = jnp.where(mask, jnp.exp(s - m), 0.0)
    return p / jnp.maximum(jnp.sum(p, axis=-1, keepdims=True), 1e-30)


def _topk_mask(score, blk, n_sel):
    nb = score.shape[-1]
    rank = jnp.zeros(score.shape, jnp.int32)
    for m in range(nb):
        sm = score[:, m:m + 1]
        beats = (sm > score) | ((sm == score) & (blk > m))
        rank = rank + beats.astype(jnp.int32)
    return rank < n_sel


def _online_update(carry, s, mask, v):
    m_prev, l_prev, acc = carry
    s = jnp.where(mask, s, NEG)
    m_new = jnp.maximum(m_prev, jnp.max(s, axis=-1, keepdims=True))
    alpha = jnp.exp(m_prev - m_new)
    p = jnp.where(mask, jnp.exp(s - m_new), 0.0)
    l_new = alpha * l_prev + jnp.sum(p, axis=-1, keepdims=True)
    acc = alpha * acc + jnp.dot(p.astype(BF16), v, preferred_element_type=F32)
    return m_new, l_new, acc


def _nsa_attn_kernel(q_ref, sm_ref, cm_ref, slc_ref, win_ref, o_ref, *, tq, kt, nb):
    qi = pl.program_id(1)
    qpos = qi * tq + _iota((tq, 1), 0)
    blk = _iota((tq, nb), 1)
    cmask = (blk + 1) * NSA_BLOCK <= qpos + 1
    kc, vc = cm_ref[:, :NSA_HD], cm_ref[:, NSA_HD:]
    o_c = []
    imp = jnp.zeros((tq, nb), F32)
    for h in range(NSA_H):
        s = lax.dot_general(q_ref[:, h * NSA_HD:(h + 1) * NSA_HD], kc, NT, precision=HI, preferred_element_type=F32)
        p = _masked_softmax(s, cmask)
        imp = imp + p
        o_c.append(jnp.dot(p, vc, precision=HI, preferred_element_type=F32))
    cur = qpos // NSA_BLOCK
    score = jnp.where(blk == cur, jnp.inf, jnp.where(cmask, imp, -jnp.inf))
    sel = _topk_mask(score, blk, min(NSA_TOPK, nb)) & (blk <= cur)
    sel_b = sel.astype(BF16)
    q_all = jnp.concatenate([q_ref[:, h * NSA_HD:(h + 1) * NSA_HD] for h in range(NSA_H)], axis=0).astype(BF16)
    kcol = _iota((tq, kt), 1)
    blk_of_col = _iota((nb, kt), 1) // NSA_BLOCK - _iota((nb, kt), 0)
    last = (qi * tq + tq - 1) // kt

    def init():
        return (jnp.full((NSA_H * tq, 1), NEG, F32), jnp.zeros((NSA_H * tq, 1), F32),
                jnp.zeros((NSA_H * tq, NSA_HD), F32))

    def slc_body(kb, carry):
        rows = pl.ds(pl.multiple_of(kb * kt, kt), kt)
        k = slc_ref[rows, :NSA_HD].astype(BF16)
        v = slc_ref[rows, NSA_HD:].astype(BF16)
        s = lax.dot_general(q_all, k, NT, preferred_element_type=F32)
        expand = (blk_of_col + kb * (kt // NSA_BLOCK) == 0).astype(BF16)
        tok = jnp.dot(sel_b, expand, preferred_element_type=F32)
        mask = (tok > 0.5) & (kb * kt + kcol <= qpos)
        return _online_update(carry, s, jnp.concatenate([mask] * NSA_H, axis=0), v)

    _, l_s, acc_s = lax.fori_loop(0, last + 1, slc_body, init())

    def win_body(kb, carry):
        rows = pl.ds(pl.multiple_of(kb * kt, kt), kt)
        k = win_ref[rows, :NSA_HD].astype(BF16)
        v = win_ref[rows, NSA_HD:].astype(BF16)
        s = lax.dot_general(q_all, k, NT, preferred_element_type=F32)
        dist = qpos - (kb * kt + kcol)
        mask = (dist >= 0) & (dist < NSA_WINDOW)
        return _online_update(carry, s, jnp.concatenate([mask] * NSA_H, axis=0), v)

    first = jnp.maximum(qi * tq - (NSA_WINDOW - 1), 0) // kt
    _, l_w, acc_w = lax.fori_loop(first, last + 1, win_body, init())
    o_s = acc_s / l_s
    o_w = acc_w / l_w
    gate = jax.nn.sigmoid(sm_ref[...])
    for h in range(NSA_H):
        g0 = SM_NG + 3 * h
        rs = slice(h * tq, (h + 1) * tq)
        o = gate[:, g0:g0 + 1] * o_c[h] + gate[:, g0 + 1:g0 + 2] * o_s[rs] + gate[:, g0 + 2:g0 + 3] * o_w[rs]
        o_ref[:, h * NSA_HD:(h + 1) * NSA_HD] = o.astype(o_ref.dtype)


def _nsa_attn(qr, small, cmean, slc, win):
    b, t, _ = qr.shape
    tq = 128
    kt = min(t, 256)
    nb = t // NSA_BLOCK
    whole = lambda n: pl.BlockSpec((None, n, 2 * NSA_HD), lambda bi, qi: (bi, 0, 0))
    return pl.pallas_call(
        functools.partial(_nsa_attn_kernel, tq=tq, kt=kt, nb=nb),
        grid=(b, t // tq),
        in_specs=[
            pl.BlockSpec((None, tq, MIX_W), lambda bi, qi: (bi, qi, 0)),
            pl.BlockSpec((None, tq, LANES), lambda bi, qi: (bi, qi, 0)),
            whole(nb), whole(t), whole(t),
        ],
        out_specs=pl.BlockSpec((None, tq, MIX_W), lambda bi, qi: (bi, qi, 0)),
        out_shape=jax.ShapeDtypeStruct((b, t, MIX_W), BF16),
        compiler_params=_cp("parallel", "parallel"),
        name="nsa_attn",
    )(qr, small, cmean, slc, win)


def _dn_prep_kernel(x_ref, halo_ref, prev_ref, cw_ref, sm_ref, alog_ref, dtb_ref,
                    q_ref, k_ref, v_ref, g_ref, bt_ref, *, tb):
    halo = jnp.where(pl.program_id(1) == 0, prev_ref[...], halo_ref[...])
    xcat = jnp.concatenate([halo, x_ref[...]], axis=0)
    conv = x_ref[...] * cw_ref[DN_CONV - 1:DN_CONV]
    for j in range(DN_CONV - 1):
        sh = DN_CONV - 1 - j
        conv = conv + pltpu.roll(xcat, sh, 0)[8:] * cw_ref[j:j + 1]
    u = jax.nn.silu(conv)
    for h in range(DN_H):
        hs = slice(h * DN_DK, (h + 1) * DN_DK)
        uq = u[:, hs]
        q_ref[:, hs] = uq * lax.rsqrt(jnp.sum(uq * uq, axis=-1, keepdims=True) + 1e-6) * (DN_DK ** -0.5)
        uk = u[:, DN_H * DN_DK + h * DN_DK:DN_H * DN_DK + (h + 1) * DN_DK]
        k_ref[:, hs] = uk * lax.rsqrt(jnp.sum(uk * uk, axis=-1, keepdims=True) + 1e-6)
    v_ref[...] = u[:, 2 * DN_H * DN_DK:]
    sm = sm_ref[...]
    g_ref[...] = -jnp.exp(alog_ref[...]) * jax.nn.softplus(sm + dtb_ref[...])
    bt_ref[...] = jax.nn.sigmoid(sm)


def _dn_prep(dn, prev8, conv_w, small, alog_row, dtb_row):
    b, t, _ = dn.shape
    tb = min(t, 256)
    hb = tb // 8
    row = pl.BlockSpec((None, tb, MIX_W), lambda bi, i: (bi, i, 0))
    sm = pl.BlockSpec((None, tb, LANES), lambda bi, i: (bi, i, 0))
    one = lambda shape: pl.BlockSpec(shape, lambda bi, i: (0,) * len(shape))
    return pl.pallas_call(
        functools.partial(_dn_prep_kernel, tb=tb),
        grid=(b, t // tb),
        in_specs=[
            pl.BlockSpec((None, tb, DN_QKV), lambda bi, i: (bi, i, 0)),
            pl.BlockSpec((None, 8, DN_QKV), lambda bi, i: (bi, jnp.maximum(i * hb - 1, 0), 0)),
            pl.BlockSpec((None, 8, DN_QKV), lambda bi, i: (bi, 0, 0)),
            one(conv_w.shape), sm, one(alog_row.shape), one(dtb_row.shape),
        ],
        out_specs=[row, row, row, sm, sm],
        out_shape=[jax.ShapeDtypeStruct((b, t, MIX_W), F32)] * 3 + [jax.ShapeDtypeStruct((b, t, LANES), F32)] * 2,
        compiler_params=_cp("parallel", "parallel"),
        name="dn_prep",
    )(dn, dn, prev8, conv_w, small, alog_row, dtb_row)


def _gdn_kernel(q_ref, k_ref, v_ref, g_ref, bt_ref, z_ref, nrm_ref, s0_ref, o_ref, sout_ref, s_sc, *, C):
    c = pl.program_id(1)

    @pl.when(c == 0)
    def _():
        s_sc[...] = s0_ref[...]

    row, col = _iota((C, C), 0), _iota((C, C), 1)
    incl, strict = col <= row, col < row
    eye = (row == col).astype(F32)
    gam_all = jnp.dot(incl.astype(F32), g_ref[...], precision=HI, preferred_element_type=F32)
    lane = _iota((C, LANES), 1)
    n_sq = C.bit_length() - 2
    for h in range(DN_H):
        hs = slice(h * DN_DK, (h + 1) * DN_DK)
        q, k, v = q_ref[:, hs], k_ref[:, hs], v_ref[:, hs]
        gcol = gam_all[:, SM_DA + h:SM_DA + h + 1]
        bcol = bt_ref[:, SM_DB + h:SM_DB + h + 1]
        onehot = (lane == SM_DA + h).astype(F32)
        grow = lax.dot_general(onehot, gam_all, NT, precision=HI, preferred_element_type=F32)
        diff = gcol - grow
        dec_incl = jnp.where(incl, jnp.exp(jnp.where(incl, diff, 0.0)), 0.0)
        dec_strict = jnp.where(strict, dec_incl, 0.0)
        kk = lax.dot_general(k, k, NT, precision=HI, preferred_element_type=F32)
        x = -(bcol * kk * dec_strict)
        p = eye + x
        for _ in range(n_sq):
            x = jnp.dot(x, x, precision=HI, preferred_element_type=F32)
            p = p + jnp.dot(p, x, precision=HI, preferred_element_type=F32)
        rhs = jnp.concatenate([bcol * v, (bcol * jnp.exp(gcol)) * k], axis=1)
        w = jnp.dot(p, rhs, precision=HI, preferred_element_type=F32)
        wv, wk = w[:, :DN_DV], w[:, DN_DV:]
        qk = lax.dot_general(q.astype(BF16), k.astype(BF16), NT, preferred_element_type=F32) * dec_incl
        s = s_sc[h]
        sb = s.astype(BF16)
        u = wv - jnp.dot(wk.astype(BF16), sb, preferred_element_type=F32)
        ub = u.astype(BF16)
        o = (jnp.dot((q * jnp.exp(gcol)).astype(BF16), sb, preferred_element_type=F32)
             + jnp.dot(qk.astype(BF16), ub, preferred_element_type=F32))
        gend = gcol[C - 1:C]
        kdec = (k * jnp.exp(gend - gcol)).astype(BF16)
        s_sc[h] = jnp.exp(gend) * s + lax.dot_general(kdec, ub, TN, preferred_element_type=F32)
        y = o * lax.rsqrt(jnp.mean(o * o, axis=-1, keepdims=True) + RMS_EPS) * nrm_ref[...]
        o_ref[:, hs] = (y * jax.nn.silu(z_ref[:, hs])).astype(o_ref.dtype)

    @pl.when(c == pl.num_programs(1) - 1)
    def _():
        sout_ref[...] = s_sc[...]


def _gdn(qd, kd, vd, gsm, btsm, dn, nrm_row, s0):
    b, t, _ = qd.shape
    C = CHUNK
    row = pl.BlockSpec((None, C, MIX_W), lambda bi, c: (bi, c, 0))
    sm = pl.BlockSpec((None, C, LANES), lambda bi, c: (bi, c, 0))
    st = pl.BlockSpec((None, DN_H, DN_DK, DN_DV), lambda bi, c: (bi, 0, 0, 0))
    return pl.pallas_call(
        functools.partial(_gdn_kernel, C=C),
        grid=(b, t // C),
        in_specs=[row, row, row, sm, sm,
                  pl.BlockSpec((None, C, MIX_W), lambda bi, c: (bi, c, DN_QKV // MIX_W)),
                  pl.BlockSpec(nrm_row.shape, lambda bi, c: (0, 0)), st],
        out_specs=[row, st],
        out_shape=[jax.ShapeDtypeStruct((b, t, MIX_W), BF16), jax.ShapeDtypeStruct(s0.shape, F32)],
        scratch_shapes=[pltpu.VMEM((DN_H, DN_DK, DN_DV), F32)],
        compiler_params=_cp("parallel", "arbitrary"),
        name="gdn",
    )(qd, kd, vd, gsm, btsm, dn, nrm_row, s0)


FOX_PAGES = 8


def _fox_dec_kernel(pt_ref, fx_ref, sm_ref, b_ref, *refs):
    kv_refs, lf_refs = refs[:FOX_PAGES], refs[FOX_PAGES:2 * FOX_PAGES]
    o_ref, lfo_ref, m_sc, l_sc, acc_sc, car_sc = refs[2 * FOX_PAGES:]
    j = pl.program_id(1)
    P = PAGE_SIZE
    rows_per_tok = 2 * FOX_H
    qs = [fx_ref[:, h * FOX_HD:(h + 1) * FOX_HD] * (FOX_HD ** -0.5) for h in range(FOX_H)]

    @pl.when(j == 0)
    def _():
        lf_new = _log_sigmoid(sm_ref[...] + b_ref[...])
        lfo_ref[...] = lf_new
        car_sc[...] = lf_new
        for h in range(FOX_H):
            k_new = fx_ref[:, MIX_W + h * FOX_HD:MIX_W + (h + 1) * FOX_HD]
            m_sc[h] = jnp.sum(qs[h] * k_new, axis=-1, keepdims=True)
            l_sc[h] = jnp.ones((1, 1), F32)
            acc_sc[h] = fx_ref[:, 2 * MIX_W + h * FOX_HD:2 * MIX_W + (h + 1) * FOX_HD]

    upper = (_iota((P, P), 1) > _iota((P, P), 0)).astype(F32)
    carry = car_sc[:, 0:FOX_H]
    m = [m_sc[h] for h in range(FOX_H)]
    l = [l_sc[h] for h in range(FOX_H)]
    acc = [acc_sc[h] for h in range(FOX_H)]
    for kv_ref, lf_ref in zip(kv_refs, lf_refs):
        lf = lf_ref[...]
        bias = jnp.dot(upper, lf, precision=HI, preferred_element_type=F32) + carry
        for h in range(FOX_H):
            kh = kv_ref[pl.ds(h, P, stride=rows_per_tok), :]
            vh = kv_ref[pl.ds(FOX_H + h, P, stride=rows_per_tok), :]
            s = jnp.sum(kh * qs[h], axis=-1, keepdims=True) + bias[:, h:h + 1]
            m_new = jnp.maximum(m[h], jnp.max(s, axis=0, keepdims=True))
            alpha = jnp.exp(m[h] - m_new)
            p = jnp.exp(s - m_new)
            l[h] = alpha * l[h] + jnp.sum(p, axis=0, keepdims=True)
            acc[h] = alpha * acc[h] + jnp.sum(p * vh, axis=0, keepdims=True)
            m[h] = m_new
        carry = carry + jnp.sum(lf, axis=0, keepdims=True)
    car_sc[:, 0:FOX_H] = carry
    for h in range(FOX_H):
        m_sc[h], l_sc[h], acc_sc[h] = m[h], l[h], acc[h]

    @pl.when(j == pl.num_programs(1) - 1)
    def _():
        for h in range(FOX_H):
            o_ref[:, h * FOX_HD:(h + 1) * FOX_HD] = acc[h] / l[h]


def _fox_decode(pt_flat, n_pages, page0, fox_new, small_new, bias_row, kv_cache, lf_cache):
    b = fox_new.shape[0]
    assert n_pages % FOX_PAGES == 0

    def page(r):
        return lambda bi, j, pt: (page0 + pt[bi * n_pages + n_pages - 1 - (j * FOX_PAGES + r)], 0, 0)

    per_b = lambda n: pl.BlockSpec((None, 1, n), lambda bi, j, pt: (bi, 0, 0))
    return pl.pallas_call(
        _fox_dec_kernel,
        grid_spec=pltpu.PrefetchScalarGridSpec(
            num_scalar_prefetch=1,
            grid=(b, n_pages // FOX_PAGES),
            in_specs=[per_b(3 * MIX_W), per_b(LANES), pl.BlockSpec((1, LANES), lambda bi, j, pt: (0, 0))]
            + [pl.BlockSpec((None, PAGE_SIZE * 2 * FOX_H, FOX_HD), page(r)) for r in range(FOX_PAGES)]
            + [pl.BlockSpec((None, PAGE_SIZE, FOX_H), page(r)) for r in range(FOX_PAGES)],
            out_specs=[per_b(MIX_W), per_b(LANES)],
            scratch_shapes=[pltpu.VMEM((FOX_H, 1, 1), F32), pltpu.VMEM((FOX_H, 1, 1), F32),
                            pltpu.VMEM((FOX_H, 1, FOX_HD), F32), pltpu.VMEM((1, LANES), F32)],
        ),
        out_shape=[jax.ShapeDtypeStruct((b, 1, MIX_W), F32), jax.ShapeDtypeStruct((b, 1, LANES), F32)],
        compiler_params=_cp("parallel", "arbitrary"),
        name="fox_decode",
    )(pt_flat, fox_new, small_new, bias_row, *([kv_cache] * FOX_PAGES), *([lf_cache] * FOX_PAGES))


def _col(row, eye):
    return jnp.sum(eye * row, axis=1, keepdims=True)


def _rec_dec_kernel(g_ref, d_ref, sm_ref, w2_ref, ba_ref, gn_ref, cp_ref, cw_ref, alog_ref, dtb_ref, nrm_ref,
                    sg_ref, sd_ref, og_ref, od_ref, sgo_ref, sdo_ref):
    sm = sm_ref[...]
    lane = _iota((8, LANES), 1)
    ga = jnp.where((lane >= SM_GA) & (lane < SM_GA + GLA_RANK), jnp.broadcast_to(sm, (8, LANES)), 0.0)
    pre = jnp.dot(ga, w2_ref[...], precision=HI, preferred_element_type=F32)[0:1] + ba_ref[...]
    loga = _log_sigmoid(pre) * (1.0 / GLA_TAU)
    eye_k = (_iota((GLA_DK, GLA_DK), 0) == _iota((GLA_DK, GLA_DK), 1)).astype(F32)
    for h in range(GLA_H):
        q = g_ref[:, h * GLA_DK:(h + 1) * GLA_DK] * (GLA_DK ** -0.5)
        k = g_ref[:, GLA_H * GLA_DK + h * GLA_DK:GLA_H * GLA_DK + (h + 1) * GLA_DK]
        v0 = 2 * GLA_H * GLA_DK + h * GLA_DV
        v = g_ref[:, v0:v0 + GLA_DV]
        gr = g_ref[:, v0 + GLA_H * GLA_DV:v0 + GLA_H * GLA_DV + GLA_DV]
        ea = jnp.exp(loga[:, h * GLA_DK:(h + 1) * GLA_DK])
        s0 = sg_ref[h]
        o = jnp.sum(q * k, axis=-1, keepdims=True) * v + jnp.sum(_col(q * ea, eye_k) * s0, axis=0, keepdims=True)
        sgo_ref[h] = _col(ea, eye_k) * s0 + _col(k, eye_k) * v
        y = o * lax.rsqrt(jnp.mean(o * o, axis=-1, keepdims=True) + RMS_EPS) * gn_ref[...]
        og_ref[:, h * GLA_DV:(h + 1) * GLA_DV] = y * jax.nn.silu(gr)
    conv = d_ref[:, :DN_QKV] * cw_ref[DN_CONV - 1:DN_CONV]
    for j in range(DN_CONV - 1):
        conv = conv + cp_ref[j:j + 1] * cw_ref[j:j + 1]
    u = jax.nn.silu(conv)
    gall = -jnp.exp(alog_ref[...]) * jax.nn.softplus(sm + dtb_ref[...])
    ball = jax.nn.sigmoid(sm)
    eye_d = (_iota((DN_DK, DN_DK), 0) == _iota((DN_DK, DN_DK), 1)).astype(F32)
    for h in range(DN_H):
        uq = u[:, h * DN_DK:(h + 1) * DN_DK]
        q = uq * lax.rsqrt(jnp.sum(uq * uq, axis=-1, keepdims=True) + 1e-6) * (DN_DK ** -0.5)
        uk = u[:, DN_H * DN_DK + h * DN_DK:DN_H * DN_DK + (h + 1) * DN_DK]
        k = uk * lax.rsqrt(jnp.sum(uk * uk, axis=-1, keepdims=True) + 1e-6)
        v = u[:, 2 * DN_H * DN_DK + h * DN_DV:2 * DN_H * DN_DK + (h + 1) * DN_DV]
        eg = jnp.exp(gall[:, SM_DA + h:SM_DA + h + 1])
        beta = ball[:, SM_DB + h:SM_DB + h + 1]
        s0 = sd_ref[h]
        kcol = _col(k, eye_d)
        ks = jnp.sum(kcol * s0, axis=0, keepdims=True)
        qs = jnp.sum(_col(q, eye_d) * s0, axis=0, keepdims=True)
        un = beta * (v - eg * ks)
        o = eg * qs + jnp.sum(q * k, axis=-1, keepdims=True) * un
        sdo_ref[h] = eg * s0 + kcol * un
        y = o * lax.rsqrt(jnp.mean(o * o, axis=-1, keepdims=True) + RMS_EPS) * nrm_ref[...]
        z = d_ref[:, DN_QKV + h * DN_DV:DN_QKV + (h + 1) * DN_DV]
        od_ref[:, h * DN_DV:(h + 1) * DN_DV] = y * jax.nn.silu(z)


def _rec_decode(gla_new, dn_new, small_new, w2p, ba_row, gn_row, conv_prev, conv_w, alog_row, dtb_row, nrm_row,
                s_gla, s_dn, row0):
    b = gla_new.shape[0]
    per_b = lambda n: pl.BlockSpec((None, 1, n), lambda bi: (bi, 0, 0))
    one = lambda a: pl.BlockSpec(a.shape, lambda bi: (0,) * a.ndim)
    sg = lambda r0: pl.BlockSpec((None, GLA_H, GLA_DK, GLA_DV), lambda bi: (r0 + bi, 0, 0, 0))
    sd = lambda r0: pl.BlockSpec((None, DN_H, DN_DK, DN_DV), lambda bi: (r0 + bi, 0, 0, 0))
    return pl.pallas_call(
        _rec_dec_kernel,
        grid=(b,),
        in_specs=[per_b(gla_new.shape[-1]), per_b(dn_new.shape[-1]), per_b(LANES), one(w2p), one(ba_row), one(gn_row),
                  pl.BlockSpec((None, DN_CONV - 1, DN_QKV), lambda bi: (row0 + bi, 0, 0)), one(conv_w), one(alog_row),
                  one(dtb_row), one(nrm_row), sg(row0), sd(row0)],
        out_specs=[per_b(MIX_W), per_b(MIX_W), sg(0), sd(0)],
        out_shape=[jax.ShapeDtypeStruct((b, 1, MIX_W), F32)] * 2
        + [jax.ShapeDtypeStruct((b,) + s_gla.shape[1:], F32), jax.ShapeDtypeStruct((b,) + s_dn.shape[1:], F32)],
        compiler_params=_cp("parallel"),
        name="rec_decode",
    )(gla_new, dn_new, small_new, w2p, ba_row, gn_row, conv_prev, conv_w, alog_row, dtb_row, nrm_row, s_gla, s_dn)


def _heads_to_rows(q_row):
    rows = [q_row[:, h * NSA_HD:(h + 1) * NSA_HD] for h in range(NSA_H)]
    return jnp.concatenate(rows + [jnp.zeros((8 - NSA_H, NSA_HD), F32)], axis=0)


CMP_PAGES = 16


def _nsa_dec_cmp_kernel(pt_ref, q_ref, *refs, nbp):
    page_refs, (kc_ref, vc_ref, oc_ref, sel_ref) = refs[:CMP_PAGES], refs[CMP_PAGES:]
    j = pl.program_id(1)
    per = PAGE_SIZE // NSA_BLOCK
    rows = CMP_PAGES * per
    dst = pl.ds(pl.multiple_of(j * rows, rows), rows)
    for kv, ref in enumerate((kc_ref, vc_ref)):
        means = [jnp.mean(r[pl.ds(kv, PAGE_SIZE, stride=2), :].reshape(per, NSA_BLOCK, NSA_HD), axis=1)
                 for r in page_refs]
        ref[dst, :] = jnp.concatenate(means, axis=0)

    @pl.when(j == pl.num_programs(1) - 1)
    def _():
        q8 = _heads_to_rows(q_ref[...])
        kc, vc = kc_ref[...], vc_ref[...]
        s = lax.dot_general(q8, kc, NT, precision=HI, preferred_element_type=F32)
        p = _masked_softmax(s, jnp.full(s.shape, True))
        oc_ref[...] = jnp.dot(p, vc, precision=HI, preferred_element_type=F32)
        imp = jnp.sum(p[0:NSA_H], axis=0, keepdims=True)
        r_i, c_i = _iota((nbp, nbp), 0), _iota((nbp, nbp), 1)
        imp_col = _col(imp, (r_i == c_i).astype(F32))
        beats = (imp_col > imp) | ((imp_col == imp) & (r_i < c_i))
        rank = jnp.sum(beats.astype(jnp.int32), axis=0, keepdims=True)
        ids = jnp.where(rank == _iota((16, nbp), 0), _iota((16, nbp), 1), 0)
        sel_ref[...] = jnp.broadcast_to(jnp.sum(ids, axis=1, keepdims=True), (16, LANES))


def _nsa_dec_cmp(pt_flat, n_pages, page0, q_new, cmp_cache):
    b = q_new.shape[0]
    nbp = n_pages * (PAGE_SIZE // NSA_BLOCK)
    n_sel = NSA_TOPK - 1
    assert nbp >= n_sel and NSA_TOPK <= 16 and n_pages % CMP_PAGES == 0
    page = lambda r: pl.BlockSpec((None, 2 * PAGE_SIZE, NSA_HD),
                                  lambda bi, j, pt: (page0 + pt[bi * n_pages + j * CMP_PAGES + r], 0, 0))
    mean_spec = pl.BlockSpec((None, nbp, NSA_HD), lambda bi, j, pt: (bi, 0, 0))
    return pl.pallas_call(
        functools.partial(_nsa_dec_cmp_kernel, nbp=nbp),
        grid_spec=pltpu.PrefetchScalarGridSpec(
            num_scalar_prefetch=1,
            grid=(b, n_pages // CMP_PAGES),
            in_specs=[pl.BlockSpec((None, 1, MIX_W), lambda bi, j, pt: (bi, 0, 0))]
            + [page(r) for r in range(CMP_PAGES)],
            out_specs=[mean_spec, mean_spec,
                       pl.BlockSpec((None, 8, NSA_HD), lambda bi, j, pt: (bi, 0, 0)),
                       pl.BlockSpec((None, 16, LANES), lambda bi, j, pt: (bi, 0, 0))],
        ),
        out_shape=[jax.ShapeDtypeStruct((b, nbp, NSA_HD), F32)] * 2
        + [jax.ShapeDtypeStruct((b, 8, NSA_HD), F32), jax.ShapeDtypeStruct((b, 16, LANES), jnp.int32)],
        compiler_params=_cp("parallel", "arbitrary"),
        name="nsa_dec_cmp",
    )(pt_flat, q_new, *([cmp_cache] * CMP_PAGES))


def _nsa_dec_attn_kernel(pt_ref, sel_ref, q_ref, blk_ref, snew_ref, win_ref, wnew_ref, oc_ref, sm_ref, o_ref,
                         m_sc, l_sc, acc_sc, *, sw):
    s_i = pl.program_id(1)
    q8 = _heads_to_rows(q_ref[...])

    @pl.when(s_i == 0)
    def _():
        m_sc[...] = jnp.sum(q8 * snew_ref[:, :NSA_HD], axis=-1, keepdims=True)
        l_sc[...] = jnp.ones_like(l_sc)
        acc_sc[...] = jnp.broadcast_to(snew_ref[:, NSA_HD:], acc_sc.shape)

    k, v = blk_ref[pl.ds(0, NSA_BLOCK, stride=2), :], blk_ref[pl.ds(1, NSA_BLOCK, stride=2), :]
    s = lax.dot_general(q8, k, NT, precision=HI, preferred_element_type=F32)
    m_prev = m_sc[...]
    m_new = jnp.maximum(m_prev, jnp.max(s, axis=-1, keepdims=True))
    alpha = jnp.exp(m_prev - m_new)
    p = jnp.exp(s - m_new)
    l_sc[...] = alpha * l_sc[...] + jnp.sum(p, axis=-1, keepdims=True)
    acc_sc[...] = alpha * acc_sc[...] + jnp.dot(p, v, precision=HI, preferred_element_type=F32)
    m_sc[...] = m_new

    @pl.when(s_i == pl.num_programs(1) - 1)
    def _():
        o_s = acc_sc[...] / l_sc[...]
        kw, vw = win_ref[pl.ds(0, sw, stride=2), :], win_ref[pl.ds(1, sw, stride=2), :]
        sw_ = lax.dot_general(q8, kw, NT, precision=HI, preferred_element_type=F32)
        wmask = (sw - _iota((8, sw), 1)) < NSA_WINDOW
        s_new = jnp.sum(q8 * wnew_ref[:, :NSA_HD], axis=-1, keepdims=True)
        m = jnp.maximum(jnp.max(jnp.where(wmask, sw_, -jnp.inf), axis=-1, keepdims=True), s_new)
        pw = jnp.where(wmask, jnp.exp(sw_ - m), 0.0)
        pn = jnp.exp(s_new - m)
        o_w = (jnp.dot(pw, vw, precision=HI, preferred_element_type=F32) + pn * wnew_ref[:, NSA_HD:]) / (
            jnp.sum(pw, axis=-1, keepdims=True) + pn)
        gate = jax.nn.sigmoid(sm_ref[...])
        o_c = oc_ref[...]
        for h in range(NSA_H):
            g0 = SM_NG + 3 * h
            o_ref[:, h * NSA_HD:(h + 1) * NSA_HD] = (
                gate[:, g0:g0 + 1] * o_c[h:h + 1] + gate[:, g0 + 1:g0 + 2] * o_s[h:h + 1]
                + gate[:, g0 + 2:g0 + 3] * o_w[h:h + 1])


def _nsa_dec_attn(pt_flat, sel_flat, n_pages, n_sel, page0, row0, q_new, slc_cache, slc_new, win_cache, win_new,
                  o_c, small_new):
    b = q_new.shape[0]
    sw = win_cache.shape[1] // 2
    per = PAGE_SIZE // NSA_BLOCK
    half = slc_cache.reshape(slc_cache.shape[0] * per, 2 * NSA_BLOCK, NSA_HD)

    def blk_map(bi, s, pt, sel):
        n = sel[bi * n_sel + s]
        return ((page0 + pt[bi * n_pages + n // per]) * per + n % per, 0, 0)

    per_b = lambda r, n: pl.BlockSpec((None, r, n), lambda bi, s, pt, sel: (bi, 0, 0))
    return pl.pallas_call(
        functools.partial(_nsa_dec_attn_kernel, sw=sw),
        grid_spec=pltpu.PrefetchScalarGridSpec(
            num_scalar_prefetch=2,
            grid=(b, n_sel),
            in_specs=[per_b(1, MIX_W), pl.BlockSpec((None, 2 * NSA_BLOCK, NSA_HD), blk_map), per_b(1, 2 * NSA_HD),
                      pl.BlockSpec((None, 2 * sw, NSA_HD), lambda bi, s, pt, sel: (row0 + bi, 0, 0)),
                      per_b(1, 2 * NSA_HD), per_b(8, NSA_HD), per_b(1, LANES)],
            out_specs=per_b(1, MIX_W),
            scratch_shapes=[pltpu.VMEM((8, 1), F32), pltpu.VMEM((8, 1), F32), pltpu.VMEM((8, NSA_HD), F32)],
        ),
        out_shape=jax.ShapeDtypeStruct((b, 1, MIX_W), F32),
        compiler_params=_cp("parallel", "arbitrary"),
        name="nsa_dec_attn",
    )(pt_flat, sel_flat, q_new, half, slc_new, win_cache, win_new, o_c, small_new)


def _lane_row(vals, at):
    return jnp.zeros((1, LANES), F32).at[0, at:at + vals.shape[0]].set(vals.astype(F32))


def _layer_weights(w_in, fox_b_f, gla_w_a2, gla_b_a, gla_norm, dn_conv_w, dn_a_log, dn_dt_bias, dn_norm,
                   w_branch, w_out, w_up, w_down):
    offs = [0]
    for s in SPLIT_SIZES:
        offs.append(offs[-1] + s)
    seg = lambda i: w_in[:, offs[i]:offs[i + 1]]
    (fq, fk, fv, ff, gq, gk, gv, ga, gr, nq, nkc, nks, nkw, ng, dqkv, da, dbeta, dz, mg) = [seg(i) for i in range(19)]
    cat = lambda parts: jnp.concatenate(parts, axis=1).astype(BF16)
    small = [ff, ga, ng, da, dbeta]
    pad = LANES - sum(p.shape[1] for p in small)
    w = dict(
        fox=cat([fq, fk, fv]), gla=cat([gq, gk, gv, gr]), nsa=cat([nq, nkc, nks, nkw]), dn=cat([dqkv, dz]),
        small=cat(small + [jnp.zeros((D_MODEL, pad), F32)]), mg=mg.astype(BF16),
        bf_row=_lane_row(fox_b_f, SM_FF),
        w2p=jnp.zeros((LANES, GLA_H * GLA_DK), F32).at[SM_GA:SM_GA + GLA_RANK].set(gla_w_a2),
        ba_row=gla_b_a.reshape(1, -1), gn_row=gla_norm.reshape(1, -1), conv_w=dn_conv_w,
        alog_row=_lane_row(dn_a_log, SM_DA), dtb_row=_lane_row(dn_dt_bias, SM_DA), dnn_row=dn_norm.reshape(1, -1),
        wb=w_branch.astype(BF16), w_out=w_out.astype(BF16), w_up=w_up.astype(BF16), w_down=w_down.astype(BF16),
    )
    return w


def _rope_tables(pos):
    half = ROPE_DIM // 2
    inv = ROPE_THETA ** (-jnp.arange(half, dtype=F32) / half)
    ang = pos.astype(F32)[:, None] * inv[None, :]
    cos, sin = jnp.cos(ang), jnp.sin(ang)
    t = pos.shape[0]
    z = lambda n: jnp.zeros((t, n), F32)
    return (jnp.concatenate([cos, cos, jnp.ones((t, LANES - ROPE_DIM), F32)], axis=1),
            jnp.concatenate([-sin, z(LANES - half)], axis=1),
            jnp.concatenate([z(half), sin, z(LANES - ROPE_DIM)], axis=1))


def _project(h, w):
    return {name: _mm(h, w[name]) for name in ("fox", "gla", "nsa", "dn", "small", "mg")}


def _mixer_prompt(h, b, t, w):
    p = _project(h, w)
    fox, gla, nsa, dn, small = (p[n].reshape(b, t, -1) for n in ("fox", "gla", "nsa", "dn", "small"))
    lf, c, ct = _fox_prep(small, w["bf_row"])
    o_fox = _fox_attn(fox, c, ct)
    o_gla, s_gla = _gla(gla, small, w["w2p"], w["ba_row"], w["gn_row"], jnp.zeros((b, GLA_H, GLA_DK, GLA_DV), F32))
    qr, cmp_kv, slc_kv, win_kv, cmean = _nsa_prep(nsa, *_rope_tables(jnp.arange(t)), True)
    o_nsa = _nsa_attn(qr, small, cmean, slc_kv, win_kv)
    qd, kd, vd, gsm, btsm = _dn_prep(dn, jnp.zeros((b, 8, DN_QKV), F32), w["conv_w"], small, w["alog_row"],
                                     w["dtb_row"])
    o_dn, s_dn = _gdn(qd, kd, vd, gsm, btsm, dn, w["dnn_row"], jnp.zeros((b, DN_H, DN_DK, DN_DV), F32))
    mix = _branch_gate([o.reshape(b * t, MIX_W) for o in (o_fox, o_gla, o_nsa, o_dn)], w["wb"], p["mg"])
    kv5 = lambda a: a.reshape(b, t, 2, 1, NSA_HD)
    wl = min(NSA_WINDOW, t)
    state = (fox[:, :, MIX_W:].reshape(b, t, 2, FOX_H, FOX_HD), lf[:, :, :FOX_H], kv5(cmp_kv), kv5(slc_kv),
             kv5(win_kv)[:, t - wl:], s_gla, s_dn, dn[:, t - (DN_CONV - 1):, :DN_QKV])
    return mix, state


def _mixer_sample(h, caches, l, pt_flat, n_pages, w):
    fox_kv_c, fox_lf_c, cmp_c, slc_c, win_c, s_gla_c, s_dn_c, conv_c = caches
    b = h.shape[0]
    past_len = n_pages * PAGE_SIZE
    p = _project(h, w)
    fox, gla, dn, small = (p[n].reshape(b, 1, -1) for n in ("fox", "gla", "dn", "small"))
    depth, n_pool = fox_kv_c.shape[:2]
    page0, row0 = l * n_pool, l * b
    o_fox, lf_new = _fox_decode(pt_flat, n_pages, page0, fox, small, w["bf_row"],
                                fox_kv_c.reshape(depth * n_pool, PAGE_SIZE * 2 * FOX_H, FOX_HD),
                                fox_lf_c.reshape(depth * n_pool, PAGE_SIZE, FOX_H))
    o_gla, o_dn, s_gla, s_dn = _rec_decode(
        gla, dn, small, w["w2p"], w["ba_row"], w["gn_row"], conv_c.reshape((depth * b,) + conv_c.shape[2:]),
        w["conv_w"], w["alog_row"], w["dtb_row"], w["dnn_row"], s_gla_c.reshape((depth * b,) + s_gla_c.shape[2:]),
        s_dn_c.reshape((depth * b,) + s_dn_c.shape[2:]), row0)
    tabs = _rope_tables(jnp.full((b,), past_len, jnp.int32))
    qr, cmp_new, slc_new, win_new = _nsa_prep(p["nsa"].reshape(1, b, -1), *tabs, False)
    qr, cmp_new, slc_new, win_new = (a.reshape(b, 1, -1) for a in (qr, cmp_new, slc_new, win_new))
    paged = lambda c: c.reshape(depth * n_pool, 2 * PAGE_SIZE, NSA_HD)
    _, _, o_c, sel = _nsa_dec_cmp(pt_flat, n_pages, page0, qr, paged(cmp_c))
    n_sel = NSA_TOPK - 1
    sel_flat = sel[:, :n_sel, 0].reshape(-1)
    sw = win_c.shape[2]
    o_nsa = _nsa_dec_attn(pt_flat, sel_flat, n_pages, n_sel, page0, row0, qr, paged(slc_c), slc_new,
                          win_c.reshape(depth * b, 2 * sw, NSA_HD), win_new, o_c, small)
    mix = _branch_gate([o.reshape(b, MIX_W).astype(BF16) for o in (o_fox, o_gla, o_nsa, o_dn)], w["wb"], p["mg"])
    kv5 = lambda a: a.reshape(b, 1, 2, 1, NSA_HD)
    wl = min(NSA_WINDOW, sw + 1)
    new_win = jnp.concatenate([win_c[l][:, sw + 1 - wl:], kv5(win_new)], axis=1)
    new_conv = jnp.concatenate([conv_c[l][:, 1:], dn[:, :, :DN_QKV]], axis=1)
    state = (fox[:, :, MIX_W:].reshape(b, 1, 2, FOX_H, FOX_HD), lf_new[:, :, :FOX_H], kv5(cmp_new), kv5(slc_new),
             new_win, s_gla, s_dn, new_conv)
    return mix, state


def _trunk_layer(x, mixer, norms, w):
    g_pre_mix, g_post_mix, g_pre_mlp, g_post_mlp = norms
    mix, state = mixer(_rmsnorm_cast(x, g_pre_mix))
    x = _mm_norm_res(mix, w["w_out"], g_post_mix, x)
    hid = _mm(_rmsnorm_cast(x, g_pre_mlp), w["w_up"], BF16, "relu2")
    x = _mm_norm_res(hid, w["w_down"], g_post_mlp, x)
    return x, state


def kernel(x_prompt, x_sample, cache_fox_kv, cache_fox_logf, cache_nsa_cmp_kv, cache_nsa_slc_kv, cache_nsa_win_kv,
           state_gla, state_dn, state_dn_conv, page_table, norm_pre_mix, norm_post_mix, norm_pre_mlp, norm_post_mlp,
           w_in, fox_b_f, gla_w_a2, gla_b_a, gla_norm, dn_conv_w, dn_a_log, dn_dt_bias, dn_norm, w_branch, w_out,
           w_up, w_down):
    bp, tp, _ = x_prompt.shape
    bs, ts, _ = x_sample.shape
    assert ts == 1
    n_pages = page_table.shape[1]
    pt_flat = page_table.reshape(-1).astype(jnp.int32)
    y_p = x_prompt.reshape(bp * tp, D_MODEL)
    y_s = x_sample.reshape(bs, D_MODEL)
    new_p = [[] for _ in range(8)]
    new_s = [[] for _ in range(8)]
    caches = (cache_fox_kv, cache_fox_logf, cache_nsa_cmp_kv, cache_nsa_slc_kv, cache_nsa_win_kv, state_gla, state_dn,
              state_dn_conv)
    for l in range(DEPTH):
        w = _layer_weights(w_in[l], fox_b_f[l], gla_w_a2[l], gla_b_a[l], gla_norm[l], dn_conv_w[l], dn_a_log[l],
                           dn_dt_bias[l], dn_norm[l], w_branch[l], w_out[l], w_up[l], w_down[l])
        norms = (norm_pre_mix[l], norm_post_mix[l], norm_pre_mlp[l], norm_post_mlp[l])
        y_p, st_p = _trunk_layer(y_p, lambda h: _mixer_prompt(h, bp, tp, w), norms, w)
        y_s, st_s = _trunk_layer(y_s, lambda h: _mixer_sample(h, caches, l, pt_flat, n_pages, w), norms, w)
        for i in range(8):
            new_p[i].append(st_p[i])
            new_s[i].append(st_s[i])
    fox_kv_p, fox_logf_p, cmp_kv_p, slc_kv_p, win_kv_p, gla_p, dn_p, conv_p = [jnp.stack(a) for a in new_p]
    fox_kv_s, fox_logf_s, cmp_kv_s, slc_kv_s, win_kv_s, gla_s, dn_s, conv_s = [jnp.stack(a) for a in new_s]
    return (y_p.reshape(bp, tp, D_MODEL), y_s.reshape(bs, ts, D_MODEL), fox_kv_p, fox_kv_s, fox_logf_p, fox_logf_s,
            cmp_kv_p, cmp_kv_s, slc_kv_p, slc_kv_s, win_kv_p, win_kv_s, gla_p, gla_s, dn_p, dn_s, conv_p, conv_s)
```

```python
import functools

import jax
import jax.numpy as jnp
from jax import lax
from jax.experimental import pallas as pl
from jax.experimental.pallas import tpu as pltpu

F32 = jnp.float32
BF16 = jnp.bfloat16
HI = lax.Precision.HIGHEST
NT = (((1,), (1,)), ((), ()))
TN = (((0,), (0,)), ((), ()))
NEG = -1e30

D_MODEL = 2048
DEPTH = 2
PAGE_SIZE = 128
N_BRANCH = 4
MIX_W = D_MODEL // 4
FOX_H = 4
FOX_HD = MIX_W // FOX_H
GLA_H = 4
GLA_DK = MIX_W // (2 * GLA_H)
GLA_DV = MIX_W // GLA_H
GLA_RANK = 16
GLA_TAU = 16.0
NSA_H = 4
NSA_HD = MIX_W // NSA_H
NSA_BLOCK = 64
NSA_TOPK = 16
NSA_WINDOW = 512
DN_H = 4
DN_DK = MIX_W // DN_H
DN_DV = MIX_W // DN_H
DN_CONV = 4
DN_QKV = DN_H * (2 * DN_DK + DN_DV)
ROPE_DIM = NSA_HD // 4
ROPE_THETA = 500000.0
D_FF = 4 * D_MODEL
CHUNK = 64
SUB = 16
RMS_EPS = 1e-6
LANES = 128
SPLIT_SIZES = (
    FOX_H * FOX_HD, FOX_H * FOX_HD, FOX_H * FOX_HD, FOX_H,
    GLA_H * GLA_DK, GLA_H * GLA_DK, GLA_H * GLA_DV, GLA_RANK, GLA_H * GLA_DV,
    NSA_H * NSA_HD, 2 * NSA_HD, 2 * NSA_HD, 2 * NSA_HD, 3 * NSA_H,
    DN_QKV, DN_H, DN_H, DN_H * DN_DV,
    N_BRANCH * D_MODEL,
)
SM_FF, SM_GA, SM_NG, SM_DA, SM_DB = 0, 4, 20, 32, 36
VMEM_LIMIT = 56 * 1024 * 1024


def _cp(*sem):
    return pltpu.CompilerParams(dimension_semantics=sem, vmem_limit_bytes=VMEM_LIMIT)


def _pick(n, cap):
    if n <= cap:
        return n
    best = None
    for t in range(LANES, cap + 1, LANES):
        if n % t == 0:
            best = t
    assert best is not None, n
    return best


def _log_sigmoid(z):
    return jnp.minimum(z, 0.0) - jnp.log1p(jnp.exp(-jnp.abs(z)))


def _iota(shape, dim):
    return lax.broadcasted_iota(jnp.int32, shape, dim)


def _dot3(a, b, dims=(((1,), (0,)), ((), ()))):
    ah, bh = a.astype(BF16), b.astype(BF16)
    al, bl = (a - ah.astype(F32)).astype(BF16), (b - bh.astype(F32)).astype(BF16)
    f = lambda x, y: lax.dot_general(x, y, dims, preferred_element_type=F32)
    return f(ah, bh) + (f(ah, bl) + f(al, bh))


def _rmsnorm_cast_kernel(x_ref, g_ref, o_ref):
    x = x_ref[...]
    y = x * lax.rsqrt(jnp.mean(x * x, axis=-1, keepdims=True) + RMS_EPS)
    o_ref[...] = (y * g_ref[...]).astype(o_ref.dtype)


def _rmsnorm_cast(x, g):
    m, d = x.shape
    tm = min(m, 512)
    return pl.pallas_call(
        _rmsnorm_cast_kernel,
        grid=(m // tm,),
        in_specs=[pl.BlockSpec((tm, d), lambda i: (i, 0)), pl.BlockSpec((1, d), lambda i: (0, 0))],
        out_specs=pl.BlockSpec((tm, d), lambda i: (i, 0)),
        out_shape=jax.ShapeDtypeStruct((m, d), BF16),
        compiler_params=_cp("parallel"),
        name="rmsnorm_cast",
    )(x, g.reshape(1, d))


def _mm_kernel(a_ref, w_ref, o_ref, *, act):
    y = jnp.dot(a_ref[...], w_ref[...], preferred_element_type=F32)
    if act == "relu2":
        y = jnp.square(jnp.maximum(y, 0.0))
    o_ref[...] = y.astype(o_ref.dtype)


def _mm(a, w, out_dtype=F32, act=None):
    m, k = a.shape
    n = w.shape[1]
    tm = min(m, 512)
    tn = _pick(n, 1024)
    return pl.pallas_call(
        functools.partial(_mm_kernel, act=act),
        grid=(n // tn, m // tm),
        in_specs=[pl.BlockSpec((tm, k), lambda j, i: (i, 0)), pl.BlockSpec((k, tn), lambda j, i: (0, j))],
        out_specs=pl.BlockSpec((tm, tn), lambda j, i: (i, j)),
        out_shape=jax.ShapeDtypeStruct((m, n), out_dtype),
        compiler_params=_cp("parallel", "parallel"),
        name="mm",
    )(a, w)


def _mm_norm_res_kernel(a_ref, w_ref, g_ref, x_ref, o_ref, acc_ref):
    k = pl.program_id(1)

    @pl.when(k == 0)
    def _():
        acc_ref[...] = jnp.zeros_like(acc_ref)

    acc_ref[...] += jnp.dot(a_ref[...], w_ref[...], preferred_element_type=F32)

    @pl.when(k == pl.num_programs(1) - 1)
    def _():
        y = acc_ref[...]
        y = y * lax.rsqrt(jnp.mean(y * y, axis=-1, keepdims=True) + RMS_EPS)
        o_ref[...] = x_ref[...] + y * g_ref[...]


def _mm_norm_res(a, w, g, x):
    m, k = a.shape
    n = w.shape[1]
    tm = min(m, 512)
    tk = _pick(k, 512)
    return pl.pallas_call(
        _mm_norm_res_kernel,
        grid=(m // tm, k // tk),
        in_specs=[
            pl.BlockSpec((tm, tk), lambda i, kk: (i, kk)),
            pl.BlockSpec((tk, n), lambda i, kk: (kk, 0)),
            pl.BlockSpec((1, n), lambda i, kk: (0, 0)),
            pl.BlockSpec((tm, n), lambda i, kk: (i, 0)),
        ],
        out_specs=pl.BlockSpec((tm, n), lambda i, kk: (i, 0)),
        out_shape=jax.ShapeDtypeStruct((m, n), F32),
        scratch_shapes=[pltpu.VMEM((tm, n), F32)],
        compiler_params=_cp("parallel", "arbitrary"),
        name="mm_norm_res",
    )(a, w, g.reshape(1, n), x)


def _branch_gate_kernel(o0, o1, o2, o3, wb_ref, m0, m1, m2, m3, out_ref):
    acc = None
    for n, (o_n, m_n) in enumerate(((o0, m0), (o1, m1), (o2, m2), (o3, m3))):
        y = jnp.dot(o_n[...], wb_ref[n], preferred_element_type=F32)
        t = jax.nn.sigmoid(m_n[...]) * y
        acc = t if acc is None else acc + t
    out_ref[...] = acc.astype(out_ref.dtype)


def _branch_gate(branches, wb, mg):
    m = mg.shape[0]
    tm = min(m, 512)
    tn = 1024
    nj = D_MODEL // tn
    o_spec = pl.BlockSpec((tm, MIX_W), lambda j, i: (i, 0))
    m_specs = [pl.BlockSpec((tm, tn), functools.partial(lambda j, i, n: (i, n * nj + j), n=n)) for n in range(N_BRANCH)]
    return pl.pallas_call(
        _branch_gate_kernel,
        grid=(nj, m // tm),
        in_specs=[o_spec] * 4 + [pl.BlockSpec((N_BRANCH, MIX_W, tn), lambda j, i: (0, 0, j))] + m_specs,
        out_specs=pl.BlockSpec((tm, tn), lambda j, i: (i, j)),
        out_shape=jax.ShapeDtypeStruct((m, D_MODEL), BF16),
        compiler_params=_cp("parallel", "parallel"),
        name="branch_gate",
    )(*branches, wb, mg, mg, mg, mg)


def _fox_prep_kernel(s_ref, b_ref, lf_ref, c_ref, ct_ref, carry_ref, *, tb):
    @pl.when(pl.program_id(1) == 0)
    def _():
        carry_ref[...] = jnp.zeros_like(carry_ref)

    lf = _log_sigmoid(s_ref[...] + b_ref[...])
    tri = (_iota((tb, tb), 1) <= _iota((tb, tb), 0)).astype(F32)
    c = jnp.dot(tri, lf, precision=HI, preferred_element_type=F32) + carry_ref[...]
    lf_ref[...] = lf
    c_ref[...] = c
    ct_ref[...] = c.T[:8]
    carry_ref[...] = c[tb - 1:tb]


def _fox_prep(small, bias_row):
    b, t, _ = small.shape
    tb = min(t, 256)
    blk = pl.BlockSpec((None, tb, LANES), lambda bi, i: (bi, i, 0))
    return pl.pallas_call(
        functools.partial(_fox_prep_kernel, tb=tb),
        grid=(b, t // tb),
        in_specs=[blk, pl.BlockSpec((1, LANES), lambda bi, i: (0, 0))],
        out_specs=[blk, blk, pl.BlockSpec((None, 8, tb), lambda bi, i: (bi, 0, i))],
        out_shape=[jax.ShapeDtypeStruct((b, t, LANES), F32)] * 2 + [jax.ShapeDtypeStruct((b, 8, t), F32)],
        scratch_shapes=[pltpu.VMEM((1, LANES), F32)],
        compiler_params=_cp("parallel", "arbitrary"),
        name="fox_prep",
    )(small, bias_row)


def _fox_attn_kernel(q_ref, k_ref, v_ref, cq_ref, ck_ref, o_ref, m_sc, l_sc, acc_sc, *, tq, tk):
    qi = pl.program_id(1)
    ki = pl.program_id(2)

    @pl.when(ki == 0)
    def _():
        m_sc[...] = jnp.full_like(m_sc, NEG)
        l_sc[...] = jnp.zeros_like(l_sc)
        acc_sc[...] = jnp.zeros_like(acc_sc)

    def step(diagonal):
        mask = _iota((tq, tk), 1) <= _iota((tq, tk), 0)
        heads = range(FOX_H)
        hsl = [slice(h * FOX_HD, (h + 1) * FOX_HD) for h in heads]
        ss = []
        for h in heads:
            q = (q_ref[:, hsl[h]] * (FOX_HD ** -0.5)).astype(BF16)
            s = lax.dot_general(q, k_ref[:, hsl[h]].astype(BF16), NT, preferred_element_type=F32)
            s = s + cq_ref[:, h:h + 1] - ck_ref[h:h + 1, :]
            ss.append(jnp.where(mask, s, NEG) if diagonal else s)
        m_prev = [m_sc[h] for h in heads]
        m_new = [jnp.maximum(m_prev[h], jnp.max(ss[h], axis=-1, keepdims=True)) for h in heads]
        ps = []
        for h in heads:
            p = jnp.exp(ss[h] - m_new[h])
            ps.append(jnp.where(mask, p, 0.0) if diagonal else p)
        pvs = [jnp.dot(ps[h].astype(BF16), v_ref[:, hsl[h]].astype(BF16), preferred_element_type=F32) for h in heads]
        for h in heads:
            alpha = jnp.exp(m_prev[h] - m_new[h])
            l_sc[h] = alpha * l_sc[h] + jnp.sum(ps[h], axis=-1, keepdims=True)
            acc_sc[h] = alpha * acc_sc[h] + pvs[h]
            m_sc[h] = m_new[h]

    pl.when(ki < qi)(functools.partial(step, False))
    pl.when(ki == qi)(functools.partial(step, True))

    @pl.when(ki == pl.num_programs(2) - 1)
    def _():
        for h in range(FOX_H):
            o_ref[:, h * FOX_HD:(h + 1) * FOX_HD] = (acc_sc[h] / l_sc[h]).astype(o_ref.dtype)


def _fox_attn(fox, c, ct):
    b, t, _ = fox.shape
    tq = tk = min(t, 512)

    def kmap(col):
        return lambda bi, qi, ki: (bi, jnp.minimum(ki, (qi * tq + tq - 1) // tk), col)

    return pl.pallas_call(
        functools.partial(_fox_attn_kernel, tq=tq, tk=tk),
        grid=(b, t // tq, t // tk),
        in_specs=[
            pl.BlockSpec((None, tq, MIX_W), lambda bi, qi, ki: (bi, qi, 0)),
            pl.BlockSpec((None, tk, MIX_W), kmap(1)),
            pl.BlockSpec((None, tk, MIX_W), kmap(2)),
            pl.BlockSpec((None, tq, LANES), lambda bi, qi, ki: (bi, qi, 0)),
            pl.BlockSpec((None, 8, tk), lambda bi, qi, ki: (bi, 0, jnp.minimum(ki, (qi * tq + tq - 1) // tk))),
        ],
        out_specs=pl.BlockSpec((None, tq, MIX_W), lambda bi, qi, ki: (bi, qi, 0)),
        out_shape=jax.ShapeDtypeStruct((b, t, MIX_W), BF16),
        scratch_shapes=[pltpu.VMEM((FOX_H, tq, 1), F32), pltpu.VMEM((FOX_H, tq, 1), F32),
                        pltpu.VMEM((FOX_H, tq, FOX_HD), F32)],
        compiler_params=_cp("parallel", "parallel", "arbitrary"),
        name="fox_attn",
    )(fox, fox, fox, c, ct)


def _gla_kernel(g_ref, sm_ref, w2_ref, ba_ref, gn_ref, s0_ref, o_ref, sout_ref, s_sc, *, C):
    c = pl.program_id(1)

    @pl.when(c == 0)
    def _():
        s_sc[...] = s0_ref[...]

    lane = _iota((C, LANES), 1)
    ga = jnp.where((lane >= SM_GA) & (lane < SM_GA + GLA_RANK), sm_ref[...], 0.0)
    pre = jnp.dot(ga, w2_ref[...], precision=HI, preferred_element_type=F32) + ba_ref[...]
    loga = _log_sigmoid(pre) * (1.0 / GLA_TAU)
    tri = (_iota((C, C), 1) <= _iota((C, C), 0)).astype(F32)
    b_all = jnp.dot(tri, loga, precision=HI, preferred_element_type=F32)
    mask3 = _iota((SUB, SUB, GLA_DK), 1) <= _iota((SUB, SUB, GLA_DK), 0)
    eye = (_iota((GLA_DK, GLA_DK), 0) == _iota((GLA_DK, GLA_DK), 1)).astype(F32)
    for h in range(GLA_H):
        q = g_ref[:, h * GLA_DK:(h + 1) * GLA_DK] * (GLA_DK ** -0.5)
        k = g_ref[:, GLA_H * GLA_DK + h * GLA_DK:GLA_H * GLA_DK + (h + 1) * GLA_DK]
        v0 = 2 * GLA_H * GLA_DK + h * GLA_DV
        vb = g_ref[:, v0:v0 + GLA_DV].astype(BF16)
        r0 = v0 + GLA_H * GLA_DV
        gr = g_ref[:, r0:r0 + GLA_DV]
        bh = b_all[:, h * GLA_DK:(h + 1) * GLA_DK]
        s = s_sc[h]
        sb = s.astype(BF16)
        rows = []
        for ib in range(C // SUB):
            a0 = ib * SUB
            bi, qi, ki = bh[a0:a0 + SUB], q[a0:a0 + SUB], k[a0:a0 + SUB]
            o_i = jnp.dot((qi * jnp.exp(bi)).astype(BF16), sb, preferred_element_type=F32)
            diff = bi[:, None, :] - bi[None, :, :]
            e = jnp.where(mask3, jnp.exp(jnp.where(mask3, diff, 0.0)), 0.0)
            d = jnp.sum(qi[:, None, :] * ki[None, :, :] * e, axis=-1)
            o_i = o_i + jnp.dot(d.astype(BF16), vb[a0:a0 + SUB], preferred_element_type=F32)
            if ib > 0:
                r = bi[0:1]
                qe = (qi * jnp.exp(bi - r)).astype(BF16)
                ke = (k[:a0] * jnp.exp(r - bh[:a0])).astype(BF16)
                att = lax.dot_general(qe, ke, NT, preferred_element_type=F32)
                o_i = o_i + jnp.dot(att.astype(BF16), vb[:a0], preferred_element_type=F32)
            rows.append(o_i)
        o = jnp.concatenate(rows, axis=0)
        bend = bh[C - 1:C]
        kdec = (k * jnp.exp(bend - bh)).astype(BF16)
        dcol = jnp.sum(eye * jnp.exp(bend), axis=1, keepdims=True)
        s_sc[h] = dcol * s + lax.dot_general(kdec, vb, TN, preferred_element_type=F32)
        y = o * lax.rsqrt(jnp.mean(o * o, axis=-1, keepdims=True) + RMS_EPS) * gn_ref[...]
        o_ref[:, h * GLA_DV:(h + 1) * GLA_DV] = (y * jax.nn.silu(gr)).astype(o_ref.dtype)

    @pl.when(c == pl.num_programs(1) - 1)
    def _():
        sout_ref[...] = s_sc[...]


def _gla(gla, small, w2p, ba_row, gn_row, s0):
    b, t, _ = gla.shape
    C = CHUNK
    n_gla = gla.shape[-1]
    return pl.pallas_call(
        functools.partial(_gla_kernel, C=C),
        grid=(b, t // C),
        in_specs=[
            pl.BlockSpec((None, C, n_gla), lambda bi, c: (bi, c, 0)),
            pl.BlockSpec((None, C, LANES), lambda bi, c: (bi, c, 0)),
            pl.BlockSpec(w2p.shape, lambda bi, c: (0, 0)),
            pl.BlockSpec(ba_row.shape, lambda bi, c: (0, 0)),
            pl.BlockSpec(gn_row.shape, lambda bi, c: (0, 0)),
            pl.BlockSpec((None, GLA_H, GLA_DK, GLA_DV), lambda bi, c: (bi, 0, 0, 0)),
        ],
        out_specs=[
            pl.BlockSpec((None, C, MIX_W), lambda bi, c: (bi, c, 0)),
            pl.BlockSpec((None, GLA_H, GLA_DK, GLA_DV), lambda bi, c: (bi, 0, 0, 0)),
        ],
        out_shape=[jax.ShapeDtypeStruct((b, t, MIX_W), BF16), jax.ShapeDtypeStruct(s0.shape, F32)],
        scratch_shapes=[pltpu.VMEM((GLA_H, GLA_DK, GLA_DV), F32)],
        compiler_params=_cp("parallel", "arbitrary"),
        name="gla",
    )(gla, small, w2p, ba_row, gn_row, s0)


def _rope(x, cos, sa, sb):
    half = ROPE_DIM // 2
    return x * cos + pltpu.roll(x, LANES - half, 1) * sa + pltpu.roll(x, half, 1) * sb


def _nsa_prep_kernel(x_ref, cos_ref, sa_ref, sb_ref, q_ref, cmp_ref, slc_ref, win_ref, *mean_ref, tb):
    cos, sa, sb = cos_ref[...], sa_ref[...], sb_ref[...]
    for h in range(NSA_H):
        hs = slice(h * NSA_HD, (h + 1) * NSA_HD)
        q_ref[:, hs] = _rope(x_ref[:, hs], cos, sa, sb) * (NSA_HD ** -0.5)
    base = NSA_H * NSA_HD
    for i, ref in enumerate((cmp_ref, slc_ref, win_ref)):
        k0 = base + i * 2 * NSA_HD
        ref[:, :NSA_HD] = _rope(x_ref[:, k0:k0 + NSA_HD], cos, sa, sb)
        ref[:, NSA_HD:] = x_ref[:, k0 + NSA_HD:k0 + 2 * NSA_HD]
    if mean_ref:
        kv = cmp_ref[...]
        mean_ref[0][...] = jnp.mean(kv.reshape(tb // NSA_BLOCK, NSA_BLOCK, 2 * NSA_HD), axis=1)


def _nsa_prep(nsa, cos, sa, sb, with_means):
    b, t, n = nsa.shape
    tb = min(t, 512)
    tab = pl.BlockSpec((tb, LANES), lambda bi, i: (i, 0))
    kv = pl.BlockSpec((None, tb, 2 * NSA_HD), lambda bi, i: (bi, i, 0))
    out_specs = [pl.BlockSpec((None, tb, MIX_W), lambda bi, i: (bi, i, 0)), kv, kv, kv]
    out_shape = [jax.ShapeDtypeStruct((b, t, MIX_W), F32)] + [jax.ShapeDtypeStruct((b, t, 2 * NSA_HD), F32)] * 3
    if with_means:
        out_specs.append(pl.BlockSpec((None, tb // NSA_BLOCK, 2 * NSA_HD), lambda bi, i: (bi, i, 0)))
        out_shape.append(jax.ShapeDtypeStruct((b, t // NSA_BLOCK, 2 * NSA_HD), F32))
    return pl.pallas_call(
        functools.partial(_nsa_prep_kernel, tb=tb),
        grid=(b, t // tb),
        in_specs=[pl.BlockSpec((None, tb, n), lambda bi, i: (bi, i, 0)), tab, tab, tab],
        out_specs=out_specs,
        out_shape=out_shape,
        compiler_params=_cp("parallel", "parallel"),
        name="nsa_prep",
    )(nsa, cos, sa, sb)


def _masked_softmax(s, mask):
    s = jnp.where(mask, s, -jnp.inf)
    m = jnp.max(s, axis=-1, keepdims=True)
    m = jnp.where(m == -jnp.inf, 0.0, m)
    p = jnp.where(mask, jnp.exp(s - m), 0.0)
    return p / jnp.maximum(jnp.sum(p, axis=-1, keepdims=True), 1e-30)


def _topk_mask_t(score_t, blk_t, n_sel):
    rank = jnp.zeros(score_t.shape, jnp.int32)
    for m in range(score_t.shape[0]):
        sm = score_t[m:m + 1, :]
        beats = (sm > score_t) | ((sm == score_t) & (blk_t > m))
        rank = rank + beats.astype(jnp.int32)
    return rank < n_sel


def _online_update(carry, qs, k, v, mask):
    ms, ls, accs = carry
    bias = jnp.where(mask, 0.0, NEG)
    n = range(len(qs))
    ss = [lax.dot_general(qs[g], k, NT, preferred_element_type=F32) + bias for g in n]
    m_new = [jnp.maximum(ms[g], jnp.max(ss[g], axis=-1, keepdims=True)) for g in n]
    ps = [jnp.exp(ss[g] - m_new[g]) for g in n]
    pvs = [jnp.dot(ps[g].astype(BF16), v, preferred_element_type=F32) for g in n]
    alphas = [jnp.exp(ms[g] - m_new[g]) for g in n]
    ls = [alphas[g] * ls[g] + jnp.sum(ps[g], axis=-1, keepdims=True) for g in n]
    accs = [alphas[g] * accs[g] + pvs[g] for g in n]
    return m_new, ls, accs


def _nsa_attn_kernel(q_ref, sm_ref, cm_ref, slc_ref, win_ref, o_ref, *, tq, kt, nb):
    qi = pl.program_id(1)
    qpos = qi * tq + _iota((tq, 1), 0)
    blk = _iota((tq, nb), 1)
    cmask = (blk + 1) * NSA_BLOCK <= qpos + 1
    kc, vc = cm_ref[:, :NSA_HD], cm_ref[:, NSA_HD:]
    o_c = []
    imp = jnp.zeros((tq, nb), F32)
    for h in range(NSA_H):
        s = lax.dot_general(q_ref[:, h * NSA_HD:(h + 1) * NSA_HD], kc, NT, precision=HI, preferred_element_type=F32)
        p = _masked_softmax(s, cmask)
        imp = imp + p
        o_c.append(jnp.dot(p, vc, precision=HI, preferred_element_type=F32))
    blk_t = _iota((nb, tq), 0)
    qpos_t = qi * tq + _iota((nb, tq), 1)
    cur_t = qpos_t // NSA_BLOCK
    score_t = jnp.where(blk_t == cur_t, jnp.inf,
                        jnp.where((blk_t + 1) * NSA_BLOCK <= qpos_t + 1, imp.T, -jnp.inf))
    sel_b = (_topk_mask_t(score_t, blk_t, min(NSA_TOPK, nb)) & (blk_t <= cur_t)).astype(BF16)
    qs = [q_ref[:, h * NSA_HD:(h + 1) * NSA_HD].astype(BF16) for h in range(NSA_H)]
    kcol = _iota((tq, kt), 1)
    blk_of_col = _iota((nb, kt), 1) // NSA_BLOCK - _iota((nb, kt), 0)
    last = (qi * tq + tq - 1) // kt

    def init():
        return ([jnp.full((tq, 1), NEG, F32)] * NSA_H, [jnp.zeros((tq, 1), F32)] * NSA_H,
                [jnp.zeros((tq, NSA_HD), F32)] * NSA_H)

    def slc_body(kb, carry):
        rows = pl.ds(pl.multiple_of(kb * kt, kt), kt)
        k = slc_ref[rows, :NSA_HD].astype(BF16)
        v = slc_ref[rows, NSA_HD:].astype(BF16)
        expand = (blk_of_col + kb * (kt // NSA_BLOCK) == 0).astype(BF16)
        tok = lax.dot_general(sel_b, expand, TN, preferred_element_type=F32)
        mask = (tok > 0.5) & (kb * kt + kcol <= qpos)
        return _online_update(carry, qs, k, v, mask)

    _, l_s, acc_s = lax.fori_loop(0, last + 1, slc_body, init())

    def win_body(kb, carry):
        rows = pl.ds(pl.multiple_of(kb * kt, kt), kt)
        k = win_ref[rows, :NSA_HD].astype(BF16)
        v = win_ref[rows, NSA_HD:].astype(BF16)
        dist = qpos - (kb * kt + kcol)
        mask = (dist >= 0) & (dist < NSA_WINDOW)
        return _online_update(carry, qs, k, v, mask)

    first = jnp.maximum(qi * tq - (NSA_WINDOW - 1), 0) // kt
    _, l_w, acc_w = lax.fori_loop(first, last + 1, win_body, init())
    gate = jax.nn.sigmoid(sm_ref[...])
    for h in range(NSA_H):
        g0 = SM_NG + 3 * h
        o = (gate[:, g0:g0 + 1] * o_c[h] + gate[:, g0 + 1:g0 + 2] * (acc_s[h] / l_s[h])
             + gate[:, g0 + 2:g0 + 3] * (acc_w[h] / l_w[h]))
        o_ref[:, h * NSA_HD:(h + 1) * NSA_HD] = o.astype(o_ref.dtype)


def _nsa_attn(qr, small, cmean, slc, win):
    b, t, _ = qr.shape
    tq = 128
    kt = min(t, 256)
    nb = t // NSA_BLOCK
    whole = lambda n: pl.BlockSpec((None, n, 2 * NSA_HD), lambda bi, qi: (bi, 0, 0))
    return pl.pallas_call(
        functools.partial(_nsa_attn_kernel, tq=tq, kt=kt, nb=nb),
        grid=(b, t // tq),
        in_specs=[
            pl.BlockSpec((None, tq, MIX_W), lambda bi, qi: (bi, qi, 0)),
            pl.BlockSpec((None, tq, LANES), lambda bi, qi: (bi, qi, 0)),
            whole(nb), whole(t), whole(t),
        ],
        out_specs=pl.BlockSpec((None, tq, MIX_W), lambda bi, qi: (bi, qi, 0)),
        out_shape=jax.ShapeDtypeStruct((b, t, MIX_W), BF16),
        compiler_params=_cp("parallel", "parallel"),
        name="nsa_attn",
    )(qr, small, cmean, slc, win)


def _dn_prep_kernel(x_ref, halo_ref, prev_ref, cw_ref, sm_ref, alog_ref, dtb_ref,
                    q_ref, k_ref, v_ref, g_ref, bt_ref, *, tb):
    halo = jnp.where(pl.program_id(1) == 0, prev_ref[...], halo_ref[...])
    xcat = jnp.concatenate([halo, x_ref[...]], axis=0)
    conv = x_ref[...] * cw_ref[DN_CONV - 1:DN_CONV]
    for j in range(DN_CONV - 1):
        sh = DN_CONV - 1 - j
        conv = conv + pltpu.roll(xcat, sh, 0)[8:] * cw_ref[j:j + 1]
    u = jax.nn.silu(conv)
    for h in range(DN_H):
        hs = slice(h * DN_DK, (h + 1) * DN_DK)
        uq = u[:, hs]
        q_ref[:, hs] = uq * lax.rsqrt(jnp.sum(uq * uq, axis=-1, keepdims=True) + 1e-6) * (DN_DK ** -0.5)
        uk = u[:, DN_H * DN_DK + h * DN_DK:DN_H * DN_DK + (h + 1) * DN_DK]
        k_ref[:, hs] = uk * lax.rsqrt(jnp.sum(uk * uk, axis=-1, keepdims=True) + 1e-6)
    v_ref[...] = u[:, 2 * DN_H * DN_DK:]
    sm = sm_ref[...]
    g_ref[...] = -jnp.exp(alog_ref[...]) * jax.nn.softplus(sm + dtb_ref[...])
    bt_ref[...] = jax.nn.sigmoid(sm)


def _dn_prep(dn, prev8, conv_w, small, alog_row, dtb_row):
    b, t, _ = dn.shape
    tb = min(t, 256)
    hb = tb // 8
    row = pl.BlockSpec((None, tb, MIX_W), lambda bi, i: (bi, i, 0))
    sm = pl.BlockSpec((None, tb, LANES), lambda bi, i: (bi, i, 0))
    one = lambda shape: pl.BlockSpec(shape, lambda bi, i: (0,) * len(shape))
    return pl.pallas_call(
        functools.partial(_dn_prep_kernel, tb=tb),
        grid=(b, t // tb),
        in_specs=[
            pl.BlockSpec((None, tb, DN_QKV), lambda bi, i: (bi, i, 0)),
            pl.BlockSpec((None, 8, DN_QKV), lambda bi, i: (bi, jnp.maximum(i * hb - 1, 0), 0)),
            pl.BlockSpec((None, 8, DN_QKV), lambda bi, i: (bi, 0, 0)),
            one(conv_w.shape), sm, one(alog_row.shape), one(dtb_row.shape),
        ],
        out_specs=[row, row, row, sm, sm],
        out_shape=[jax.ShapeDtypeStruct((b, t, MIX_W), F32)] * 3 + [jax.ShapeDtypeStruct((b, t, LANES), F32)] * 2,
        compiler_params=_cp("parallel", "parallel"),
        name="dn_prep",
    )(dn, dn, prev8, conv_w, small, alog_row, dtb_row)


def _gdn_pre_kernel(q_ref, k_ref, v_ref, g_ref, bt_ref, wv_ref, wk_ref, qe_ref, kd_ref, qk_ref, gam_ref, *, C, nck):
    row, col = _iota((C, C), 0), _iota((C, C), 1)
    incl, strict = col <= row, col < row
    eye = (row == col).astype(F32)
    tri = incl.astype(F32)
    n_sq = C.bit_length() - 2
    chains = []
    for c in range(nck):
        rs = slice(c * C, (c + 1) * C)
        gam_all = jnp.dot(tri, g_ref[rs, :], precision=HI, preferred_element_type=F32)
        gam_ref[rs, :] = gam_all
        gam_t = gam_all.T
        for h in range(DN_H):
            hs = slice(h * DN_DK, (h + 1) * DN_DK)
            q, k = q_ref[rs, hs], k_ref[rs, hs]
            gcol = gam_all[:, SM_DA + h:SM_DA + h + 1]
            bcol = bt_ref[rs, SM_DB + h:SM_DB + h + 1]
            diff = gcol - gam_t[SM_DA + h:SM_DA + h + 1, :]
            dec_incl = jnp.where(incl, jnp.exp(jnp.where(incl, diff, 0.0)), 0.0)
            qk = lax.dot_general(q.astype(BF16), k.astype(BF16), NT, preferred_element_type=F32) * dec_incl
            qk_ref[rs, h * C:(h + 1) * C] = qk.astype(BF16)
            qe_ref[rs, hs] = (q * jnp.exp(gcol)).astype(BF16)
            kd_ref[rs, hs] = (k * jnp.exp(gcol[C - 1:C] - gcol)).astype(BF16)
            x = -(bcol * _dot3(k, k, NT) * jnp.where(strict, dec_incl, 0.0))
            chains.append((rs, hs, x, gcol, bcol))
    xs = [ch[2] for ch in chains]
    ps = [eye + x for x in xs]
    for _ in range(n_sq):
        xs = [_dot3(x, x) for x in xs]
        ps = [p + _dot3(p, x) for p, x in zip(ps, xs)]
    for (rs, hs, _, gcol, bcol), p in zip(chains, ps):
        k, v = k_ref[rs, hs], v_ref[rs, hs]
        rhs = jnp.concatenate([bcol * v, (bcol * jnp.exp(gcol)) * k], axis=1)
        w = _dot3(p, rhs)
        wv_ref[rs, hs] = w[:, :DN_DV]
        wk_ref[rs, hs] = w[:, DN_DV:].astype(BF16)


def _gdn_rec_kernel(wv_ref, wk_ref, qe_ref, kd_ref, qk_ref, gam_ref, z_ref, nrm_ref, s0_ref, o_ref, sout_ref, s_sc,
                    *, C, nck):
    i = pl.program_id(1)

    @pl.when(i == 0)
    def _():
        s_sc[...] = s0_ref[...]

    state = [s_sc[h] for h in range(DN_H)]
    heads = range(DN_H)
    hsl = [slice(h * DN_DK, (h + 1) * DN_DK) for h in heads]
    for c in range(nck):
        rs = slice(c * C, (c + 1) * C)
        sbs = [state[h].astype(BF16) for h in heads]
        ubs = [(wv_ref[rs, hsl[h]] - jnp.dot(wk_ref[rs, hsl[h]], sbs[h], preferred_element_type=F32)).astype(BF16)
               for h in heads]
        outs = [jnp.dot(qe_ref[rs, hsl[h]], sbs[h], preferred_element_type=F32)
                + jnp.dot(qk_ref[rs, h * C:(h + 1) * C], ubs[h], preferred_element_type=F32) for h in heads]
        gend = gam_ref[c * C + C - 1:(c + 1) * C, :]
        state = [jnp.exp(gend[:, SM_DA + h:SM_DA + h + 1]) * state[h]
                 + lax.dot_general(kd_ref[rs, hsl[h]], ubs[h], TN, preferred_element_type=F32) for h in heads]
        for h in heads:
            o = outs[h]
            y = o * lax.rsqrt(jnp.mean(o * o, axis=-1, keepdims=True) + RMS_EPS) * nrm_ref[...]
            o_ref[rs, hsl[h]] = (y * jax.nn.silu(z_ref[rs, hsl[h]])).astype(o_ref.dtype)
    for h in range(DN_H):
        s_sc[h] = state[h]

    @pl.when(i == pl.num_programs(1) - 1)
    def _():
        for h in range(DN_H):
            sout_ref[h] = state[h]


GDN_BLOCK = 256


def _gdn(qd, kd, vd, gsm, btsm, dn, nrm_row, s0):
    b, t, _ = qd.shape
    C = CHUNK
    tb = min(t, GDN_BLOCK)
    nck = tb // C
    row = pl.BlockSpec((None, tb, MIX_W), lambda bi, i: (bi, i, 0))
    sm = pl.BlockSpec((None, tb, LANES), lambda bi, i: (bi, i, 0))
    qk_spec = pl.BlockSpec((None, tb, DN_H * C), lambda bi, i: (bi, i, 0))
    st = pl.BlockSpec((None, DN_H, DN_DK, DN_DV), lambda bi, i: (bi, 0, 0, 0))
    wide = lambda dt: jax.ShapeDtypeStruct((b, t, MIX_W), dt)
    wv, wk, qe, kdc, qk, gam = pl.pallas_call(
        functools.partial(_gdn_pre_kernel, C=C, nck=nck),
        grid=(b, t // tb),
        in_specs=[row, row, row, sm, sm],
        out_specs=[row, row, row, row, qk_spec, sm],
        out_shape=[wide(F32), wide(BF16), wide(BF16), wide(BF16), jax.ShapeDtypeStruct((b, t, DN_H * C), BF16),
                   jax.ShapeDtypeStruct((b, t, LANES), F32)],
        compiler_params=_cp("parallel", "parallel"),
        name="gdn_pre",
    )(qd, kd, vd, gsm, btsm)
    return pl.pallas_call(
        functools.partial(_gdn_rec_kernel, C=C, nck=nck),
        grid=(b, t // tb),
        in_specs=[row, row, row, row, qk_spec, sm,
                  pl.BlockSpec((None, tb, MIX_W), lambda bi, i: (bi, i, DN_QKV // MIX_W)),
                  pl.BlockSpec(nrm_row.shape, lambda bi, i: (0, 0)), st],
        out_specs=[row, st],
        out_shape=[wide(BF16), jax.ShapeDtypeStruct(s0.shape, F32)],
        scratch_shapes=[pltpu.VMEM((DN_H, DN_DK, DN_DV), F32)],
        compiler_params=_cp("parallel", "arbitrary"),
        name="gdn_rec",
    )(wv, wk, qe, kdc, qk, gam, dn, nrm_row, s0)


FOX_PAGES = 8


def _fox_dec_kernel(pt_ref, fx_ref, sm_ref, b_ref, *refs):
    kv_refs, lf_refs = refs[:FOX_PAGES], refs[FOX_PAGES:2 * FOX_PAGES]
    o_ref, lfo_ref, m_sc, l_sc, acc_sc, car_sc = refs[2 * FOX_PAGES:]
    j = pl.program_id(1)
    P = PAGE_SIZE
    rows_per_tok = 2 * FOX_H
    qs = [fx_ref[:, h * FOX_HD:(h + 1) * FOX_HD] * (FOX_HD ** -0.5) for h in range(FOX_H)]

    @pl.when(j == 0)
    def _():
        lf_new = _log_sigmoid(sm_ref[...] + b_ref[...])
        lfo_ref[...] = lf_new
        car_sc[...] = lf_new
        for h in range(FOX_H):
            k_new = fx_ref[:, MIX_W + h * FOX_HD:MIX_W + (h + 1) * FOX_HD]
            m_sc[h] = jnp.sum(qs[h] * k_new, axis=-1, keepdims=True)
            l_sc[h] = jnp.ones((1, 1), F32)
            acc_sc[h] = fx_ref[:, 2 * MIX_W + h * FOX_HD:2 * MIX_W + (h + 1) * FOX_HD]

    upper = (_iota((P, P), 1) > _iota((P, P), 0)).astype(F32)
    carry = car_sc[:, 0:FOX_H]
    scores = []
    for kv_ref, lf_ref in zip(kv_refs, lf_refs):
        lf = lf_ref[...]
        bias = jnp.dot(upper, lf, precision=HI, preferred_element_type=F32) + carry
        scores.append([jnp.sum(kv_ref[pl.ds(h, P, stride=rows_per_tok), :] * qs[h], axis=-1, keepdims=True)
                       + bias[:, h:h + 1] for h in range(FOX_H)])
        carry = carry + jnp.sum(lf, axis=0, keepdims=True)
    car_sc[:, 0:FOX_H] = carry
    outs = []
    for h in range(FOX_H):
        m_step = jnp.max(scores[0][h], axis=0, keepdims=True)
        for g in range(1, FOX_PAGES):
            m_step = jnp.maximum(m_step, jnp.max(scores[g][h], axis=0, keepdims=True))
        m_prev = m_sc[h]
        m_new = jnp.maximum(m_prev, m_step)
        alpha = jnp.exp(m_prev - m_new)
        l_new = alpha * l_sc[h]
        acc = alpha * acc_sc[h]
        for g, kv_ref in enumerate(kv_refs):
            p = jnp.exp(scores[g][h] - m_new)
            l_new = l_new + jnp.sum(p, axis=0, keepdims=True)
            acc = acc + jnp.sum(p * kv_ref[pl.ds(FOX_H + h, P, stride=rows_per_tok), :], axis=0, keepdims=True)
        m_sc[h], l_sc[h], acc_sc[h] = m_new, l_new, acc
        outs.append(acc / l_new)

    @pl.when(j == pl.num_programs(1) - 1)
    def _():
        for h in range(FOX_H):
            o_ref[:, h * FOX_HD:(h + 1) * FOX_HD] = outs[h]


def _fox_decode(pt_flat, n_pages, page0, fox_new, small_new, bias_row, kv_cache, lf_cache):
    b = fox_new.shape[0]
    assert n_pages % FOX_PAGES == 0

    def page(r):
        return lambda bi, j, pt: (page0 + pt[bi * n_pages + n_pages - 1 - (j * FOX_PAGES + r)], 0, 0)

    per_b = lambda n: pl.BlockSpec((None, 1, n), lambda bi, j, pt: (bi, 0, 0))
    return pl.pallas_call(
        _fox_dec_kernel,
        grid_spec=pltpu.PrefetchScalarGridSpec(
            num_scalar_prefetch=1,
            grid=(b, n_pages // FOX_PAGES),
            in_specs=[per_b(3 * MIX_W), per_b(LANES), pl.BlockSpec((1, LANES), lambda bi, j, pt: (0, 0))]
            + [pl.BlockSpec((None, PAGE_SIZE * 2 * FOX_H, FOX_HD), page(r)) for r in range(FOX_PAGES)]
            + [pl.BlockSpec((None, PAGE_SIZE, FOX_H), page(r)) for r in range(FOX_PAGES)],
            out_specs=[per_b(MIX_W), per_b(LANES)],
            scratch_shapes=[pltpu.VMEM((FOX_H, 1, 1), F32), pltpu.VMEM((FOX_H, 1, 1), F32),
                            pltpu.VMEM((FOX_H, 1, FOX_HD), F32), pltpu.VMEM((1, LANES), F32)],
        ),
        out_shape=[jax.ShapeDtypeStruct((b, 1, MIX_W), F32), jax.ShapeDtypeStruct((b, 1, LANES), F32)],
        compiler_params=_cp("parallel", "arbitrary"),
        name="fox_decode",
    )(pt_flat, fox_new, small_new, bias_row, *([kv_cache] * FOX_PAGES), *([lf_cache] * FOX_PAGES))


def _col(row, eye):
    return jnp.sum(eye * row, axis=1, keepdims=True)


def _rec_dec_kernel(g_ref, d_ref, sm_ref, w2_ref, ba_ref, gn_ref, cp_ref, cw_ref, alog_ref, dtb_ref, nrm_ref,
                    sg_ref, sd_ref, og_ref, od_ref, sgo_ref, sdo_ref):
    sm = sm_ref[...]
    lane = _iota((8, LANES), 1)
    ga = jnp.where((lane >= SM_GA) & (lane < SM_GA + GLA_RANK), jnp.broadcast_to(sm, (8, LANES)), 0.0)
    pre = jnp.dot(ga, w2_ref[...], precision=HI, preferred_element_type=F32)[0:1] + ba_ref[...]
    loga = _log_sigmoid(pre) * (1.0 / GLA_TAU)
    eye_k = (_iota((GLA_DK, GLA_DK), 0) == _iota((GLA_DK, GLA_DK), 1)).astype(F32)
    for h in range(GLA_H):
        q = g_ref[:, h * GLA_DK:(h + 1) * GLA_DK] * (GLA_DK ** -0.5)
        k = g_ref[:, GLA_H * GLA_DK + h * GLA_DK:GLA_H * GLA_DK + (h + 1) * GLA_DK]
        v0 = 2 * GLA_H * GLA_DK + h * GLA_DV
        v = g_ref[:, v0:v0 + GLA_DV]
        gr = g_ref[:, v0 + GLA_H * GLA_DV:v0 + GLA_H * GLA_DV + GLA_DV]
        ea = jnp.exp(loga[:, h * GLA_DK:(h + 1) * GLA_DK])
        s0 = sg_ref[h]
        o = jnp.sum(q * k, axis=-1, keepdims=True) * v + jnp.sum(_col(q * ea, eye_k) * s0, axis=0, keepdims=True)
        sgo_ref[h] = _col(ea, eye_k) * s0 + _col(k, eye_k) * v
        y = o * lax.rsqrt(jnp.mean(o * o, axis=-1, keepdims=True) + RMS_EPS) * gn_ref[...]
        og_ref[:, h * GLA_DV:(h + 1) * GLA_DV] = y * jax.nn.silu(gr)
    conv = d_ref[:, :DN_QKV] * cw_ref[DN_CONV - 1:DN_CONV]
    for j in range(DN_CONV - 1):
        conv = conv + cp_ref[j:j + 1] * cw_ref[j:j + 1]
    u = jax.nn.silu(conv)
    gall = -jnp.exp(alog_ref[...]) * jax.nn.softplus(sm + dtb_ref[...])
    ball = jax.nn.sigmoid(sm)
    eye_d = (_iota((DN_DK, DN_DK), 0) == _iota((DN_DK, DN_DK), 1)).astype(F32)
    for h in range(DN_H):
        uq = u[:, h * DN_DK:(h + 1) * DN_DK]
        q = uq * lax.rsqrt(jnp.sum(uq * uq, axis=-1, keepdims=True) + 1e-6) * (DN_DK ** -0.5)
        uk = u[:, DN_H * DN_DK + h * DN_DK:DN_H * DN_DK + (h + 1) * DN_DK]
        k = uk * lax.rsqrt(jnp.sum(uk * uk, axis=-1, keepdims=True) + 1e-6)
        v = u[:, 2 * DN_H * DN_DK + h * DN_DV:2 * DN_H * DN_DK + (h + 1) * DN_DV]
        eg = jnp.exp(gall[:, SM_DA + h:SM_DA + h + 1])
        beta = ball[:, SM_DB + h:SM_DB + h + 1]
        s0 = sd_ref[h]
        kcol = _col(k, eye_d)
        ks = jnp.sum(kcol * s0, axis=0, keepdims=True)
        qs = jnp.sum(_col(q, eye_d) * s0, axis=0, keepdims=True)
        un = beta * (v - eg * ks)
        o = eg * qs + jnp.sum(q * k, axis=-1, keepdims=True) * un
        sdo_ref[h] = eg * s0 + kcol * un
        y = o * lax.rsqrt(jnp.mean(o * o, axis=-1, keepdims=True) + RMS_EPS) * nrm_ref[...]
        z = d_ref[:, DN_QKV + h * DN_DV:DN_QKV + (h + 1) * DN_DV]
        od_ref[:, h * DN_DV:(h + 1) * DN_DV] = y * jax.nn.silu(z)


def _rec_decode(gla_new, dn_new, small_new, w2p, ba_row, gn_row, conv_prev, conv_w, alog_row, dtb_row, nrm_row,
                s_gla, s_dn, row0):
    b = gla_new.shape[0]
    per_b = lambda n: pl.BlockSpec((None, 1, n), lambda bi: (bi, 0, 0))
    one = lambda a: pl.BlockSpec(a.shape, lambda bi: (0,) * a.ndim)
    sg = lambda r0: pl.BlockSpec((None, GLA_H, GLA_DK, GLA_DV), lambda bi: (r0 + bi, 0, 0, 0))
    sd = lambda r0: pl.BlockSpec((None, DN_H, DN_DK, DN_DV), lambda bi: (r0 + bi, 0, 0, 0))
    return pl.pallas_call(
        _rec_dec_kernel,
        grid=(b,),
        in_specs=[per_b(gla_new.shape[-1]), per_b(dn_new.shape[-1]), per_b(LANES), one(w2p), one(ba_row), one(gn_row),
                  pl.BlockSpec((None, DN_CONV - 1, DN_QKV), lambda bi: (row0 + bi, 0, 0)), one(conv_w), one(alog_row),
                  one(dtb_row), one(nrm_row), sg(row0), sd(row0)],
        out_specs=[per_b(MIX_W), per_b(MIX_W), sg(0), sd(0)],
        out_shape=[jax.ShapeDtypeStruct((b, 1, MIX_W), F32)] * 2
        + [jax.ShapeDtypeStruct((b,) + s_gla.shape[1:], F32), jax.ShapeDtypeStruct((b,) + s_dn.shape[1:], F32)],
        compiler_params=_cp("parallel"),
        name="rec_decode",
    )(gla_new, dn_new, small_new, w2p, ba_row, gn_row, conv_prev, conv_w, alog_row, dtb_row, nrm_row, s_gla, s_dn)


def _heads_to_rows(q_row):
    rows = [q_row[:, h * NSA_HD:(h + 1) * NSA_HD] for h in range(NSA_H)]
    return jnp.concatenate(rows + [jnp.zeros((8 - NSA_H, NSA_HD), F32)], axis=0)


CMP_PAGES = 16


def _nsa_dec_cmp_kernel(pt_ref, q_ref, *refs, nbp):
    page_refs, (kc_ref, vc_ref, oc_ref, sel_ref) = refs[:CMP_PAGES], refs[CMP_PAGES:]
    j = pl.program_id(1)
    per = PAGE_SIZE // NSA_BLOCK
    rows = CMP_PAGES * per
    dst = pl.ds(pl.multiple_of(j * rows, rows), rows)
    for kv, ref in enumerate((kc_ref, vc_ref)):
        means = [jnp.mean(r[pl.ds(kv, PAGE_SIZE, stride=2), :].reshape(per, NSA_BLOCK, NSA_HD), axis=1)
                 for r in page_refs]
        ref[dst, :] = jnp.concatenate(means, axis=0)

    @pl.when(j == pl.num_programs(1) - 1)
    def _():
        q8 = _heads_to_rows(q_ref[...])
        kc, vc = kc_ref[...], vc_ref[...]
        s = lax.dot_general(q8, kc, NT, precision=HI, preferred_element_type=F32)
        p = _masked_softmax(s, jnp.full(s.shape, True))
        oc_ref[...] = jnp.dot(p, vc, precision=HI, preferred_element_type=F32)
        imp = jnp.sum(p[0:NSA_H], axis=0, keepdims=True)
        r_i, c_i = _iota((nbp, nbp), 0), _iota((nbp, nbp), 1)
        imp_col = _col(imp, (r_i == c_i).astype(F32))
        beats = (imp_col > imp) | ((imp_col == imp) & (r_i < c_i))
        rank = jnp.sum(beats.astype(jnp.int32), axis=0, keepdims=True)
        ids = jnp.where(rank == _iota((16, nbp), 0), _iota((16, nbp), 1), 0)
        sel_ref[...] = jnp.broadcast_to(jnp.sum(ids, axis=1, keepdims=True), (16, LANES))


def _nsa_dec_cmp(pt_flat, n_pages, page0, q_new, cmp_cache):
    b = q_new.shape[0]
    nbp = n_pages * (PAGE_SIZE // NSA_BLOCK)
    n_sel = NSA_TOPK - 1
    assert nbp >= n_sel and NSA_TOPK <= 16 and n_pages % CMP_PAGES == 0
    page = lambda r: pl.BlockSpec((None, 2 * PAGE_SIZE, NSA_HD),
                                  lambda bi, j, pt: (page0 + pt[bi * n_pages + j * CMP_PAGES + r], 0, 0))
    mean_spec = pl.BlockSpec((None, nbp, NSA_HD), lambda bi, j, pt: (bi, 0, 0))
    return pl.pallas_call(
        functools.partial(_nsa_dec_cmp_kernel, nbp=nbp),
        grid_spec=pltpu.PrefetchScalarGridSpec(
            num_scalar_prefetch=1,
            grid=(b, n_pages // CMP_PAGES),
            in_specs=[pl.BlockSpec((None, 1, MIX_W), lambda bi, j, pt: (bi, 0, 0))]
            + [page(r) for r in range(CMP_PAGES)],
            out_specs=[mean_spec, mean_spec,
                       pl.BlockSpec((None, 8, NSA_HD), lambda bi, j, pt: (bi, 0, 0)),
                       pl.BlockSpec((None, 16, LANES), lambda bi, j, pt: (bi, 0, 0))],
        ),
        out_shape=[jax.ShapeDtypeStruct((b, nbp, NSA_HD), F32)] * 2
        + [jax.ShapeDtypeStruct((b, 8, NSA_HD), F32), jax.ShapeDtypeStruct((b, 16, LANES), jnp.int32)],
        compiler_params=_cp("parallel", "arbitrary"),
        name="nsa_dec_cmp",
    )(pt_flat, q_new, *([cmp_cache] * CMP_PAGES))


def _nsa_dec_attn_kernel(pt_ref, sel_ref, q_ref, blk_ref, snew_ref, win_ref, wnew_ref, oc_ref, sm_ref, o_ref,
                         m_sc, l_sc, acc_sc, *, sw):
    s_i = pl.program_id(1)
    q8 = _heads_to_rows(q_ref[...])

    @pl.when(s_i == 0)
    def _():
        m_sc[...] = jnp.sum(q8 * snew_ref[:, :NSA_HD], axis=-1, keepdims=True)
        l_sc[...] = jnp.ones_like(l_sc)
        acc_sc[...] = jnp.broadcast_to(snew_ref[:, NSA_HD:], acc_sc.shape)

    k, v = blk_ref[pl.ds(0, NSA_BLOCK, stride=2), :], blk_ref[pl.ds(1, NSA_BLOCK, stride=2), :]
    s = lax.dot_general(q8, k, NT, precision=HI, preferred_element_type=F32)
    m_prev = m_sc[...]
    m_new = jnp.maximum(m_prev, jnp.max(s, axis=-1, keepdims=True))
    alpha = jnp.exp(m_prev - m_new)
    p = jnp.exp(s - m_new)
    l_sc[...] = alpha * l_sc[...] + jnp.sum(p, axis=-1, keepdims=True)
    acc_sc[...] = alpha * acc_sc[...] + jnp.dot(p, v, precision=HI, preferred_element_type=F32)
    m_sc[...] = m_new

    @pl.when(s_i == pl.num_programs(1) - 1)
    def _():
        o_s = acc_sc[...] / l_sc[...]
        kw, vw = win_ref[pl.ds(0, sw, stride=2), :], win_ref[pl.ds(1, sw, stride=2), :]
        sw_ = lax.dot_general(q8, kw, NT, precision=HI, preferred_element_type=F32)
        wmask = (sw - _iota((8, sw), 1)) < NSA_WINDOW
        s_new = jnp.sum(q8 * wnew_ref[:, :NSA_HD], axis=-1, keepdims=True)
        m = jnp.maximum(jnp.max(jnp.where(wmask, sw_, -jnp.inf), axis=-1, keepdims=True), s_new)
        pw = jnp.where(wmask, jnp.exp(sw_ - m), 0.0)
        pn = jnp.exp(s_new - m)
        o_w = (jnp.dot(pw, vw, precision=HI, preferred_element_type=F32) + pn * wnew_ref[:, NSA_HD:]) / (
            jnp.sum(pw, axis=-1, keepdims=True) + pn)
        gate = jax.nn.sigmoid(sm_ref[...])
        o_c = oc_ref[...]
        for h in range(NSA_H):
            g0 = SM_NG + 3 * h
            o_ref[:, h * NSA_HD:(h + 1) * NSA_HD] = (
                gate[:, g0:g0 + 1] * o_c[h:h + 1] + gate[:, g0 + 1:g0 + 2] * o_s[h:h + 1]
                + gate[:, g0 + 2:g0 + 3] * o_w[h:h + 1])


def _nsa_dec_attn(pt_flat, sel_flat, n_pages, n_sel, page0, row0, q_new, slc_cache, slc_new, win_cache, win_new,
                  o_c, small_new):
    b = q_new.shape[0]
    sw = win_cache.shape[1] // 2
    per = PAGE_SIZE // NSA_BLOCK
    half = slc_cache.reshape(slc_cache.shape[0] * per, 2 * NSA_BLOCK, NSA_HD)

    def blk_map(bi, s, pt, sel):
        n = sel[bi * n_sel + s]
        return ((page0 + pt[bi * n_pages + n // per]) * per + n % per, 0, 0)

    per_b = lambda r, n: pl.BlockSpec((None, r, n), lambda bi, s, pt, sel: (bi, 0, 0))
    return pl.pallas_call(
        functools.partial(_nsa_dec_attn_kernel, sw=sw),
        grid_spec=pltpu.PrefetchScalarGridSpec(
            num_scalar_prefetch=2,
            grid=(b, n_sel),
            in_specs=[per_b(1, MIX_W), pl.BlockSpec((None, 2 * NSA_BLOCK, NSA_HD), blk_map), per_b(1, 2 * NSA_HD),
                      pl.BlockSpec((None, 2 * sw, NSA_HD), lambda bi, s, pt, sel: (row0 + bi, 0, 0)),
                      per_b(1, 2 * NSA_HD), per_b(8, NSA_HD), per_b(1, LANES)],
            out_specs=per_b(1, MIX_W),
            scratch_shapes=[pltpu.VMEM((8, 1), F32), pltpu.VMEM((8, 1), F32), pltpu.VMEM((8, NSA_HD), F32)],
        ),
        out_shape=jax.ShapeDtypeStruct((b, 1, MIX_W), F32),
        compiler_params=_cp("parallel", "arbitrary"),
        name="nsa_dec_attn",
    )(pt_flat, sel_flat, q_new, half, slc_new, win_cache, win_new, o_c, small_new)


def _lane_row(vals, at):
    return jnp.zeros((1, LANES), F32).at[0, at:at + vals.shape[0]].set(vals.astype(F32))


def _layer_weights(w_in, fox_b_f, gla_w_a2, gla_b_a, gla_norm, dn_conv_w, dn_a_log, dn_dt_bias, dn_norm,
                   w_branch, w_out, w_up, w_down):
    offs = [0]
    for s in SPLIT_SIZES:
        offs.append(offs[-1] + s)
    seg = lambda i: w_in[:, offs[i]:offs[i + 1]]
    (fq, fk, fv, ff, gq, gk, gv, ga, gr, nq, nkc, nks, nkw, ng, dqkv, da, dbeta, dz, mg) = [seg(i) for i in range(19)]
    cat = lambda parts: jnp.concatenate(parts, axis=1).astype(BF16)
    small = [ff, ga, ng, da, dbeta]
    pad = LANES - sum(p.shape[1] for p in small)
    w = dict(
        fox=cat([fq, fk, fv]), gla=cat([gq, gk, gv, gr]), nsa=cat([nq, nkc, nks, nkw]), dn=cat([dqkv, dz]),
        small=cat(small + [jnp.zeros((D_MODEL, pad), F32)]), mg=mg.astype(BF16),
        bf_row=_lane_row(fox_b_f, SM_FF),
        w2p=jnp.zeros((LANES, GLA_H * GLA_DK), F32).at[SM_GA:SM_GA + GLA_RANK].set(gla_w_a2),
        ba_row=gla_b_a.reshape(1, -1), gn_row=gla_norm.reshape(1, -1), conv_w=dn_conv_w,
        alog_row=_lane_row(dn_a_log, SM_DA), dtb_row=_lane_row(dn_dt_bias, SM_DA), dnn_row=dn_norm.reshape(1, -1),
        wb=w_branch.astype(BF16), w_out=w_out.astype(BF16), w_up=w_up.astype(BF16), w_down=w_down.astype(BF16),
    )
    return w


def _rope_tables(pos):
    half = ROPE_DIM // 2
    inv = ROPE_THETA ** (-jnp.arange(half, dtype=F32) / half)
    ang = pos.astype(F32)[:, None] * inv[None, :]
    cos, sin = jnp.cos(ang), jnp.sin(ang)
    t = pos.shape[0]
    z = lambda n: jnp.zeros((t, n), F32)
    return (jnp.concatenate([cos, cos, jnp.ones((t, LANES - ROPE_DIM), F32)], axis=1),
            jnp.concatenate([-sin, z(LANES - half)], axis=1),
            jnp.concatenate([z(half), sin, z(LANES - ROPE_DIM)], axis=1))


def _project(h, w):
    return {name: _mm(h, w[name]) for name in ("fox", "gla", "nsa", "dn", "small", "mg")}


def _mixer_prompt(h, b, t, w):
    p = _project(h, w)
    fox, gla, nsa, dn, small = (p[n].reshape(b, t, -1) for n in ("fox", "gla", "nsa", "dn", "small"))
    lf, c, ct = _fox_prep(small, w["bf_row"])
    o_fox = _fox_attn(fox, c, ct)
    o_gla, s_gla = _gla(gla, small, w["w2p"], w["ba_row"], w["gn_row"], jnp.zeros((b, GLA_H, GLA_DK, GLA_DV), F32))
    qr, cmp_kv, slc_kv, win_kv, cmean = _nsa_prep(nsa, *_rope_tables(jnp.arange(t)), True)
    o_nsa = _nsa_attn(qr, small, cmean, slc_kv, win_kv)
    qd, kd, vd, gsm, btsm = _dn_prep(dn, jnp.zeros((b, 8, DN_QKV), F32), w["conv_w"], small, w["alog_row"],
                                     w["dtb_row"])
    o_dn, s_dn = _gdn(qd, kd, vd, gsm, btsm, dn, w["dnn_row"], jnp.zeros((b, DN_H, DN_DK, DN_DV), F32))
    mix = _branch_gate([o.reshape(b * t, MIX_W) for o in (o_fox, o_gla, o_nsa, o_dn)], w["wb"], p["mg"])
    kv5 = lambda a: a.reshape(b, t, 2, 1, NSA_HD)
    wl = min(NSA_WINDOW, t)
    state = (fox[:, :, MIX_W:].reshape(b, t, 2, FOX_H, FOX_HD), lf[:, :, :FOX_H], kv5(cmp_kv), kv5(slc_kv),
             kv5(win_kv)[:, t - wl:], s_gla, s_dn, dn[:, t - (DN_CONV - 1):, :DN_QKV])
    return mix, state


def _mixer_sample(h, caches, l, pt_flat, n_pages, w):
    fox_kv_c, fox_lf_c, cmp_c, slc_c, win_c, s_gla_c, s_dn_c, conv_c = caches
    b = h.shape[0]
    past_len = n_pages * PAGE_SIZE
    p = _project(h, w)
    fox, gla, dn, small = (p[n].reshape(b, 1, -1) for n in ("fox", "gla", "dn", "small"))
    depth, n_pool = fox_kv_c.shape[:2]
    page0, row0 = l * n_pool, l * b
    o_fox, lf_new = _fox_decode(pt_flat, n_pages, page0, fox, small, w["bf_row"],
                                fox_kv_c.reshape(depth * n_pool, PAGE_SIZE * 2 * FOX_H, FOX_HD),
                                fox_lf_c.reshape(depth * n_pool, PAGE_SIZE, FOX_H))
    o_gla, o_dn, s_gla, s_dn = _rec_decode(
        gla, dn, small, w["w2p"], w["ba_row"], w["gn_row"], conv_c.reshape((depth * b,) + conv_c.shape[2:]),
        w["conv_w"], w["alog_row"], w["dtb_row"], w["dnn_row"], s_gla_c.reshape((depth * b,) + s_gla_c.shape[2:]),
        s_dn_c.reshape((depth * b,) + s_dn_c.shape[2:]), row0)
    tabs = _rope_tables(jnp.full((b,), past_len, jnp.int32))
    qr, cmp_new, slc_new, win_new = _nsa_prep(p["nsa"].reshape(1, b, -1), *tabs, False)
    qr, cmp_new, slc_new, win_new = (a.reshape(b, 1, -1) for a in (qr, cmp_new, slc_new, win_new))
    paged = lambda c: c.reshape(depth * n_pool, 2 * PAGE_SIZE, NSA_HD)
    _, _, o_c, sel = _nsa_dec_cmp(pt_flat, n_pages, page0, qr, paged(cmp_c))
    n_sel = NSA_TOPK - 1
    sel_flat = sel[:, :n_sel, 0].reshape(-1)
    sw = win_c.shape[2]
    o_nsa = _nsa_dec_attn(pt_flat, sel_flat, n_pages, n_sel, page0, row0, qr, paged(slc_c), slc_new,
                          win_c.reshape(depth * b, 2 * sw, NSA_HD), win_new, o_c, small)
    mix = _branch_gate([o.reshape(b, MIX_W).astype(BF16) for o in (o_fox, o_gla, o_nsa, o_dn)], w["wb"], p["mg"])
    kv5 = lambda a: a.reshape(b, 1, 2, 1, NSA_HD)
    wl = min(NSA_WINDOW, sw + 1)
    new_win = jnp.concatenate([win_c[l][:, sw + 1 - wl:], kv5(win_new)], axis=1)
    new_conv = jnp.concatenate([conv_c[l][:, 1:], dn[:, :, :DN_QKV]], axis=1)
    state = (fox[:, :, MIX_W:].reshape(b, 1, 2, FOX_H, FOX_HD), lf_new[:, :, :FOX_H], kv5(cmp_new), kv5(slc_new),
             new_win, s_gla, s_dn, new_conv)
    return mix, state


def _trunk_layer(x, mixer, norms, w):
    g_pre_mix, g_post_mix, g_pre_mlp, g_post_mlp = norms
    mix, state = mixer(_rmsnorm_cast(x, g_pre_mix))
    x = _mm_norm_res(mix, w["w_out"], g_post_mix, x)
    hid = _mm(_rmsnorm_cast(x, g_pre_mlp), w["w_up"], BF16, "relu2")
    x = _mm_norm_res(hid, w["w_down"], g_post_mlp, x)
    return x, state


def kernel(x_prompt, x_sample, cache_fox_kv, cache_fox_logf, cache_nsa_cmp_kv, cache_nsa_slc_kv, cache_nsa_win_kv,
           state_gla, state_dn, state_dn_conv, page_table, norm_pre_mix, norm_post_mix, norm_pre_mlp, norm_post_mlp,
           w_in, fox_b_f, gla_w_a2, gla_b_a, gla_norm, dn_conv_w, dn_a_log, dn_dt_bias, dn_norm, w_branch, w_out,
           w_up, w_down):
    bp, tp, _ = x_prompt.shape
    bs, ts, _ = x_sample.shape
    assert ts == 1
    n_pages = page_table.shape[1]
    pt_flat = page_table.reshape(-1).astype(jnp.int32)
    y_p = x_prompt.reshape(bp * tp, D_MODEL)
    y_s = x_sample.reshape(bs, D_MODEL)
    new_p = [[] for _ in range(8)]
    new_s = [[] for _ in range(8)]
    caches = (cache_fox_kv, cache_fox_logf, cache_nsa_cmp_kv, cache_nsa_slc_kv, cache_nsa_win_kv, state_gla, state_dn,
              state_dn_conv)
    for l in range(DEPTH):
        w = _layer_weights(w_in[l], fox_b_f[l], gla_w_a2[l], gla_b_a[l], gla_norm[l], dn_conv_w[l], dn_a_log[l],
                           dn_dt_bias[l], dn_norm[l], w_branch[l], w_out[l], w_up[l], w_down[l])
        norms = (norm_pre_mix[l], norm_post_mix[l], norm_pre_mlp[l], norm_post_mlp[l])
        y_p, st_p = _trunk_layer(y_p, lambda h: _mixer_prompt(h, bp, tp, w), norms, w)
        y_s, st_s = _trunk_layer(y_s, lambda h: _mixer_sample(h, caches, l, pt_flat, n_pages, w), norms, w)
        for i in range(8):
            new_p[i].append(st_p[i])
            new_s[i].append(st_s[i])
    fox_kv_p, fox_logf_p, cmp_kv_p, slc_kv_p, win_kv_p, gla_p, dn_p, conv_p = [jnp.stack(a) for a in new_p]
    fox_kv_s, fox_logf_s, cmp_kv_s, slc_kv_s, win_kv_s, gla_s, dn_s, conv_s = [jnp.stack(a) for a in new_s]
    return (y_p.reshape(bp, tp, D_MODEL), y_s.reshape(bs, ts, D_MODEL), fox_kv_p, fox_kv_s, fox_logf_p, fox_logf_s,
            cmp_kv_p, cmp_kv_s, slc_kv_p, slc_kv_s, win_kv_p, win_kv_s, gla_p, gla_s, dn_p, dn_s, conv_p, conv_s)
```

```python
import functools

import jax
import jax.numpy as jnp
from jax import lax
from jax.experimental import pallas as pl
from jax.experimental.pallas import tpu as pltpu

F32 = jnp.float32
BF16 = jnp.bfloat16
HI = lax.Precision.HIGHEST
NT = (((1,), (1,)), ((), ()))
TN = (((0,), (0,)), ((), ()))
NEG = -1e30

D_MODEL = 2048
DEPTH = 2
PAGE_SIZE = 128
N_BRANCH = 4
MIX_W = D_MODEL // 4
FOX_H = 4
FOX_HD = MIX_W // FOX_H
GLA_H = 4
GLA_DK = MIX_W // (2 * GLA_H)
GLA_DV = MIX_W // GLA_H
GLA_RANK = 16
GLA_TAU = 16.0
NSA_H = 4
NSA_HD = MIX_W // NSA_H
NSA_BLOCK = 64
NSA_TOPK = 16
NSA_WINDOW = 512
DN_H = 4
DN_DK = MIX_W // DN_H
DN_DV = MIX_W // DN_H
DN_CONV = 4
DN_QKV = DN_H * (2 * DN_DK + DN_DV)
ROPE_DIM = NSA_HD // 4
ROPE_THETA = 500000.0
D_FF = 4 * D_MODEL
CHUNK = 64
SUB = 16
RMS_EPS = 1e-6
LANES = 128
SPLIT_SIZES = (
    FOX_H * FOX_HD, FOX_H * FOX_HD, FOX_H * FOX_HD, FOX_H,
    GLA_H * GLA_DK, GLA_H * GLA_DK, GLA_H * GLA_DV, GLA_RANK, GLA_H * GLA_DV,
    NSA_H * NSA_HD, 2 * NSA_HD, 2 * NSA_HD, 2 * NSA_HD, 3 * NSA_H,
    DN_QKV, DN_H, DN_H, DN_H * DN_DV,
    N_BRANCH * D_MODEL,
)
SM_FF, SM_GA, SM_NG, SM_DA, SM_DB = 0, 4, 20, 32, 36
VMEM_LIMIT = 56 * 1024 * 1024


def _cp(*sem):
    return pltpu.CompilerParams(dimension_semantics=sem, vmem_limit_bytes=VMEM_LIMIT)


def _pick(n, cap):
    if n <= cap:
        return n
    best = None
    for t in range(LANES, cap + 1, LANES):
        if n % t == 0:
            best = t
    assert best is not None, n
    return best


def _log_sigmoid(z):
    return jnp.minimum(z, 0.0) - jnp.log1p(jnp.exp(-jnp.abs(z)))


def _iota(shape, dim):
    return lax.broadcasted_iota(jnp.int32, shape, dim)


def _dot3(a, b, dims=(((1,), (0,)), ((), ()))):
    ah, bh = a.astype(BF16), b.astype(BF16)
    al, bl = (a - ah.astype(F32)).astype(BF16), (b - bh.astype(F32)).astype(BF16)
    f = lambda x, y: lax.dot_general(x, y, dims, preferred_element_type=F32)
    return f(ah, bh) + (f(ah, bl) + f(al, bh))


def _rmsnorm_cast_kernel(x_ref, g_ref, o_ref):
    x = x_ref[...]
    y = x * lax.rsqrt(jnp.mean(x * x, axis=-1, keepdims=True) + RMS_EPS)
    o_ref[...] = (y * g_ref[...]).astype(o_ref.dtype)


def _rmsnorm_cast(x, g):
    m, d = x.shape
    tm = min(m, 512)
    return pl.pallas_call(
        _rmsnorm_cast_kernel,
        grid=(m // tm,),
        in_specs=[pl.BlockSpec((tm, d), lambda i: (i, 0)), pl.BlockSpec((1, d), lambda i: (0, 0))],
        out_specs=pl.BlockSpec((tm, d), lambda i: (i, 0)),
        out_shape=jax.ShapeDtypeStruct((m, d), BF16),
        compiler_params=_cp("parallel"),
        name="rmsnorm_cast",
    )(x, g.reshape(1, d))


def _mm_kernel(a_ref, w_ref, o_ref, *, act):
    y = jnp.dot(a_ref[...], w_ref[...], preferred_element_type=F32)
    if act == "relu2":
        y = jnp.square(jnp.maximum(y, 0.0))
    o_ref[...] = y.astype(o_ref.dtype)


def _mm(a, w, out_dtype=F32, act=None):
    m, k = a.shape
    n = w.shape[1]
    tm = min(m, 512)
    tn = _pick(n, 1024)
    return pl.pallas_call(
        functools.partial(_mm_kernel, act=act),
        grid=(n // tn, m // tm),
        in_specs=[pl.BlockSpec((tm, k), lambda j, i: (i, 0)), pl.BlockSpec((k, tn), lambda j, i: (0, j))],
        out_specs=pl.BlockSpec((tm, tn), lambda j, i: (i, j)),
        out_shape=jax.ShapeDtypeStruct((m, n), out_dtype),
        compiler_params=_cp("parallel", "parallel"),
        name="mm",
    )(a, w)


def _mm_norm_res_kernel(a_ref, w_ref, g_ref, x_ref, o_ref, acc_ref):
    k = pl.program_id(1)

    @pl.when(k == 0)
    def _():
        acc_ref[...] = jnp.zeros_like(acc_ref)

    acc_ref[...] += jnp.dot(a_ref[...], w_ref[...], preferred_element_type=F32)

    @pl.when(k == pl.num_programs(1) - 1)
    def _():
        y = acc_ref[...]
        y = y * lax.rsqrt(jnp.mean(y * y, axis=-1, keepdims=True) + RMS_EPS)
        o_ref[...] = x_ref[...] + y * g_ref[...]


def _mm_norm_res(a, w, g, x):
    m, k = a.shape
    n = w.shape[1]
    tm = min(m, 512)
    tk = _pick(k, 2048)
    return pl.pallas_call(
        _mm_norm_res_kernel,
        grid=(m // tm, k // tk),
        in_specs=[
            pl.BlockSpec((tm, tk), lambda i, kk: (i, kk)),
            pl.BlockSpec((tk, n), lambda i, kk: (kk, 0)),
            pl.BlockSpec((1, n), lambda i, kk: (0, 0)),
            pl.BlockSpec((tm, n), lambda i, kk: (i, 0)),
        ],
        out_specs=pl.BlockSpec((tm, n), lambda i, kk: (i, 0)),
        out_shape=jax.ShapeDtypeStruct((m, n), F32),
        scratch_shapes=[pltpu.VMEM((tm, n), F32)],
        compiler_params=_cp("parallel", "arbitrary"),
        name="mm_norm_res",
    )(a, w, g.reshape(1, n), x)


def _branch_gate_kernel(o0, o1, o2, o3, wb_ref, m0, m1, m2, m3, out_ref):
    acc = None
    for n, (o_n, m_n) in enumerate(((o0, m0), (o1, m1), (o2, m2), (o3, m3))):
        y = jnp.dot(o_n[...], wb_ref[n], preferred_element_type=F32)
        t = jax.nn.sigmoid(m_n[...]) * y
        acc = t if acc is None else acc + t
    out_ref[...] = acc.astype(out_ref.dtype)


def _branch_gate(branches, wb, mg):
    m = mg.shape[0]
    tm = min(m, 512)
    tn = 1024
    nj = D_MODEL // tn
    o_spec = pl.BlockSpec((tm, MIX_W), lambda j, i: (i, 0))
    m_specs = [pl.BlockSpec((tm, tn), functools.partial(lambda j, i, n: (i, n * nj + j), n=n)) for n in range(N_BRANCH)]
    return pl.pallas_call(
        _branch_gate_kernel,
        grid=(nj, m // tm),
        in_specs=[o_spec] * 4 + [pl.BlockSpec((N_BRANCH, MIX_W, tn), lambda j, i: (0, 0, j))] + m_specs,
        out_specs=pl.BlockSpec((tm, tn), lambda j, i: (i, j)),
        out_shape=jax.ShapeDtypeStruct((m, D_MODEL), BF16),
        compiler_params=_cp("parallel", "parallel"),
        name="branch_gate",
    )(*branches, wb, mg, mg, mg, mg)


def _fox_prep_kernel(s_ref, b_ref, lf_ref, c_ref, ct_ref, carry_ref, *, tb):
    @pl.when(pl.program_id(1) == 0)
    def _():
        carry_ref[...] = jnp.zeros_like(carry_ref)

    lf = _log_sigmoid(s_ref[...] + b_ref[...])
    tri = (_iota((tb, tb), 1) <= _iota((tb, tb), 0)).astype(F32)
    c = jnp.dot(tri, lf, precision=HI, preferred_element_type=F32) + carry_ref[...]
    lf_ref[...] = lf
    c_ref[...] = c
    ct_ref[...] = c.T[:8]
    carry_ref[...] = c[tb - 1:tb]


def _fox_prep(small, bias_row):
    b, t, _ = small.shape
    tb = min(t, 256)
    blk = pl.BlockSpec((None, tb, LANES), lambda bi, i: (bi, i, 0))
    return pl.pallas_call(
        functools.partial(_fox_prep_kernel, tb=tb),
        grid=(b, t // tb),
        in_specs=[blk, pl.BlockSpec((1, LANES), lambda bi, i: (0, 0))],
        out_specs=[blk, blk, pl.BlockSpec((None, 8, tb), lambda bi, i: (bi, 0, i))],
        out_shape=[jax.ShapeDtypeStruct((b, t, LANES), F32)] * 2 + [jax.ShapeDtypeStruct((b, 8, t), F32)],
        scratch_shapes=[pltpu.VMEM((1, LANES), F32)],
        compiler_params=_cp("parallel", "arbitrary"),
        name="fox_prep",
    )(small, bias_row)


def _fox_attn_kernel(q_ref, k_ref, v_ref, cq_ref, ck_ref, o_ref, m_sc, l_sc, acc_sc, *, tq, tk):
    qi = pl.program_id(1)
    ki = pl.program_id(2)

    @pl.when(ki == 0)
    def _():
        m_sc[...] = jnp.full_like(m_sc, NEG)
        l_sc[...] = jnp.zeros_like(l_sc)
        acc_sc[...] = jnp.zeros_like(acc_sc)

    def step(diagonal):
        mask = _iota((tk, tq), 0) <= _iota((tk, tq), 1)
        heads = range(FOX_H)
        hsl = [slice(h * FOX_HD, (h + 1) * FOX_HD) for h in heads]
        ss = []
        for h in heads:
            q = (q_ref[:, hsl[h]] * (FOX_HD ** -0.5)).astype(BF16)
            s = lax.dot_general(k_ref[:, hsl[h]].astype(BF16), q, NT, preferred_element_type=F32)
            s = s + cq_ref[h:h + 1, :] - ck_ref[:, h:h + 1]
            ss.append(jnp.where(mask, s, NEG) if diagonal else s)
        m_prev = [m_sc[h] for h in heads]
        m_new = [jnp.maximum(m_prev[h], jnp.max(ss[h], axis=0, keepdims=True)) for h in heads]
        ps = []
        for h in heads:
            p = jnp.exp(ss[h] - m_new[h])
            ps.append(jnp.where(mask, p, 0.0) if diagonal else p)
        pvs = [jnp.dot(v_ref[:, hsl[h]].T.astype(BF16), ps[h].astype(BF16), preferred_element_type=F32)
               for h in heads]
        for h in heads:
            alpha = jnp.exp(m_prev[h] - m_new[h])
            l_sc[h] = alpha * l_sc[h] + jnp.sum(ps[h], axis=0, keepdims=True)
            acc_sc[h] = alpha * acc_sc[h] + pvs[h]
            m_sc[h] = m_new[h]

    pl.when(ki < qi)(functools.partial(step, False))
    pl.when(ki == qi)(functools.partial(step, True))

    @pl.when(ki == pl.num_programs(2) - 1)
    def _():
        for h in range(FOX_H):
            o_ref[:, h * FOX_HD:(h + 1) * FOX_HD] = (acc_sc[h] / l_sc[h]).T.astype(o_ref.dtype)


def _fox_attn(fox, c, ct):
    b, t, _ = fox.shape
    tq = tk = min(t, 512)

    def kmap(col):
        return lambda bi, qi, ki: (bi, jnp.minimum(ki, (qi * tq + tq - 1) // tk), col)

    return pl.pallas_call(
        functools.partial(_fox_attn_kernel, tq=tq, tk=tk),
        grid=(b, t // tq, t // tk),
        in_specs=[
            pl.BlockSpec((None, tq, MIX_W), lambda bi, qi, ki: (bi, qi, 0)),
            pl.BlockSpec((None, tk, MIX_W), kmap(1)),
            pl.BlockSpec((None, tk, MIX_W), kmap(2)),
            pl.BlockSpec((None, 8, tq), lambda bi, qi, ki: (bi, 0, qi)),
            pl.BlockSpec((None, tk, LANES), kmap(0)),
        ],
        out_specs=pl.BlockSpec((None, tq, MIX_W), lambda bi, qi, ki: (bi, qi, 0)),
        out_shape=jax.ShapeDtypeStruct((b, t, MIX_W), BF16),
        scratch_shapes=[pltpu.VMEM((FOX_H, 1, tq), F32), pltpu.VMEM((FOX_H, 1, tq), F32),
                        pltpu.VMEM((FOX_H, FOX_HD, tq), F32)],
        compiler_params=_cp("parallel", "parallel", "arbitrary"),
        name="fox_attn",
    )(fox, fox, fox, ct, c)


def _gla_kernel(g_ref, sm_ref, w2_ref, ba_ref, gn_ref, s0_ref, o_ref, sout_ref, s_sc, *, C):
    c = pl.program_id(1)

    @pl.when(c == 0)
    def _():
        s_sc[...] = s0_ref[...]

    lane = _iota((C, LANES), 1)
    ga = jnp.where((lane >= SM_GA) & (lane < SM_GA + GLA_RANK), sm_ref[...], 0.0)
    pre = jnp.dot(ga, w2_ref[...], precision=HI, preferred_element_type=F32) + ba_ref[...]
    loga = _log_sigmoid(pre) * (1.0 / GLA_TAU)
    tri = (_iota((C, C), 1) <= _iota((C, C), 0)).astype(F32)
    b_all = jnp.dot(tri, loga, precision=HI, preferred_element_type=F32)
    eye = (_iota((GLA_DK, GLA_DK), 0) == _iota((GLA_DK, GLA_DK), 1)).astype(F32)
    heads, blocks = range(GLA_H), range(C // SUB)
    kofs, vofs = GLA_H * GLA_DK, 2 * GLA_H * GLA_DK
    qs = [g_ref[:, h * GLA_DK:(h + 1) * GLA_DK] * (GLA_DK ** -0.5) for h in heads]
    ks = [g_ref[:, kofs + h * GLA_DK:kofs + (h + 1) * GLA_DK] for h in heads]
    vbs = [g_ref[:, vofs + h * GLA_DV:vofs + (h + 1) * GLA_DV].astype(BF16) for h in heads]
    bhs = [b_all[:, h * GLA_DK:(h + 1) * GLA_DK] for h in heads]
    states = [s_sc[h] for h in heads]
    o_inter = [jnp.dot((qs[h] * jnp.exp(bhs[h])).astype(BF16), states[h].astype(BF16), preferred_element_type=F32)
               for h in heads]
    atts = {}
    for h in heads:
        for ib in blocks[1:]:
            a0 = ib * SUB
            bi = bhs[h][a0:a0 + SUB]
            r = bi[0:1]
            qe = (qs[h][a0:a0 + SUB] * jnp.exp(bi - r)).astype(BF16)
            ke = (ks[h][:a0] * jnp.exp(r - bhs[h][:a0])).astype(BF16)
            atts[h, ib] = lax.dot_general(qe, ke, NT, preferred_element_type=F32)
    pair = 2 * GLA_DK
    mask3 = _iota((SUB, SUB, pair), 1) <= _iota((SUB, SUB, pair), 0)
    low = _iota((SUB, SUB, pair), 2) < GLA_DK
    ds = {}
    for hp in range(GLA_H // 2):
        q2 = g_ref[:, hp * pair:(hp + 1) * pair] * (GLA_DK ** -0.5)
        k2 = g_ref[:, kofs + hp * pair:kofs + (hp + 1) * pair]
        b2 = b_all[:, hp * pair:(hp + 1) * pair]
        for ib in blocks:
            rs = slice(ib * SUB, (ib + 1) * SUB)
            bi = b2[rs]
            diff = bi[:, None, :] - bi[None, :, :]
            e = jnp.where(mask3, jnp.exp(jnp.where(mask3, diff, 0.0)), 0.0)
            prod = q2[rs][:, None, :] * k2[rs][None, :, :] * e
            ds[2 * hp, ib] = jnp.sum(jnp.where(low, prod, 0.0), axis=-1)
            ds[2 * hp + 1, ib] = jnp.sum(jnp.where(low, 0.0, prod), axis=-1)
    rows = {}
    for h in heads:
        for ib in blocks:
            a0 = ib * SUB
            o_i = jnp.dot(ds[h, ib].astype(BF16), vbs[h][a0:a0 + SUB], preferred_element_type=F32)
            if ib > 0:
                o_i = o_i + jnp.dot(atts[h, ib].astype(BF16), vbs[h][:a0], preferred_element_type=F32)
            rows[h, ib] = o_i
    for h in heads:
        bend = bhs[h][C - 1:C]
        kdec = (ks[h] * jnp.exp(bend - bhs[h])).astype(BF16)
        dcol = jnp.sum(eye * jnp.exp(bend), axis=1, keepdims=True)
        s_sc[h] = dcol * states[h] + lax.dot_general(kdec, vbs[h], TN, preferred_element_type=F32)
    for h in heads:
        o = jnp.concatenate([rows[h, ib] for ib in blocks], axis=0) + o_inter[h]
        y = o * lax.rsqrt(jnp.mean(o * o, axis=-1, keepdims=True) + RMS_EPS) * gn_ref[...]
        gr = g_ref[:, vofs + GLA_H * GLA_DV + h * GLA_DV:vofs + GLA_H * GLA_DV + (h + 1) * GLA_DV]
        o_ref[:, h * GLA_DV:(h + 1) * GLA_DV] = (y * jax.nn.silu(gr)).astype(o_ref.dtype)

    @pl.when(c == pl.num_programs(1) - 1)
    def _():
        sout_ref[...] = s_sc[...]


def _gla(gla, small, w2p, ba_row, gn_row, s0):
    b, t, _ = gla.shape
    C = CHUNK
    n_gla = gla.shape[-1]
    return pl.pallas_call(
        functools.partial(_gla_kernel, C=C),
        grid=(b, t // C),
        in_specs=[
            pl.BlockSpec((None, C, n_gla), lambda bi, c: (bi, c, 0)),
            pl.BlockSpec((None, C, LANES), lambda bi, c: (bi, c, 0)),
            pl.BlockSpec(w2p.shape, lambda bi, c: (0, 0)),
            pl.BlockSpec(ba_row.shape, lambda bi, c: (0, 0)),
            pl.BlockSpec(gn_row.shape, lambda bi, c: (0, 0)),
            pl.BlockSpec((None, GLA_H, GLA_DK, GLA_DV), lambda bi, c: (bi, 0, 0, 0)),
        ],
        out_specs=[
            pl.BlockSpec((None, C, MIX_W), lambda bi, c: (bi, c, 0)),
            pl.BlockSpec((None, GLA_H, GLA_DK, GLA_DV), lambda bi, c: (bi, 0, 0, 0)),
        ],
        out_shape=[jax.ShapeDtypeStruct((b, t, MIX_W), BF16), jax.ShapeDtypeStruct(s0.shape, F32)],
        scratch_shapes=[pltpu.VMEM((GLA_H, GLA_DK, GLA_DV), F32)],
        compiler_params=_cp("parallel", "arbitrary"),
        name="gla",
    )(gla, small, w2p, ba_row, gn_row, s0)


def _rope(x, cos, sa, sb):
    half = ROPE_DIM // 2
    return x * cos + pltpu.roll(x, LANES - half, 1) * sa + pltpu.roll(x, half, 1) * sb


def _nsa_prep_kernel(x_ref, cos_ref, sa_ref, sb_ref, q_ref, cmp_ref, slc_ref, win_ref, *mean_ref, tb):
    cos, sa, sb = cos_ref[...], sa_ref[...], sb_ref[...]
    for h in range(NSA_H):
        hs = slice(h * NSA_HD, (h + 1) * NSA_HD)
        q_ref[:, hs] = _rope(x_ref[:, hs], cos, sa, sb) * (NSA_HD ** -0.5)
    base = NSA_H * NSA_HD
    for i, ref in enumerate((cmp_ref, slc_ref, win_ref)):
        k0 = base + i * 2 * NSA_HD
        ref[:, :NSA_HD] = _rope(x_ref[:, k0:k0 + NSA_HD], cos, sa, sb)
        ref[:, NSA_HD:] = x_ref[:, k0 + NSA_HD:k0 + 2 * NSA_HD]
    if mean_ref:
        kv = cmp_ref[...]
        mean_ref[0][...] = jnp.mean(kv.reshape(tb // NSA_BLOCK, NSA_BLOCK, 2 * NSA_HD), axis=1)


def _nsa_prep(nsa, cos, sa, sb, with_means):
    b, t, n = nsa.shape
    tb = min(t, 512)
    tab = pl.BlockSpec((tb, LANES), lambda bi, i: (i, 0))
    kv = pl.BlockSpec((None, tb, 2 * NSA_HD), lambda bi, i: (bi, i, 0))
    out_specs = [pl.BlockSpec((None, tb, MIX_W), lambda bi, i: (bi, i, 0)), kv, kv, kv]
    out_shape = [jax.ShapeDtypeStruct((b, t, MIX_W), F32)] + [jax.ShapeDtypeStruct((b, t, 2 * NSA_HD), F32)] * 3
    if with_means:
        out_specs.append(pl.BlockSpec((None, tb // NSA_BLOCK, 2 * NSA_HD), lambda bi, i: (bi, i, 0)))
        out_shape.append(jax.ShapeDtypeStruct((b, t // NSA_BLOCK, 2 * NSA_HD), F32))
    return pl.pallas_call(
        functools.partial(_nsa_prep_kernel, tb=tb),
        grid=(b, t // tb),
        in_specs=[pl.BlockSpec((None, tb, n), lambda bi, i: (bi, i, 0)), tab, tab, tab],
        out_specs=out_specs,
        out_shape=out_shape,
        compiler_params=_cp("parallel", "parallel"),
        name="nsa_prep",
    )(nsa, cos, sa, sb)


def _masked_softmax(s, mask):
    s = jnp.where(mask, s, -jnp.inf)
    m = jnp.max(s, axis=-1, keepdims=True)
    m = jnp.where(m == -jnp.inf, 0.0, m)
    p = jnp.where(mask, jnp.exp(s - m), 0.0)
    return p / jnp.maximum(jnp.sum(p, axis=-1, keepdims=True), 1e-30)


def _topk_mask_t(score_t, blk_t, n_sel):
    rank = jnp.zeros(score_t.shape, jnp.int32)
    for m in range(score_t.shape[0]):
        sm = score_t[m:m + 1, :]
        beats = (sm > score_t) | ((sm == score_t) & (blk_t > m))
        rank = rank + beats.astype(jnp.int32)
    return rank < n_sel


def _online_update(carry, qs, k, v_t, mask):
    ms, ls, accs = carry
    bias = jnp.where(mask, 0.0, NEG)
    n = range(len(qs))
    ss = [lax.dot_general(k, qs[g], NT, preferred_element_type=F32) + bias for g in n]
    m_new = [jnp.maximum(ms[g], jnp.max(ss[g], axis=0, keepdims=True)) for g in n]
    ps = [jnp.exp(ss[g] - m_new[g]) for g in n]
    pvs = [jnp.dot(v_t, ps[g].astype(BF16), preferred_element_type=F32) for g in n]
    alphas = [jnp.exp(ms[g] - m_new[g]) for g in n]
    ls = [alphas[g] * ls[g] + jnp.sum(ps[g], axis=0, keepdims=True) for g in n]
    accs = [alphas[g] * accs[g] + pvs[g] for g in n]
    return m_new, ls, accs


def _nsa_attn_kernel(q_ref, sm_ref, cm_ref, slc_ref, win_ref, o_ref, *, tq, kt, nb):
    qi = pl.program_id(1)
    qpos = qi * tq + _iota((tq, 1), 0)
    blk = _iota((tq, nb), 1)
    cmask = (blk + 1) * NSA_BLOCK <= qpos + 1
    kc, vc = cm_ref[:, :NSA_HD], cm_ref[:, NSA_HD:]
    o_c = []
    imp = jnp.zeros((tq, nb), F32)
    for h in range(NSA_H):
        s = lax.dot_general(q_ref[:, h * NSA_HD:(h + 1) * NSA_HD], kc, NT, precision=HI, preferred_element_type=F32)
        p = _masked_softmax(s, cmask)
        imp = imp + p
        o_c.append(jnp.dot(p, vc, precision=HI, preferred_element_type=F32))
    blk_t = _iota((nb, tq), 0)
    qpos_t = qi * tq + _iota((nb, tq), 1)
    cur_t = qpos_t // NSA_BLOCK
    score_t = jnp.where(blk_t == cur_t, jnp.inf,
                        jnp.where((blk_t + 1) * NSA_BLOCK <= qpos_t + 1, imp.T, -jnp.inf))
    sel_b = (_topk_mask_t(score_t, blk_t, min(NSA_TOPK, nb)) & (blk_t <= cur_t)).astype(BF16)
    qs = [q_ref[:, h * NSA_HD:(h + 1) * NSA_HD].astype(BF16) for h in range(NSA_H)]
    krow = _iota((kt, tq), 0)
    qpos_row = qi * tq + _iota((1, tq), 1)
    blk_of_col = _iota((nb, kt), 1) // NSA_BLOCK - _iota((nb, kt), 0)
    last = (qi * tq + tq - 1) // kt

    def init():
        return ([jnp.full((1, tq), NEG, F32)] * NSA_H, [jnp.zeros((1, tq), F32)] * NSA_H,
                [jnp.zeros((NSA_HD, tq), F32)] * NSA_H)

    def slc_body(kb, carry):
        rows = pl.ds(pl.multiple_of(kb * kt, kt), kt)
        k = slc_ref[rows, :NSA_HD].astype(BF16)
        v_t = slc_ref[rows, NSA_HD:].T.astype(BF16)
        expand = (blk_of_col + kb * (kt // NSA_BLOCK) == 0).astype(BF16)
        tok = lax.dot_general(expand, sel_b, TN, preferred_element_type=F32)
        mask = (tok > 0.5) & (kb * kt + krow <= qpos_row)
        return _online_update(carry, qs, k, v_t, mask)

    _, l_s, acc_s = lax.fori_loop(0, last + 1, slc_body, init())

    def win_body(kb, carry):
        rows = pl.ds(pl.multiple_of(kb * kt, kt), kt)
        k = win_ref[rows, :NSA_HD].astype(BF16)
        v_t = win_ref[rows, NSA_HD:].T.astype(BF16)
        dist = qpos_row - (kb * kt + krow)
        mask = (dist >= 0) & (dist < NSA_WINDOW)
        return _online_update(carry, qs, k, v_t, mask)

    first = jnp.maximum(qi * tq - (NSA_WINDOW - 1), 0) // kt
    _, l_w, acc_w = lax.fori_loop(first, last + 1, win_body, init())
    gate = jax.nn.sigmoid(sm_ref[...])
    for h in range(NSA_H):
        g0 = SM_NG + 3 * h
        o = (gate[:, g0:g0 + 1] * o_c[h] + gate[:, g0 + 1:g0 + 2] * (acc_s[h] / l_s[h]).T
             + gate[:, g0 + 2:g0 + 3] * (acc_w[h] / l_w[h]).T)
        o_ref[:, h * NSA_HD:(h + 1) * NSA_HD] = o.astype(o_ref.dtype)


def _nsa_attn(qr, small, cmean, slc, win):
    b, t, _ = qr.shape
    tq = 128
    kt = min(t, 512)
    nb = t // NSA_BLOCK
    whole = lambda n: pl.BlockSpec((None, n, 2 * NSA_HD), lambda bi, qi: (bi, 0, 0))
    return pl.pallas_call(
        functools.partial(_nsa_attn_kernel, tq=tq, kt=kt, nb=nb),
        grid=(b, t // tq),
        in_specs=[
            pl.BlockSpec((None, tq, MIX_W), lambda bi, qi: (bi, qi, 0)),
            pl.BlockSpec((None, tq, LANES), lambda bi, qi: (bi, qi, 0)),
            whole(nb), whole(t), whole(t),
        ],
        out_specs=pl.BlockSpec((None, tq, MIX_W), lambda bi, qi: (bi, qi, 0)),
        out_shape=jax.ShapeDtypeStruct((b, t, MIX_W), BF16),
        compiler_params=_cp("parallel", "parallel"),
        name="nsa_attn",
    )(qr, small, cmean, slc, win)


def _dn_prep_kernel(x_ref, halo_ref, prev_ref, cw_ref, sm_ref, alog_ref, dtb_ref,
                    q_ref, k_ref, v_ref, g_ref, bt_ref, *, tb):
    halo = jnp.where(pl.program_id(1) == 0, prev_ref[...], halo_ref[...])
    xcat = jnp.concatenate([halo, x_ref[...]], axis=0)
    conv = x_ref[...] * cw_ref[DN_CONV - 1:DN_CONV]
    for j in range(DN_CONV - 1):
        sh = DN_CONV - 1 - j
        conv = conv + pltpu.roll(xcat, sh, 0)[8:] * cw_ref[j:j + 1]
    u = jax.nn.silu(conv)
    for h in range(DN_H):
        hs = slice(h * DN_DK, (h + 1) * DN_DK)
        uq = u[:, hs]
        q_ref[:, hs] = uq * lax.rsqrt(jnp.sum(uq * uq, axis=-1, keepdims=True) + 1e-6) * (DN_DK ** -0.5)
        uk = u[:, DN_H * DN_DK + h * DN_DK:DN_H * DN_DK + (h + 1) * DN_DK]
        k_ref[:, hs] = uk * lax.rsqrt(jnp.sum(uk * uk, axis=-1, keepdims=True) + 1e-6)
    v_ref[...] = u[:, 2 * DN_H * DN_DK:]
    sm = sm_ref[...]
    g_ref[...] = -jnp.exp(alog_ref[...]) * jax.nn.softplus(sm + dtb_ref[...])
    bt_ref[...] = jax.nn.sigmoid(sm)


def _dn_prep(dn, prev8, conv_w, small, alog_row, dtb_row):
    b, t, _ = dn.shape
    tb = min(t, 256)
    hb = tb // 8
    row = pl.BlockSpec((None, tb, MIX_W), lambda bi, i: (bi, i, 0))
    sm = pl.BlockSpec((None, tb, LANES), lambda bi, i: (bi, i, 0))
    one = lambda shape: pl.BlockSpec(shape, lambda bi, i: (0,) * len(shape))
    return pl.pallas_call(
        functools.partial(_dn_prep_kernel, tb=tb),
        grid=(b, t // tb),
        in_specs=[
            pl.BlockSpec((None, tb, DN_QKV), lambda bi, i: (bi, i, 0)),
            pl.BlockSpec((None, 8, DN_QKV), lambda bi, i: (bi, jnp.maximum(i * hb - 1, 0), 0)),
            pl.BlockSpec((None, 8, DN_QKV), lambda bi, i: (bi, 0, 0)),
            one(conv_w.shape), sm, one(alog_row.shape), one(dtb_row.shape),
        ],
        out_specs=[row, row, row, sm, sm],
        out_shape=[jax.ShapeDtypeStruct((b, t, MIX_W), F32)] * 3 + [jax.ShapeDtypeStruct((b, t, LANES), F32)] * 2,
        compiler_params=_cp("parallel", "parallel"),
        name="dn_prep",
    )(dn, dn, prev8, conv_w, small, alog_row, dtb_row)


def _gdn_pre_kernel(q_ref, k_ref, v_ref, g_ref, bt_ref, wv_ref, wk_ref, qe_ref, kd_ref, qk_ref, gam_ref, *, C, nck):
    row, col = _iota((C, C), 0), _iota((C, C), 1)
    incl, strict = col <= row, col < row
    eye = (row == col).astype(F32)
    tri = incl.astype(F32)
    n_sq = C.bit_length() - 2
    chains = []
    for c in range(nck):
        rs = slice(c * C, (c + 1) * C)
        gam_all = jnp.dot(tri, g_ref[rs, :], precision=HI, preferred_element_type=F32)
        gam_ref[rs, :] = gam_all
        gam_t = gam_all.T
        for h in range(DN_H):
            hs = slice(h * DN_DK, (h + 1) * DN_DK)
            q, k = q_ref[rs, hs], k_ref[rs, hs]
            gcol = gam_all[:, SM_DA + h:SM_DA + h + 1]
            bcol = bt_ref[rs, SM_DB + h:SM_DB + h + 1]
            diff = gcol - gam_t[SM_DA + h:SM_DA + h + 1, :]
            dec_incl = jnp.where(incl, jnp.exp(jnp.where(incl, diff, 0.0)), 0.0)
            qk = lax.dot_general(q.astype(BF16), k.astype(BF16), NT, preferred_element_type=F32) * dec_incl
            qk_ref[rs, h * C:(h + 1) * C] = qk.astype(BF16)
            qe_ref[rs, hs] = (q * jnp.exp(gcol)).astype(BF16)
            kd_ref[rs, hs] = (k * jnp.exp(gcol[C - 1:C] - gcol)).astype(BF16)
            x = -(bcol * _dot3(k, k, NT) * jnp.where(strict, dec_incl, 0.0))
            chains.append((rs, hs, x, gcol, bcol))
    xs = [ch[2] for ch in chains]
    ps = [eye + x for x in xs]
    for _ in range(n_sq):
        xs = [_dot3(x, x) for x in xs]
        ps = [p + _dot3(p, x) for p, x in zip(ps, xs)]
    for (rs, hs, _, gcol, bcol), p in zip(chains, ps):
        k, v = k_ref[rs, hs], v_ref[rs, hs]
        rhs = jnp.concatenate([bcol * v, (bcol * jnp.exp(gcol)) * k], axis=1)
        w = _dot3(p, rhs)
        wv_ref[rs, hs] = w[:, :DN_DV]
        wk_ref[rs, hs] = w[:, DN_DV:].astype(BF16)


def _gdn_rec_kernel(wv_ref, wk_ref, qe_ref, kd_ref, qk_ref, gam_ref, z_ref, nrm_ref, s0_ref, o_ref, sout_ref, s_sc,
                    *, C, nck):
    i = pl.program_id(1)

    @pl.when(i == 0)
    def _():
        s_sc[...] = s0_ref[...]

    state = [s_sc[h] for h in range(DN_H)]
    heads = range(DN_H)
    hsl = [slice(h * DN_DK, (h + 1) * DN_DK) for h in heads]
    for c in range(nck):
        rs = slice(c * C, (c + 1) * C)
        sbs = [state[h].astype(BF16) for h in heads]
        ubs = [(wv_ref[rs, hsl[h]] - jnp.dot(wk_ref[rs, hsl[h]], sbs[h], preferred_element_type=F32)).astype(BF16)
               for h in heads]
        outs = [jnp.dot(qe_ref[rs, hsl[h]], sbs[h], preferred_element_type=F32)
                + jnp.dot(qk_ref[rs, h * C:(h + 1) * C], ubs[h], preferred_element_type=F32) for h in heads]
        gend = gam_ref[c * C + C - 1:(c + 1) * C, :]
        state = [jnp.exp(gend[:, SM_DA + h:SM_DA + h + 1]) * state[h]
                 + lax.dot_general(kd_ref[rs, hsl[h]], ubs[h], TN, preferred_element_type=F32) for h in heads]
        for h in heads:
            o = outs[h]
            y = o * lax.rsqrt(jnp.mean(o * o, axis=-1, keepdims=True) + RMS_EPS) * nrm_ref[...]
            o_ref[rs, hsl[h]] = (y * jax.nn.silu(z_ref[rs, hsl[h]])).astype(o_ref.dtype)
    for h in range(DN_H):
        s_sc[h] = state[h]

    @pl.when(i == pl.num_programs(1) - 1)
    def _():
        for h in range(DN_H):
            sout_ref[h] = state[h]


GDN_BLOCK = 256


def _gdn(qd, kd, vd, gsm, btsm, dn, nrm_row, s0):
    b, t, _ = qd.shape
    C = CHUNK
    tb = min(t, GDN_BLOCK)
    nck = tb // C
    row = pl.BlockSpec((None, tb, MIX_W), lambda bi, i: (bi, i, 0))
    sm = pl.BlockSpec((None, tb, LANES), lambda bi, i: (bi, i, 0))
    qk_spec = pl.BlockSpec((None, tb, DN_H * C), lambda bi, i: (bi, i, 0))
    st = pl.BlockSpec((None, DN_H, DN_DK, DN_DV), lambda bi, i: (bi, 0, 0, 0))
    wide = lambda dt: jax.ShapeDtypeStruct((b, t, MIX_W), dt)
    wv, wk, qe, kdc, qk, gam = pl.pallas_call(
        functools.partial(_gdn_pre_kernel, C=C, nck=nck),
        grid=(b, t // tb),
        in_specs=[row, row, row, sm, sm],
        out_specs=[row, row, row, row, qk_spec, sm],
        out_shape=[wide(F32), wide(BF16), wide(BF16), wide(BF16), jax.ShapeDtypeStruct((b, t, DN_H * C), BF16),
                   jax.ShapeDtypeStruct((b, t, LANES), F32)],
        compiler_params=_cp("parallel", "parallel"),
        name="gdn_pre",
    )(qd, kd, vd, gsm, btsm)
    return pl.pallas_call(
        functools.partial(_gdn_rec_kernel, C=C, nck=nck),
        grid=(b, t // tb),
        in_specs=[row, row, row, row, qk_spec, sm,
                  pl.BlockSpec((None, tb, MIX_W), lambda bi, i: (bi, i, DN_QKV // MIX_W)),
                  pl.BlockSpec(nrm_row.shape, lambda bi, i: (0, 0)), st],
        out_specs=[row, st],
        out_shape=[wide(BF16), jax.ShapeDtypeStruct(s0.shape, F32)],
        scratch_shapes=[pltpu.VMEM((DN_H, DN_DK, DN_DV), F32)],
        compiler_params=_cp("parallel", "arbitrary"),
        name="gdn_rec",
    )(wv, wk, qe, kdc, qk, gam, dn, nrm_row, s0)


FOX_PAGES = 8


def _fox_dec_kernel(pt_ref, fx_ref, sm_ref, b_ref, *refs):
    kv_refs, lf_refs = refs[:FOX_PAGES], refs[FOX_PAGES:2 * FOX_PAGES]
    o_ref, lfo_ref, m_sc, l_sc, acc_sc, car_sc = refs[2 * FOX_PAGES:]
    j = pl.program_id(1)
    P = PAGE_SIZE
    rows_per_tok = 2 * FOX_H
    qs = [fx_ref[:, h * FOX_HD:(h + 1) * FOX_HD] * (FOX_HD ** -0.5) for h in range(FOX_H)]

    @pl.when(j == 0)
    def _():
        lf_new = _log_sigmoid(sm_ref[...] + b_ref[...])
        lfo_ref[...] = lf_new
        car_sc[...] = lf_new
        for h in range(FOX_H):
            k_new = fx_ref[:, MIX_W + h * FOX_HD:MIX_W + (h + 1) * FOX_HD]
            m_sc[h] = jnp.sum(qs[h] * k_new, axis=-1, keepdims=True)
            l_sc[h] = jnp.ones((1, 1), F32)
            acc_sc[h] = fx_ref[:, 2 * MIX_W + h * FOX_HD:2 * MIX_W + (h + 1) * FOX_HD]

    upper = (_iota((P, P), 1) > _iota((P, P), 0)).astype(F32)
    carry = car_sc[:, 0:FOX_H]
    scores = []
    for kv_ref, lf_ref in zip(kv_refs, lf_refs):
        lf = lf_ref[...]
        bias = jnp.dot(upper, lf, precision=HI, preferred_element_type=F32) + carry
        scores.append([jnp.sum(kv_ref[pl.ds(h, P, stride=rows_per_tok), :] * qs[h], axis=-1, keepdims=True)
                       + bias[:, h:h + 1] for h in range(FOX_H)])
        carry = carry + jnp.sum(lf, axis=0, keepdims=True)
    car_sc[:, 0:FOX_H] = carry
    outs = []
    for h in range(FOX_H):
        m_step = jnp.max(scores[0][h], axis=0, keepdims=True)
        for g in range(1, FOX_PAGES):
            m_step = jnp.maximum(m_step, jnp.max(scores[g][h], axis=0, keepdims=True))
        m_prev = m_sc[h]
        m_new = jnp.maximum(m_prev, m_step)
        alpha = jnp.exp(m_prev - m_new)
        l_new = alpha * l_sc[h]
        acc = alpha * acc_sc[h]
        for g, kv_ref in enumerate(kv_refs):
            p = jnp.exp(scores[g][h] - m_new)
            l_new = l_new + jnp.sum(p, axis=0, keepdims=True)
            acc = acc + jnp.sum(p * kv_ref[pl.ds(FOX_H + h, P, stride=rows_per_tok), :], axis=0, keepdims=True)
        m_sc[h], l_sc[h], acc_sc[h] = m_new, l_new, acc
        outs.append(acc / l_new)

    @pl.when(j == pl.num_programs(1) - 1)
    def _():
        for h in range(FOX_H):
            o_ref[:, h * FOX_HD:(h + 1) * FOX_HD] = outs[h]


def _fox_decode(pt_flat, n_pages, page0, fox_new, small_new, bias_row, kv_cache, lf_cache):
    b = fox_new.shape[0]
    assert n_pages % FOX_PAGES == 0

    def page(r):
        return lambda bi, j, pt: (page0 + pt[bi * n_pages + n_pages - 1 - (j * FOX_PAGES + r)], 0, 0)

    per_b = lambda n: pl.BlockSpec((None, 1, n), lambda bi, j, pt: (bi, 0, 0))
    return pl.pallas_call(
        _fox_dec_kernel,
        grid_spec=pltpu.PrefetchScalarGridSpec(
            num_scalar_prefetch=1,
            grid=(b, n_pages // FOX_PAGES),
            in_specs=[per_b(3 * MIX_W), per_b(LANES), pl.BlockSpec((1, LANES), lambda bi, j, pt: (0, 0))]
            + [pl.BlockSpec((None, PAGE_SIZE * 2 * FOX_H, FOX_HD), page(r)) for r in range(FOX_PAGES)]
            + [pl.BlockSpec((None, PAGE_SIZE, FOX_H), page(r)) for r in range(FOX_PAGES)],
            out_specs=[per_b(MIX_W), per_b(LANES)],
            scratch_shapes=[pltpu.VMEM((FOX_H, 1, 1), F32), pltpu.VMEM((FOX_H, 1, 1), F32),
                            pltpu.VMEM((FOX_H, 1, FOX_HD), F32), pltpu.VMEM((1, LANES), F32)],
        ),
        out_shape=[jax.ShapeDtypeStruct((b, 1, MIX_W), F32), jax.ShapeDtypeStruct((b, 1, LANES), F32)],
        compiler_params=_cp("parallel", "arbitrary"),
        name="fox_decode",
    )(pt_flat, fox_new, small_new, bias_row, *([kv_cache] * FOX_PAGES), *([lf_cache] * FOX_PAGES))


def _col(row, eye):
    return jnp.sum(eye * row, axis=1, keepdims=True)


def _rec_dec_kernel(g_ref, d_ref, sm_ref, w2_ref, ba_ref, gn_ref, cp_ref, cw_ref, alog_ref, dtb_ref, nrm_ref,
                    sg_ref, sd_ref, og_ref, od_ref, sgo_ref, sdo_ref):
    sm = sm_ref[...]
    lane = _iota((8, LANES), 1)
    ga = jnp.where((lane >= SM_GA) & (lane < SM_GA + GLA_RANK), jnp.broadcast_to(sm, (8, LANES)), 0.0)
    pre = jnp.dot(ga, w2_ref[...], precision=HI, preferred_element_type=F32)[0:1] + ba_ref[...]
    loga = _log_sigmoid(pre) * (1.0 / GLA_TAU)
    eye_k = (_iota((GLA_DK, GLA_DK), 0) == _iota((GLA_DK, GLA_DK), 1)).astype(F32)
    for h in range(GLA_H):
        q = g_ref[:, h * GLA_DK:(h + 1) * GLA_DK] * (GLA_DK ** -0.5)
        k = g_ref[:, GLA_H * GLA_DK + h * GLA_DK:GLA_H * GLA_DK + (h + 1) * GLA_DK]
        v0 = 2 * GLA_H * GLA_DK + h * GLA_DV
        v = g_ref[:, v0:v0 + GLA_DV]
        gr = g_ref[:, v0 + GLA_H * GLA_DV:v0 + GLA_H * GLA_DV + GLA_DV]
        ea = jnp.exp(loga[:, h * GLA_DK:(h + 1) * GLA_DK])
        s0 = sg_ref[h]
        o = jnp.sum(q * k, axis=-1, keepdims=True) * v + jnp.sum(_col(q * ea, eye_k) * s0, axis=0, keepdims=True)
        sgo_ref[h] = _col(ea, eye_k) * s0 + _col(k, eye_k) * v
        y = o * lax.rsqrt(jnp.mean(o * o, axis=-1, keepdims=True) + RMS_EPS) * gn_ref[...]
        og_ref[:, h * GLA_DV:(h + 1) * GLA_DV] = y * jax.nn.silu(gr)
    conv = d_ref[:, :DN_QKV] * cw_ref[DN_CONV - 1:DN_CONV]
    for j in range(DN_CONV - 1):
        conv = conv + cp_ref[j:j + 1] * cw_ref[j:j + 1]
    u = jax.nn.silu(conv)
    gall = -jnp.exp(alog_ref[...]) * jax.nn.softplus(sm + dtb_ref[...])
    ball = jax.nn.sigmoid(sm)
    eye_d = (_iota((DN_DK, DN_DK), 0) == _iota((DN_DK, DN_DK), 1)).astype(F32)
    for h in range(DN_H):
        uq = u[:, h * DN_DK:(h + 1) * DN_DK]
        q = uq * lax.rsqrt(jnp.sum(uq * uq, axis=-1, keepdims=True) + 1e-6) * (DN_DK ** -0.5)
        uk = u[:, DN_H * DN_DK + h * DN_DK:DN_H * DN_DK + (h + 1) * DN_DK]
        k = uk * lax.rsqrt(jnp.sum(uk * uk, axis=-1, keepdims=True) + 1e-6)
        v = u[:, 2 * DN_H * DN_DK + h * DN_DV:2 * DN_H * DN_DK + (h + 1) * DN_DV]
        eg = jnp.exp(gall[:, SM_DA + h:SM_DA + h + 1])
        beta = ball[:, SM_DB + h:SM_DB + h + 1]
        s0 = sd_ref[h]
        kcol = _col(k, eye_d)
        ks = jnp.sum(kcol * s0, axis=0, keepdims=True)
        qs = jnp.sum(_col(q, eye_d) * s0, axis=0, keepdims=True)
        un = beta * (v - eg * ks)
        o = eg * qs + jnp.sum(q * k, axis=-1, keepdims=True) * un
        sdo_ref[h] = eg * s0 + kcol * un
        y = o * lax.rsqrt(jnp.mean(o * o, axis=-1, keepdims=True) + RMS_EPS) * nrm_ref[...]
        z = d_ref[:, DN_QKV + h * DN_DV:DN_QKV + (h + 1) * DN_DV]
        od_ref[:, h * DN_DV:(h + 1) * DN_DV] = y * jax.nn.silu(z)


def _rec_decode(gla_new, dn_new, small_new, w2p, ba_row, gn_row, conv_prev, conv_w, alog_row, dtb_row, nrm_row,
                s_gla, s_dn, row0):
    b = gla_new.shape[0]
    per_b = lambda n: pl.BlockSpec((None, 1, n), lambda bi: (bi, 0, 0))
    one = lambda a: pl.BlockSpec(a.shape, lambda bi: (0,) * a.ndim)
    sg = lambda r0: pl.BlockSpec((None, GLA_H, GLA_DK, GLA_DV), lambda bi: (r0 + bi, 0, 0, 0))
    sd = lambda r0: pl.BlockSpec((None, DN_H, DN_DK, DN_DV), lambda bi: (r0 + bi, 0, 0, 0))
    return pl.pallas_call(
        _rec_dec_kernel,
        grid=(b,),
        in_specs=[per_b(gla_new.shape[-1]), per_b(dn_new.shape[-1]), per_b(LANES), one(w2p), one(ba_row), one(gn_row),
                  pl.BlockSpec((None, DN_CONV - 1, DN_QKV), lambda bi: (row0 + bi, 0, 0)), one(conv_w), one(alog_row),
                  one(dtb_row), one(nrm_row), sg(row0), sd(row0)],
        out_specs=[per_b(MIX_W), per_b(MIX_W), sg(0), sd(0)],
        out_shape=[jax.ShapeDtypeStruct((b, 1, MIX_W), F32)] * 2
        + [jax.ShapeDtypeStruct((b,) + s_gla.shape[1:], F32), jax.ShapeDtypeStruct((b,) + s_dn.shape[1:], F32)],
        compiler_params=_cp("parallel"),
        name="rec_decode",
    )(gla_new, dn_new, small_new, w2p, ba_row, gn_row, conv_prev, conv_w, alog_row, dtb_row, nrm_row, s_gla, s_dn)


def _heads_to_rows(q_row):
    rows = [q_row[:, h * NSA_HD:(h + 1) * NSA_HD] for h in range(NSA_H)]
    return jnp.concatenate(rows + [jnp.zeros((8 - NSA_H, NSA_HD), F32)], axis=0)


CMP_PAGES = 16


def _nsa_dec_cmp_kernel(pt_ref, q_ref, *refs, nbp):
    page_refs, (kc_ref, vc_ref, oc_ref, sel_ref) = refs[:CMP_PAGES], refs[CMP_PAGES:]
    j = pl.program_id(1)
    per = PAGE_SIZE // NSA_BLOCK
    rows = CMP_PAGES * per
    dst = pl.ds(pl.multiple_of(j * rows, rows), rows)
    for kv, ref in enumerate((kc_ref, vc_ref)):
        means = [jnp.mean(r[pl.ds(kv, PAGE_SIZE, stride=2), :].reshape(per, NSA_BLOCK, NSA_HD), axis=1)
                 for r in page_refs]
        ref[dst, :] = jnp.concatenate(means, axis=0)

    @pl.when(j == pl.num_programs(1) - 1)
    def _():
        q8 = _heads_to_rows(q_ref[...])
        kc, vc = kc_ref[...], vc_ref[...]
        s = lax.dot_general(q8, kc, NT, precision=HI, preferred_element_type=F32)
        p = _masked_softmax(s, jnp.full(s.shape, True))
        oc_ref[...] = jnp.dot(p, vc, precision=HI, preferred_element_type=F32)
        imp = jnp.sum(p[0:NSA_H], axis=0, keepdims=True)
        r_i, c_i = _iota((nbp, nbp), 0), _iota((nbp, nbp), 1)
        imp_col = _col(imp, (r_i == c_i).astype(F32))
        beats = (imp_col > imp) | ((imp_col == imp) & (r_i < c_i))
        rank = jnp.sum(beats.astype(jnp.int32), axis=0, keepdims=True)
        ids = jnp.where(rank == _iota((16, nbp), 0), _iota((16, nbp), 1), 0)
        sel_ref[...] = jnp.broadcast_to(jnp.sum(ids, axis=1, keepdims=True), (16, LANES))


def _nsa_dec_cmp(pt_flat, n_pages, page0, q_new, cmp_cache):
    b = q_new.shape[0]
    nbp = n_pages * (PAGE_SIZE // NSA_BLOCK)
    n_sel = NSA_TOPK - 1
    assert nbp >= n_sel and NSA_TOPK <= 16 and n_pages % CMP_PAGES == 0
    page = lambda r: pl.BlockSpec((None, 2 * PAGE_SIZE, NSA_HD),
                                  lambda bi, j, pt: (page0 + pt[bi * n_pages + j * CMP_PAGES + r], 0, 0))
    mean_spec = pl.BlockSpec((None, nbp, NSA_HD), lambda bi, j, pt: (bi, 0, 0))
    return pl.pallas_call(
        functools.partial(_nsa_dec_cmp_kernel, nbp=nbp),
        grid_spec=pltpu.PrefetchScalarGridSpec(
            num_scalar_prefetch=1,
            grid=(b, n_pages // CMP_PAGES),
            in_specs=[pl.BlockSpec((None, 1, MIX_W), lambda bi, j, pt: (bi, 0, 0))]
            + [page(r) for r in range(CMP_PAGES)],
            out_specs=[mean_spec, mean_spec,
                       pl.BlockSpec((None, 8, NSA_HD), lambda bi, j, pt: (bi, 0, 0)),
                       pl.BlockSpec((None, 16, LANES), lambda bi, j, pt: (bi, 0, 0))],
        ),
        out_shape=[jax.ShapeDtypeStruct((b, nbp, NSA_HD), F32)] * 2
        + [jax.ShapeDtypeStruct((b, 8, NSA_HD), F32), jax.ShapeDtypeStruct((b, 16, LANES), jnp.int32)],
        compiler_params=_cp("parallel", "arbitrary"),
        name="nsa_dec_cmp",
    )(pt_flat, q_new, *([cmp_cache] * CMP_PAGES))


def _nsa_dec_attn_kernel(pt_ref, sel_ref, q_ref, blk_ref, snew_ref, win_ref, wnew_ref, oc_ref, sm_ref, o_ref,
                         m_sc, l_sc, acc_sc, *, sw):
    s_i = pl.program_id(1)
    q8 = _heads_to_rows(q_ref[...])

    @pl.when(s_i == 0)
    def _():
        m_sc[...] = jnp.sum(q8 * snew_ref[:, :NSA_HD], axis=-1, keepdims=True)
        l_sc[...] = jnp.ones_like(l_sc)
        acc_sc[...] = jnp.broadcast_to(snew_ref[:, NSA_HD:], acc_sc.shape)

    k, v = blk_ref[pl.ds(0, NSA_BLOCK, stride=2), :], blk_ref[pl.ds(1, NSA_BLOCK, stride=2), :]
    s = lax.dot_general(q8, k, NT, precision=HI, preferred_element_type=F32)
    m_prev = m_sc[...]
    m_new = jnp.maximum(m_prev, jnp.max(s, axis=-1, keepdims=True))
    alpha = jnp.exp(m_prev - m_new)
    p = jnp.exp(s - m_new)
    l_sc[...] = alpha * l_sc[...] + jnp.sum(p, axis=-1, keepdims=True)
    acc_sc[...] = alpha * acc_sc[...] + jnp.dot(p, v, precision=HI, preferred_element_type=F32)
    m_sc[...] = m_new

    @pl.when(s_i == pl.num_programs(1) - 1)
    def _():
        o_s = acc_sc[...] / l_sc[...]
        kw, vw = win_ref[pl.ds(0, sw, stride=2), :], win_ref[pl.ds(1, sw, stride=2), :]
        sw_ = lax.dot_general(q8, kw, NT, precision=HI, preferred_element_type=F32)
        wmask = (sw - _iota((8, sw), 1)) < NSA_WINDOW
        s_new = jnp.sum(q8 * wnew_ref[:, :NSA_HD], axis=-1, keepdims=True)
        m = jnp.maximum(jnp.max(jnp.where(wmask, sw_, -jnp.inf), axis=-1, keepdims=True), s_new)
        pw = jnp.where(wmask, jnp.exp(sw_ - m), 0.0)
        pn = jnp.exp(s_new - m)
        o_w = (jnp.dot(pw, vw, precision=HI, preferred_element_type=F32) + pn * wnew_ref[:, NSA_HD:]) / (
            jnp.sum(pw, axis=-1, keepdims=True) + pn)
        gate = jax.nn.sigmoid(sm_ref[...])
        o_c = oc_ref[...]
        for h in range(NSA_H):
            g0 = SM_NG + 3 * h
            o_ref[:, h * NSA_HD:(h + 1) * NSA_HD] = (
                gate[:, g0:g0 + 1] * o_c[h:h + 1] + gate[:, g0 + 1:g0 + 2] * o_s[h:h + 1]
                + gate[:, g0 + 2:g0 + 3] * o_w[h:h + 1])


def _nsa_dec_attn(pt_flat, sel_flat, n_pages, n_sel, page0, row0, q_new, slc_cache, slc_new, win_cache, win_new,
                  o_c, small_new):
    b = q_new.shape[0]
    sw = win_cache.shape[1] // 2
    per = PAGE_SIZE // NSA_BLOCK
    half = slc_cache.reshape(slc_cache.shape[0] * per, 2 * NSA_BLOCK, NSA_HD)

    def blk_map(bi, s, pt, sel):
        n = sel[bi * n_sel + s]
        return ((page0 + pt[bi * n_pages + n // per]) * per + n % per, 0, 0)

    per_b = lambda r, n: pl.BlockSpec((None, r, n), lambda bi, s, pt, sel: (bi, 0, 0))
    return pl.pallas_call(
        functools.partial(_nsa_dec_attn_kernel, sw=sw),
        grid_spec=pltpu.PrefetchScalarGridSpec(
            num_scalar_prefetch=2,
            grid=(b, n_sel),
            in_specs=[per_b(1, MIX_W), pl.BlockSpec((None, 2 * NSA_BLOCK, NSA_HD), blk_map), per_b(1, 2 * NSA_HD),
                      pl.BlockSpec((None, 2 * sw, NSA_HD), lambda bi, s, pt, sel: (row0 + bi, 0, 0)),
                      per_b(1, 2 * NSA_HD), per_b(8, NSA_HD), per_b(1, LANES)],
            out_specs=per_b(1, MIX_W),
            scratch_shapes=[pltpu.VMEM((8, 1), F32), pltpu.VMEM((8, 1), F32), pltpu.VMEM((8, NSA_HD), F32)],
        ),
        out_shape=jax.ShapeDtypeStruct((b, 1, MIX_W), F32),
        compiler_params=_cp("parallel", "arbitrary"),
        name="nsa_dec_attn",
    )(pt_flat, sel_flat, q_new, half, slc_new, win_cache, win_new, o_c, small_new)


def _lane_row(vals, at):
    return jnp.zeros((1, LANES), F32).at[0, at:at + vals.shape[0]].set(vals.astype(F32))


def _layer_weights(w_in, fox_b_f, gla_w_a2, gla_b_a, gla_norm, dn_conv_w, dn_a_log, dn_dt_bias, dn_norm,
                   w_branch, w_out, w_up, w_down):
    offs = [0]
    for s in SPLIT_SIZES:
        offs.append(offs[-1] + s)
    seg = lambda i: w_in[:, offs[i]:offs[i + 1]]
    (fq, fk, fv, ff, gq, gk, gv, ga, gr, nq, nkc, nks, nkw, ng, dqkv, da, dbeta, dz, mg) = [seg(i) for i in range(19)]
    cat = lambda parts: jnp.concatenate(parts, axis=1).astype(BF16)
    small = [ff, ga, ng, da, dbeta]
    pad = LANES - sum(p.shape[1] for p in small)
    w = dict(
        fox=cat([fq, fk, fv]), gla=cat([gq, gk, gv, gr]), nsa=cat([nq, nkc, nks, nkw]), dn=cat([dqkv, dz]),
        small=cat(small + [jnp.zeros((D_MODEL, pad), F32)]), mg=mg.astype(BF16),
        bf_row=_lane_row(fox_b_f, SM_FF),
        w2p=jnp.zeros((LANES, GLA_H * GLA_DK), F32).at[SM_GA:SM_GA + GLA_RANK].set(gla_w_a2),
        ba_row=gla_b_a.reshape(1, -1), gn_row=gla_norm.reshape(1, -1), conv_w=dn_conv_w,
        alog_row=_lane_row(dn_a_log, SM_DA), dtb_row=_lane_row(dn_dt_bias, SM_DA), dnn_row=dn_norm.reshape(1, -1),
        wb=w_branch.astype(BF16), w_out=w_out.astype(BF16), w_up=w_up.astype(BF16), w_down=w_down.astype(BF16),
    )
    return w


def _rope_tables(pos):
    half = ROPE_DIM // 2
    inv = ROPE_THETA ** (-jnp.arange(half, dtype=F32) / half)
    ang = pos.astype(F32)[:, None] * inv[None, :]
    cos, sin = jnp.cos(ang), jnp.sin(ang)
    t = pos.shape[0]
    z = lambda n: jnp.zeros((t, n), F32)
    return (jnp.concatenate([cos, cos, jnp.ones((t, LANES - ROPE_DIM), F32)], axis=1),
            jnp.concatenate([-sin, z(LANES - half)], axis=1),
            jnp.concatenate([z(half), sin, z(LANES - ROPE_DIM)], axis=1))


def _project(h, w):
    return {name: _mm(h, w[name]) for name in ("fox", "gla", "nsa", "dn", "small", "mg")}


def _mixer_prompt(h, b, t, w):
    p = _project(h, w)
    fox, gla, nsa, dn, small = (p[n].reshape(b, t, -1) for n in ("fox", "gla", "nsa", "dn", "small"))
    lf, c, ct = _fox_prep(small, w["bf_row"])
    o_fox = _fox_attn(fox, c, ct)
    o_gla, s_gla = _gla(gla, small, w["w2p"], w["ba_row"], w["gn_row"], jnp.zeros((b, GLA_H, GLA_DK, GLA_DV), F32))
    qr, cmp_kv, slc_kv, win_kv, cmean = _nsa_prep(nsa, *_rope_tables(jnp.arange(t)), True)
    o_nsa = _nsa_attn(qr, small, cmean, slc_kv, win_kv)
    qd, kd, vd, gsm, btsm = _dn_prep(dn, jnp.zeros((b, 8, DN_QKV), F32), w["conv_w"], small, w["alog_row"],
                                     w["dtb_row"])
    o_dn, s_dn = _gdn(qd, kd, vd, gsm, btsm, dn, w["dnn_row"], jnp.zeros((b, DN_H, DN_DK, DN_DV), F32))
    mix = _branch_gate([o.reshape(b * t, MIX_W) for o in (o_fox, o_gla, o_nsa, o_dn)], w["wb"], p["mg"])
    kv5 = lambda a: a.reshape(b, t, 2, 1, NSA_HD)
    wl = min(NSA_WINDOW, t)
    state = (fox[:, :, MIX_W:].reshape(b, t, 2, FOX_H, FOX_HD), lf[:, :, :FOX_H], kv5(cmp_kv), kv5(slc_kv),
             kv5(win_kv)[:, t - wl:], s_gla, s_dn, dn[:, t - (DN_CONV - 1):, :DN_QKV])
    return mix, state


def _mixer_sample(h, caches, l, pt_flat, n_pages, w):
    fox_kv_c, fox_lf_c, cmp_c, slc_c, win_c, s_gla_c, s_dn_c, conv_c = caches
    b = h.shape[0]
    past_len = n_pages * PAGE_SIZE
    p = _project(h, w)
    fox, gla, dn, small = (p[n].reshape(b, 1, -1) for n in ("fox", "gla", "dn", "small"))
    depth, n_pool = fox_kv_c.shape[:2]
    page0, row0 = l * n_pool, l * b
    o_fox, lf_new = _fox_decode(pt_flat, n_pages, page0, fox, small, w["bf_row"],
                                fox_kv_c.reshape(depth * n_pool, PAGE_SIZE * 2 * FOX_H, FOX_HD),
                                fox_lf_c.reshape(depth * n_pool, PAGE_SIZE, FOX_H))
    o_gla, o_dn, s_gla, s_dn = _rec_decode(
        gla, dn, small, w["w2p"], w["ba_row"], w["gn_row"], conv_c.reshape((depth * b,) + conv_c.shape[2:]),
        w["conv_w"], w["alog_row"], w["dtb_row"], w["dnn_row"], s_gla_c.reshape((depth * b,) + s_gla_c.shape[2:]),
        s_dn_c.reshape((depth * b,) + s_dn_c.shape[2:]), row0)
    tabs = _rope_tables(jnp.full((b,), past_len, jnp.int32))
    qr, cmp_new, slc_new, win_new = _nsa_prep(p["nsa"].reshape(1, b, -1), *tabs, False)
    qr, cmp_new, slc_new, win_new = (a.reshape(b, 1, -1) for a in (qr, cmp_new, slc_new, win_new))
    paged = lambda c: c.reshape(depth * n_pool, 2 * PAGE_SIZE, NSA_HD)
    _, _, o_c, sel = _nsa_dec_cmp(pt_flat, n_pages, page0, qr, paged(cmp_c))
    n_sel = NSA_TOPK - 1
    sel_flat = sel[:, :n_sel, 0].reshape(-1)
    sw = win_c.shape[2]
    o_nsa = _nsa_dec_attn(pt_flat, sel_flat, n_pages, n_sel, page0, row0, qr, paged(slc_c), slc_new,
                          win_c.reshape(depth * b, 2 * sw, NSA_HD), win_new, o_c, small)
    mix = _branch_gate([o.reshape(b, MIX_W).astype(BF16) for o in (o_fox, o_gla, o_nsa, o_dn)], w["wb"], p["mg"])
    kv5 = lambda a: a.reshape(b, 1, 2, 1, NSA_HD)
    wl = min(NSA_WINDOW, sw + 1)
    new_win = jnp.concatenate([win_c[l][:, sw + 1 - wl:], kv5(win_new)], axis=1)
    new_conv = jnp.concatenate([conv_c[l][:, 1:], dn[:, :, :DN_QKV]], axis=1)
    state = (fox[:, :, MIX_W:].reshape(b, 1, 2, FOX_H, FOX_HD), lf_new[:, :, :FOX_H], kv5(cmp_new), kv5(slc_new),
             new_win, s_gla, s_dn, new_conv)
    return mix, state


def _trunk_layer(x, mixer, norms, w):
    g_pre_mix, g_post_mix, g_pre_mlp, g_post_mlp = norms
    mix, state = mixer(_rmsnorm_cast(x, g_pre_mix))
    x = _mm_norm_res(mix, w["w_out"], g_post_mix, x)
    hid = _mm(_rmsnorm_cast(x, g_pre_mlp), w["w_up"], BF16, "relu2")
    x = _mm_norm_res(hid, w["w_down"], g_post_mlp, x)
    return x, state


def kernel(x_prompt, x_sample, cache_fox_kv, cache_fox_logf, cache_nsa_cmp_kv, cache_nsa_slc_kv, cache_nsa_win_kv,
           state_gla, state_dn, state_dn_conv, page_table, norm_pre_mix, norm_post_mix, norm_pre_mlp, norm_post_mlp,
           w_in, fox_b_f, gla_w_a2, gla_b_a, gla_norm, dn_conv_w, dn_a_log, dn_dt_bias, dn_norm, w_branch, w_out,
           w_up, w_down):
    bp, tp, _ = x_prompt.shape
    bs, ts, _ = x_sample.shape
    assert ts == 1
    n_pages = page_table.shape[1]
    pt_flat = page_table.reshape(-1).astype(jnp.int32)
    y_p = x_prompt.reshape(bp * tp, D_MODEL)
    y_s = x_sample.reshape(bs, D_MODEL)
    new_p = [[] for _ in range(8)]
    new_s = [[] for _ in range(8)]
    caches = (cache_fox_kv, cache_fox_logf, cache_nsa_cmp_kv, cache_nsa_slc_kv, cache_nsa_win_kv, state_gla, state_dn,
              state_dn_conv)
    for l in range(DEPTH):
        w = _layer_weights(w_in[l], fox_b_f[l], gla_w_a2[l], gla_b_a[l], gla_norm[l], dn_conv_w[l], dn_a_log[l],
                           dn_dt_bias[l], dn_norm[l], w_branch[l], w_out[l], w_up[l], w_down[l])
        norms = (norm_pre_mix[l], norm_post_mix[l], norm_pre_mlp[l], norm_post_mlp[l])
        y_p, st_p = _trunk_layer(y_p, lambda h: _mixer_prompt(h, bp, tp, w), norms, w)
        y_s, st_s = _trunk_layer(y_s, lambda h: _mixer_sample(h, caches, l, pt_flat, n_pages, w), norms, w)
        for i in range(8):
            new_p[i].append(st_p[i])
            new_s[i].append(st_s[i])
    fox_kv_p, fox_logf_p, cmp_kv_p, slc_kv_p, win_kv_p, gla_p, dn_p, conv_p = [jnp.stack(a) for a in new_p]
    fox_kv_s, fox_logf_s, cmp_kv_s, slc_kv_s, win_kv_s, gla_s, dn_s, conv_s = [jnp.stack(a) for a in new_s]
    return (y_p.reshape(bp, tp, D_MODEL), y_s.reshape(bs, ts, D_MODEL), fox_kv_p, fox_kv_s, fox_logf_p, fox_logf_s,
            cmp_kv_p, cmp_kv_s, slc_kv_p, slc_kv_s, win_kv_p, win_kv_s, gla_p, gla_s, dn_p, dn_s, conv_p, conv_s)
```

```python
import functools

import jax
import jax.numpy as jnp
from jax import lax
from jax.experimental import pallas as pl
from jax.experimental.pallas import tpu as pltpu

F32 = jnp.float32
BF16 = jnp.bfloat16
HI = lax.Precision.HIGHEST
NT = (((1,), (1,)), ((), ()))
TN = (((0,), (0,)), ((), ()))
NEG = -1e30

D_MODEL = 2048
DEPTH = 2
PAGE_SIZE = 128
N_BRANCH = 4
MIX_W = D_MODEL // 4
FOX_H = 4
FOX_HD = MIX_W // FOX_H
GLA_H = 4
GLA_DK = MIX_W // (2 * GLA_H)
GLA_DV = MIX_W // GLA_H
GLA_RANK = 16
GLA_TAU = 16.0
NSA_H = 4
NSA_HD = MIX_W // NSA_H
NSA_BLOCK = 64
NSA_TOPK = 16
NSA_WINDOW = 512
DN_H = 4
DN_DK = MIX_W // DN_H
DN_DV = MIX_W // DN_H
DN_CONV = 4
DN_QKV = DN_H * (2 * DN_DK + DN_DV)
ROPE_DIM = NSA_HD // 4
ROPE_THETA = 500000.0
D_FF = 4 * D_MODEL
CHUNK = 64
SUB = 16
RMS_EPS = 1e-6
LANES = 128
SPLIT_SIZES = (
    FOX_H * FOX_HD, FOX_H * FOX_HD, FOX_H * FOX_HD, FOX_H,
    GLA_H * GLA_DK, GLA_H * GLA_DK, GLA_H * GLA_DV, GLA_RANK, GLA_H * GLA_DV,
    NSA_H * NSA_HD, 2 * NSA_HD, 2 * NSA_HD, 2 * NSA_HD, 3 * NSA_H,
    DN_QKV, DN_H, DN_H, DN_H * DN_DV,
    N_BRANCH * D_MODEL,
)
SM_FF, SM_GA, SM_NG, SM_DA, SM_DB = 0, 4, 20, 32, 36
VMEM_LIMIT = 56 * 1024 * 1024


def _cp(*sem):
    return pltpu.CompilerParams(dimension_semantics=sem, vmem_limit_bytes=VMEM_LIMIT)


def _pick(n, cap):
    if n <= cap:
        return n
    best = None
    for t in range(LANES, cap + 1, LANES):
        if n % t == 0:
            best = t
    assert best is not None, n
    return best


def _log_sigmoid(z):
    return jnp.minimum(z, 0.0) - jnp.log1p(jnp.exp(-jnp.abs(z)))


def _iota(shape, dim):
    return lax.broadcasted_iota(jnp.int32, shape, dim)


def _dot3(a, b, dims=(((1,), (0,)), ((), ()))):
    ah, bh = a.astype(BF16), b.astype(BF16)
    al, bl = (a - ah.astype(F32)).astype(BF16), (b - bh.astype(F32)).astype(BF16)
    f = lambda x, y: lax.dot_general(x, y, dims, preferred_element_type=F32)
    return f(ah, bh) + (f(ah, bl) + f(al, bh))


def _rmsnorm_cast_kernel(x_ref, g_ref, o_ref):
    x = x_ref[...]
    y = x * lax.rsqrt(jnp.mean(x * x, axis=-1, keepdims=True) + RMS_EPS)
    o_ref[...] = (y * g_ref[...]).astype(o_ref.dtype)


def _rmsnorm_cast(x, g):
    m, d = x.shape
    tm = min(m, 512)
    return pl.pallas_call(
        _rmsnorm_cast_kernel,
        grid=(m // tm,),
        in_specs=[pl.BlockSpec((tm, d), lambda i: (i, 0)), pl.BlockSpec((1, d), lambda i: (0, 0))],
        out_specs=pl.BlockSpec((tm, d), lambda i: (i, 0)),
        out_shape=jax.ShapeDtypeStruct((m, d), BF16),
        compiler_params=_cp("parallel"),
        name="rmsnorm_cast",
    )(x, g.reshape(1, d))


def _mm_kernel(a_ref, w_ref, o_ref, *, act):
    y = jnp.dot(a_ref[...], w_ref[...], preferred_element_type=F32)
    if act == "relu2":
        y = jnp.square(jnp.maximum(y, 0.0))
    o_ref[...] = y.astype(o_ref.dtype)


def _mm(a, w, l, col0=0, n=None, out_dtype=F32, act=None):
    m, k = a.shape
    n = w.shape[2] if n is None else n
    tm = min(m, 1024)
    tn = max(t for t in range(LANES, min(n, 2048) + 1, LANES) if n % t == 0 and col0 % t == 0)
    c0 = col0 // tn
    return pl.pallas_call(
        functools.partial(_mm_kernel, act=act),
        grid=(n // tn, m // tm),
        in_specs=[pl.BlockSpec((tm, k), lambda j, i: (i, 0)), pl.BlockSpec((None, k, tn), lambda j, i: (l, 0, c0 + j))],
        out_specs=pl.BlockSpec((tm, tn), lambda j, i: (i, j)),
        out_shape=jax.ShapeDtypeStruct((m, n), out_dtype),
        compiler_params=_cp("parallel", "parallel"),
        name="mm",
    )(a, w)


def _mm_norm_res_kernel(a_ref, w_ref, g_ref, x_ref, o_ref, acc_ref):
    k = pl.program_id(1)

    @pl.when(k == 0)
    def _():
        acc_ref[...] = jnp.zeros_like(acc_ref)

    acc_ref[...] += jnp.dot(a_ref[...], w_ref[...], preferred_element_type=F32)

    @pl.when(k == pl.num_programs(1) - 1)
    def _():
        y = acc_ref[...]
        y = y * lax.rsqrt(jnp.mean(y * y, axis=-1, keepdims=True) + RMS_EPS)
        o_ref[...] = x_ref[...] + y * g_ref[...]


def _mm_norm_res(a, w, l, g, x):
    m, k = a.shape
    n = w.shape[2]
    tm = min(m, 512)
    tk = _pick(k, 2048)
    return pl.pallas_call(
        _mm_norm_res_kernel,
        grid=(m // tm, k // tk),
        in_specs=[
            pl.BlockSpec((tm, tk), lambda i, kk: (i, kk)),
            pl.BlockSpec((None, tk, n), lambda i, kk: (l, kk, 0)),
            pl.BlockSpec((1, n), lambda i, kk: (0, 0)),
            pl.BlockSpec((tm, n), lambda i, kk: (i, 0)),
        ],
        out_specs=pl.BlockSpec((tm, n), lambda i, kk: (i, 0)),
        out_shape=jax.ShapeDtypeStruct((m, n), F32),
        scratch_shapes=[pltpu.VMEM((tm, n), F32)],
        compiler_params=_cp("parallel", "arbitrary"),
        name="mm_norm_res",
    )(a, w, g.reshape(1, n), x)


def _branch_gate_kernel(o0, o1, o2, o3, wb_ref, m0, m1, m2, m3, out_ref):
    acc = None
    for n, (o_n, m_n) in enumerate(((o0, m0), (o1, m1), (o2, m2), (o3, m3))):
        y = jnp.dot(o_n[...], wb_ref[n], preferred_element_type=F32)
        t = jax.nn.sigmoid(m_n[...]) * y
        acc = t if acc is None else acc + t
    out_ref[...] = acc.astype(out_ref.dtype)


def _branch_gate(branches, wb, l, mg):
    m = mg.shape[0]
    tm = min(m, 512)
    tn = 1024
    nj = D_MODEL // tn
    o_spec = pl.BlockSpec((tm, MIX_W), lambda j, i: (i, 0))
    m_specs = [pl.BlockSpec((tm, tn), functools.partial(lambda j, i, n: (i, n * nj + j), n=n)) for n in range(N_BRANCH)]
    return pl.pallas_call(
        _branch_gate_kernel,
        grid=(nj, m // tm),
        in_specs=[o_spec] * 4 + [pl.BlockSpec((None, N_BRANCH, MIX_W, tn), lambda j, i: (l, 0, 0, j))] + m_specs,
        out_specs=pl.BlockSpec((tm, tn), lambda j, i: (i, j)),
        out_shape=jax.ShapeDtypeStruct((m, D_MODEL), BF16),
        compiler_params=_cp("parallel", "parallel"),
        name="branch_gate",
    )(*branches, wb, mg, mg, mg, mg)


def _fox_prep_kernel(s_ref, b_ref, k_ref, v_ref, lf_ref, c_ref, ct_ref, kv_ref, carry_ref, *, tb):
    @pl.when(pl.program_id(1) == 0)
    def _():
        carry_ref[...] = jnp.zeros_like(carry_ref)

    for i, ref in enumerate((k_ref, v_ref)):
        for h in range(FOX_H):
            kv_ref[pl.ds(i * FOX_H + h, tb, stride=2 * FOX_H), :] = ref[:, h * FOX_HD:(h + 1) * FOX_HD]

    lf = _log_sigmoid(s_ref[...] + b_ref[...])
    tri = (_iota((tb, tb), 1) <= _iota((tb, tb), 0)).astype(F32)
    c = jnp.dot(tri, lf, precision=HI, preferred_element_type=F32) + carry_ref[...]
    lf_ref[...] = lf
    c_ref[...] = c
    ct_ref[...] = c.T[:8]
    carry_ref[...] = c[tb - 1:tb]


def _fox_prep(small, bias_row, fox):
    b, t, _ = small.shape
    tb = min(t, 256)
    blk = pl.BlockSpec((None, tb, LANES), lambda bi, i: (bi, i, 0))
    rows = 2 * FOX_H
    return pl.pallas_call(
        functools.partial(_fox_prep_kernel, tb=tb),
        grid=(b, t // tb),
        in_specs=[blk, pl.BlockSpec((1, LANES), lambda bi, i: (0, 0)),
                  pl.BlockSpec((None, tb, MIX_W), lambda bi, i: (bi, i, 1)),
                  pl.BlockSpec((None, tb, MIX_W), lambda bi, i: (bi, i, 2))],
        out_specs=[blk, blk, pl.BlockSpec((None, 8, tb), lambda bi, i: (bi, 0, i)),
                   pl.BlockSpec((None, tb * rows, FOX_HD), lambda bi, i: (bi, i, 0))],
        out_shape=[jax.ShapeDtypeStruct((b, t, LANES), F32)] * 2 + [jax.ShapeDtypeStruct((b, 8, t), F32),
                                                                  jax.ShapeDtypeStruct((b, t * rows, FOX_HD), F32)],
        scratch_shapes=[pltpu.VMEM((1, LANES), F32)],
        compiler_params=_cp("parallel", "arbitrary"),
        name="fox_prep",
    )(small, bias_row, fox, fox)


def _fox_attn_kernel(q_ref, k_ref, v_ref, cq_ref, ck_ref, o_ref, m_sc, l_sc, acc_sc, *, tq, tk):
    qi = pl.program_id(1)
    ki = pl.program_id(2)

    @pl.when(ki == 0)
    def _():
        m_sc[...] = jnp.full_like(m_sc, NEG)
        l_sc[...] = jnp.zeros_like(l_sc)
        acc_sc[...] = jnp.zeros_like(acc_sc)

    def step(diagonal):
        mask = _iota((tk, tq), 0) <= _iota((tk, tq), 1)
        heads = range(FOX_H)
        hsl = [slice(h * FOX_HD, (h + 1) * FOX_HD) for h in heads]
        ss = []
        for h in heads:
            q = (q_ref[:, hsl[h]] * (FOX_HD ** -0.5)).astype(BF16)
            s = lax.dot_general(k_ref[:, hsl[h]].astype(BF16), q, NT, preferred_element_type=F32)
            s = s + cq_ref[h:h + 1, :] - ck_ref[:, h:h + 1]
            ss.append(jnp.where(mask, s, NEG) if diagonal else s)
        m_prev = [m_sc[h] for h in heads]
        m_new = [jnp.maximum(m_prev[h], jnp.max(ss[h], axis=0, keepdims=True)) for h in heads]
        ps = []
        for h in heads:
            p = jnp.exp(ss[h] - m_new[h])
            ps.append(jnp.where(mask, p, 0.0) if diagonal else p)
        pvs = [jnp.dot(v_ref[:, hsl[h]].T.astype(BF16), ps[h].astype(BF16), preferred_element_type=F32)
               for h in heads]
        for h in heads:
            alpha = jnp.exp(m_prev[h] - m_new[h])
            l_sc[h] = alpha * l_sc[h] + jnp.sum(ps[h], axis=0, keepdims=True)
            acc_sc[h] = alpha * acc_sc[h] + pvs[h]
            m_sc[h] = m_new[h]

    pl.when(ki < qi)(functools.partial(step, False))
    pl.when(ki == qi)(functools.partial(step, True))

    @pl.when(ki == pl.num_programs(2) - 1)
    def _():
        for h in range(FOX_H):
            o_ref[:, h * FOX_HD:(h + 1) * FOX_HD] = (acc_sc[h] / l_sc[h]).T.astype(o_ref.dtype)


def _fox_attn(fox, c, ct):
    b, t, _ = fox.shape
    tq = tk = min(t, 512)

    def kmap(col):
        return lambda bi, qi, ki: (bi, jnp.minimum(ki, (qi * tq + tq - 1) // tk), col)

    return pl.pallas_call(
        functools.partial(_fox_attn_kernel, tq=tq, tk=tk),
        grid=(b, t // tq, t // tk),
        in_specs=[
            pl.BlockSpec((None, tq, MIX_W), lambda bi, qi, ki: (bi, qi, 0)),
            pl.BlockSpec((None, tk, MIX_W), kmap(1)),
            pl.BlockSpec((None, tk, MIX_W), kmap(2)),
            pl.BlockSpec((None, 8, tq), lambda bi, qi, ki: (bi, 0, qi)),
            pl.BlockSpec((None, tk, LANES), kmap(0)),
        ],
        out_specs=pl.BlockSpec((None, tq, MIX_W), lambda bi, qi, ki: (bi, qi, 0)),
        out_shape=jax.ShapeDtypeStruct((b, t, MIX_W), BF16),
        scratch_shapes=[pltpu.VMEM((FOX_H, 1, tq), F32), pltpu.VMEM((FOX_H, 1, tq), F32),
                        pltpu.VMEM((FOX_H, FOX_HD, tq), F32)],
        compiler_params=_cp("parallel", "parallel", "arbitrary"),
        name="fox_attn",
    )(fox, fox, fox, ct, c)


def _gla_kernel(g_ref, sm_ref, w2_ref, ba_ref, gn_ref, s0_ref, o_ref, sout_ref, s_sc, *, C):
    c = pl.program_id(1)

    @pl.when(c == 0)
    def _():
        s_sc[...] = s0_ref[...]

    lane = _iota((C, LANES), 1)
    ga = jnp.where((lane >= SM_GA) & (lane < SM_GA + GLA_RANK), sm_ref[...], 0.0)
    pre = jnp.dot(ga, w2_ref[...], precision=HI, preferred_element_type=F32) + ba_ref[...]
    loga = _log_sigmoid(pre) * (1.0 / GLA_TAU)
    tri = (_iota((C, C), 1) <= _iota((C, C), 0)).astype(F32)
    b_all = jnp.dot(tri, loga, precision=HI, preferred_element_type=F32)
    eye = (_iota((GLA_DK, GLA_DK), 0) == _iota((GLA_DK, GLA_DK), 1)).astype(F32)
    heads, blocks = range(GLA_H), range(C // SUB)
    kofs, vofs = GLA_H * GLA_DK, 2 * GLA_H * GLA_DK
    qs = [g_ref[:, h * GLA_DK:(h + 1) * GLA_DK] * (GLA_DK ** -0.5) for h in heads]
    ks = [g_ref[:, kofs + h * GLA_DK:kofs + (h + 1) * GLA_DK] for h in heads]
    vbs = [g_ref[:, vofs + h * GLA_DV:vofs + (h + 1) * GLA_DV].astype(BF16) for h in heads]
    bhs = [b_all[:, h * GLA_DK:(h + 1) * GLA_DK] for h in heads]
    states = [s_sc[h] for h in heads]
    o_inter = [jnp.dot((qs[h] * jnp.exp(bhs[h])).astype(BF16), states[h].astype(BF16), preferred_element_type=F32)
               for h in heads]
    atts = {}
    for h in heads:
        for ib in blocks[1:]:
            a0 = ib * SUB
            bi = bhs[h][a0:a0 + SUB]
            r = bi[0:1]
            qe = (qs[h][a0:a0 + SUB] * jnp.exp(bi - r)).astype(BF16)
            ke = (ks[h][:a0] * jnp.exp(r - bhs[h][:a0])).astype(BF16)
            atts[h, ib] = lax.dot_general(qe, ke, NT, preferred_element_type=F32)
    pair = 2 * GLA_DK
    mask3 = _iota((SUB, SUB, pair), 1) <= _iota((SUB, SUB, pair), 0)
    low = _iota((SUB, SUB, pair), 2) < GLA_DK
    ds = {}
    for hp in range(GLA_H // 2):
        q2 = g_ref[:, hp * pair:(hp + 1) * pair] * (GLA_DK ** -0.5)
        k2 = g_ref[:, kofs + hp * pair:kofs + (hp + 1) * pair]
        b2 = b_all[:, hp * pair:(hp + 1) * pair]
        for ib in blocks:
            rs = slice(ib * SUB, (ib + 1) * SUB)
            bi = b2[rs]
            diff = bi[:, None, :] - bi[None, :, :]
            e = jnp.where(mask3, jnp.exp(jnp.where(mask3, diff, 0.0)), 0.0)
            prod = q2[rs][:, None, :] * k2[rs][None, :, :] * e
            ds[2 * hp, ib] = jnp.sum(jnp.where(low, prod, 0.0), axis=-1)
            ds[2 * hp + 1, ib] = jnp.sum(jnp.where(low, 0.0, prod), axis=-1)
    rows = {}
    for h in heads:
        for ib in blocks:
            a0 = ib * SUB
            o_i = jnp.dot(ds[h, ib].astype(BF16), vbs[h][a0:a0 + SUB], preferred_element_type=F32)
            if ib > 0:
                o_i = o_i + jnp.dot(atts[h, ib].astype(BF16), vbs[h][:a0], preferred_element_type=F32)
            rows[h, ib] = o_i
    for h in heads:
        bend = bhs[h][C - 1:C]
        kdec = (ks[h] * jnp.exp(bend - bhs[h])).astype(BF16)
        dcol = jnp.sum(eye * jnp.exp(bend), axis=1, keepdims=True)
        s_sc[h] = dcol * states[h] + lax.dot_general(kdec, vbs[h], TN, preferred_element_type=F32)
    for h in heads:
        o = jnp.concatenate([rows[h, ib] for ib in blocks], axis=0) + o_inter[h]
        y = o * lax.rsqrt(jnp.mean(o * o, axis=-1, keepdims=True) + RMS_EPS) * gn_ref[...]
        gr = g_ref[:, vofs + GLA_H * GLA_DV + h * GLA_DV:vofs + GLA_H * GLA_DV + (h + 1) * GLA_DV]
        o_ref[:, h * GLA_DV:(h + 1) * GLA_DV] = (y * jax.nn.silu(gr)).astype(o_ref.dtype)

    @pl.when(c == pl.num_programs(1) - 1)
    def _():
        sout_ref[...] = s_sc[...]


def _gla(gla, small, w2p, ba_row, gn_row, s0):
    b, t, _ = gla.shape
    C = CHUNK
    n_gla = gla.shape[-1]
    return pl.pallas_call(
        functools.partial(_gla_kernel, C=C),
        grid=(b, t // C),
        in_specs=[
            pl.BlockSpec((None, C, n_gla), lambda bi, c: (bi, c, 0)),
            pl.BlockSpec((None, C, LANES), lambda bi, c: (bi, c, 0)),
            pl.BlockSpec(w2p.shape, lambda bi, c: (0, 0)),
            pl.BlockSpec(ba_row.shape, lambda bi, c: (0, 0)),
            pl.BlockSpec(gn_row.shape, lambda bi, c: (0, 0)),
            pl.BlockSpec((None, GLA_H, GLA_DK, GLA_DV), lambda bi, c: (bi, 0, 0, 0)),
        ],
        out_specs=[
            pl.BlockSpec((None, C, MIX_W), lambda bi, c: (bi, c, 0)),
            pl.BlockSpec((None, GLA_H, GLA_DK, GLA_DV), lambda bi, c: (bi, 0, 0, 0)),
        ],
        out_shape=[jax.ShapeDtypeStruct((b, t, MIX_W), BF16), jax.ShapeDtypeStruct(s0.shape, F32)],
        scratch_shapes=[pltpu.VMEM((GLA_H, GLA_DK, GLA_DV), F32)],
        compiler_params=_cp("parallel", "arbitrary"),
        name="gla",
    )(gla, small, w2p, ba_row, gn_row, s0)


def _rope(x, cos, sa, sb):
    half = ROPE_DIM // 2
    return x * cos + pltpu.roll(x, LANES - half, 1) * sa + pltpu.roll(x, half, 1) * sb


def _nsa_prep_kernel(x_ref, cos_ref, sa_ref, sb_ref, q_ref, cmp_ref, slc_ref, win_ref, *mean_ref, tb):
    cos, sa, sb = cos_ref[...], sa_ref[...], sb_ref[...]
    for h in range(NSA_H):
        hs = slice(h * NSA_HD, (h + 1) * NSA_HD)
        q_ref[:, hs] = _rope(x_ref[:, hs], cos, sa, sb) * (NSA_HD ** -0.5)
    base = NSA_H * NSA_HD
    for i, ref in enumerate((cmp_ref, slc_ref, win_ref)):
        k0 = base + i * 2 * NSA_HD
        ref[:, :NSA_HD] = _rope(x_ref[:, k0:k0 + NSA_HD], cos, sa, sb)
        ref[:, NSA_HD:] = x_ref[:, k0 + NSA_HD:k0 + 2 * NSA_HD]
    if mean_ref:
        kv = cmp_ref[...]
        mean_ref[0][...] = jnp.mean(kv.reshape(tb // NSA_BLOCK, NSA_BLOCK, 2 * NSA_HD), axis=1)


def _nsa_prep(nsa, cos, sa, sb, with_means):
    b, t, n = nsa.shape
    tb = min(t, 512)
    tab = pl.BlockSpec((tb, LANES), lambda bi, i: (i, 0))
    kv = pl.BlockSpec((None, tb, 2 * NSA_HD), lambda bi, i: (bi, i, 0))
    out_specs = [pl.BlockSpec((None, tb, MIX_W), lambda bi, i: (bi, i, 0)), kv, kv, kv]
    out_shape = [jax.ShapeDtypeStruct((b, t, MIX_W), F32)] + [jax.ShapeDtypeStruct((b, t, 2 * NSA_HD), F32)] * 3
    if with_means:
        out_specs.append(pl.BlockSpec((None, tb // NSA_BLOCK, 2 * NSA_HD), lambda bi, i: (bi, i, 0)))
        out_shape.append(jax.ShapeDtypeStruct((b, t // NSA_BLOCK, 2 * NSA_HD), F32))
    return pl.pallas_call(
        functools.partial(_nsa_prep_kernel, tb=tb),
        grid=(b, t // tb),
        in_specs=[pl.BlockSpec((None, tb, n), lambda bi, i: (bi, i, 0)), tab, tab, tab],
        out_specs=out_specs,
        out_shape=out_shape,
        compiler_params=_cp("parallel", "parallel"),
        name="nsa_prep",
    )(nsa, cos, sa, sb)


def _masked_softmax(s, mask):
    s = jnp.where(mask, s, -jnp.inf)
    m = jnp.max(s, axis=-1, keepdims=True)
    m = jnp.where(m == -jnp.inf, 0.0, m)
    p = jnp.where(mask, jnp.exp(s - m), 0.0)
    return p / jnp.maximum(jnp.sum(p, axis=-1, keepdims=True), 1e-30)


def _topk_mask_t(score_t, blk_t, n_sel):
    rank = jnp.zeros(score_t.shape, jnp.int32)
    for m in range(score_t.shape[0]):
        sm = score_t[m:m + 1, :]
        beats = (sm > score_t) | ((sm == score_t) & (blk_t > m))
        rank = rank + beats.astype(jnp.int32)
    return rank < n_sel


def _online_update(carry, qs, k, v_t, mask):
    ms, ls, accs = carry
    bias = jnp.where(mask, 0.0, NEG)
    n = range(len(qs))
    ss = [lax.dot_general(k, qs[g], NT, preferred_element_type=F32) + bias for g in n]
    m_new = [jnp.maximum(ms[g], jnp.max(ss[g], axis=0, keepdims=True)) for g in n]
    ps = [jnp.exp(ss[g] - m_new[g]) for g in n]
    pvs = [jnp.dot(v_t, ps[g].astype(BF16), preferred_element_type=F32) for g in n]
    alphas = [jnp.exp(ms[g] - m_new[g]) for g in n]
    ls = [alphas[g] * ls[g] + jnp.sum(ps[g], axis=0, keepdims=True) for g in n]
    accs = [alphas[g] * accs[g] + pvs[g] for g in n]
    return m_new, ls, accs


def _nsa_attn_kernel(q_ref, sm_ref, cm_ref, slc_ref, win_ref, o_ref, *, tq, kt, nb):
    qi = pl.program_id(1)
    qpos = qi * tq + _iota((tq, 1), 0)
    blk = _iota((tq, nb), 1)
    cmask = (blk + 1) * NSA_BLOCK <= qpos + 1
    kc, vc = cm_ref[:, :NSA_HD], cm_ref[:, NSA_HD:]
    o_c = []
    imp = jnp.zeros((tq, nb), F32)
    for h in range(NSA_H):
        s = lax.dot_general(q_ref[:, h * NSA_HD:(h + 1) * NSA_HD], kc, NT, precision=HI, preferred_element_type=F32)
        p = _masked_softmax(s, cmask)
        imp = imp + p
        o_c.append(jnp.dot(p, vc, precision=HI, preferred_element_type=F32))
    blk_t = _iota((nb, tq), 0)
    qpos_t = qi * tq + _iota((nb, tq), 1)
    cur_t = qpos_t // NSA_BLOCK
    score_t = jnp.where(blk_t == cur_t, jnp.inf,
                        jnp.where((blk_t + 1) * NSA_BLOCK <= qpos_t + 1, imp.T, -jnp.inf))
    sel_b = (_topk_mask_t(score_t, blk_t, min(NSA_TOPK, nb)) & (blk_t <= cur_t)).astype(BF16)
    qs = [q_ref[:, h * NSA_HD:(h + 1) * NSA_HD].astype(BF16) for h in range(NSA_H)]
    krow = _iota((kt, tq), 0)
    qpos_row = qi * tq + _iota((1, tq), 1)
    blk_of_col = _iota((nb, kt), 1) // NSA_BLOCK - _iota((nb, kt), 0)
    last = (qi * tq + tq - 1) // kt

    def init():
        return ([jnp.full((1, tq), NEG, F32)] * NSA_H, [jnp.zeros((1, tq), F32)] * NSA_H,
                [jnp.zeros((NSA_HD, tq), F32)] * NSA_H)

    def slc_body(kb, carry):
        rows = pl.ds(pl.multiple_of(kb * kt, kt), kt)
        k = slc_ref[rows, :NSA_HD].astype(BF16)
        v_t = slc_ref[rows, NSA_HD:].T.astype(BF16)
        expand = (blk_of_col + kb * (kt // NSA_BLOCK) == 0).astype(BF16)
        tok = lax.dot_general(expand, sel_b, TN, preferred_element_type=F32)
        mask = (tok > 0.5) & (kb * kt + krow <= qpos_row)
        return _online_update(carry, qs, k, v_t, mask)

    _, l_s, acc_s = lax.fori_loop(0, last + 1, slc_body, init())

    def win_body(kb, carry):
        rows = pl.ds(pl.multiple_of(kb * kt, kt), kt)
        k = win_ref[rows, :NSA_HD].astype(BF16)
        v_t = win_ref[rows, NSA_HD:].T.astype(BF16)
        dist = qpos_row - (kb * kt + krow)
        mask = (dist >= 0) & (dist < NSA_WINDOW)
        return _online_update(carry, qs, k, v_t, mask)

    first = jnp.maximum(qi * tq - (NSA_WINDOW - 1), 0) // kt
    _, l_w, acc_w = lax.fori_loop(first, last + 1, win_body, init())
    gate = jax.nn.sigmoid(sm_ref[...])
    for h in range(NSA_H):
        g0 = SM_NG + 3 * h
        o = (gate[:, g0:g0 + 1] * o_c[h] + gate[:, g0 + 1:g0 + 2] * (acc_s[h] / l_s[h]).T
             + gate[:, g0 + 2:g0 + 3] * (acc_w[h] / l_w[h]).T)
        o_ref[:, h * NSA_HD:(h + 1) * NSA_HD] = o.astype(o_ref.dtype)


def _nsa_attn(qr, small, cmean, slc, win):
    b, t, _ = qr.shape
    tq = 128
    kt = min(t, 512)
    nb = t // NSA_BLOCK
    whole = lambda n: pl.BlockSpec((None, n, 2 * NSA_HD), lambda bi, qi: (bi, 0, 0))
    return pl.pallas_call(
        functools.partial(_nsa_attn_kernel, tq=tq, kt=kt, nb=nb),
        grid=(b, t // tq),
        in_specs=[
            pl.BlockSpec((None, tq, MIX_W), lambda bi, qi: (bi, qi, 0)),
            pl.BlockSpec((None, tq, LANES), lambda bi, qi: (bi, qi, 0)),
            whole(nb), whole(t), whole(t),
        ],
        out_specs=pl.BlockSpec((None, tq, MIX_W), lambda bi, qi: (bi, qi, 0)),
        out_shape=jax.ShapeDtypeStruct((b, t, MIX_W), BF16),
        compiler_params=_cp("parallel", "parallel"),
        name="nsa_attn",
    )(qr, small, cmean, slc, win)


def _dn_prep_kernel(x_ref, halo_ref, prev_ref, cw_ref, sm_ref, alog_ref, dtb_ref,
                    q_ref, k_ref, v_ref, g_ref, bt_ref, *, tb):
    halo = jnp.where(pl.program_id(1) == 0, prev_ref[...], halo_ref[...])
    xcat = jnp.concatenate([halo, x_ref[...]], axis=0)
    conv = x_ref[...] * cw_ref[DN_CONV - 1:DN_CONV]
    for j in range(DN_CONV - 1):
        sh = DN_CONV - 1 - j
        conv = conv + pltpu.roll(xcat, sh, 0)[8:] * cw_ref[j:j + 1]
    u = jax.nn.silu(conv)
    for h in range(DN_H):
        hs = slice(h * DN_DK, (h + 1) * DN_DK)
        uq = u[:, hs]
        q_ref[:, hs] = uq * lax.rsqrt(jnp.sum(uq * uq, axis=-1, keepdims=True) + 1e-6) * (DN_DK ** -0.5)
        uk = u[:, DN_H * DN_DK + h * DN_DK:DN_H * DN_DK + (h + 1) * DN_DK]
        k_ref[:, hs] = uk * lax.rsqrt(jnp.sum(uk * uk, axis=-1, keepdims=True) + 1e-6)
    v_ref[...] = u[:, 2 * DN_H * DN_DK:]
    sm = sm_ref[...]
    g_ref[...] = -jnp.exp(alog_ref[...]) * jax.nn.softplus(sm + dtb_ref[...])
    bt_ref[...] = jax.nn.sigmoid(sm)


def _dn_prep(dn, prev8, conv_w, small, alog_row, dtb_row):
    b, t, _ = dn.shape
    tb = min(t, 256)
    hb = tb // 8
    row = pl.BlockSpec((None, tb, MIX_W), lambda bi, i: (bi, i, 0))
    sm = pl.BlockSpec((None, tb, LANES), lambda bi, i: (bi, i, 0))
    one = lambda shape: pl.BlockSpec(shape, lambda bi, i: (0,) * len(shape))
    return pl.pallas_call(
        functools.partial(_dn_prep_kernel, tb=tb),
        grid=(b, t // tb),
        in_specs=[
            pl.BlockSpec((None, tb, DN_QKV), lambda bi, i: (bi, i, 0)),
            pl.BlockSpec((None, 8, DN_QKV), lambda bi, i: (bi, jnp.maximum(i * hb - 1, 0), 0)),
            pl.BlockSpec((None, 8, DN_QKV), lambda bi, i: (bi, 0, 0)),
            one(conv_w.shape), sm, one(alog_row.shape), one(dtb_row.shape),
        ],
        out_specs=[row, row, row, sm, sm],
        out_shape=[jax.ShapeDtypeStruct((b, t, MIX_W), F32)] * 3 + [jax.ShapeDtypeStruct((b, t, LANES), F32)] * 2,
        compiler_params=_cp("parallel", "parallel"),
        name="dn_prep",
    )(dn, dn, prev8, conv_w, small, alog_row, dtb_row)


def _gdn_pre_kernel(q_ref, k_ref, v_ref, g_ref, bt_ref, wv_ref, wk_ref, qe_ref, kd_ref, qk_ref, gam_ref, *, C, nck):
    row, col = _iota((C, C), 0), _iota((C, C), 1)
    incl, strict = col <= row, col < row
    eye = (row == col).astype(F32)
    tri = incl.astype(F32)
    n_sq = C.bit_length() - 2
    chains = []
    for c in range(nck):
        rs = slice(c * C, (c + 1) * C)
        gam_all = jnp.dot(tri, g_ref[rs, :], precision=HI, preferred_element_type=F32)
        gam_ref[rs, :] = gam_all
        gam_t = gam_all.T
        for h in range(DN_H):
            hs = slice(h * DN_DK, (h + 1) * DN_DK)
            q, k = q_ref[rs, hs], k_ref[rs, hs]
            gcol = gam_all[:, SM_DA + h:SM_DA + h + 1]
            bcol = bt_ref[rs, SM_DB + h:SM_DB + h + 1]
            diff = gcol - gam_t[SM_DA + h:SM_DA + h + 1, :]
            dec_incl = jnp.where(incl, jnp.exp(jnp.where(incl, diff, 0.0)), 0.0)
            qk = lax.dot_general(q.astype(BF16), k.astype(BF16), NT, preferred_element_type=F32) * dec_incl
            qk_ref[rs, h * C:(h + 1) * C] = qk.astype(BF16)
            qe_ref[rs, hs] = (q * jnp.exp(gcol)).astype(BF16)
            kd_ref[rs, hs] = (k * jnp.exp(gcol[C - 1:C] - gcol)).astype(BF16)
            x = -(bcol * _dot3(k, k, NT) * jnp.where(strict, dec_incl, 0.0))
            chains.append((rs, hs, x, gcol, bcol))
    xs = [ch[2] for ch in chains]
    ps = [eye + x for x in xs]
    for _ in range(n_sq):
        xs = [_dot3(x, x) for x in xs]
        ps = [p + _dot3(p, x) for p, x in zip(ps, xs)]
    for (rs, hs, _, gcol, bcol), p in zip(chains, ps):
        k, v = k_ref[rs, hs], v_ref[rs, hs]
        rhs = jnp.concatenate([bcol * v, (bcol * jnp.exp(gcol)) * k], axis=1)
        w = _dot3(p, rhs)
        wv_ref[rs, hs] = w[:, :DN_DV]
        wk_ref[rs, hs] = w[:, DN_DV:].astype(BF16)


def _gdn_rec_kernel(wv_ref, wk_ref, qe_ref, kd_ref, qk_ref, gam_ref, z_ref, nrm_ref, s0_ref, o_ref, sout_ref, s_sc,
                    *, C, nck):
    i = pl.program_id(1)

    @pl.when(i == 0)
    def _():
        s_sc[...] = s0_ref[...]

    state = [s_sc[h] for h in range(DN_H)]
    heads = range(DN_H)
    hsl = [slice(h * DN_DK, (h + 1) * DN_DK) for h in heads]
    for c in range(nck):
        rs = slice(c * C, (c + 1) * C)
        sbs = [state[h].astype(BF16) for h in heads]
        ubs = [(wv_ref[rs, hsl[h]] - jnp.dot(wk_ref[rs, hsl[h]], sbs[h], preferred_element_type=F32)).astype(BF16)
               for h in heads]
        outs = [jnp.dot(qe_ref[rs, hsl[h]], sbs[h], preferred_element_type=F32)
                + jnp.dot(qk_ref[rs, h * C:(h + 1) * C], ubs[h], preferred_element_type=F32) for h in heads]
        gend = gam_ref[c * C + C - 1:(c + 1) * C, :]
        state = [jnp.exp(gend[:, SM_DA + h:SM_DA + h + 1]) * state[h]
                 + lax.dot_general(kd_ref[rs, hsl[h]], ubs[h], TN, preferred_element_type=F32) for h in heads]
        for h in heads:
            o = outs[h]
            y = o * lax.rsqrt(jnp.mean(o * o, axis=-1, keepdims=True) + RMS_EPS) * nrm_ref[...]
            o_ref[rs, hsl[h]] = (y * jax.nn.silu(z_ref[rs, hsl[h]])).astype(o_ref.dtype)
    for h in range(DN_H):
        s_sc[h] = state[h]

    @pl.when(i == pl.num_programs(1) - 1)
    def _():
        for h in range(DN_H):
            sout_ref[h] = state[h]


GDN_BLOCK = 256


def _gdn(qd, kd, vd, gsm, btsm, dn, nrm_row, s0):
    b, t, _ = qd.shape
    C = CHUNK
    tb = min(t, GDN_BLOCK)
    nck = tb // C
    row = pl.BlockSpec((None, tb, MIX_W), lambda bi, i: (bi, i, 0))
    sm = pl.BlockSpec((None, tb, LANES), lambda bi, i: (bi, i, 0))
    qk_spec = pl.BlockSpec((None, tb, DN_H * C), lambda bi, i: (bi, i, 0))
    st = pl.BlockSpec((None, DN_H, DN_DK, DN_DV), lambda bi, i: (bi, 0, 0, 0))
    wide = lambda dt: jax.ShapeDtypeStruct((b, t, MIX_W), dt)
    wv, wk, qe, kdc, qk, gam = pl.pallas_call(
        functools.partial(_gdn_pre_kernel, C=C, nck=nck),
        grid=(b, t // tb),
        in_specs=[row, row, row, sm, sm],
        out_specs=[row, row, row, row, qk_spec, sm],
        out_shape=[wide(F32), wide(BF16), wide(BF16), wide(BF16), jax.ShapeDtypeStruct((b, t, DN_H * C), BF16),
                   jax.ShapeDtypeStruct((b, t, LANES), F32)],
        compiler_params=_cp("parallel", "parallel"),
        name="gdn_pre",
    )(qd, kd, vd, gsm, btsm)
    return pl.pallas_call(
        functools.partial(_gdn_rec_kernel, C=C, nck=nck),
        grid=(b, t // tb),
        in_specs=[row, row, row, row, qk_spec, sm,
                  pl.BlockSpec((None, tb, MIX_W), lambda bi, i: (bi, i, DN_QKV // MIX_W)),
                  pl.BlockSpec(nrm_row.shape, lambda bi, i: (0, 0)), st],
        out_specs=[row, st],
        out_shape=[wide(BF16), jax.ShapeDtypeStruct(s0.shape, F32)],
        scratch_shapes=[pltpu.VMEM((DN_H, DN_DK, DN_DV), F32)],
        compiler_params=_cp("parallel", "arbitrary"),
        name="gdn_rec",
    )(wv, wk, qe, kdc, qk, gam, dn, nrm_row, s0)


FOX_PAGES = 8


def _fox_dec_kernel(pt_ref, fx_ref, sm_ref, b_ref, *refs):
    kv_refs, lf_refs = refs[:FOX_PAGES], refs[FOX_PAGES:2 * FOX_PAGES]
    o_ref, lfo_ref, m_sc, l_sc, acc_sc, car_sc = refs[2 * FOX_PAGES:]
    j = pl.program_id(1)
    P = PAGE_SIZE
    rows_per_tok = 2 * FOX_H
    qs = [fx_ref[:, h * FOX_HD:(h + 1) * FOX_HD] * (FOX_HD ** -0.5) for h in range(FOX_H)]

    @pl.when(j == 0)
    def _():
        lf_new = _log_sigmoid(sm_ref[...] + b_ref[...])
        lfo_ref[...] = lf_new
        on_diag = _iota((8, LANES), 1) == _iota((8, LANES), 0)
        car_sc[...] = jnp.broadcast_to(jnp.sum(jnp.where(on_diag, lf_new, 0.0), axis=1, keepdims=True), (8, LANES))
        for h in range(FOX_H):
            k_new = fx_ref[:, MIX_W + h * FOX_HD:MIX_W + (h + 1) * FOX_HD]
            m_sc[h] = jnp.sum(qs[h] * k_new, axis=-1, keepdims=True)
            l_sc[h] = jnp.ones((1, 1), F32)
            acc_sc[h] = fx_ref[:, 2 * MIX_W + h * FOX_HD:2 * MIX_W + (h + 1) * FOX_HD]

    upper = (_iota((P, P), 1) > _iota((P, P), 0)).astype(F32)
    carry = car_sc[...]
    pad = jnp.zeros((8 - FOX_H, P), F32)
    scores = []
    for kv_ref, lf_ref in zip(kv_refs, lf_refs):
        lf_t = jnp.concatenate([lf_ref[...], pad], axis=0)
        bias = (lax.dot_general(lf_t, upper, NT, precision=HI, preferred_element_type=F32) + carry).T
        scores.append([jnp.sum(kv_ref[pl.ds(h, P, stride=rows_per_tok), :] * qs[h], axis=-1, keepdims=True)
                       + bias[:, h:h + 1] for h in range(FOX_H)])
        carry = carry + jnp.sum(lf_t, axis=1, keepdims=True)
    car_sc[...] = carry
    outs = []
    for h in range(FOX_H):
        m_step = jnp.max(scores[0][h], axis=0, keepdims=True)
        for g in range(1, FOX_PAGES):
            m_step = jnp.maximum(m_step, jnp.max(scores[g][h], axis=0, keepdims=True))
        m_prev = m_sc[h]
        m_new = jnp.maximum(m_prev, m_step)
        alpha = jnp.exp(m_prev - m_new)
        l_new = alpha * l_sc[h]
        acc = alpha * acc_sc[h]
        for g, kv_ref in enumerate(kv_refs):
            p = jnp.exp(scores[g][h] - m_new)
            l_new = l_new + jnp.sum(p, axis=0, keepdims=True)
            acc = acc + jnp.sum(p * kv_ref[pl.ds(FOX_H + h, P, stride=rows_per_tok), :], axis=0, keepdims=True)
        m_sc[h], l_sc[h], acc_sc[h] = m_new, l_new, acc
        outs.append(acc / l_new)

    @pl.when(j == pl.num_programs(1) - 1)
    def _():
        for h in range(FOX_H):
            o_ref[:, h * FOX_HD:(h + 1) * FOX_HD] = outs[h]


def _fox_decode(pt_flat, n_pages, page0, fox_new, small_new, bias_row, kv_cache, lf_cache):
    b = fox_new.shape[0]
    assert n_pages % FOX_PAGES == 0

    def page(r):
        return lambda bi, j, pt: (page0 + pt[bi * n_pages + n_pages - 1 - (j * FOX_PAGES + r)], 0, 0)

    per_b = lambda n: pl.BlockSpec((None, 1, n), lambda bi, j, pt: (bi, 0, 0))
    return pl.pallas_call(
        _fox_dec_kernel,
        grid_spec=pltpu.PrefetchScalarGridSpec(
            num_scalar_prefetch=1,
            grid=(b, n_pages // FOX_PAGES),
            in_specs=[per_b(3 * MIX_W), per_b(LANES), pl.BlockSpec((1, LANES), lambda bi, j, pt: (0, 0))]
            + [pl.BlockSpec((None, PAGE_SIZE * 2 * FOX_H, FOX_HD), page(r)) for r in range(FOX_PAGES)]
            + [pl.BlockSpec((None, FOX_H, PAGE_SIZE), page(r)) for r in range(FOX_PAGES)],
            out_specs=[per_b(MIX_W), per_b(LANES)],
            scratch_shapes=[pltpu.VMEM((FOX_H, 1, 1), F32), pltpu.VMEM((FOX_H, 1, 1), F32),
                            pltpu.VMEM((FOX_H, 1, FOX_HD), F32), pltpu.VMEM((8, PAGE_SIZE), F32)],
        ),
        out_shape=[jax.ShapeDtypeStruct((b, 1, MIX_W), F32), jax.ShapeDtypeStruct((b, 1, LANES), F32)],
        compiler_params=_cp("parallel", "arbitrary"),
        name="fox_decode",
    )(pt_flat, fox_new, small_new, bias_row, *([kv_cache] * FOX_PAGES), *([lf_cache] * FOX_PAGES))


def _col(row, eye):
    return jnp.sum(eye * row, axis=1, keepdims=True)


def _rec_dec_kernel(g_ref, d_ref, sm_ref, w2_ref, ba_ref, gn_ref, cp_ref, cw_ref, alog_ref, dtb_ref, nrm_ref,
                    sg_ref, sd_ref, og_ref, od_ref, sgo_ref, sdo_ref):
    sm = sm_ref[...]
    lane = _iota((8, LANES), 1)
    ga = jnp.where((lane >= SM_GA) & (lane < SM_GA + GLA_RANK), jnp.broadcast_to(sm, (8, LANES)), 0.0)
    pre = jnp.dot(ga, w2_ref[...], precision=HI, preferred_element_type=F32)[0:1] + ba_ref[...]
    loga = _log_sigmoid(pre) * (1.0 / GLA_TAU)
    eye_k = (_iota((GLA_DK, GLA_DK), 0) == _iota((GLA_DK, GLA_DK), 1)).astype(F32)
    for h in range(GLA_H):
        q = g_ref[:, h * GLA_DK:(h + 1) * GLA_DK] * (GLA_DK ** -0.5)
        k = g_ref[:, GLA_H * GLA_DK + h * GLA_DK:GLA_H * GLA_DK + (h + 1) * GLA_DK]
        v0 = 2 * GLA_H * GLA_DK + h * GLA_DV
        v = g_ref[:, v0:v0 + GLA_DV]
        gr = g_ref[:, v0 + GLA_H * GLA_DV:v0 + GLA_H * GLA_DV + GLA_DV]
        ea = jnp.exp(loga[:, h * GLA_DK:(h + 1) * GLA_DK])
        s0 = sg_ref[h]
        o = jnp.sum(q * k, axis=-1, keepdims=True) * v + jnp.sum(_col(q * ea, eye_k) * s0, axis=0, keepdims=True)
        sgo_ref[h] = _col(ea, eye_k) * s0 + _col(k, eye_k) * v
        y = o * lax.rsqrt(jnp.mean(o * o, axis=-1, keepdims=True) + RMS_EPS) * gn_ref[...]
        og_ref[:, h * GLA_DV:(h + 1) * GLA_DV] = y * jax.nn.silu(gr)
    conv = d_ref[:, :DN_QKV] * cw_ref[DN_CONV - 1:DN_CONV]
    for j in range(DN_CONV - 1):
        conv = conv + cp_ref[j:j + 1] * cw_ref[j:j + 1]
    u = jax.nn.silu(conv)
    gall = -jnp.exp(alog_ref[...]) * jax.nn.softplus(sm + dtb_ref[...])
    ball = jax.nn.sigmoid(sm)
    eye_d = (_iota((DN_DK, DN_DK), 0) == _iota((DN_DK, DN_DK), 1)).astype(F32)
    for h in range(DN_H):
        uq = u[:, h * DN_DK:(h + 1) * DN_DK]
        q = uq * lax.rsqrt(jnp.sum(uq * uq, axis=-1, keepdims=True) + 1e-6) * (DN_DK ** -0.5)
        uk = u[:, DN_H * DN_DK + h * DN_DK:DN_H * DN_DK + (h + 1) * DN_DK]
        k = uk * lax.rsqrt(jnp.sum(uk * uk, axis=-1, keepdims=True) + 1e-6)
        v = u[:, 2 * DN_H * DN_DK + h * DN_DV:2 * DN_H * DN_DK + (h + 1) * DN_DV]
        eg = jnp.exp(gall[:, SM_DA + h:SM_DA + h + 1])
        beta = ball[:, SM_DB + h:SM_DB + h + 1]
        s0 = sd_ref[h]
        kcol = _col(k, eye_d)
        ks = jnp.sum(kcol * s0, axis=0, keepdims=True)
        qs = jnp.sum(_col(q, eye_d) * s0, axis=0, keepdims=True)
        un = beta * (v - eg * ks)
        o = eg * qs + jnp.sum(q * k, axis=-1, keepdims=True) * un
        sdo_ref[h] = eg * s0 + kcol * un
        y = o * lax.rsqrt(jnp.mean(o * o, axis=-1, keepdims=True) + RMS_EPS) * nrm_ref[...]
        z = d_ref[:, DN_QKV + h * DN_DV:DN_QKV + (h + 1) * DN_DV]
        od_ref[:, h * DN_DV:(h + 1) * DN_DV] = y * jax.nn.silu(z)


def _rec_decode(gla_new, dn_new, small_new, w2p, ba_row, gn_row, conv_prev, conv_w, alog_row, dtb_row, nrm_row,
                s_gla, s_dn, row0):
    b = gla_new.shape[0]
    per_b = lambda n: pl.BlockSpec((None, 1, n), lambda bi: (bi, 0, 0))
    one = lambda a: pl.BlockSpec(a.shape, lambda bi: (0,) * a.ndim)
    sg = lambda r0: pl.BlockSpec((None, GLA_H, GLA_DK, GLA_DV), lambda bi: (r0 + bi, 0, 0, 0))
    sd = lambda r0: pl.BlockSpec((None, DN_H, DN_DK, DN_DV), lambda bi: (r0 + bi, 0, 0, 0))
    return pl.pallas_call(
        _rec_dec_kernel,
        grid=(b,),
        in_specs=[per_b(gla_new.shape[-1]), per_b(dn_new.shape[-1]), per_b(LANES), one(w2p), one(ba_row), one(gn_row),
                  pl.BlockSpec((None, DN_CONV - 1, DN_QKV), lambda bi: (row0 + bi, 0, 0)), one(conv_w), one(alog_row),
                  one(dtb_row), one(nrm_row), sg(row0), sd(row0)],
        out_specs=[per_b(MIX_W), per_b(MIX_W), sg(0), sd(0)],
        out_shape=[jax.ShapeDtypeStruct((b, 1, MIX_W), F32)] * 2
        + [jax.ShapeDtypeStruct((b,) + s_gla.shape[1:], F32), jax.ShapeDtypeStruct((b,) + s_dn.shape[1:], F32)],
        compiler_params=_cp("parallel"),
        name="rec_decode",
    )(gla_new, dn_new, small_new, w2p, ba_row, gn_row, conv_prev, conv_w, alog_row, dtb_row, nrm_row, s_gla, s_dn)


def _heads_to_rows(q_row):
    rows = [q_row[:, h * NSA_HD:(h + 1) * NSA_HD] for h in range(NSA_H)]
    return jnp.concatenate(rows + [jnp.zeros((8 - NSA_H, NSA_HD), F32)], axis=0)


CMP_PAGES = 16


def _nsa_dec_cmp_kernel(pt_ref, q_ref, *refs, nbp):
    page_refs, (kc_ref, vc_ref, oc_ref, sel_ref) = refs[:CMP_PAGES], refs[CMP_PAGES:]
    j = pl.program_id(1)
    per = PAGE_SIZE // NSA_BLOCK
    rows = CMP_PAGES * per
    dst = pl.ds(pl.multiple_of(j * rows, rows), rows)
    for kv, ref in enumerate((kc_ref, vc_ref)):
        means = [jnp.mean(r[pl.ds(kv, PAGE_SIZE, stride=2), :].reshape(per, NSA_BLOCK, NSA_HD), axis=1)
                 for r in page_refs]
        ref[dst, :] = jnp.concatenate(means, axis=0)

    @pl.when(j == pl.num_programs(1) - 1)
    def _():
        q8 = _heads_to_rows(q_ref[...])
        kc, vc = kc_ref[...], vc_ref[...]
        s = lax.dot_general(q8, kc, NT, precision=HI, preferred_element_type=F32)
        p = _masked_softmax(s, jnp.full(s.shape, True))
        oc_ref[...] = jnp.dot(p, vc, precision=HI, preferred_element_type=F32)
        imp = jnp.sum(p[0:NSA_H], axis=0, keepdims=True)
        r_i, c_i = _iota((nbp, nbp), 0), _iota((nbp, nbp), 1)
        imp_col = _col(imp, (r_i == c_i).astype(F32))
        beats = (imp_col > imp) | ((imp_col == imp) & (r_i < c_i))
        rank = jnp.sum(beats.astype(jnp.int32), axis=0, keepdims=True)
        ids = jnp.where(rank == _iota((16, nbp), 0), _iota((16, nbp), 1), 0)
        sel_ref[...] = jnp.broadcast_to(jnp.sum(ids, axis=1, keepdims=True), (16, LANES))


def _nsa_dec_cmp(pt_flat, n_pages, page0, q_new, cmp_cache):
    b = q_new.shape[0]
    nbp = n_pages * (PAGE_SIZE // NSA_BLOCK)
    n_sel = NSA_TOPK - 1
    assert nbp >= n_sel and NSA_TOPK <= 16 and n_pages % CMP_PAGES == 0
    page = lambda r: pl.BlockSpec((None, 2 * PAGE_SIZE, NSA_HD),
                                  lambda bi, j, pt: (page0 + pt[bi * n_pages + j * CMP_PAGES + r], 0, 0))
    mean_spec = pl.BlockSpec((None, nbp, NSA_HD), lambda bi, j, pt: (bi, 0, 0))
    return pl.pallas_call(
        functools.partial(_nsa_dec_cmp_kernel, nbp=nbp),
        grid_spec=pltpu.PrefetchScalarGridSpec(
            num_scalar_prefetch=1,
            grid=(b, n_pages // CMP_PAGES),
            in_specs=[pl.BlockSpec((None, 1, MIX_W), lambda bi, j, pt: (bi, 0, 0))]
            + [page(r) for r in range(CMP_PAGES)],
            out_specs=[mean_spec, mean_spec,
                       pl.BlockSpec((None, 8, NSA_HD), lambda bi, j, pt: (bi, 0, 0)),
                       pl.BlockSpec((None, 16, LANES), lambda bi, j, pt: (bi, 0, 0))],
        ),
        out_shape=[jax.ShapeDtypeStruct((b, nbp, NSA_HD), F32)] * 2
        + [jax.ShapeDtypeStruct((b, 8, NSA_HD), F32), jax.ShapeDtypeStruct((b, 16, LANES), jnp.int32)],
        compiler_params=_cp("parallel", "arbitrary"),
        name="nsa_dec_cmp",
    )(pt_flat, q_new, *([cmp_cache] * CMP_PAGES))


def _nsa_dec_attn_kernel(pt_ref, sel_ref, q_ref, blk_ref, snew_ref, win_ref, wnew_ref, oc_ref, sm_ref, o_ref,
                         m_sc, l_sc, acc_sc, *, sw):
    s_i = pl.program_id(1)
    q8 = _heads_to_rows(q_ref[...])

    @pl.when(s_i == 0)
    def _():
        m_sc[...] = jnp.sum(q8 * snew_ref[:, :NSA_HD], axis=-1, keepdims=True)
        l_sc[...] = jnp.ones_like(l_sc)
        acc_sc[...] = jnp.broadcast_to(snew_ref[:, NSA_HD:], acc_sc.shape)

    k, v = blk_ref[pl.ds(0, NSA_BLOCK, stride=2), :], blk_ref[pl.ds(1, NSA_BLOCK, stride=2), :]
    s = lax.dot_general(q8, k, NT, precision=HI, preferred_element_type=F32)
    m_prev = m_sc[...]
    m_new = jnp.maximum(m_prev, jnp.max(s, axis=-1, keepdims=True))
    alpha = jnp.exp(m_prev - m_new)
    p = jnp.exp(s - m_new)
    l_sc[...] = alpha * l_sc[...] + jnp.sum(p, axis=-1, keepdims=True)
    acc_sc[...] = alpha * acc_sc[...] + jnp.dot(p, v, precision=HI, preferred_element_type=F32)
    m_sc[...] = m_new

    @pl.when(s_i == pl.num_programs(1) - 1)
    def _():
        o_s = acc_sc[...] / l_sc[...]
        kw, vw = win_ref[pl.ds(0, sw, stride=2), :], win_ref[pl.ds(1, sw, stride=2), :]
        sw_ = lax.dot_general(q8, kw, NT, precision=HI, preferred_element_type=F32)
        wmask = (sw - _iota((8, sw), 1)) < NSA_WINDOW
        s_new = jnp.sum(q8 * wnew_ref[:, :NSA_HD], axis=-1, keepdims=True)
        m = jnp.maximum(jnp.max(jnp.where(wmask, sw_, -jnp.inf), axis=-1, keepdims=True), s_new)
        pw = jnp.where(wmask, jnp.exp(sw_ - m), 0.0)
        pn = jnp.exp(s_new - m)
        o_w = (jnp.dot(pw, vw, precision=HI, preferred_element_type=F32) + pn * wnew_ref[:, NSA_HD:]) / (
            jnp.sum(pw, axis=-1, keepdims=True) + pn)
        gate = jax.nn.sigmoid(sm_ref[...])
        o_c = oc_ref[...]
        for h in range(NSA_H):
            g0 = SM_NG + 3 * h
            o_ref[:, h * NSA_HD:(h + 1) * NSA_HD] = (
                gate[:, g0:g0 + 1] * o_c[h:h + 1] + gate[:, g0 + 1:g0 + 2] * o_s[h:h + 1]
                + gate[:, g0 + 2:g0 + 3] * o_w[h:h + 1])


def _nsa_dec_attn(pt_flat, sel_flat, n_pages, n_sel, page0, row0, q_new, slc_cache, slc_new, win_cache, win_new,
                  o_c, small_new):
    b = q_new.shape[0]
    sw = win_cache.shape[1] // 2
    per = PAGE_SIZE // NSA_BLOCK
    half = slc_cache.reshape(slc_cache.shape[0] * per, 2 * NSA_BLOCK, NSA_HD)

    def blk_map(bi, s, pt, sel):
        n = sel[bi * n_sel + s]
        return ((page0 + pt[bi * n_pages + n // per]) * per + n % per, 0, 0)

    per_b = lambda r, n: pl.BlockSpec((None, r, n), lambda bi, s, pt, sel: (bi, 0, 0))
    return pl.pallas_call(
        functools.partial(_nsa_dec_attn_kernel, sw=sw),
        grid_spec=pltpu.PrefetchScalarGridSpec(
            num_scalar_prefetch=2,
            grid=(b, n_sel),
            in_specs=[per_b(1, MIX_W), pl.BlockSpec((None, 2 * NSA_BLOCK, NSA_HD), blk_map), per_b(1, 2 * NSA_HD),
                      pl.BlockSpec((None, 2 * sw, NSA_HD), lambda bi, s, pt, sel: (row0 + bi, 0, 0)),
                      per_b(1, 2 * NSA_HD), per_b(8, NSA_HD), per_b(1, LANES)],
            out_specs=per_b(1, MIX_W),
            scratch_shapes=[pltpu.VMEM((8, 1), F32), pltpu.VMEM((8, 1), F32), pltpu.VMEM((8, NSA_HD), F32)],
        ),
        out_shape=jax.ShapeDtypeStruct((b, 1, MIX_W), F32),
        compiler_params=_cp("parallel", "arbitrary"),
        name="nsa_dec_attn",
    )(pt_flat, sel_flat, q_new, half, slc_new, win_cache, win_new, o_c, small_new)


def _lane_row(vals, at):
    return jnp.zeros((1, LANES), F32).at[0, at:at + vals.shape[0]].set(vals.astype(F32))


SEG_MG, SEG_DN, SEG_FOX, SEG_GLA, SEG_NSA, SEG_SMALL = (0, 8192), (8192, 2048), (10240, 1536), (11776, 1536), \
    (13312, 1280), (14592, 128)
SEGS = dict(mg=SEG_MG, dn=SEG_DN, fox=SEG_FOX, gla=SEG_GLA, nsa=SEG_NSA, small=SEG_SMALL)


def _stacked_weights(w_in, w_branch, w_out, w_up, w_down):
    offs = [0]
    for s in SPLIT_SIZES:
        offs.append(offs[-1] + s)
    seg = lambda i: w_in[:, :, offs[i]:offs[i + 1]]
    (fq, fk, fv, ff, gq, gk, gv, ga, gr, nq, nkc, nks, nkw, ng, dqkv, da, dbeta, dz, mg) = [seg(i) for i in range(19)]
    small = [ff, ga, ng, da, dbeta]
    pad = jnp.zeros(w_in.shape[:2] + (LANES - sum(p.shape[2] for p in small),), F32)
    w_all = jnp.concatenate([mg, dqkv, dz, fq, fk, fv, gq, gk, gv, gr, nq, nkc, nks, nkw] + small + [pad],
                            axis=2).astype(BF16)
    assert w_all.shape[2] == SEG_SMALL[0] + SEG_SMALL[1]
    return dict(w_in=w_all, wb=w_branch.astype(BF16), w_out=w_out.astype(BF16), w_up=w_up.astype(BF16),
                w_down=w_down.astype(BF16))


def _layer_rows(fox_b_f, gla_w_a2, gla_b_a, gla_norm, dn_conv_w, dn_a_log, dn_dt_bias, dn_norm):
    return dict(
        bf_row=_lane_row(fox_b_f, SM_FF),
        w2p=jnp.zeros((LANES, GLA_H * GLA_DK), F32).at[SM_GA:SM_GA + GLA_RANK].set(gla_w_a2),
        ba_row=gla_b_a.reshape(1, -1), gn_row=gla_norm.reshape(1, -1), conv_w=dn_conv_w,
        alog_row=_lane_row(dn_a_log, SM_DA), dtb_row=_lane_row(dn_dt_bias, SM_DA), dnn_row=dn_norm.reshape(1, -1))


def _rope_tables(pos):
    half = ROPE_DIM // 2
    inv = ROPE_THETA ** (-jnp.arange(half, dtype=F32) / half)
    ang = pos.astype(F32)[:, None] * inv[None, :]
    cos, sin = jnp.cos(ang), jnp.sin(ang)
    t = pos.shape[0]
    z = lambda n: jnp.zeros((t, n), F32)
    return (jnp.concatenate([cos, cos, jnp.ones((t, LANES - ROPE_DIM), F32)], axis=1),
            jnp.concatenate([-sin, z(LANES - half)], axis=1),
            jnp.concatenate([z(half), sin, z(LANES - ROPE_DIM)], axis=1))


def _project(h, w, l):
    return {name: _mm(h, w["w_in"], l, *SEGS[name]) for name in ("fox", "gla", "nsa", "dn", "small", "mg")}


def _mixer_prompt(h, b, t, w, l):
    p = _project(h, w, l)
    fox, gla, nsa, dn, small = (p[n].reshape(b, t, -1) for n in ("fox", "gla", "nsa", "dn", "small"))
    lf, c, ct, fox_kv = _fox_prep(small, w["bf_row"], fox)
    o_fox = _fox_attn(fox, c, ct)
    o_gla, s_gla = _gla(gla, small, w["w2p"], w["ba_row"], w["gn_row"], jnp.zeros((b, GLA_H, GLA_DK, GLA_DV), F32))
    qr, cmp_kv, slc_kv, win_kv, cmean = _nsa_prep(nsa, *_rope_tables(jnp.arange(t)), True)
    o_nsa = _nsa_attn(qr, small, cmean, slc_kv, win_kv)
    qd, kd, vd, gsm, btsm = _dn_prep(dn, jnp.zeros((b, 8, DN_QKV), F32), w["conv_w"], small, w["alog_row"],
                                     w["dtb_row"])
    o_dn, s_dn = _gdn(qd, kd, vd, gsm, btsm, dn, w["dnn_row"], jnp.zeros((b, DN_H, DN_DK, DN_DV), F32))
    mix = _branch_gate([o.reshape(b * t, MIX_W) for o in (o_fox, o_gla, o_nsa, o_dn)], w["wb"], l, p["mg"])
    kv5 = lambda a: a.reshape(b, t, 2, 1, NSA_HD)
    wl = min(NSA_WINDOW, t)
    state = (fox_kv.reshape(b, t, 2, FOX_H, FOX_HD), lf[:, :, :FOX_H], kv5(cmp_kv), kv5(slc_kv),
             kv5(win_kv)[:, t - wl:], s_gla, s_dn, dn[:, t - (DN_CONV - 1):, :DN_QKV])
    return mix, state


def _mixer_sample(h, caches, l, pt_flat, n_pages, w):
    fox_kv_c, fox_lf_c, cmp_c, slc_c, win_c, s_gla_c, s_dn_c, conv_c = caches
    b = h.shape[0]
    past_len = n_pages * PAGE_SIZE
    p = _project(h, w, l)
    fox, gla, dn, small = (p[n].reshape(b, 1, -1) for n in ("fox", "gla", "dn", "small"))
    depth, n_pool = fox_kv_c.shape[:2]
    page0, row0 = l * n_pool, l * b
    o_fox, lf_new = _fox_decode(pt_flat, n_pages, page0, fox, small, w["bf_row"],
                                fox_kv_c.reshape(depth * n_pool, PAGE_SIZE * 2 * FOX_H, FOX_HD),
                                jnp.swapaxes(fox_lf_c, 2, 3).reshape(depth * n_pool, FOX_H, PAGE_SIZE))
    o_gla, o_dn, s_gla, s_dn = _rec_decode(
        gla, dn, small, w["w2p"], w["ba_row"], w["gn_row"], conv_c.reshape((depth * b,) + conv_c.shape[2:]),
        w["conv_w"], w["alog_row"], w["dtb_row"], w["dnn_row"], s_gla_c.reshape((depth * b,) + s_gla_c.shape[2:]),
        s_dn_c.reshape((depth * b,) + s_dn_c.shape[2:]), row0)
    tabs = _rope_tables(jnp.full((b,), past_len, jnp.int32))
    qr, cmp_new, slc_new, win_new = _nsa_prep(p["nsa"].reshape(1, b, -1), *tabs, False)
    qr, cmp_new, slc_new, win_new = (a.reshape(b, 1, -1) for a in (qr, cmp_new, slc_new, win_new))
    paged = lambda c: c.reshape(depth * n_pool, 2 * PAGE_SIZE, NSA_HD)
    _, _, o_c, sel = _nsa_dec_cmp(pt_flat, n_pages, page0, qr, paged(cmp_c))
    n_sel = NSA_TOPK - 1
    sel_flat = sel[:, :n_sel, 0].reshape(-1)
    sw = win_c.shape[2]
    o_nsa = _nsa_dec_attn(pt_flat, sel_flat, n_pages, n_sel, page0, row0, qr, paged(slc_c), slc_new,
                          win_c.reshape(depth * b, 2 * sw, NSA_HD), win_new, o_c, small)
    mix = _branch_gate([o.reshape(b, MIX_W).astype(BF16) for o in (o_fox, o_gla, o_nsa, o_dn)], w["wb"], l, p["mg"])
    kv5 = lambda a: a.reshape(b, 1, 2, 1, NSA_HD)
    wl = min(NSA_WINDOW, sw + 1)
    new_win = jnp.concatenate([win_c[l][:, sw + 1 - wl:], kv5(win_new)], axis=1)
    new_conv = jnp.concatenate([conv_c[l][:, 1:], dn[:, :, :DN_QKV]], axis=1)
    state = (fox[:, :, MIX_W:].reshape(b, 1, 2, FOX_H, FOX_HD), lf_new[:, :, :FOX_H], kv5(cmp_new), kv5(slc_new),
             new_win, s_gla, s_dn, new_conv)
    return mix, state


def _trunk_layer(x, mixer, norms, w, l):
    g_pre_mix, g_post_mix, g_pre_mlp, g_post_mlp = norms
    mix, state = mixer(_rmsnorm_cast(x, g_pre_mix))
    x = _mm_norm_res(mix, w["w_out"], l, g_post_mix, x)
    hid = _mm(_rmsnorm_cast(x, g_pre_mlp), w["w_up"], l, out_dtype=BF16, act="relu2")
    x = _mm_norm_res(hid, w["w_down"], l, g_post_mlp, x)
    return x, state


def kernel(x_prompt, x_sample, cache_fox_kv, cache_fox_logf, cache_nsa_cmp_kv, cache_nsa_slc_kv, cache_nsa_win_kv,
           state_gla, state_dn, state_dn_conv, page_table, norm_pre_mix, norm_post_mix, norm_pre_mlp, norm_post_mlp,
           w_in, fox_b_f, gla_w_a2, gla_b_a, gla_norm, dn_conv_w, dn_a_log, dn_dt_bias, dn_norm, w_branch, w_out,
           w_up, w_down):
    bp, tp, _ = x_prompt.shape
    bs, ts, _ = x_sample.shape
    assert ts == 1
    n_pages = page_table.shape[1]
    pt_flat = page_table.reshape(-1).astype(jnp.int32)
    y_p = x_prompt.reshape(bp * tp, D_MODEL)
    y_s = x_sample.reshape(bs, D_MODEL)
    new_p = [[] for _ in range(8)]
    new_s = [[] for _ in range(8)]
    caches = (cache_fox_kv, cache_fox_logf, cache_nsa_cmp_kv, cache_nsa_slc_kv, cache_nsa_win_kv, state_gla, state_dn,
              state_dn_conv)
    big = _stacked_weights(w_in, w_branch, w_out, w_up, w_down)
    for l in range(DEPTH):
        w = dict(big, **_layer_rows(fox_b_f[l], gla_w_a2[l], gla_b_a[l], gla_norm[l], dn_conv_w[l], dn_a_log[l],
                                    dn_dt_bias[l], dn_norm[l]))
        norms = (norm_pre_mix[l], norm_post_mix[l], norm_pre_mlp[l], norm_post_mlp[l])
        y_p, st_p = _trunk_layer(y_p, lambda h: _mixer_prompt(h, bp, tp, w, l), norms, w, l)
        y_s, st_s = _trunk_layer(y_s, lambda h: _mixer_sample(h, caches, l, pt_flat, n_pages, w), norms, w, l)
        for i in range(8):
            new_p[i].append(st_p[i])
            new_s[i].append(st_s[i])
    fox_kv_p, fox_logf_p, cmp_kv_p, slc_kv_p, win_kv_p, gla_p, dn_p, conv_p = [jnp.stack(a) for a in new_p]
    fox_kv_s, fox_logf_s, cmp_kv_s, slc_kv_s, win_kv_s, gla_s, dn_s, conv_s = [jnp.stack(a) for a in new_s]
    return (y_p.reshape(bp, tp, D_MODEL), y_s.reshape(bs, ts, D_MODEL), fox_kv_p, fox_kv_s, fox_logf_p, fox_logf_s,
            cmp_kv_p, cmp_kv_s, slc_kv_p, slc_kv_s, win_kv_p, win_kv_s, gla_p, gla_s, dn_p, dn_s, conv_p, conv_s)
```

```python
import functools

import jax
import jax.numpy as jnp
from jax import lax
from jax.experimental import pallas as pl
from jax.experimental.pallas import tpu as pltpu

F32 = jnp.float32
BF16 = jnp.bfloat16
HI = lax.Precision.HIGHEST
NT = (((1,), (1,)), ((), ()))
TN = (((0,), (0,)), ((), ()))
NEG = -1e30

D_MODEL = 2048
DEPTH = 2
PAGE_SIZE = 128
N_BRANCH = 4
MIX_W = D_MODEL // 4
FOX_H = 4
FOX_HD = MIX_W // FOX_H
GLA_H = 4
GLA_DK = MIX_W // (2 * GLA_H)
GLA_DV = MIX_W // GLA_H
GLA_RANK = 16
GLA_TAU = 16.0
NSA_H = 4
NSA_HD = MIX_W // NSA_H
NSA_BLOCK = 64
NSA_TOPK = 16
NSA_WINDOW = 512
DN_H = 4
DN_DK = MIX_W // DN_H
DN_DV = MIX_W // DN_H
DN_CONV = 4
DN_QKV = DN_H * (2 * DN_DK + DN_DV)
ROPE_DIM = NSA_HD // 4
ROPE_THETA = 500000.0
D_FF = 4 * D_MODEL
CHUNK = 64
SUB = 16
RMS_EPS = 1e-6
LANES = 128
SPLIT_SIZES = (
    FOX_H * FOX_HD, FOX_H * FOX_HD, FOX_H * FOX_HD, FOX_H,
    GLA_H * GLA_DK, GLA_H * GLA_DK, GLA_H * GLA_DV, GLA_RANK, GLA_H * GLA_DV,
    NSA_H * NSA_HD, 2 * NSA_HD, 2 * NSA_HD, 2 * NSA_HD, 3 * NSA_H,
    DN_QKV, DN_H, DN_H, DN_H * DN_DV,
    N_BRANCH * D_MODEL,
)
SM_FF, SM_GA, SM_NG, SM_DA, SM_DB = 0, 4, 20, 32, 36
VMEM_LIMIT = 56 * 1024 * 1024


def _cp(*sem):
    return pltpu.CompilerParams(dimension_semantics=sem, vmem_limit_bytes=VMEM_LIMIT)


def _pick(n, cap):
    if n <= cap:
        return n
    best = None
    for t in range(LANES, cap + 1, LANES):
        if n % t == 0:
            best = t
    assert best is not None, n
    return best


def _log_sigmoid(z):
    return jnp.minimum(z, 0.0) - jnp.log1p(jnp.exp(-jnp.abs(z)))


def _iota(shape, dim):
    return lax.broadcasted_iota(jnp.int32, shape, dim)


def _dot3(a, b, dims=(((1,), (0,)), ((), ()))):
    ah, bh = a.astype(BF16), b.astype(BF16)
    al, bl = (a - ah.astype(F32)).astype(BF16), (b - bh.astype(F32)).astype(BF16)
    f = lambda x, y: lax.dot_general(x, y, dims, preferred_element_type=F32)
    return f(ah, bh) + (f(ah, bl) + f(al, bh))


def _rmsnorm_cast_kernel(x_ref, g_ref, o_ref):
    x = x_ref[...]
    y = x * lax.rsqrt(jnp.mean(x * x, axis=-1, keepdims=True) + RMS_EPS)
    o_ref[...] = (y * g_ref[...]).astype(o_ref.dtype)


def _rmsnorm_cast(x, g):
    m, d = x.shape
    tm = min(m, 512)
    return pl.pallas_call(
        _rmsnorm_cast_kernel,
        grid=(m // tm,),
        in_specs=[pl.BlockSpec((tm, d), lambda i: (i, 0)), pl.BlockSpec((1, d), lambda i: (0, 0))],
        out_specs=pl.BlockSpec((tm, d), lambda i: (i, 0)),
        out_shape=jax.ShapeDtypeStruct((m, d), BF16),
        compiler_params=_cp("parallel"),
        name="rmsnorm_cast",
    )(x, g.reshape(1, d))


def _mm_kernel(a_ref, w_ref, o_ref, *, act):
    y = jnp.dot(a_ref[...], w_ref[...], preferred_element_type=F32)
    if act == "relu2":
        y = jnp.square(jnp.maximum(y, 0.0))
    o_ref[...] = y.astype(o_ref.dtype)


def _mm(a, w, l, col0=0, n=None, out_dtype=F32, act=None):
    m, k = a.shape
    n = w.shape[2] if n is None else n
    tm = min(m, 1024)
    tn = max(t for t in range(LANES, min(n, 2048) + 1, LANES) if n % t == 0 and col0 % t == 0)
    c0 = col0 // tn
    return pl.pallas_call(
        functools.partial(_mm_kernel, act=act),
        grid=(n // tn, m // tm),
        in_specs=[pl.BlockSpec((tm, k), lambda j, i: (i, 0)), pl.BlockSpec((None, k, tn), lambda j, i: (l, 0, c0 + j))],
        out_specs=pl.BlockSpec((tm, tn), lambda j, i: (i, j)),
        out_shape=jax.ShapeDtypeStruct((m, n), out_dtype),
        compiler_params=_cp("parallel", "parallel"),
        name="mm",
    )(a, w)


def _mm_norm_res_kernel(a_ref, w_ref, g_ref, x_ref, *refs):
    (gn_ref, o_ref, h_ref, acc_ref) = refs if len(refs) == 4 else (None, refs[0], None, refs[1])
    k = pl.program_id(1)

    @pl.when(k == 0)
    def _():
        acc_ref[...] = jnp.zeros_like(acc_ref)

    acc_ref[...] += jnp.dot(a_ref[...], w_ref[...], preferred_element_type=F32)

    @pl.when(k == pl.num_programs(1) - 1)
    def _():
        y = acc_ref[...]
        y = y * lax.rsqrt(jnp.mean(y * y, axis=-1, keepdims=True) + RMS_EPS)
        x = x_ref[...] + y * g_ref[...]
        o_ref[...] = x
        if h_ref is not None:
            h = x * lax.rsqrt(jnp.mean(x * x, axis=-1, keepdims=True) + RMS_EPS)
            h_ref[...] = (h * gn_ref[...]).astype(h_ref.dtype)


def _mm_norm_res(a, w, l, g, x, g_next=None):
    m, k = a.shape
    n = w.shape[2]
    tm = min(m, 512)
    tk = _pick(k, 2048)
    row = pl.BlockSpec((1, n), lambda i, kk: (0, 0))
    tile = pl.BlockSpec((tm, n), lambda i, kk: (i, 0))
    fused = g_next is not None
    return pl.pallas_call(
        _mm_norm_res_kernel,
        grid=(m // tm, k // tk),
        in_specs=[pl.BlockSpec((tm, tk), lambda i, kk: (i, kk)), pl.BlockSpec((None, tk, n), lambda i, kk: (l, kk, 0)),
                  row, tile] + ([row] if fused else []),
        out_specs=[tile, tile] if fused else tile,
        out_shape=[jax.ShapeDtypeStruct((m, n), F32), jax.ShapeDtypeStruct((m, n), BF16)] if fused
        else jax.ShapeDtypeStruct((m, n), F32),
        scratch_shapes=[pltpu.VMEM((tm, n), F32)],
        compiler_params=_cp("parallel", "arbitrary"),
        name="mm_norm_res",
    )(a, w, g.reshape(1, n), x, *([g_next.reshape(1, n)] if fused else []))


def _branch_gate_kernel(o0, o1, o2, o3, wb_ref, m0, m1, m2, m3, out_ref):
    acc = None
    for n, (o_n, m_n) in enumerate(((o0, m0), (o1, m1), (o2, m2), (o3, m3))):
        y = jnp.dot(o_n[...], wb_ref[n], preferred_element_type=F32)
        t = jax.nn.sigmoid(m_n[...]) * y
        acc = t if acc is None else acc + t
    out_ref[...] = acc.astype(out_ref.dtype)


def _branch_gate(branches, wb, l, mg):
    m = mg.shape[0]
    tm = min(m, 512)
    tn = 1024
    nj = D_MODEL // tn
    o_spec = pl.BlockSpec((tm, MIX_W), lambda j, i: (i, 0))
    m_specs = [pl.BlockSpec((tm, tn), functools.partial(lambda j, i, n: (i, n * nj + j), n=n)) for n in range(N_BRANCH)]
    return pl.pallas_call(
        _branch_gate_kernel,
        grid=(nj, m // tm),
        in_specs=[o_spec] * 4 + [pl.BlockSpec((None, N_BRANCH, MIX_W, tn), lambda j, i: (l, 0, 0, j))] + m_specs,
        out_specs=pl.BlockSpec((tm, tn), lambda j, i: (i, j)),
        out_shape=jax.ShapeDtypeStruct((m, D_MODEL), BF16),
        compiler_params=_cp("parallel", "parallel"),
        name="branch_gate",
    )(*branches, wb, mg, mg, mg, mg)


def _fox_prep_kernel(s_ref, b_ref, k_ref, v_ref, lf_ref, c_ref, ct_ref, kv_ref, carry_ref, *, tb):
    @pl.when(pl.program_id(1) == 0)
    def _():
        carry_ref[...] = jnp.zeros_like(carry_ref)

    for i, ref in enumerate((k_ref, v_ref)):
        for h in range(FOX_H):
            kv_ref[pl.ds(i * FOX_H + h, tb, stride=2 * FOX_H), :] = ref[:, h * FOX_HD:(h + 1) * FOX_HD]

    lf = _log_sigmoid(s_ref[...] + b_ref[...])
    tri = (_iota((tb, tb), 1) <= _iota((tb, tb), 0)).astype(F32)
    c = jnp.dot(tri, lf, precision=HI, preferred_element_type=F32) + carry_ref[...]
    lf_ref[...] = lf
    c_ref[...] = c
    ct_ref[...] = c.T[:8]
    carry_ref[...] = c[tb - 1:tb]


def _fox_prep(small, bias_row, fox):
    b, t, _ = small.shape
    tb = min(t, 256)
    blk = pl.BlockSpec((None, tb, LANES), lambda bi, i: (bi, i, 0))
    rows = 2 * FOX_H
    return pl.pallas_call(
        functools.partial(_fox_prep_kernel, tb=tb),
        grid=(b, t // tb),
        in_specs=[blk, pl.BlockSpec((1, LANES), lambda bi, i: (0, 0)),
                  pl.BlockSpec((None, tb, MIX_W), lambda bi, i: (bi, i, 1)),
                  pl.BlockSpec((None, tb, MIX_W), lambda bi, i: (bi, i, 2))],
        out_specs=[blk, blk, pl.BlockSpec((None, 8, tb), lambda bi, i: (bi, 0, i)),
                   pl.BlockSpec((None, tb * rows, FOX_HD), lambda bi, i: (bi, i, 0))],
        out_shape=[jax.ShapeDtypeStruct((b, t, LANES), F32)] * 2 + [jax.ShapeDtypeStruct((b, 8, t), F32),
                                                                  jax.ShapeDtypeStruct((b, t * rows, FOX_HD), F32)],
        scratch_shapes=[pltpu.VMEM((1, LANES), F32)],
        compiler_params=_cp("parallel", "arbitrary"),
        name="fox_prep",
    )(small, bias_row, fox, fox)


def _fox_attn_kernel(q_ref, k_ref, v_ref, cq_ref, ck_ref, o_ref, m_sc, l_sc, acc_sc, *, tq, tk):
    qi = pl.program_id(1)
    ki = pl.program_id(2)

    @pl.when(ki == 0)
    def _():
        m_sc[...] = jnp.full_like(m_sc, NEG)
        l_sc[...] = jnp.zeros_like(l_sc)
        acc_sc[...] = jnp.zeros_like(acc_sc)

    def step(diagonal):
        mask = _iota((tk, tq), 0) <= _iota((tk, tq), 1)
        heads = range(FOX_H)
        hsl = [slice(h * FOX_HD, (h + 1) * FOX_HD) for h in heads]
        ss = []
        for h in heads:
            q = (q_ref[:, hsl[h]] * (FOX_HD ** -0.5)).astype(BF16)
            s = lax.dot_general(k_ref[:, hsl[h]].astype(BF16), q, NT, preferred_element_type=F32)
            s = s + cq_ref[h:h + 1, :] - ck_ref[:, h:h + 1]
            ss.append(jnp.where(mask, s, NEG) if diagonal else s)
        m_prev = [m_sc[h] for h in heads]
        m_new = [jnp.maximum(m_prev[h], jnp.max(ss[h], axis=0, keepdims=True)) for h in heads]
        ps = []
        for h in heads:
            p = jnp.exp(ss[h] - m_new[h])
            ps.append(jnp.where(mask, p, 0.0) if diagonal else p)
        pvs = [jnp.dot(v_ref[:, hsl[h]].T.astype(BF16), ps[h].astype(BF16), preferred_element_type=F32)
               for h in heads]
        for h in heads:
            alpha = jnp.exp(m_prev[h] - m_new[h])
            l_sc[h] = alpha * l_sc[h] + jnp.sum(ps[h], axis=0, keepdims=True)
            acc_sc[h] = alpha * acc_sc[h] + pvs[h]
            m_sc[h] = m_new[h]

    pl.when(ki < qi)(functools.partial(step, False))
    pl.when(ki == qi)(functools.partial(step, True))

    @pl.when(ki == pl.num_programs(2) - 1)
    def _():
        for h in range(FOX_H):
            o_ref[:, h * FOX_HD:(h + 1) * FOX_HD] = (acc_sc[h] / l_sc[h]).T.astype(o_ref.dtype)


def _fox_attn(fox, c, ct):
    b, t, _ = fox.shape
    tq = tk = min(t, 512)

    def kmap(col):
        return lambda bi, qi, ki: (bi, jnp.minimum(ki, (qi * tq + tq - 1) // tk), col)

    return pl.pallas_call(
        functools.partial(_fox_attn_kernel, tq=tq, tk=tk),
        grid=(b, t // tq, t // tk),
        in_specs=[
            pl.BlockSpec((None, tq, MIX_W), lambda bi, qi, ki: (bi, qi, 0)),
            pl.BlockSpec((None, tk, MIX_W), kmap(1)),
            pl.BlockSpec((None, tk, MIX_W), kmap(2)),
            pl.BlockSpec((None, 8, tq), lambda bi, qi, ki: (bi, 0, qi)),
            pl.BlockSpec((None, tk, LANES), kmap(0)),
        ],
        out_specs=pl.BlockSpec((None, tq, MIX_W), lambda bi, qi, ki: (bi, qi, 0)),
        out_shape=jax.ShapeDtypeStruct((b, t, MIX_W), BF16),
        scratch_shapes=[pltpu.VMEM((FOX_H, 1, tq), F32), pltpu.VMEM((FOX_H, 1, tq), F32),
                        pltpu.VMEM((FOX_H, FOX_HD, tq), F32)],
        compiler_params=_cp("parallel", "parallel", "arbitrary"),
        name="fox_attn",
    )(fox, fox, fox, ct, c)


def _gla_kernel(g_ref, sm_ref, w2_ref, ba_ref, gn_ref, s0_ref, o_ref, sout_ref, s_sc, *, C):
    c = pl.program_id(1)

    @pl.when(c == 0)
    def _():
        s_sc[...] = s0_ref[...]

    lane = _iota((C, LANES), 1)
    ga = jnp.where((lane >= SM_GA) & (lane < SM_GA + GLA_RANK), sm_ref[...], 0.0)
    pre = jnp.dot(ga, w2_ref[...], precision=HI, preferred_element_type=F32) + ba_ref[...]
    loga = _log_sigmoid(pre) * (1.0 / GLA_TAU)
    tri = (_iota((C, C), 1) <= _iota((C, C), 0)).astype(F32)
    b_all = jnp.dot(tri, loga, precision=HI, preferred_element_type=F32)
    eye = (_iota((GLA_DK, GLA_DK), 0) == _iota((GLA_DK, GLA_DK), 1)).astype(F32)
    heads, blocks = range(GLA_H), range(C // SUB)
    kofs, vofs = GLA_H * GLA_DK, 2 * GLA_H * GLA_DK
    qs = [g_ref[:, h * GLA_DK:(h + 1) * GLA_DK] * (GLA_DK ** -0.5) for h in heads]
    ks = [g_ref[:, kofs + h * GLA_DK:kofs + (h + 1) * GLA_DK] for h in heads]
    vbs = [g_ref[:, vofs + h * GLA_DV:vofs + (h + 1) * GLA_DV].astype(BF16) for h in heads]
    bhs = [b_all[:, h * GLA_DK:(h + 1) * GLA_DK] for h in heads]
    states = [s_sc[h] for h in heads]
    o_inter = [jnp.dot((qs[h] * jnp.exp(bhs[h])).astype(BF16), states[h].astype(BF16), preferred_element_type=F32)
               for h in heads]
    atts = {}
    for h in heads:
        for ib in blocks[1:]:
            a0 = ib * SUB
            bi = bhs[h][a0:a0 + SUB]
            r = bi[0:1]
            qe = (qs[h][a0:a0 + SUB] * jnp.exp(bi - r)).astype(BF16)
            ke = (ks[h][:a0] * jnp.exp(r - bhs[h][:a0])).astype(BF16)
            atts[h, ib] = lax.dot_general(qe, ke, NT, preferred_element_type=F32)
    pair = 2 * GLA_DK
    mask3 = _iota((SUB, SUB, pair), 1) <= _iota((SUB, SUB, pair), 0)
    low = _iota((SUB, SUB, pair), 2) < GLA_DK
    ds = {}
    for hp in range(GLA_H // 2):
        q2 = g_ref[:, hp * pair:(hp + 1) * pair] * (GLA_DK ** -0.5)
        k2 = g_ref[:, kofs + hp * pair:kofs + (hp + 1) * pair]
        b2 = b_all[:, hp * pair:(hp + 1) * pair]
        for ib in blocks:
            rs = slice(ib * SUB, (ib + 1) * SUB)
            bi = b2[rs]
            diff = bi[:, None, :] - bi[None, :, :]
            e = jnp.where(mask3, jnp.exp(jnp.where(mask3, diff, 0.0)), 0.0)
            prod = q2[rs][:, None, :] * k2[rs][None, :, :] * e
            ds[2 * hp, ib] = jnp.sum(jnp.where(low, prod, 0.0), axis=-1)
            ds[2 * hp + 1, ib] = jnp.sum(jnp.where(low, 0.0, prod), axis=-1)
    rows = {}
    for h in heads:
        for ib in blocks:
            a0 = ib * SUB
            o_i = jnp.dot(ds[h, ib].astype(BF16), vbs[h][a0:a0 + SUB], preferred_element_type=F32)
            if ib > 0:
                o_i = o_i + jnp.dot(atts[h, ib].astype(BF16), vbs[h][:a0], preferred_element_type=F32)
            rows[h, ib] = o_i
    for h in heads:
        bend = bhs[h][C - 1:C]
        kdec = (ks[h] * jnp.exp(bend - bhs[h])).astype(BF16)
        dcol = jnp.sum(eye * jnp.exp(bend), axis=1, keepdims=True)
        s_sc[h] = dcol * states[h] + lax.dot_general(kdec, vbs[h], TN, preferred_element_type=F32)
    for h in heads:
        o = jnp.concatenate([rows[h, ib] for ib in blocks], axis=0) + o_inter[h]
        y = o * lax.rsqrt(jnp.mean(o * o, axis=-1, keepdims=True) + RMS_EPS) * gn_ref[...]
        gr = g_ref[:, vofs + GLA_H * GLA_DV + h * GLA_DV:vofs + GLA_H * GLA_DV + (h + 1) * GLA_DV]
        o_ref[:, h * GLA_DV:(h + 1) * GLA_DV] = (y * jax.nn.silu(gr)).astype(o_ref.dtype)

    @pl.when(c == pl.num_programs(1) - 1)
    def _():
        sout_ref[...] = s_sc[...]


def _gla(gla, small, w2p, ba_row, gn_row, s0):
    b, t, _ = gla.shape
    C = CHUNK
    n_gla = gla.shape[-1]
    return pl.pallas_call(
        functools.partial(_gla_kernel, C=C),
        grid=(b, t // C),
        in_specs=[
            pl.BlockSpec((None, C, n_gla), lambda bi, c: (bi, c, 0)),
            pl.BlockSpec((None, C, LANES), lambda bi, c: (bi, c, 0)),
            pl.BlockSpec(w2p.shape, lambda bi, c: (0, 0)),
            pl.BlockSpec(ba_row.shape, lambda bi, c: (0, 0)),
            pl.BlockSpec(gn_row.shape, lambda bi, c: (0, 0)),
            pl.BlockSpec((None, GLA_H, GLA_DK, GLA_DV), lambda bi, c: (bi, 0, 0, 0)),
        ],
        out_specs=[
            pl.BlockSpec((None, C, MIX_W), lambda bi, c: (bi, c, 0)),
            pl.BlockSpec((None, GLA_H, GLA_DK, GLA_DV), lambda bi, c: (bi, 0, 0, 0)),
        ],
        out_shape=[jax.ShapeDtypeStruct((b, t, MIX_W), BF16), jax.ShapeDtypeStruct(s0.shape, F32)],
        scratch_shapes=[pltpu.VMEM((GLA_H, GLA_DK, GLA_DV), F32)],
        compiler_params=_cp("parallel", "arbitrary"),
        name="gla",
    )(gla, small, w2p, ba_row, gn_row, s0)


def _rope(x, cos, sa, sb):
    half = ROPE_DIM // 2
    return x * cos + pltpu.roll(x, LANES - half, 1) * sa + pltpu.roll(x, half, 1) * sb


def _nsa_prep_kernel(x_ref, cos_ref, sa_ref, sb_ref, q_ref, cmp_ref, slc_ref, win_ref, *mean_ref, tb):
    cos, sa, sb = cos_ref[...], sa_ref[...], sb_ref[...]
    for h in range(NSA_H):
        hs = slice(h * NSA_HD, (h + 1) * NSA_HD)
        q_ref[:, hs] = _rope(x_ref[:, hs], cos, sa, sb) * (NSA_HD ** -0.5)
    base = NSA_H * NSA_HD
    for i, ref in enumerate((cmp_ref, slc_ref, win_ref)):
        k0 = base + i * 2 * NSA_HD
        ref[:, :NSA_HD] = _rope(x_ref[:, k0:k0 + NSA_HD], cos, sa, sb)
        ref[:, NSA_HD:] = x_ref[:, k0 + NSA_HD:k0 + 2 * NSA_HD]
    if mean_ref:
        kv = cmp_ref[...]
        mean_ref[0][...] = jnp.mean(kv.reshape(tb // NSA_BLOCK, NSA_BLOCK, 2 * NSA_HD), axis=1)


def _nsa_prep(nsa, cos, sa, sb, with_means):
    b, t, n = nsa.shape
    tb = min(t, 512)
    tab = pl.BlockSpec((tb, LANES), lambda bi, i: (i, 0))
    kv = pl.BlockSpec((None, tb, 2 * NSA_HD), lambda bi, i: (bi, i, 0))
    out_specs = [pl.BlockSpec((None, tb, MIX_W), lambda bi, i: (bi, i, 0)), kv, kv, kv]
    out_shape = [jax.ShapeDtypeStruct((b, t, MIX_W), F32)] + [jax.ShapeDtypeStruct((b, t, 2 * NSA_HD), F32)] * 3
    if with_means:
        out_specs.append(pl.BlockSpec((None, tb // NSA_BLOCK, 2 * NSA_HD), lambda bi, i: (bi, i, 0)))
        out_shape.append(jax.ShapeDtypeStruct((b, t // NSA_BLOCK, 2 * NSA_HD), F32))
    return pl.pallas_call(
        functools.partial(_nsa_prep_kernel, tb=tb),
        grid=(b, t // tb),
        in_specs=[pl.BlockSpec((None, tb, n), lambda bi, i: (bi, i, 0)), tab, tab, tab],
        out_specs=out_specs,
        out_shape=out_shape,
        compiler_params=_cp("parallel", "parallel"),
        name="nsa_prep",
    )(nsa, cos, sa, sb)


def _masked_softmax(s, mask):
    s = jnp.where(mask, s, -jnp.inf)
    m = jnp.max(s, axis=-1, keepdims=True)
    m = jnp.where(m == -jnp.inf, 0.0, m)
    p = jnp.where(mask, jnp.exp(s - m), 0.0)
    return p / jnp.maximum(jnp.sum(p, axis=-1, keepdims=True), 1e-30)


def _topk_mask_t(score_t, blk_t, n_sel):
    rank = jnp.zeros(score_t.shape, jnp.int32)
    for m in range(score_t.shape[0]):
        sm = score_t[m:m + 1, :]
        beats = (sm > score_t) | ((sm == score_t) & (blk_t > m))
        rank = rank + beats.astype(jnp.int32)
    return rank < n_sel


def _masked_scores(qs, k, mask):
    bias = jnp.where(mask, 0.0, NEG)
    return [lax.dot_general(k, q, NT, preferred_element_type=F32) + bias for q in qs]


def _softmax_step(carry, ss, v_t):
    ms, ls, accs = carry
    n = range(len(ss))
    m_new = [jnp.maximum(ms[g], jnp.max(ss[g], axis=0, keepdims=True)) for g in n]
    ps = [jnp.exp(ss[g] - m_new[g]) for g in n]
    pvs = [jnp.dot(v_t, ps[g].astype(BF16), preferred_element_type=F32) for g in n]
    alphas = [jnp.exp(ms[g] - m_new[g]) for g in n]
    ls = [alphas[g] * ls[g] + jnp.sum(ps[g], axis=0, keepdims=True) for g in n]
    accs = [alphas[g] * accs[g] + pvs[g] for g in n]
    return m_new, ls, accs


def _nsa_attn_kernel(q_ref, sm_ref, cm_ref, slc_ref, win_ref, o_ref, *, tq, kt, nb):
    qi = pl.program_id(1)
    qpos = qi * tq + _iota((tq, 1), 0)
    blk = _iota((tq, nb), 1)
    cmask = (blk + 1) * NSA_BLOCK <= qpos + 1
    kc, vc = cm_ref[:, :NSA_HD], cm_ref[:, NSA_HD:]
    o_c = []
    imp = jnp.zeros((tq, nb), F32)
    for h in range(NSA_H):
        s = lax.dot_general(q_ref[:, h * NSA_HD:(h + 1) * NSA_HD], kc, NT, precision=HI, preferred_element_type=F32)
        p = _masked_softmax(s, cmask)
        imp = imp + p
        o_c.append(jnp.dot(p, vc, precision=HI, preferred_element_type=F32))
    blk_t = _iota((nb, tq), 0)
    qpos_t = qi * tq + _iota((nb, tq), 1)
    cur_t = qpos_t // NSA_BLOCK
    score_t = jnp.where(blk_t == cur_t, jnp.inf,
                        jnp.where((blk_t + 1) * NSA_BLOCK <= qpos_t + 1, imp.T, -jnp.inf))
    sel_b = (_topk_mask_t(score_t, blk_t, min(NSA_TOPK, nb)) & (blk_t <= cur_t)).astype(BF16)
    qs = [q_ref[:, h * NSA_HD:(h + 1) * NSA_HD].astype(BF16) for h in range(NSA_H)]
    krow = _iota((kt, tq), 0)
    qpos_row = qi * tq + _iota((1, tq), 1)
    blk_of_col = _iota((nb, kt), 1) // NSA_BLOCK - _iota((nb, kt), 0)
    last = (qi * tq + tq - 1) // kt

    def init():
        return ([jnp.full((1, tq), NEG, F32)] * NSA_H, [jnp.zeros((1, tq), F32)] * NSA_H,
                [jnp.zeros((NSA_HD, tq), F32)] * NSA_H)

    def tile_rows(kb):
        return pl.ds(pl.multiple_of(kb * kt, kt), kt)

    def slc_scores(kb):
        expand = (blk_of_col + kb * (kt // NSA_BLOCK) == 0).astype(BF16)
        tok = lax.dot_general(expand, sel_b, TN, preferred_element_type=F32)
        mask = (tok > 0.5) & (kb * kt + krow <= qpos_row)
        return _masked_scores(qs, slc_ref[tile_rows(kb), :NSA_HD].astype(BF16), mask)

    def win_scores(kb):
        dist = qpos_row - (kb * kt + krow)
        mask = (dist >= 0) & (dist < NSA_WINDOW)
        return _masked_scores(qs, win_ref[tile_rows(kb), :NSA_HD].astype(BF16), mask)

    def pipelined(scores_of, kv_ref, first):
        def body(kb, carry):
            state, ss = carry
            ss_next = scores_of(jnp.minimum(kb + 1, last))
            v_t = kv_ref[tile_rows(kb), NSA_HD:].T.astype(BF16)
            return _softmax_step(state, ss, v_t), ss_next

        (_, ls, accs), _ = lax.fori_loop(first, last + 1, body, (init(), scores_of(first)))
        return ls, accs

    l_s, acc_s = pipelined(slc_scores, slc_ref, 0)
    l_w, acc_w = pipelined(win_scores, win_ref, jnp.maximum(qi * tq - (NSA_WINDOW - 1), 0) // kt)
    gate = jax.nn.sigmoid(sm_ref[...])
    for h in range(NSA_H):
        g0 = SM_NG + 3 * h
        o = (gate[:, g0:g0 + 1] * o_c[h] + gate[:, g0 + 1:g0 + 2] * (acc_s[h] / l_s[h]).T
             + gate[:, g0 + 2:g0 + 3] * (acc_w[h] / l_w[h]).T)
        o_ref[:, h * NSA_HD:(h + 1) * NSA_HD] = o.astype(o_ref.dtype)


def _nsa_attn(qr, small, cmean, slc, win):
    b, t, _ = qr.shape
    tq = 128
    kt = min(t, 512)
    nb = t // NSA_BLOCK
    whole = lambda n: pl.BlockSpec((None, n, 2 * NSA_HD), lambda bi, qi: (bi, 0, 0))
    return pl.pallas_call(
        functools.partial(_nsa_attn_kernel, tq=tq, kt=kt, nb=nb),
        grid=(b, t // tq),
        in_specs=[
            pl.BlockSpec((None, tq, MIX_W), lambda bi, qi: (bi, qi, 0)),
            pl.BlockSpec((None, tq, LANES), lambda bi, qi: (bi, qi, 0)),
            whole(nb), whole(t), whole(t),
        ],
        out_specs=pl.BlockSpec((None, tq, MIX_W), lambda bi, qi: (bi, qi, 0)),
        out_shape=jax.ShapeDtypeStruct((b, t, MIX_W), BF16),
        compiler_params=_cp("parallel", "parallel"),
        name="nsa_attn",
    )(qr, small, cmean, slc, win)


def _dn_prep_kernel(x_ref, halo_ref, prev_ref, cw_ref, sm_ref, alog_ref, dtb_ref,
                    q_ref, k_ref, v_ref, g_ref, bt_ref, *, tb):
    halo = jnp.where(pl.program_id(1) == 0, prev_ref[...], halo_ref[...])
    xcat = jnp.concatenate([halo, x_ref[...]], axis=0)
    conv = x_ref[...] * cw_ref[DN_CONV - 1:DN_CONV]
    for j in range(DN_CONV - 1):
        sh = DN_CONV - 1 - j
        conv = conv + pltpu.roll(xcat, sh, 0)[8:] * cw_ref[j:j + 1]
    u = jax.nn.silu(conv)
    for h in range(DN_H):
        hs = slice(h * DN_DK, (h + 1) * DN_DK)
        uq = u[:, hs]
        q_ref[:, hs] = uq * lax.rsqrt(jnp.sum(uq * uq, axis=-1, keepdims=True) + 1e-6) * (DN_DK ** -0.5)
        uk = u[:, DN_H * DN_DK + h * DN_DK:DN_H * DN_DK + (h + 1) * DN_DK]
        k_ref[:, hs] = uk * lax.rsqrt(jnp.sum(uk * uk, axis=-1, keepdims=True) + 1e-6)
    v_ref[...] = u[:, 2 * DN_H * DN_DK:]
    sm = sm_ref[...]
    g_ref[...] = -jnp.exp(alog_ref[...]) * jax.nn.softplus(sm + dtb_ref[...])
    bt_ref[...] = jax.nn.sigmoid(sm)


def _dn_prep(dn, prev8, conv_w, small, alog_row, dtb_row):
    b, t, _ = dn.shape
    tb = min(t, 256)
    hb = tb // 8
    row = pl.BlockSpec((None, tb, MIX_W), lambda bi, i: (bi, i, 0))
    sm = pl.BlockSpec((None, tb, LANES), lambda bi, i: (bi, i, 0))
    one = lambda shape: pl.BlockSpec(shape, lambda bi, i: (0,) * len(shape))
    return pl.pallas_call(
        functools.partial(_dn_prep_kernel, tb=tb),
        grid=(b, t // tb),
        in_specs=[
            pl.BlockSpec((None, tb, DN_QKV), lambda bi, i: (bi, i, 0)),
            pl.BlockSpec((None, 8, DN_QKV), lambda bi, i: (bi, jnp.maximum(i * hb - 1, 0), 0)),
            pl.BlockSpec((None, 8, DN_QKV), lambda bi, i: (bi, 0, 0)),
            one(conv_w.shape), sm, one(alog_row.shape), one(dtb_row.shape),
        ],
        out_specs=[row, row, row, sm, sm],
        out_shape=[jax.ShapeDtypeStruct((b, t, MIX_W), F32)] * 3 + [jax.ShapeDtypeStruct((b, t, LANES), F32)] * 2,
        compiler_params=_cp("parallel", "parallel"),
        name="dn_prep",
    )(dn, dn, prev8, conv_w, small, alog_row, dtb_row)


def _gdn_pre_kernel(q_ref, k_ref, v_ref, g_ref, bt_ref, wv_ref, wk_ref, qe_ref, kd_ref, qk_ref, gam_ref, *, C, nck):
    row, col = _iota((C, C), 0), _iota((C, C), 1)
    incl, strict = col <= row, col < row
    eye = (row == col).astype(F32)
    tri = incl.astype(F32)
    n_sq = C.bit_length() - 2
    chains = []
    for c in range(nck):
        rs = slice(c * C, (c + 1) * C)
        gam_all = jnp.dot(tri, g_ref[rs, :], precision=HI, preferred_element_type=F32)
        gam_ref[rs, :] = gam_all
        gam_t = gam_all.T
        for h in range(DN_H):
            hs = slice(h * DN_DK, (h + 1) * DN_DK)
            q, k = q_ref[rs, hs], k_ref[rs, hs]
            gcol = gam_all[:, SM_DA + h:SM_DA + h + 1]
            bcol = bt_ref[rs, SM_DB + h:SM_DB + h + 1]
            diff = gcol - gam_t[SM_DA + h:SM_DA + h + 1, :]
            dec_incl = jnp.where(incl, jnp.exp(jnp.where(incl, diff, 0.0)), 0.0)
            qk = lax.dot_general(q.astype(BF16), k.astype(BF16), NT, preferred_element_type=F32) * dec_incl
            qk_ref[rs, h * C:(h + 1) * C] = qk.astype(BF16)
            qe_ref[rs, hs] = (q * jnp.exp(gcol)).astype(BF16)
            kd_ref[rs, hs] = (k * jnp.exp(gcol[C - 1:C] - gcol)).astype(BF16)
            x = -(bcol * _dot3(k, k, NT) * jnp.where(strict, dec_incl, 0.0))
            chains.append((rs, hs, x, gcol, bcol))
    xs = [ch[2] for ch in chains]
    ps = [eye + x for x in xs]
    for _ in range(n_sq):
        xs = [_dot3(x, x) for x in xs]
        ps = [p + _dot3(p, x) for p, x in zip(ps, xs)]
    for (rs, hs, _, gcol, bcol), p in zip(chains, ps):
        k, v = k_ref[rs, hs], v_ref[rs, hs]
        rhs = jnp.concatenate([bcol * v, (bcol * jnp.exp(gcol)) * k], axis=1)
        w = _dot3(p, rhs)
        wv_ref[rs, hs] = w[:, :DN_DV]
        wk_ref[rs, hs] = w[:, DN_DV:].astype(BF16)


def _gdn_rec_kernel(wv_ref, wk_ref, qe_ref, kd_ref, qk_ref, gam_ref, z_ref, nrm_ref, s0_ref, o_ref, sout_ref, s_sc,
                    *, C, nck):
    i = pl.program_id(1)

    @pl.when(i == 0)
    def _():
        s_sc[...] = s0_ref[...]

    state = [s_sc[h] for h in range(DN_H)]
    heads = range(DN_H)
    hsl = [slice(h * DN_DK, (h + 1) * DN_DK) for h in heads]
    for c in range(nck):
        rs = slice(c * C, (c + 1) * C)
        sbs = [state[h].astype(BF16) for h in heads]
        ubs = [(wv_ref[rs, hsl[h]] - jnp.dot(wk_ref[rs, hsl[h]], sbs[h], preferred_element_type=F32)).astype(BF16)
               for h in heads]
        outs = [jnp.dot(qe_ref[rs, hsl[h]], sbs[h], preferred_element_type=F32)
                + jnp.dot(qk_ref[rs, h * C:(h + 1) * C], ubs[h], preferred_element_type=F32) for h in heads]
        gend = gam_ref[c * C + C - 1:(c + 1) * C, :]
        state = [jnp.exp(gend[:, SM_DA + h:SM_DA + h + 1]) * state[h]
                 + lax.dot_general(kd_ref[rs, hsl[h]], ubs[h], TN, preferred_element_type=F32) for h in heads]
        for h in heads:
            o = outs[h]
            y = o * lax.rsqrt(jnp.mean(o * o, axis=-1, keepdims=True) + RMS_EPS) * nrm_ref[...]
            o_ref[rs, hsl[h]] = (y * jax.nn.silu(z_ref[rs, hsl[h]])).astype(o_ref.dtype)
    for h in range(DN_H):
        s_sc[h] = state[h]

    @pl.when(i == pl.num_programs(1) - 1)
    def _():
        for h in range(DN_H):
            sout_ref[h] = state[h]


GDN_BLOCK = 256


def _gdn(qd, kd, vd, gsm, btsm, dn, nrm_row, s0):
    b, t, _ = qd.shape
    C = CHUNK
    tb = min(t, GDN_BLOCK)
    nck = tb // C
    row = pl.BlockSpec((None, tb, MIX_W), lambda bi, i: (bi, i, 0))
    sm = pl.BlockSpec((None, tb, LANES), lambda bi, i: (bi, i, 0))
    qk_spec = pl.BlockSpec((None, tb, DN_H * C), lambda bi, i: (bi, i, 0))
    st = pl.BlockSpec((None, DN_H, DN_DK, DN_DV), lambda bi, i: (bi, 0, 0, 0))
    wide = lambda dt: jax.ShapeDtypeStruct((b, t, MIX_W), dt)
    wv, wk, qe, kdc, qk, gam = pl.pallas_call(
        functools.partial(_gdn_pre_kernel, C=C, nck=nck),
        grid=(b, t // tb),
        in_specs=[row, row, row, sm, sm],
        out_specs=[row, row, row, row, qk_spec, sm],
        out_shape=[wide(F32), wide(BF16), wide(BF16), wide(BF16), jax.ShapeDtypeStruct((b, t, DN_H * C), BF16),
                   jax.ShapeDtypeStruct((b, t, LANES), F32)],
        compiler_params=_cp("parallel", "parallel"),
        name="gdn_pre",
    )(qd, kd, vd, gsm, btsm)
    return pl.pallas_call(
        functools.partial(_gdn_rec_kernel, C=C, nck=nck),
        grid=(b, t // tb),
        in_specs=[row, row, row, row, qk_spec, sm,
                  pl.BlockSpec((None, tb, MIX_W), lambda bi, i: (bi, i, DN_QKV // MIX_W)),
                  pl.BlockSpec(nrm_row.shape, lambda bi, i: (0, 0)), st],
        out_specs=[row, st],
        out_shape=[wide(BF16), jax.ShapeDtypeStruct(s0.shape, F32)],
        scratch_shapes=[pltpu.VMEM((DN_H, DN_DK, DN_DV), F32)],
        compiler_params=_cp("parallel", "arbitrary"),
        name="gdn_rec",
    )(wv, wk, qe, kdc, qk, gam, dn, nrm_row, s0)


FOX_PAGES = 8


def _fox_dec_kernel(pt_ref, fx_ref, sm_ref, b_ref, *refs):
    kv_refs, lf_refs = refs[:FOX_PAGES], refs[FOX_PAGES:2 * FOX_PAGES]
    o_ref, lfo_ref, m_sc, l_sc, acc_sc, car_sc = refs[2 * FOX_PAGES:]
    j = pl.program_id(1)
    P = PAGE_SIZE
    rows_per_tok = 2 * FOX_H
    qs = [fx_ref[:, h * FOX_HD:(h + 1) * FOX_HD] * (FOX_HD ** -0.5) for h in range(FOX_H)]

    @pl.when(j == 0)
    def _():
        lf_new = _log_sigmoid(sm_ref[...] + b_ref[...])
        lfo_ref[...] = lf_new
        on_diag = _iota((8, LANES), 1) == _iota((8, LANES), 0)
        car_sc[...] = jnp.broadcast_to(jnp.sum(jnp.where(on_diag, lf_new, 0.0), axis=1, keepdims=True), (8, LANES))
        for h in range(FOX_H):
            k_new = fx_ref[:, MIX_W + h * FOX_HD:MIX_W + (h + 1) * FOX_HD]
            m_sc[h] = jnp.sum(qs[h] * k_new, axis=-1, keepdims=True)
            l_sc[h] = jnp.ones((1, 1), F32)
            acc_sc[h] = fx_ref[:, 2 * MIX_W + h * FOX_HD:2 * MIX_W + (h + 1) * FOX_HD]

    upper = (_iota((P, P), 1) > _iota((P, P), 0)).astype(F32)
    carry = car_sc[...]
    pad = jnp.zeros((8 - FOX_H, P), F32)
    scores = []
    for kv_ref, lf_ref in zip(kv_refs, lf_refs):
        lf_t = jnp.concatenate([lf_ref[...], pad], axis=0)
        bias = (lax.dot_general(lf_t, upper, NT, precision=HI, preferred_element_type=F32) + carry).T
        scores.append([jnp.sum(kv_ref[pl.ds(h, P, stride=rows_per_tok), :] * qs[h], axis=-1, keepdims=True)
                       + bias[:, h:h + 1] for h in range(FOX_H)])
        carry = carry + jnp.sum(lf_t, axis=1, keepdims=True)
    car_sc[...] = carry
    outs = []
    for h in range(FOX_H):
        m_step = jnp.max(scores[0][h], axis=0, keepdims=True)
        for g in range(1, FOX_PAGES):
            m_step = jnp.maximum(m_step, jnp.max(scores[g][h], axis=0, keepdims=True))
        m_prev = m_sc[h]
        m_new = jnp.maximum(m_prev, m_step)
        alpha = jnp.exp(m_prev - m_new)
        l_new = alpha * l_sc[h]
        acc = alpha * acc_sc[h]
        for g, kv_ref in enumerate(kv_refs):
            p = jnp.exp(scores[g][h] - m_new)
            l_new = l_new + jnp.sum(p, axis=0, keepdims=True)
            acc = acc + jnp.sum(p * kv_ref[pl.ds(FOX_H + h, P, stride=rows_per_tok), :], axis=0, keepdims=True)
        m_sc[h], l_sc[h], acc_sc[h] = m_new, l_new, acc
        outs.append(acc / l_new)

    @pl.when(j == pl.num_programs(1) - 1)
    def _():
        for h in range(FOX_H):
            o_ref[:, h * FOX_HD:(h + 1) * FOX_HD] = outs[h]


def _fox_decode(pt_flat, n_pages, page0, fox_new, small_new, bias_row, kv_cache, lf_cache):
    b = fox_new.shape[0]
    assert n_pages % FOX_PAGES == 0

    def page(r):
        return lambda bi, j, pt: (page0 + pt[bi * n_pages + n_pages - 1 - (j * FOX_PAGES + r)], 0, 0)

    per_b = lambda n: pl.BlockSpec((None, 1, n), lambda bi, j, pt: (bi, 0, 0))
    return pl.pallas_call(
        _fox_dec_kernel,
        grid_spec=pltpu.PrefetchScalarGridSpec(
            num_scalar_prefetch=1,
            grid=(b, n_pages // FOX_PAGES),
            in_specs=[per_b(3 * MIX_W), per_b(LANES), pl.BlockSpec((1, LANES), lambda bi, j, pt: (0, 0))]
            + [pl.BlockSpec((None, PAGE_SIZE * 2 * FOX_H, FOX_HD), page(r)) for r in range(FOX_PAGES)]
            + [pl.BlockSpec((None, FOX_H, PAGE_SIZE), page(r)) for r in range(FOX_PAGES)],
            out_specs=[per_b(MIX_W), per_b(LANES)],
            scratch_shapes=[pltpu.VMEM((FOX_H, 1, 1), F32), pltpu.VMEM((FOX_H, 1, 1), F32),
                            pltpu.VMEM((FOX_H, 1, FOX_HD), F32), pltpu.VMEM((8, PAGE_SIZE), F32)],
        ),
        out_shape=[jax.ShapeDtypeStruct((b, 1, MIX_W), F32), jax.ShapeDtypeStruct((b, 1, LANES), F32)],
        compiler_params=_cp("parallel", "arbitrary"),
        name="fox_decode",
    )(pt_flat, fox_new, small_new, bias_row, *([kv_cache] * FOX_PAGES), *([lf_cache] * FOX_PAGES))


def _col(row, eye):
    return jnp.sum(eye * row, axis=1, keepdims=True)


def _rec_dec_kernel(g_ref, d_ref, sm_ref, w2_ref, ba_ref, gn_ref, cp_ref, cw_ref, alog_ref, dtb_ref, nrm_ref,
                    sg_ref, sd_ref, og_ref, od_ref, sgo_ref, sdo_ref):
    sm = sm_ref[...]
    lane = _iota((8, LANES), 1)
    ga = jnp.where((lane >= SM_GA) & (lane < SM_GA + GLA_RANK), jnp.broadcast_to(sm, (8, LANES)), 0.0)
    pre = jnp.dot(ga, w2_ref[...], precision=HI, preferred_element_type=F32)[0:1] + ba_ref[...]
    loga = _log_sigmoid(pre) * (1.0 / GLA_TAU)
    eye_k = (_iota((GLA_DK, GLA_DK), 0) == _iota((GLA_DK, GLA_DK), 1)).astype(F32)
    for h in range(GLA_H):
        q = g_ref[:, h * GLA_DK:(h + 1) * GLA_DK] * (GLA_DK ** -0.5)
        k = g_ref[:, GLA_H * GLA_DK + h * GLA_DK:GLA_H * GLA_DK + (h + 1) * GLA_DK]
        v0 = 2 * GLA_H * GLA_DK + h * GLA_DV
        v = g_ref[:, v0:v0 + GLA_DV]
        gr = g_ref[:, v0 + GLA_H * GLA_DV:v0 + GLA_H * GLA_DV + GLA_DV]
        ea = jnp.exp(loga[:, h * GLA_DK:(h + 1) * GLA_DK])
        s0 = sg_ref[h]
        o = jnp.sum(q * k, axis=-1, keepdims=True) * v + jnp.sum(_col(q * ea, eye_k) * s0, axis=0, keepdims=True)
        sgo_ref[h] = _col(ea, eye_k) * s0 + _col(k, eye_k) * v
        y = o * lax.rsqrt(jnp.mean(o * o, axis=-1, keepdims=True) + RMS_EPS) * gn_ref[...]
        og_ref[:, h * GLA_DV:(h + 1) * GLA_DV] = y * jax.nn.silu(gr)
    conv = d_ref[:, :DN_QKV] * cw_ref[DN_CONV - 1:DN_CONV]
    for j in range(DN_CONV - 1):
        conv = conv + cp_ref[j:j + 1] * cw_ref[j:j + 1]
    u = jax.nn.silu(conv)
    gall = -jnp.exp(alog_ref[...]) * jax.nn.softplus(sm + dtb_ref[...])
    ball = jax.nn.sigmoid(sm)
    eye_d = (_iota((DN_DK, DN_DK), 0) == _iota((DN_DK, DN_DK), 1)).astype(F32)
    for h in range(DN_H):
        uq = u[:, h * DN_DK:(h + 1) * DN_DK]
        q = uq * lax.rsqrt(jnp.sum(uq * uq, axis=-1, keepdims=True) + 1e-6) * (DN_DK ** -0.5)
        uk = u[:, DN_H * DN_DK + h * DN_DK:DN_H * DN_DK + (h + 1) * DN_DK]
        k = uk * lax.rsqrt(jnp.sum(uk * uk, axis=-1, keepdims=True) + 1e-6)
        v = u[:, 2 * DN_H * DN_DK + h * DN_DV:2 * DN_H * DN_DK + (h + 1) * DN_DV]
        eg = jnp.exp(gall[:, SM_DA + h:SM_DA + h + 1])
        beta = ball[:, SM_DB + h:SM_DB + h + 1]
        s0 = sd_ref[h]
        kcol = _col(k, eye_d)
        ks = jnp.sum(kcol * s0, axis=0, keepdims=True)
        qs = jnp.sum(_col(q, eye_d) * s0, axis=0, keepdims=True)
        un = beta * (v - eg * ks)
        o = eg * qs + jnp.sum(q * k, axis=-1, keepdims=True) * un
        sdo_ref[h] = eg * s0 + kcol * un
        y = o * lax.rsqrt(jnp.mean(o * o, axis=-1, keepdims=True) + RMS_EPS) * nrm_ref[...]
        z = d_ref[:, DN_QKV + h * DN_DV:DN_QKV + (h + 1) * DN_DV]
        od_ref[:, h * DN_DV:(h + 1) * DN_DV] = y * jax.nn.silu(z)


def _rec_decode(gla_new, dn_new, small_new, w2p, ba_row, gn_row, conv_prev, conv_w, alog_row, dtb_row, nrm_row,
                s_gla, s_dn, row0):
    b = gla_new.shape[0]
    per_b = lambda n: pl.BlockSpec((None, 1, n), lambda bi: (bi, 0, 0))
    one = lambda a: pl.BlockSpec(a.shape, lambda bi: (0,) * a.ndim)
    sg = lambda r0: pl.BlockSpec((None, GLA_H, GLA_DK, GLA_DV), lambda bi: (r0 + bi, 0, 0, 0))
    sd = lambda r0: pl.BlockSpec((None, DN_H, DN_DK, DN_DV), lambda bi: (r0 + bi, 0, 0, 0))
    return pl.pallas_call(
        _rec_dec_kernel,
        grid=(b,),
        in_specs=[per_b(gla_new.shape[-1]), per_b(dn_new.shape[-1]), per_b(LANES), one(w2p), one(ba_row), one(gn_row),
                  pl.BlockSpec((None, DN_CONV - 1, DN_QKV), lambda bi: (row0 + bi, 0, 0)), one(conv_w), one(alog_row),
                  one(dtb_row), one(nrm_row), sg(row0), sd(row0)],
        out_specs=[per_b(MIX_W), per_b(MIX_W), sg(0), sd(0)],
        out_shape=[jax.ShapeDtypeStruct((b, 1, MIX_W), F32)] * 2
        + [jax.ShapeDtypeStruct((b,) + s_gla.shape[1:], F32), jax.ShapeDtypeStruct((b,) + s_dn.shape[1:], F32)],
        compiler_params=_cp("parallel"),
        name="rec_decode",
    )(gla_new, dn_new, small_new, w2p, ba_row, gn_row, conv_prev, conv_w, alog_row, dtb_row, nrm_row, s_gla, s_dn)


def _heads_to_rows(q_row):
    rows = [q_row[:, h * NSA_HD:(h + 1) * NSA_HD] for h in range(NSA_H)]
    return jnp.concatenate(rows + [jnp.zeros((8 - NSA_H, NSA_HD), F32)], axis=0)


CMP_PAGES = 16


def _nsa_dec_cmp_kernel(pt_ref, q_ref, *refs, nbp):
    page_refs, (kc_ref, vc_ref, oc_ref, sel_ref) = refs[:CMP_PAGES], refs[CMP_PAGES:]
    j = pl.program_id(1)
    per = PAGE_SIZE // NSA_BLOCK
    rows = CMP_PAGES * per
    dst = pl.ds(pl.multiple_of(j * rows, rows), rows)
    for kv, ref in enumerate((kc_ref, vc_ref)):
        means = [jnp.mean(r[pl.ds(kv, PAGE_SIZE, stride=2), :].reshape(per, NSA_BLOCK, NSA_HD), axis=1)
                 for r in page_refs]
        ref[dst, :] = jnp.concatenate(means, axis=0)

    @pl.when(j == pl.num_programs(1) - 1)
    def _():
        q8 = _heads_to_rows(q_ref[...])
        kc, vc = kc_ref[...], vc_ref[...]
        s = lax.dot_general(q8, kc, NT, precision=HI, preferred_element_type=F32)
        p = _masked_softmax(s, jnp.full(s.shape, True))
        oc_ref[...] = jnp.dot(p, vc, precision=HI, preferred_element_type=F32)
        imp = jnp.sum(p[0:NSA_H], axis=0, keepdims=True)
        r_i, c_i = _iota((nbp, nbp), 0), _iota((nbp, nbp), 1)
        imp_col = _col(imp, (r_i == c_i).astype(F32))
        beats = (imp_col > imp) | ((imp_col == imp) & (r_i < c_i))
        rank = jnp.sum(beats.astype(jnp.int32), axis=0, keepdims=True)
        ids = jnp.where(rank == _iota((16, nbp), 0), _iota((16, nbp), 1), 0)
        sel_ref[...] = jnp.broadcast_to(jnp.sum(ids, axis=1, keepdims=True), (16, LANES))


def _nsa_dec_cmp(pt_flat, n_pages, page0, q_new, cmp_cache):
    b = q_new.shape[0]
    nbp = n_pages * (PAGE_SIZE // NSA_BLOCK)
    n_sel = NSA_TOPK - 1
    assert nbp >= n_sel and NSA_TOPK <= 16 and n_pages % CMP_PAGES == 0
    page = lambda r: pl.BlockSpec((None, 2 * PAGE_SIZE, NSA_HD),
                                  lambda bi, j, pt: (page0 + pt[bi * n_pages + j * CMP_PAGES + r], 0, 0))
    mean_spec = pl.BlockSpec((None, nbp, NSA_HD), lambda bi, j, pt: (bi, 0, 0))
    return pl.pallas_call(
        functools.partial(_nsa_dec_cmp_kernel, nbp=nbp),
        grid_spec=pltpu.PrefetchScalarGridSpec(
            num_scalar_prefetch=1,
            grid=(b, n_pages // CMP_PAGES),
            in_specs=[pl.BlockSpec((None, 1, MIX_W), lambda bi, j, pt: (bi, 0, 0))]
            + [page(r) for r in range(CMP_PAGES)],
            out_specs=[mean_spec, mean_spec,
                       pl.BlockSpec((None, 8, NSA_HD), lambda bi, j, pt: (bi, 0, 0)),
                       pl.BlockSpec((None, 16, LANES), lambda bi, j, pt: (bi, 0, 0))],
        ),
        out_shape=[jax.ShapeDtypeStruct((b, nbp, NSA_HD), F32)] * 2
        + [jax.ShapeDtypeStruct((b, 8, NSA_HD), F32), jax.ShapeDtypeStruct((b, 16, LANES), jnp.int32)],
        compiler_params=_cp("parallel", "arbitrary"),
        name="nsa_dec_cmp",
    )(pt_flat, q_new, *([cmp_cache] * CMP_PAGES))


def _nsa_dec_attn_kernel(pt_ref, sel_ref, q_ref, blk_ref, snew_ref, win_ref, wnew_ref, oc_ref, sm_ref, o_ref,
                         m_sc, l_sc, acc_sc, *, sw):
    s_i = pl.program_id(1)
    q8 = _heads_to_rows(q_ref[...])

    @pl.when(s_i == 0)
    def _():
        m_sc[...] = jnp.sum(q8 * snew_ref[:, :NSA_HD], axis=-1, keepdims=True)
        l_sc[...] = jnp.ones_like(l_sc)
        acc_sc[...] = jnp.broadcast_to(snew_ref[:, NSA_HD:], acc_sc.shape)

    k, v = blk_ref[pl.ds(0, NSA_BLOCK, stride=2), :], blk_ref[pl.ds(1, NSA_BLOCK, stride=2), :]
    s = lax.dot_general(q8, k, NT, precision=HI, preferred_element_type=F32)
    m_prev = m_sc[...]
    m_new = jnp.maximum(m_prev, jnp.max(s, axis=-1, keepdims=True))
    alpha = jnp.exp(m_prev - m_new)
    p = jnp.exp(s - m_new)
    l_sc[...] = alpha * l_sc[...] + jnp.sum(p, axis=-1, keepdims=True)
    acc_sc[...] = alpha * acc_sc[...] + jnp.dot(p, v, precision=HI, preferred_element_type=F32)
    m_sc[...] = m_new

    @pl.when(s_i == pl.num_programs(1) - 1)
    def _():
        o_s = acc_sc[...] / l_sc[...]
        kw, vw = win_ref[pl.ds(0, sw, stride=2), :], win_ref[pl.ds(1, sw, stride=2), :]
        sw_ = lax.dot_general(q8, kw, NT, precision=HI, preferred_element_type=F32)
        wmask = (sw - _iota((8, sw), 1)) < NSA_WINDOW
        s_new = jnp.sum(q8 * wnew_ref[:, :NSA_HD], axis=-1, keepdims=True)
        m = jnp.maximum(jnp.max(jnp.where(wmask, sw_, -jnp.inf), axis=-1, keepdims=True), s_new)
        pw = jnp.where(wmask, jnp.exp(sw_ - m), 0.0)
        pn = jnp.exp(s_new - m)
        o_w = (jnp.dot(pw, vw, precision=HI, preferred_element_type=F32) + pn * wnew_ref[:, NSA_HD:]) / (
            jnp.sum(pw, axis=-1, keepdims=True) + pn)
        gate = jax.nn.sigmoid(sm_ref[...])
        o_c = oc_ref[...]
        for h in range(NSA_H):
            g0 = SM_NG + 3 * h
            o_ref[:, h * NSA_HD:(h + 1) * NSA_HD] = (
                gate[:, g0:g0 + 1] * o_c[h:h + 1] + gate[:, g0 + 1:g0 + 2] * o_s[h:h + 1]
                + gate[:, g0 + 2:g0 + 3] * o_w[h:h + 1])


def _nsa_dec_attn(pt_flat, sel_flat, n_pages, n_sel, page0, row0, q_new, slc_cache, slc_new, win_cache, win_new,
                  o_c, small_new):
    b = q_new.shape[0]
    sw = win_cache.shape[1] // 2
    per = PAGE_SIZE // NSA_BLOCK
    half = slc_cache.reshape(slc_cache.shape[0] * per, 2 * NSA_BLOCK, NSA_HD)

    def blk_map(bi, s, pt, sel):
        n = sel[bi * n_sel + s]
        return ((page0 + pt[bi * n_pages + n // per]) * per + n % per, 0, 0)

    per_b = lambda r, n: pl.BlockSpec((None, r, n), lambda bi, s, pt, sel: (bi, 0, 0))
    return pl.pallas_call(
        functools.partial(_nsa_dec_attn_kernel, sw=sw),
        grid_spec=pltpu.PrefetchScalarGridSpec(
            num_scalar_prefetch=2,
            grid=(b, n_sel),
            in_specs=[per_b(1, MIX_W), pl.BlockSpec((None, 2 * NSA_BLOCK, NSA_HD), blk_map), per_b(1, 2 * NSA_HD),
                      pl.BlockSpec((None, 2 * sw, NSA_HD), lambda bi, s, pt, sel: (row0 + bi, 0, 0)),
                      per_b(1, 2 * NSA_HD), per_b(8, NSA_HD), per_b(1, LANES)],
            out_specs=per_b(1, MIX_W),
            scratch_shapes=[pltpu.VMEM((8, 1), F32), pltpu.VMEM((8, 1), F32), pltpu.VMEM((8, NSA_HD), F32)],
        ),
        out_shape=jax.ShapeDtypeStruct((b, 1, MIX_W), F32),
        compiler_params=_cp("parallel", "arbitrary"),
        name="nsa_dec_attn",
    )(pt_flat, sel_flat, q_new, half, slc_new, win_cache, win_new, o_c, small_new)


def _lane_row(vals, at):
    return jnp.zeros((1, LANES), F32).at[0, at:at + vals.shape[0]].set(vals.astype(F32))


def _stacked_weights(w_in, w_branch, w_out, w_up, w_down):
    offs = [0]
    for s in SPLIT_SIZES:
        offs.append(offs[-1] + s)
    seg = lambda i: w_in[:, :, offs[i]:offs[i + 1]]
    (fq, fk, fv, ff, gq, gk, gv, ga, gr, nq, nkc, nks, nkw, ng, dqkv, da, dbeta, dz, mg) = [seg(i) for i in range(19)]
    cat = lambda parts: jnp.concatenate(parts, axis=2).astype(BF16)
    small = [ff, ga, ng, da, dbeta]
    pad = jnp.zeros(w_in.shape[:2] + (LANES - sum(p.shape[2] for p in small),), F32)
    return dict(fox=cat([fq, fk, fv]), gla=cat([gq, gk, gv, gr]), nsa=cat([nq, nkc, nks, nkw]), dn=cat([dqkv, dz]),
                small=cat(small + [pad]), mg=mg.astype(BF16), wb=w_branch.astype(BF16), w_out=w_out.astype(BF16),
                w_up=w_up.astype(BF16), w_down=w_down.astype(BF16))


def _layer_rows(fox_b_f, gla_w_a2, gla_b_a, gla_norm, dn_conv_w, dn_a_log, dn_dt_bias, dn_norm):
    return dict(
        bf_row=_lane_row(fox_b_f, SM_FF),
        w2p=jnp.zeros((LANES, GLA_H * GLA_DK), F32).at[SM_GA:SM_GA + GLA_RANK].set(gla_w_a2),
        ba_row=gla_b_a.reshape(1, -1), gn_row=gla_norm.reshape(1, -1), conv_w=dn_conv_w,
        alog_row=_lane_row(dn_a_log, SM_DA), dtb_row=_lane_row(dn_dt_bias, SM_DA), dnn_row=dn_norm.reshape(1, -1))


def _rope_tables(pos):
    half = ROPE_DIM // 2
    inv = ROPE_THETA ** (-jnp.arange(half, dtype=F32) / half)
    ang = pos.astype(F32)[:, None] * inv[None, :]
    cos, sin = jnp.cos(ang), jnp.sin(ang)
    t = pos.shape[0]
    z = lambda n: jnp.zeros((t, n), F32)
    return (jnp.concatenate([cos, cos, jnp.ones((t, LANES - ROPE_DIM), F32)], axis=1),
            jnp.concatenate([-sin, z(LANES - half)], axis=1),
            jnp.concatenate([z(half), sin, z(LANES - ROPE_DIM)], axis=1))


def _project(h, w, l):
    return {name: _mm(h, w[name], l) for name in ("fox", "gla", "nsa", "dn", "small", "mg")}


def _mixer_prompt(h, b, t, w, l):
    p = _project(h, w, l)
    fox, gla, nsa, dn, small = (p[n].reshape(b, t, -1) for n in ("fox", "gla", "nsa", "dn", "small"))
    lf, c, ct, fox_kv = _fox_prep(small, w["bf_row"], fox)
    o_fox = _fox_attn(fox, c, ct)
    o_gla, s_gla = _gla(gla, small, w["w2p"], w["ba_row"], w["gn_row"], jnp.zeros((b, GLA_H, GLA_DK, GLA_DV), F32))
    qr, cmp_kv, slc_kv, win_kv, cmean = _nsa_prep(nsa, *_rope_tables(jnp.arange(t)), True)
    o_nsa = _nsa_attn(qr, small, cmean, slc_kv, win_kv)
    qd, kd, vd, gsm, btsm = _dn_prep(dn, jnp.zeros((b, 8, DN_QKV), F32), w["conv_w"], small, w["alog_row"],
                                     w["dtb_row"])
    o_dn, s_dn = _gdn(qd, kd, vd, gsm, btsm, dn, w["dnn_row"], jnp.zeros((b, DN_H, DN_DK, DN_DV), F32))
    mix = _branch_gate([o.reshape(b * t, MIX_W) for o in (o_fox, o_gla, o_nsa, o_dn)], w["wb"], l, p["mg"])
    kv5 = lambda a: a.reshape(b, t, 2, 1, NSA_HD)
    wl = min(NSA_WINDOW, t)
    state = (fox_kv.reshape(b, t, 2, FOX_H, FOX_HD), lf[:, :, :FOX_H], kv5(cmp_kv), kv5(slc_kv),
             kv5(win_kv)[:, t - wl:], s_gla, s_dn, dn[:, t - (DN_CONV - 1):, :DN_QKV])
    return mix, state


def _mixer_sample(h, caches, l, pt_flat, n_pages, w):
    fox_kv_c, fox_lf_c, cmp_c, slc_c, win_c, s_gla_c, s_dn_c, conv_c = caches
    b = h.shape[0]
    past_len = n_pages * PAGE_SIZE
    p = _project(h, w, l)
    fox, gla, dn, small = (p[n].reshape(b, 1, -1) for n in ("fox", "gla", "dn", "small"))
    depth, n_pool = fox_kv_c.shape[:2]
    page0, row0 = l * n_pool, l * b
    o_fox, lf_new = _fox_decode(pt_flat, n_pages, page0, fox, small, w["bf_row"],
                                fox_kv_c.reshape(depth * n_pool, PAGE_SIZE * 2 * FOX_H, FOX_HD),
                                jnp.swapaxes(fox_lf_c, 2, 3).reshape(depth * n_pool, FOX_H, PAGE_SIZE))
    o_gla, o_dn, s_gla, s_dn = _rec_decode(
        gla, dn, small, w["w2p"], w["ba_row"], w["gn_row"], conv_c.reshape((depth * b,) + conv_c.shape[2:]),
        w["conv_w"], w["alog_row"], w["dtb_row"], w["dnn_row"], s_gla_c.reshape((depth * b,) + s_gla_c.shape[2:]),
        s_dn_c.reshape((depth * b,) + s_dn_c.shape[2:]), row0)
    tabs = _rope_tables(jnp.full((b,), past_len, jnp.int32))
    qr, cmp_new, slc_new, win_new = _nsa_prep(p["nsa"].reshape(1, b, -1), *tabs, False)
    qr, cmp_new, slc_new, win_new = (a.reshape(b, 1, -1) for a in (qr, cmp_new, slc_new, win_new))
    paged = lambda c: c.reshape(depth * n_pool, 2 * PAGE_SIZE, NSA_HD)
    _, _, o_c, sel = _nsa_dec_cmp(pt_flat, n_pages, page0, qr, paged(cmp_c))
    n_sel = NSA_TOPK - 1
    sel_flat = sel[:, :n_sel, 0].reshape(-1)
    sw = win_c.shape[2]
    o_nsa = _nsa_dec_attn(pt_flat, sel_flat, n_pages, n_sel, page0, row0, qr, paged(slc_c), slc_new,
                          win_c.reshape(depth * b, 2 * sw, NSA_HD), win_new, o_c, small)
    mix = _branch_gate([o.reshape(b, MIX_W).astype(BF16) for o in (o_fox, o_gla, o_nsa, o_dn)], w["wb"], l, p["mg"])
    kv5 = lambda a: a.reshape(b, 1, 2, 1, NSA_HD)
    wl = min(NSA_WINDOW, sw + 1)
    new_win = jnp.concatenate([win_c[l][:, sw + 1 - wl:], kv5(win_new)], axis=1)
    new_conv = jnp.concatenate([conv_c[l][:, 1:], dn[:, :, :DN_QKV]], axis=1)
    state = (fox[:, :, MIX_W:].reshape(b, 1, 2, FOX_H, FOX_HD), lf_new[:, :, :FOX_H], kv5(cmp_new), kv5(slc_new),
             new_win, s_gla, s_dn, new_conv)
    return mix, state


def _trunk_layer(x, h, mixer, norms, g_next, w, l):
    _, g_post_mix, g_pre_mlp, g_post_mlp = norms
    mix, state = mixer(h)
    x, h_mlp = _mm_norm_res(mix, w["w_out"], l, g_post_mix, x, g_pre_mlp)
    hid = _mm(h_mlp, w["w_up"], l, out_dtype=BF16, act="relu2")
    if g_next is None:
        return _mm_norm_res(hid, w["w_down"], l, g_post_mlp, x), None, state
    x, h_next = _mm_norm_res(hid, w["w_down"], l, g_post_mlp, x, g_next)
    return x, h_next, state


def kernel(x_prompt, x_sample, cache_fox_kv, cache_fox_logf, cache_nsa_cmp_kv, cache_nsa_slc_kv, cache_nsa_win_kv,
           state_gla, state_dn, state_dn_conv, page_table, norm_pre_mix, norm_post_mix, norm_pre_mlp, norm_post_mlp,
           w_in, fox_b_f, gla_w_a2, gla_b_a, gla_norm, dn_conv_w, dn_a_log, dn_dt_bias, dn_norm, w_branch, w_out,
           w_up, w_down):
    bp, tp, _ = x_prompt.shape
    bs, ts, _ = x_sample.shape
    assert ts == 1
    n_pages = page_table.shape[1]
    pt_flat = page_table.reshape(-1).astype(jnp.int32)
    y_p = x_prompt.reshape(bp * tp, D_MODEL)
    y_s = x_sample.reshape(bs, D_MODEL)
    new_p = [[] for _ in range(8)]
    new_s = [[] for _ in range(8)]
    caches = (cache_fox_kv, cache_fox_logf, cache_nsa_cmp_kv, cache_nsa_slc_kv, cache_nsa_win_kv, state_gla, state_dn,
              state_dn_conv)
    big = _stacked_weights(w_in, w_branch, w_out, w_up, w_down)
    h_p = _rmsnorm_cast(y_p, norm_pre_mix[0])
    h_s = _rmsnorm_cast(y_s, norm_pre_mix[0])
    for l in range(DEPTH):
        w = dict(big, **_layer_rows(fox_b_f[l], gla_w_a2[l], gla_b_a[l], gla_norm[l], dn_conv_w[l], dn_a_log[l],
                                    dn_dt_bias[l], dn_norm[l]))
        norms = (norm_pre_mix[l], norm_post_mix[l], norm_pre_mlp[l], norm_post_mlp[l])
        g_next = norm_pre_mix[l + 1] if l + 1 < DEPTH else None
        y_p, h_p, st_p = _trunk_layer(y_p, h_p, lambda h: _mixer_prompt(h, bp, tp, w, l), norms, g_next, w, l)
        y_s, h_s, st_s = _trunk_layer(y_s, h_s, lambda h: _mixer_sample(h, caches, l, pt_flat, n_pages, w), norms,
                                      g_next, w, l)
        for i in range(8):
            new_p[i].append(st_p[i])
            new_s[i].append(st_s[i])
    fox_kv_p, fox_logf_p, cmp_kv_p, slc_kv_p, win_kv_p, gla_p, dn_p, conv_p = [jnp.stack(a) for a in new_p]
    fox_kv_s, fox_logf_s, cmp_kv_s, slc_kv_s, win_kv_s, gla_s, dn_s, conv_s = [jnp.stack(a) for a in new_s]
    return (y_p.reshape(bp, tp, D_MODEL), y_s.reshape(bs, ts, D_MODEL), fox_kv_p, fox_kv_s, fox_logf_p, fox_logf_s,
            cmp_kv_p, cmp_kv_s, slc_kv_p, slc_kv_s, win_kv_p, win_kv_s, gla_p, gla_s, dn_p, dn_s, conv_p, conv_s)
```

```python
import functools

import jax
import jax.numpy as jnp
from jax import lax
from jax.experimental import pallas as pl
from jax.experimental.pallas import tpu as pltpu

F32 = jnp.float32
BF16 = jnp.bfloat16
HI = lax.Precision.HIGHEST
NT = (((1,), (1,)), ((), ()))
TN = (((0,), (0,)), ((), ()))
NEG = -1e30

D_MODEL = 2048
DEPTH = 2
PAGE_SIZE = 128
N_BRANCH = 4
MIX_W = D_MODEL // 4
FOX_H = 4
FOX_HD = MIX_W // FOX_H
GLA_H = 4
GLA_DK = MIX_W // (2 * GLA_H)
GLA_DV = MIX_W // GLA_H
GLA_RANK = 16
GLA_TAU = 16.0
NSA_H = 4
NSA_HD = MIX_W // NSA_H
NSA_BLOCK = 64
NSA_TOPK = 16
NSA_WINDOW = 512
DN_H = 4
DN_DK = MIX_W // DN_H
DN_DV = MIX_W // DN_H
DN_CONV = 4
DN_QKV = DN_H * (2 * DN_DK + DN_DV)
ROPE_DIM = NSA_HD // 4
ROPE_THETA = 500000.0
D_FF = 4 * D_MODEL
CHUNK = 64
SUB = 16
RMS_EPS = 1e-6
LANES = 128
SPLIT_SIZES = (
    FOX_H * FOX_HD, FOX_H * FOX_HD, FOX_H * FOX_HD, FOX_H,
    GLA_H * GLA_DK, GLA_H * GLA_DK, GLA_H * GLA_DV, GLA_RANK, GLA_H * GLA_DV,
    NSA_H * NSA_HD, 2 * NSA_HD, 2 * NSA_HD, 2 * NSA_HD, 3 * NSA_H,
    DN_QKV, DN_H, DN_H, DN_H * DN_DV,
    N_BRANCH * D_MODEL,
)
SM_FF, SM_GA, SM_NG, SM_DA, SM_DB = 0, 4, 20, 32, 36
VMEM_LIMIT = 56 * 1024 * 1024


def _cp(*sem):
    return pltpu.CompilerParams(dimension_semantics=sem, vmem_limit_bytes=VMEM_LIMIT)


def _pick(n, cap):
    if n <= cap:
        return n
    best = None
    for t in range(LANES, cap + 1, LANES):
        if n % t == 0:
            best = t
    assert best is not None, n
    return best


def _log_sigmoid(z):
    return jnp.minimum(z, 0.0) - jnp.log1p(jnp.exp(-jnp.abs(z)))


def _iota(shape, dim):
    return lax.broadcasted_iota(jnp.int32, shape, dim)


def _dot3(a, b, dims=(((1,), (0,)), ((), ()))):
    ah, bh = a.astype(BF16), b.astype(BF16)
    al, bl = (a - ah.astype(F32)).astype(BF16), (b - bh.astype(F32)).astype(BF16)
    f = lambda x, y: lax.dot_general(x, y, dims, preferred_element_type=F32)
    return f(ah, bh) + (f(ah, bl) + f(al, bh))


def _rmsnorm_cast_kernel(x_ref, g_ref, o_ref):
    x = x_ref[...]
    y = x * lax.rsqrt(jnp.mean(x * x, axis=-1, keepdims=True) + RMS_EPS)
    o_ref[...] = (y * g_ref[...]).astype(o_ref.dtype)


def _rmsnorm_cast(x, g):
    m, d = x.shape
    tm = min(m, 512)
    return pl.pallas_call(
        _rmsnorm_cast_kernel,
        grid=(m // tm,),
        in_specs=[pl.BlockSpec((tm, d), lambda i: (i, 0)), pl.BlockSpec((1, d), lambda i: (0, 0))],
        out_specs=pl.BlockSpec((tm, d), lambda i: (i, 0)),
        out_shape=jax.ShapeDtypeStruct((m, d), BF16),
        compiler_params=_cp("parallel"),
        name="rmsnorm_cast",
    )(x, g.reshape(1, d))


def _mm_kernel(a_ref, w_ref, o_ref, *, act):
    y = jnp.dot(a_ref[...], w_ref[...], preferred_element_type=F32)
    if act == "relu2":
        y = jnp.square(jnp.maximum(y, 0.0))
    o_ref[...] = y.astype(o_ref.dtype)


def _mm(a, w, l, col0=0, n=None, out_dtype=F32, act=None):
    m, k = a.shape
    n = w.shape[2] if n is None else n
    tm = min(m, 1024)
    tn = max(t for t in range(LANES, min(n, 2048) + 1, LANES) if n % t == 0 and col0 % t == 0)
    c0 = col0 // tn
    return pl.pallas_call(
        functools.partial(_mm_kernel, act=act),
        grid=(n // tn, m // tm),
        in_specs=[pl.BlockSpec((tm, k), lambda j, i: (i, 0)), pl.BlockSpec((None, k, tn), lambda j, i: (l, 0, c0 + j))],
        out_specs=pl.BlockSpec((tm, tn), lambda j, i: (i, j)),
        out_shape=jax.ShapeDtypeStruct((m, n), out_dtype),
        compiler_params=_cp("parallel", "parallel"),
        name="mm",
    )(a, w)


def _mm_norm_res_kernel(a_ref, w_ref, g_ref, x_ref, *refs):
    (gn_ref, o_ref, h_ref, acc_ref) = refs if len(refs) == 4 else (None, refs[0], None, refs[1])
    k = pl.program_id(1)

    @pl.when(k == 0)
    def _():
        acc_ref[...] = jnp.zeros_like(acc_ref)

    acc_ref[...] += jnp.dot(a_ref[...], w_ref[...], preferred_element_type=F32)

    @pl.when(k == pl.num_programs(1) - 1)
    def _():
        y = acc_ref[...]
        y = y * lax.rsqrt(jnp.mean(y * y, axis=-1, keepdims=True) + RMS_EPS)
        x = x_ref[...] + y * g_ref[...]
        o_ref[...] = x
        if h_ref is not None:
            h = x * lax.rsqrt(jnp.mean(x * x, axis=-1, keepdims=True) + RMS_EPS)
            h_ref[...] = (h * gn_ref[...]).astype(h_ref.dtype)


def _mm_norm_res(a, w, l, g, x, g_next=None):
    m, k = a.shape
    n = w.shape[2]
    tm = min(m, 512)
    tk = _pick(k, 2048)
    row = pl.BlockSpec((1, n), lambda i, kk: (0, 0))
    tile = pl.BlockSpec((tm, n), lambda i, kk: (i, 0))
    fused = g_next is not None
    return pl.pallas_call(
        _mm_norm_res_kernel,
        grid=(m // tm, k // tk),
        in_specs=[pl.BlockSpec((tm, tk), lambda i, kk: (i, kk)), pl.BlockSpec((None, tk, n), lambda i, kk: (l, kk, 0)),
                  row, tile] + ([row] if fused else []),
        out_specs=[tile, tile] if fused else tile,
        out_shape=[jax.ShapeDtypeStruct((m, n), F32), jax.ShapeDtypeStruct((m, n), BF16)] if fused
        else jax.ShapeDtypeStruct((m, n), F32),
        scratch_shapes=[pltpu.VMEM((tm, n), F32)],
        compiler_params=_cp("parallel", "arbitrary"),
        name="mm_norm_res",
    )(a, w, g.reshape(1, n), x, *([g_next.reshape(1, n)] if fused else []))


def _branch_gate_kernel(o0, o1, o2, o3, wb_ref, m0, m1, m2, m3, out_ref):
    acc = None
    for n, (o_n, m_n) in enumerate(((o0, m0), (o1, m1), (o2, m2), (o3, m3))):
        y = jnp.dot(o_n[...], wb_ref[n], preferred_element_type=F32)
        t = jax.nn.sigmoid(m_n[...]) * y
        acc = t if acc is None else acc + t
    out_ref[...] = acc.astype(out_ref.dtype)


def _branch_gate(branches, wb, l, mg):
    m = mg.shape[0]
    tm = min(m, 512)
    tn = 1024
    nj = D_MODEL // tn
    o_spec = pl.BlockSpec((tm, MIX_W), lambda j, i: (i, 0))
    m_specs = [pl.BlockSpec((tm, tn), functools.partial(lambda j, i, n: (i, n * nj + j), n=n)) for n in range(N_BRANCH)]
    return pl.pallas_call(
        _branch_gate_kernel,
        grid=(nj, m // tm),
        in_specs=[o_spec] * 4 + [pl.BlockSpec((None, N_BRANCH, MIX_W, tn), lambda j, i: (l, 0, 0, j))] + m_specs,
        out_specs=pl.BlockSpec((tm, tn), lambda j, i: (i, j)),
        out_shape=jax.ShapeDtypeStruct((m, D_MODEL), BF16),
        compiler_params=_cp("parallel", "parallel"),
        name="branch_gate",
    )(*branches, wb, mg, mg, mg, mg)


def _fox_prep_kernel(s_ref, b_ref, k_ref, v_ref, lf_ref, c_ref, ct_ref, kv_ref, carry_ref, *, tb):
    @pl.when(pl.program_id(1) == 0)
    def _():
        carry_ref[...] = jnp.zeros_like(carry_ref)

    for i, ref in enumerate((k_ref, v_ref)):
        for h in range(FOX_H):
            kv_ref[pl.ds(i * FOX_H + h, tb, stride=2 * FOX_H), :] = ref[:, h * FOX_HD:(h + 1) * FOX_HD]

    lf = _log_sigmoid(s_ref[...] + b_ref[...])
    tri = (_iota((tb, tb), 1) <= _iota((tb, tb), 0)).astype(F32)
    c = jnp.dot(tri, lf, precision=HI, preferred_element_type=F32) + carry_ref[...]
    lf_ref[...] = lf
    c_ref[...] = c
    ct_ref[...] = c.T[:8]
    carry_ref[...] = c[tb - 1:tb]


def _fox_prep(small, bias_row, fox):
    b, t, _ = small.shape
    tb = min(t, 256)
    blk = pl.BlockSpec((None, tb, LANES), lambda bi, i: (bi, i, 0))
    rows = 2 * FOX_H
    return pl.pallas_call(
        functools.partial(_fox_prep_kernel, tb=tb),
        grid=(b, t // tb),
        in_specs=[blk, pl.BlockSpec((1, LANES), lambda bi, i: (0, 0)),
                  pl.BlockSpec((None, tb, MIX_W), lambda bi, i: (bi, i, 1)),
                  pl.BlockSpec((None, tb, MIX_W), lambda bi, i: (bi, i, 2))],
        out_specs=[blk, blk, pl.BlockSpec((None, 8, tb), lambda bi, i: (bi, 0, i)),
                   pl.BlockSpec((None, tb * rows, FOX_HD), lambda bi, i: (bi, i, 0))],
        out_shape=[jax.ShapeDtypeStruct((b, t, LANES), F32)] * 2 + [jax.ShapeDtypeStruct((b, 8, t), F32),
                                                                  jax.ShapeDtypeStruct((b, t * rows, FOX_HD), F32)],
        scratch_shapes=[pltpu.VMEM((1, LANES), F32)],
        compiler_params=_cp("parallel", "arbitrary"),
        name="fox_prep",
    )(small, bias_row, fox, fox)


def _fox_attn_kernel(q_ref, k_ref, v_ref, cq_ref, ck_ref, o_ref, m_sc, l_sc, acc_sc, *, tq, tk):
    qi = pl.program_id(1)
    ki = pl.program_id(2)

    @pl.when(ki == 0)
    def _():
        m_sc[...] = jnp.full_like(m_sc, NEG)
        l_sc[...] = jnp.zeros_like(l_sc)
        acc_sc[...] = jnp.zeros_like(acc_sc)

    def step(diagonal):
        mask = _iota((tk, tq), 0) <= _iota((tk, tq), 1)
        heads = range(FOX_H)
        hsl = [slice(h * FOX_HD, (h + 1) * FOX_HD) for h in heads]
        ss = []
        for h in heads:
            q = (q_ref[:, hsl[h]] * (FOX_HD ** -0.5)).astype(BF16)
            s = lax.dot_general(k_ref[:, hsl[h]].astype(BF16), q, NT, preferred_element_type=F32)
            s = s + cq_ref[h:h + 1, :] - ck_ref[:, h:h + 1]
            ss.append(jnp.where(mask, s, NEG) if diagonal else s)
        m_prev = [m_sc[h] for h in heads]
        m_new = [jnp.maximum(m_prev[h], jnp.max(ss[h], axis=0, keepdims=True)) for h in heads]
        ps = []
        for h in heads:
            p = jnp.exp(ss[h] - m_new[h])
            ps.append(jnp.where(mask, p, 0.0) if diagonal else p)
        pvs = [jnp.dot(v_ref[:, hsl[h]].T.astype(BF16), ps[h].astype(BF16), preferred_element_type=F32)
               for h in heads]
        for h in heads:
            alpha = jnp.exp(m_prev[h] - m_new[h])
            l_sc[h] = alpha * l_sc[h] + jnp.sum(ps[h], axis=0, keepdims=True)
            acc_sc[h] = alpha * acc_sc[h] + pvs[h]
            m_sc[h] = m_new[h]

    pl.when(ki < qi)(functools.partial(step, False))
    pl.when(ki == qi)(functools.partial(step, True))

    @pl.when(ki == pl.num_programs(2) - 1)
    def _():
        for h in range(FOX_H):
            o_ref[:, h * FOX_HD:(h + 1) * FOX_HD] = (acc_sc[h] / l_sc[h]).T.astype(o_ref.dtype)


def _fox_attn(fox, c, ct):
    b, t, _ = fox.shape
    tq = tk = min(t, 512)

    def kmap(col):
        return lambda bi, qi, ki: (bi, jnp.minimum(ki, (qi * tq + tq - 1) // tk), col)

    return pl.pallas_call(
        functools.partial(_fox_attn_kernel, tq=tq, tk=tk),
        grid=(b, t // tq, t // tk),
        in_specs=[
            pl.BlockSpec((None, tq, MIX_W), lambda bi, qi, ki: (bi, qi, 0)),
            pl.BlockSpec((None, tk, MIX_W), kmap(1)),
            pl.BlockSpec((None, tk, MIX_W), kmap(2)),
            pl.BlockSpec((None, 8, tq), lambda bi, qi, ki: (bi, 0, qi)),
            pl.BlockSpec((None, tk, LANES), kmap(0)),
        ],
        out_specs=pl.BlockSpec((None, tq, MIX_W), lambda bi, qi, ki: (bi, qi, 0)),
        out_shape=jax.ShapeDtypeStruct((b, t, MIX_W), BF16),
        scratch_shapes=[pltpu.VMEM((FOX_H, 1, tq), F32), pltpu.VMEM((FOX_H, 1, tq), F32),
                        pltpu.VMEM((FOX_H, FOX_HD, tq), F32)],
        compiler_params=_cp("parallel", "parallel", "arbitrary"),
        name="fox_attn",
    )(fox, fox, fox, ct, c)


def _gla_kernel(g_ref, sm_ref, w2_ref, ba_ref, gn_ref, s0_ref, o_ref, sout_ref, s_sc, *, C):
    c = pl.program_id(1)

    @pl.when(c == 0)
    def _():
        s_sc[...] = s0_ref[...]

    lane = _iota((C, LANES), 1)
    ga = jnp.where((lane >= SM_GA) & (lane < SM_GA + GLA_RANK), sm_ref[...], 0.0)
    pre = jnp.dot(ga, w2_ref[...], precision=HI, preferred_element_type=F32) + ba_ref[...]
    loga = _log_sigmoid(pre) * (1.0 / GLA_TAU)
    tri = (_iota((C, C), 1) <= _iota((C, C), 0)).astype(F32)
    b_all = jnp.dot(tri, loga, precision=HI, preferred_element_type=F32)
    eye = (_iota((GLA_DK, GLA_DK), 0) == _iota((GLA_DK, GLA_DK), 1)).astype(F32)
    heads, blocks = range(GLA_H), range(C // SUB)
    kofs, vofs = GLA_H * GLA_DK, 2 * GLA_H * GLA_DK
    qs = [g_ref[:, h * GLA_DK:(h + 1) * GLA_DK] * (GLA_DK ** -0.5) for h in heads]
    ks = [g_ref[:, kofs + h * GLA_DK:kofs + (h + 1) * GLA_DK] for h in heads]
    vbs = [g_ref[:, vofs + h * GLA_DV:vofs + (h + 1) * GLA_DV].astype(BF16) for h in heads]
    bhs = [b_all[:, h * GLA_DK:(h + 1) * GLA_DK] for h in heads]
    states = [s_sc[h] for h in heads]
    o_inter = [jnp.dot((qs[h] * jnp.exp(bhs[h])).astype(BF16), states[h].astype(BF16), preferred_element_type=F32)
               for h in heads]
    atts = {}
    for h in heads:
        for ib in blocks[1:]:
            a0 = ib * SUB
            bi = bhs[h][a0:a0 + SUB]
            r = bi[0:1]
            qe = (qs[h][a0:a0 + SUB] * jnp.exp(bi - r)).astype(BF16)
            ke = (ks[h][:a0] * jnp.exp(r - bhs[h][:a0])).astype(BF16)
            atts[h, ib] = lax.dot_general(qe, ke, NT, preferred_element_type=F32)
    pair = 2 * GLA_DK
    mask3 = _iota((SUB, SUB, pair), 1) <= _iota((SUB, SUB, pair), 0)
    low = _iota((SUB, SUB, pair), 2) < GLA_DK
    ds = {}
    for hp in range(GLA_H // 2):
        q2 = g_ref[:, hp * pair:(hp + 1) * pair] * (GLA_DK ** -0.5)
        k2 = g_ref[:, kofs + hp * pair:kofs + (hp + 1) * pair]
        b2 = b_all[:, hp * pair:(hp + 1) * pair]
        for ib in blocks:
            rs = slice(ib * SUB, (ib + 1) * SUB)
            bi = b2[rs]
            diff = bi[:, None, :] - bi[None, :, :]
            e = jnp.where(mask3, jnp.exp(jnp.where(mask3, diff, 0.0)), 0.0)
            prod = q2[rs][:, None, :] * k2[rs][None, :, :] * e
            ds[2 * hp, ib] = jnp.sum(jnp.where(low, prod, 0.0), axis=-1)
            ds[2 * hp + 1, ib] = jnp.sum(jnp.where(low, 0.0, prod), axis=-1)
    rows = {}
    for h in heads:
        for ib in blocks:
            a0 = ib * SUB
            o_i = jnp.dot(ds[h, ib].astype(BF16), vbs[h][a0:a0 + SUB], preferred_element_type=F32)
            if ib > 0:
                o_i = o_i + jnp.dot(atts[h, ib].astype(BF16), vbs[h][:a0], preferred_element_type=F32)
            rows[h, ib] = o_i
    for h in heads:
        bend = bhs[h][C - 1:C]
        kdec = (ks[h] * jnp.exp(bend - bhs[h])).astype(BF16)
        dcol = jnp.sum(eye * jnp.exp(bend), axis=1, keepdims=True)
        s_sc[h] = dcol * states[h] + lax.dot_general(kdec, vbs[h], TN, preferred_element_type=F32)
    for h in heads:
        o = jnp.concatenate([rows[h, ib] for ib in blocks], axis=0) + o_inter[h]
        y = o * lax.rsqrt(jnp.mean(o * o, axis=-1, keepdims=True) + RMS_EPS) * gn_ref[...]
        gr = g_ref[:, vofs + GLA_H * GLA_DV + h * GLA_DV:vofs + GLA_H * GLA_DV + (h + 1) * GLA_DV]
        o_ref[:, h * GLA_DV:(h + 1) * GLA_DV] = (y * jax.nn.silu(gr)).astype(o_ref.dtype)

    @pl.when(c == pl.num_programs(1) - 1)
    def _():
        sout_ref[...] = s_sc[...]


def _gla(gla, small, w2p, ba_row, gn_row, s0):
    b, t, _ = gla.shape
    C = CHUNK
    n_gla = gla.shape[-1]
    return pl.pallas_call(
        functools.partial(_gla_kernel, C=C),
        grid=(b, t // C),
        in_specs=[
            pl.BlockSpec((None, C, n_gla), lambda bi, c: (bi, c, 0)),
            pl.BlockSpec((None, C, LANES), lambda bi, c: (bi, c, 0)),
            pl.BlockSpec(w2p.shape, lambda bi, c: (0, 0)),
            pl.BlockSpec(ba_row.shape, lambda bi, c: (0, 0)),
            pl.BlockSpec(gn_row.shape, lambda bi, c: (0, 0)),
            pl.BlockSpec((None, GLA_H, GLA_DK, GLA_DV), lambda bi, c: (bi, 0, 0, 0)),
        ],
        out_specs=[
            pl.BlockSpec((None, C, MIX_W), lambda bi, c: (bi, c, 0)),
            pl.BlockSpec((None, GLA_H, GLA_DK, GLA_DV), lambda bi, c: (bi, 0, 0, 0)),
        ],
        out_shape=[jax.ShapeDtypeStruct((b, t, MIX_W), BF16), jax.ShapeDtypeStruct(s0.shape, F32)],
        scratch_shapes=[pltpu.VMEM((GLA_H, GLA_DK, GLA_DV), F32)],
        compiler_params=_cp("parallel", "arbitrary"),
        name="gla",
    )(gla, small, w2p, ba_row, gn_row, s0)


def _rope(x, cos, sa, sb):
    half = ROPE_DIM // 2
    return x * cos + pltpu.roll(x, LANES - half, 1) * sa + pltpu.roll(x, half, 1) * sb


def _nsa_prep_kernel(x_ref, cos_ref, sa_ref, sb_ref, q_ref, cmp_ref, slc_ref, win_ref, cmp_c, slc_c, win_c,
                     *mean_ref, tb):
    cos, sa, sb = cos_ref[...], sa_ref[...], sb_ref[...]
    for h in range(NSA_H):
        hs = slice(h * NSA_HD, (h + 1) * NSA_HD)
        q_ref[:, hs] = _rope(x_ref[:, hs], cos, sa, sb) * (NSA_HD ** -0.5)
    base = NSA_H * NSA_HD
    for i, (ref, cache_ref) in enumerate(((cmp_ref, cmp_c), (slc_ref, slc_c), (win_ref, win_c))):
        k0 = base + i * 2 * NSA_HD
        k = _rope(x_ref[:, k0:k0 + NSA_HD], cos, sa, sb)
        v = x_ref[:, k0 + NSA_HD:k0 + 2 * NSA_HD]
        ref[:, :NSA_HD] = k
        ref[:, NSA_HD:] = v
        cache_ref[pl.ds(0, tb, stride=2), :] = k
        cache_ref[pl.ds(1, tb, stride=2), :] = v
    if mean_ref:
        kv = cmp_ref[...]
        mean_ref[0][...] = jnp.mean(kv.reshape(tb // NSA_BLOCK, NSA_BLOCK, 2 * NSA_HD), axis=1)


def _nsa_prep(nsa, cos, sa, sb, with_means):
    b, t, n = nsa.shape
    tb = min(t, 512)
    tab = pl.BlockSpec((tb, LANES), lambda bi, i: (i, 0))
    kv = pl.BlockSpec((None, tb, 2 * NSA_HD), lambda bi, i: (bi, i, 0))
    kvc = pl.BlockSpec((None, 2 * tb, NSA_HD), lambda bi, i: (bi, i, 0))
    out_specs = [pl.BlockSpec((None, tb, MIX_W), lambda bi, i: (bi, i, 0)), kv, kv, kv, kvc, kvc, kvc]
    out_shape = ([jax.ShapeDtypeStruct((b, t, MIX_W), F32)] + [jax.ShapeDtypeStruct((b, t, 2 * NSA_HD), F32)] * 3
                 + [jax.ShapeDtypeStruct((b, 2 * t, NSA_HD), F32)] * 3)
    if with_means:
        out_specs.append(pl.BlockSpec((None, tb // NSA_BLOCK, 2 * NSA_HD), lambda bi, i: (bi, i, 0)))
        out_shape.append(jax.ShapeDtypeStruct((b, t // NSA_BLOCK, 2 * NSA_HD), F32))
    return pl.pallas_call(
        functools.partial(_nsa_prep_kernel, tb=tb),
        grid=(b, t // tb),
        in_specs=[pl.BlockSpec((None, tb, n), lambda bi, i: (bi, i, 0)), tab, tab, tab],
        out_specs=out_specs,
        out_shape=out_shape,
        compiler_params=_cp("parallel", "parallel"),
        name="nsa_prep",
    )(nsa, cos, sa, sb)


def _masked_softmax(s, mask):
    s = jnp.where(mask, s, -jnp.inf)
    m = jnp.max(s, axis=-1, keepdims=True)
    m = jnp.where(m == -jnp.inf, 0.0, m)
    p = jnp.where(mask, jnp.exp(s - m), 0.0)
    return p / jnp.maximum(jnp.sum(p, axis=-1, keepdims=True), 1e-30)


def _topk_mask_t(score_t, blk_t, n_sel):
    rank = jnp.zeros(score_t.shape, jnp.int32)
    for m in range(score_t.shape[0]):
        sm = score_t[m:m + 1, :]
        beats = (sm > score_t) | ((sm == score_t) & (blk_t > m))
        rank = rank + beats.astype(jnp.int32)
    return rank < n_sel


def _masked_scores(qs, k, mask):
    bias = jnp.where(mask, 0.0, NEG)
    return [lax.dot_general(k, q, NT, preferred_element_type=F32) + bias for q in qs]


def _softmax_step(carry, ss, v_t):
    ms, ls, accs = carry
    n = range(len(ss))
    m_new = [jnp.maximum(ms[g], jnp.max(ss[g], axis=0, keepdims=True)) for g in n]
    ps = [jnp.exp(ss[g] - m_new[g]) for g in n]
    pvs = [jnp.dot(v_t, ps[g].astype(BF16), preferred_element_type=F32) for g in n]
    alphas = [jnp.exp(ms[g] - m_new[g]) for g in n]
    ls = [alphas[g] * ls[g] + jnp.sum(ps[g], axis=0, keepdims=True) for g in n]
    accs = [alphas[g] * accs[g] + pvs[g] for g in n]
    return m_new, ls, accs


def _nsa_attn_kernel(q_ref, sm_ref, cm_ref, slc_ref, win_ref, o_ref, *, tq, kt, nb):
    qi = pl.program_id(1)
    qpos = qi * tq + _iota((tq, 1), 0)
    blk = _iota((tq, nb), 1)
    cmask = (blk + 1) * NSA_BLOCK <= qpos + 1
    kc, vc = cm_ref[:, :NSA_HD], cm_ref[:, NSA_HD:]
    o_c = []
    imp = jnp.zeros((tq, nb), F32)
    for h in range(NSA_H):
        s = lax.dot_general(q_ref[:, h * NSA_HD:(h + 1) * NSA_HD], kc, NT, precision=HI, preferred_element_type=F32)
        p = _masked_softmax(s, cmask)
        imp = imp + p
        o_c.append(jnp.dot(p, vc, precision=HI, preferred_element_type=F32))
    blk_t = _iota((nb, tq), 0)
    qpos_t = qi * tq + _iota((nb, tq), 1)
    cur_t = qpos_t // NSA_BLOCK
    score_t = jnp.where(blk_t == cur_t, jnp.inf,
                        jnp.where((blk_t + 1) * NSA_BLOCK <= qpos_t + 1, imp.T, -jnp.inf))
    sel_b = (_topk_mask_t(score_t, blk_t, min(NSA_TOPK, nb)) & (blk_t <= cur_t)).astype(BF16)
    qs = [q_ref[:, h * NSA_HD:(h + 1) * NSA_HD].astype(BF16) for h in range(NSA_H)]
    krow = _iota((kt, tq), 0)
    qpos_row = qi * tq + _iota((1, tq), 1)
    blk_of_col = _iota((nb, kt), 1) // NSA_BLOCK - _iota((nb, kt), 0)
    last = (qi * tq + tq - 1) // kt

    def init():
        return ([jnp.full((1, tq), NEG, F32)] * NSA_H, [jnp.zeros((1, tq), F32)] * NSA_H,
                [jnp.zeros((NSA_HD, tq), F32)] * NSA_H)

    def tile_rows(kb):
        return pl.ds(pl.multiple_of(kb * kt, kt), kt)

    def slc_scores(kb):
        expand = (blk_of_col + kb * (kt // NSA_BLOCK) == 0).astype(BF16)
        tok = lax.dot_general(expand, sel_b, TN, preferred_element_type=F32)
        mask = (tok > 0.5) & (kb * kt + krow <= qpos_row)
        return _masked_scores(qs, slc_ref[tile_rows(kb), :NSA_HD].astype(BF16), mask)

    def win_scores(kb):
        dist = qpos_row - (kb * kt + krow)
        mask = (dist >= 0) & (dist < NSA_WINDOW)
        return _masked_scores(qs, win_ref[tile_rows(kb), :NSA_HD].astype(BF16), mask)

    def attend(scores_of, kv_ref, first):
        def body(kb, state):
            v_t = kv_ref[tile_rows(kb), NSA_HD:].T.astype(BF16)
            return _softmax_step(state, scores_of(kb), v_t)

        _, ls, accs = lax.fori_loop(first, last + 1, body, init())
        return ls, accs

    l_s, acc_s = attend(slc_scores, slc_ref, 0)
    l_w, acc_w = attend(win_scores, win_ref, jnp.maximum(qi * tq - (NSA_WINDOW - 1), 0) // kt)
    gate = jax.nn.sigmoid(sm_ref[...])
    for h in range(NSA_H):
        g0 = SM_NG + 3 * h
        o = (gate[:, g0:g0 + 1] * o_c[h] + gate[:, g0 + 1:g0 + 2] * (acc_s[h] / l_s[h]).T
             + gate[:, g0 + 2:g0 + 3] * (acc_w[h] / l_w[h]).T)
        o_ref[:, h * NSA_HD:(h + 1) * NSA_HD] = o.astype(o_ref.dtype)


def _nsa_attn(qr, small, cmean, slc, win):
    b, t, _ = qr.shape
    tq = 128
    kt = min(t, 512)
    nb = t // NSA_BLOCK
    whole = lambda n: pl.BlockSpec((None, n, 2 * NSA_HD), lambda bi, qi: (bi, 0, 0))
    return pl.pallas_call(
        functools.partial(_nsa_attn_kernel, tq=tq, kt=kt, nb=nb),
        grid=(b, t // tq),
        in_specs=[
            pl.BlockSpec((None, tq, MIX_W), lambda bi, qi: (bi, qi, 0)),
            pl.BlockSpec((None, tq, LANES), lambda bi, qi: (bi, qi, 0)),
            whole(nb), whole(t), whole(t),
        ],
        out_specs=pl.BlockSpec((None, tq, MIX_W), lambda bi, qi: (bi, qi, 0)),
        out_shape=jax.ShapeDtypeStruct((b, t, MIX_W), BF16),
        compiler_params=_cp("parallel", "parallel"),
        name="nsa_attn",
    )(qr, small, cmean, slc, win)


def _dn_prep_kernel(x_ref, halo_ref, prev_ref, cw_ref, sm_ref, alog_ref, dtb_ref,
                    q_ref, k_ref, v_ref, g_ref, bt_ref, *, tb):
    halo = jnp.where(pl.program_id(1) == 0, prev_ref[...], halo_ref[...])
    xcat = jnp.concatenate([halo, x_ref[...]], axis=0)
    conv = x_ref[...] * cw_ref[DN_CONV - 1:DN_CONV]
    for j in range(DN_CONV - 1):
        sh = DN_CONV - 1 - j
        conv = conv + pltpu.roll(xcat, sh, 0)[8:] * cw_ref[j:j + 1]
    u = jax.nn.silu(conv)
    for h in range(DN_H):
        hs = slice(h * DN_DK, (h + 1) * DN_DK)
        uq = u[:, hs]
        q_ref[:, hs] = uq * lax.rsqrt(jnp.sum(uq * uq, axis=-1, keepdims=True) + 1e-6) * (DN_DK ** -0.5)
        uk = u[:, DN_H * DN_DK + h * DN_DK:DN_H * DN_DK + (h + 1) * DN_DK]
        k_ref[:, hs] = uk * lax.rsqrt(jnp.sum(uk * uk, axis=-1, keepdims=True) + 1e-6)
    v_ref[...] = u[:, 2 * DN_H * DN_DK:]
    sm = sm_ref[...]
    g_ref[...] = -jnp.exp(alog_ref[...]) * jax.nn.softplus(sm + dtb_ref[...])
    bt_ref[...] = jax.nn.sigmoid(sm)


def _dn_prep(dn, prev8, conv_w, small, alog_row, dtb_row):
    b, t, _ = dn.shape
    tb = min(t, 256)
    hb = tb // 8
    row = pl.BlockSpec((None, tb, MIX_W), lambda bi, i: (bi, i, 0))
    sm = pl.BlockSpec((None, tb, LANES), lambda bi, i: (bi, i, 0))
    one = lambda shape: pl.BlockSpec(shape, lambda bi, i: (0,) * len(shape))
    return pl.pallas_call(
        functools.partial(_dn_prep_kernel, tb=tb),
        grid=(b, t // tb),
        in_specs=[
            pl.BlockSpec((None, tb, DN_QKV), lambda bi, i: (bi, i, 0)),
            pl.BlockSpec((None, 8, DN_QKV), lambda bi, i: (bi, jnp.maximum(i * hb - 1, 0), 0)),
            pl.BlockSpec((None, 8, DN_QKV), lambda bi, i: (bi, 0, 0)),
            one(conv_w.shape), sm, one(alog_row.shape), one(dtb_row.shape),
        ],
        out_specs=[row, row, row, sm, sm],
        out_shape=[jax.ShapeDtypeStruct((b, t, MIX_W), F32)] * 3 + [jax.ShapeDtypeStruct((b, t, LANES), F32)] * 2,
        compiler_params=_cp("parallel", "parallel"),
        name="dn_prep",
    )(dn, dn, prev8, conv_w, small, alog_row, dtb_row)


def _gdn_pre_kernel(q_ref, k_ref, v_ref, g_ref, bt_ref, wv_ref, wk_ref, qe_ref, kd_ref, qk_ref, gam_ref, *, C, nck):
    row, col = _iota((C, C), 0), _iota((C, C), 1)
    incl, strict = col <= row, col < row
    eye = (row == col).astype(F32)
    tri = incl.astype(F32)
    n_sq = C.bit_length() - 2
    chains = []
    for c in range(nck):
        rs = slice(c * C, (c + 1) * C)
        gam_all = jnp.dot(tri, g_ref[rs, :], precision=HI, preferred_element_type=F32)
        gam_ref[rs, :] = gam_all
        gam_t = gam_all.T
        for h in range(DN_H):
            hs = slice(h * DN_DK, (h + 1) * DN_DK)
            q, k = q_ref[rs, hs], k_ref[rs, hs]
            gcol = gam_all[:, SM_DA + h:SM_DA + h + 1]
            bcol = bt_ref[rs, SM_DB + h:SM_DB + h + 1]
            diff = gcol - gam_t[SM_DA + h:SM_DA + h + 1, :]
            dec_incl = jnp.where(incl, jnp.exp(jnp.where(incl, diff, 0.0)), 0.0)
            qk = lax.dot_general(q.astype(BF16), k.astype(BF16), NT, preferred_element_type=F32) * dec_incl
            qk_ref[rs, h * C:(h + 1) * C] = qk.astype(BF16)
            qe_ref[rs, hs] = (q * jnp.exp(gcol)).astype(BF16)
            kd_ref[rs, hs] = (k * jnp.exp(gcol[C - 1:C] - gcol)).astype(BF16)
            x = -(bcol * _dot3(k, k, NT) * jnp.where(strict, dec_incl, 0.0))
            chains.append((rs, hs, x, gcol, bcol))
    xs = [ch[2] for ch in chains]
    ps = [eye + x for x in xs]
    for _ in range(n_sq):
        xs = [_dot3(x, x) for x in xs]
        ps = [p + _dot3(p, x) for p, x in zip(ps, xs)]
    for (rs, hs, _, gcol, bcol), p in zip(chains, ps):
        k, v = k_ref[rs, hs], v_ref[rs, hs]
        rhs = jnp.concatenate([bcol * v, (bcol * jnp.exp(gcol)) * k], axis=1)
        w = _dot3(p, rhs)
        wv_ref[rs, hs] = w[:, :DN_DV]
        wk_ref[rs, hs] = w[:, DN_DV:].astype(BF16)


def _gdn_rec_kernel(wv_ref, wk_ref, qe_ref, kd_ref, qk_ref, gam_ref, z_ref, nrm_ref, s0_ref, o_ref, sout_ref, s_sc,
                    *, C, nck):
    i = pl.program_id(1)

    @pl.when(i == 0)
    def _():
        s_sc[...] = s0_ref[...]

    state = [s_sc[h] for h in range(DN_H)]
    heads = range(DN_H)
    hsl = [slice(h * DN_DK, (h + 1) * DN_DK) for h in heads]
    for c in range(nck):
        rs = slice(c * C, (c + 1) * C)
        sbs = [state[h].astype(BF16) for h in heads]
        ubs = [(wv_ref[rs, hsl[h]] - jnp.dot(wk_ref[rs, hsl[h]], sbs[h], preferred_element_type=F32)).astype(BF16)
               for h in heads]
        outs = [jnp.dot(qe_ref[rs, hsl[h]], sbs[h], preferred_element_type=F32)
                + jnp.dot(qk_ref[rs, h * C:(h + 1) * C], ubs[h], preferred_element_type=F32) for h in heads]
        gend = gam_ref[c * C + C - 1:(c + 1) * C, :]
        state = [jnp.exp(gend[:, SM_DA + h:SM_DA + h + 1]) * state[h]
                 + lax.dot_general(kd_ref[rs, hsl[h]], ubs[h], TN, preferred_element_type=F32) for h in heads]
        for h in heads:
            o = outs[h]
            y = o * lax.rsqrt(jnp.mean(o * o, axis=-1, keepdims=True) + RMS_EPS) * nrm_ref[...]
            o_ref[rs, hsl[h]] = (y * jax.nn.silu(z_ref[rs, hsl[h]])).astype(o_ref.dtype)
    for h in range(DN_H):
        s_sc[h] = state[h]

    @pl.when(i == pl.num_programs(1) - 1)
    def _():
        for h in range(DN_H):
            sout_ref[h] = state[h]


GDN_BLOCK = 256


def _gdn(qd, kd, vd, gsm, btsm, dn, nrm_row, s0):
    b, t, _ = qd.shape
    C = CHUNK
    tb = min(t, GDN_BLOCK)
    nck = tb // C
    row = pl.BlockSpec((None, tb, MIX_W), lambda bi, i: (bi, i, 0))
    sm = pl.BlockSpec((None, tb, LANES), lambda bi, i: (bi, i, 0))
    qk_spec = pl.BlockSpec((None, tb, DN_H * C), lambda bi, i: (bi, i, 0))
    st = pl.BlockSpec((None, DN_H, DN_DK, DN_DV), lambda bi, i: (bi, 0, 0, 0))
    wide = lambda dt: jax.ShapeDtypeStruct((b, t, MIX_W), dt)
    wv, wk, qe, kdc, qk, gam = pl.pallas_call(
        functools.partial(_gdn_pre_kernel, C=C, nck=nck),
        grid=(b, t // tb),
        in_specs=[row, row, row, sm, sm],
        out_specs=[row, row, row, row, qk_spec, sm],
        out_shape=[wide(F32), wide(BF16), wide(BF16), wide(BF16), jax.ShapeDtypeStruct((b, t, DN_H * C), BF16),
                   jax.ShapeDtypeStruct((b, t, LANES), F32)],
        compiler_params=_cp("parallel", "parallel"),
        name="gdn_pre",
    )(qd, kd, vd, gsm, btsm)
    return pl.pallas_call(
        functools.partial(_gdn_rec_kernel, C=C, nck=nck),
        grid=(b, t // tb),
        in_specs=[row, row, row, row, qk_spec, sm,
                  pl.BlockSpec((None, tb, MIX_W), lambda bi, i: (bi, i, DN_QKV // MIX_W)),
                  pl.BlockSpec(nrm_row.shape, lambda bi, i: (0, 0)), st],
        out_specs=[row, st],
        out_shape=[wide(BF16), jax.ShapeDtypeStruct(s0.shape, F32)],
        scratch_shapes=[pltpu.VMEM((DN_H, DN_DK, DN_DV), F32)],
        compiler_params=_cp("parallel", "arbitrary"),
        name="gdn_rec",
    )(wv, wk, qe, kdc, qk, gam, dn, nrm_row, s0)


FOX_PAGES = 8


def _fox_dec_kernel(pt_ref, fx_ref, sm_ref, b_ref, *refs):
    kv_refs, lf_refs = refs[:FOX_PAGES], refs[FOX_PAGES:2 * FOX_PAGES]
    o_ref, lfo_ref, m_sc, l_sc, acc_sc, car_sc = refs[2 * FOX_PAGES:]
    j = pl.program_id(1)
    P = PAGE_SIZE
    rows_per_tok = 2 * FOX_H
    qs = [fx_ref[:, h * FOX_HD:(h + 1) * FOX_HD] * (FOX_HD ** -0.5) for h in range(FOX_H)]

    @pl.when(j == 0)
    def _():
        lf_new = _log_sigmoid(sm_ref[...] + b_ref[...])
        lfo_ref[...] = lf_new
        on_diag = _iota((8, LANES), 1) == _iota((8, LANES), 0)
        car_sc[...] = jnp.broadcast_to(jnp.sum(jnp.where(on_diag, lf_new, 0.0), axis=1, keepdims=True), (8, LANES))
        for h in range(FOX_H):
            k_new = fx_ref[:, MIX_W + h * FOX_HD:MIX_W + (h + 1) * FOX_HD]
            m_sc[h] = jnp.sum(qs[h] * k_new, axis=-1, keepdims=True)
            l_sc[h] = jnp.ones((1, 1), F32)
            acc_sc[h] = fx_ref[:, 2 * MIX_W + h * FOX_HD:2 * MIX_W + (h + 1) * FOX_HD]

    upper = (_iota((P, P), 1) > _iota((P, P), 0)).astype(F32)
    carry = car_sc[...]
    pad = jnp.zeros((8 - FOX_H, P), F32)
    scores = []
    for kv_ref, lf_ref in zip(kv_refs, lf_refs):
        lf_t = jnp.concatenate([lf_ref[...], pad], axis=0)
        bias = (lax.dot_general(lf_t, upper, NT, precision=HI, preferred_element_type=F32) + carry).T
        scores.append([jnp.sum(kv_ref[pl.ds(h, P, stride=rows_per_tok), :] * qs[h], axis=-1, keepdims=True)
                       + bias[:, h:h + 1] for h in range(FOX_H)])
        carry = carry + jnp.sum(lf_t, axis=1, keepdims=True)
    car_sc[...] = carry
    outs = []
    for h in range(FOX_H):
        m_step = jnp.max(scores[0][h], axis=0, keepdims=True)
        for g in range(1, FOX_PAGES):
            m_step = jnp.maximum(m_step, jnp.max(scores[g][h], axis=0, keepdims=True))
        m_prev = m_sc[h]
        m_new = jnp.maximum(m_prev, m_step)
        alpha = jnp.exp(m_prev - m_new)
        l_new = alpha * l_sc[h]
        acc = alpha * acc_sc[h]
        for g, kv_ref in enumerate(kv_refs):
            p = jnp.exp(scores[g][h] - m_new)
            l_new = l_new + jnp.sum(p, axis=0, keepdims=True)
            acc = acc + jnp.sum(p * kv_ref[pl.ds(FOX_H + h, P, stride=rows_per_tok), :], axis=0, keepdims=True)
        m_sc[h], l_sc[h], acc_sc[h] = m_new, l_new, acc
        outs.append(acc / l_new)

    @pl.when(j == pl.num_programs(1) - 1)
    def _():
        for h in range(FOX_H):
            o_ref[:, h * FOX_HD:(h + 1) * FOX_HD] = outs[h]


def _fox_decode(pt_flat, n_pages, page0, fox_new, small_new, bias_row, kv_cache, lf_cache):
    b = fox_new.shape[0]
    assert n_pages % FOX_PAGES == 0

    def page(r):
        return lambda bi, j, pt: (page0 + pt[bi * n_pages + n_pages - 1 - (j * FOX_PAGES + r)], 0, 0)

    per_b = lambda n: pl.BlockSpec((None, 1, n), lambda bi, j, pt: (bi, 0, 0))
    return pl.pallas_call(
        _fox_dec_kernel,
        grid_spec=pltpu.PrefetchScalarGridSpec(
            num_scalar_prefetch=1,
            grid=(b, n_pages // FOX_PAGES),
            in_specs=[per_b(3 * MIX_W), per_b(LANES), pl.BlockSpec((1, LANES), lambda bi, j, pt: (0, 0))]
            + [pl.BlockSpec((None, PAGE_SIZE * 2 * FOX_H, FOX_HD), page(r)) for r in range(FOX_PAGES)]
            + [pl.BlockSpec((None, FOX_H, PAGE_SIZE), page(r)) for r in range(FOX_PAGES)],
            out_specs=[per_b(MIX_W), per_b(LANES)],
            scratch_shapes=[pltpu.VMEM((FOX_H, 1, 1), F32), pltpu.VMEM((FOX_H, 1, 1), F32),
                            pltpu.VMEM((FOX_H, 1, FOX_HD), F32), pltpu.VMEM((8, PAGE_SIZE), F32)],
        ),
        out_shape=[jax.ShapeDtypeStruct((b, 1, MIX_W), F32), jax.ShapeDtypeStruct((b, 1, LANES), F32)],
        compiler_params=_cp("parallel", "arbitrary"),
        name="fox_decode",
    )(pt_flat, fox_new, small_new, bias_row, *([kv_cache] * FOX_PAGES), *([lf_cache] * FOX_PAGES))


def _col(row, eye):
    return jnp.sum(eye * row, axis=1, keepdims=True)


def _rec_dec_kernel(g_ref, d_ref, sm_ref, w2_ref, ba_ref, gn_ref, cp_ref, cw_ref, alog_ref, dtb_ref, nrm_ref,
                    sg_ref, sd_ref, og_ref, od_ref, sgo_ref, sdo_ref):
    sm = sm_ref[...]
    lane = _iota((8, LANES), 1)
    ga = jnp.where((lane >= SM_GA) & (lane < SM_GA + GLA_RANK), jnp.broadcast_to(sm, (8, LANES)), 0.0)
    pre = jnp.dot(ga, w2_ref[...], precision=HI, preferred_element_type=F32)[0:1] + ba_ref[...]
    loga = _log_sigmoid(pre) * (1.0 / GLA_TAU)
    eye_k = (_iota((GLA_DK, GLA_DK), 0) == _iota((GLA_DK, GLA_DK), 1)).astype(F32)
    for h in range(GLA_H):
        q = g_ref[:, h * GLA_DK:(h + 1) * GLA_DK] * (GLA_DK ** -0.5)
        k = g_ref[:, GLA_H * GLA_DK + h * GLA_DK:GLA_H * GLA_DK + (h + 1) * GLA_DK]
        v0 = 2 * GLA_H * GLA_DK + h * GLA_DV
        v = g_ref[:, v0:v0 + GLA_DV]
        gr = g_ref[:, v0 + GLA_H * GLA_DV:v0 + GLA_H * GLA_DV + GLA_DV]
        ea = jnp.exp(loga[:, h * GLA_DK:(h + 1) * GLA_DK])
        s0 = sg_ref[h]
        o = jnp.sum(q * k, axis=-1, keepdims=True) * v + jnp.sum(_col(q * ea, eye_k) * s0, axis=0, keepdims=True)
        sgo_ref[h] = _col(ea, eye_k) * s0 + _col(k, eye_k) * v
        y = o * lax.rsqrt(jnp.mean(o * o, axis=-1, keepdims=True) + RMS_EPS) * gn_ref[...]
        og_ref[:, h * GLA_DV:(h + 1) * GLA_DV] = y * jax.nn.silu(gr)
    conv = d_ref[:, :DN_QKV] * cw_ref[DN_CONV - 1:DN_CONV]
    for j in range(DN_CONV - 1):
        conv = conv + cp_ref[j:j + 1] * cw_ref[j:j + 1]
    u = jax.nn.silu(conv)
    gall = -jnp.exp(alog_ref[...]) * jax.nn.softplus(sm + dtb_ref[...])
    ball = jax.nn.sigmoid(sm)
    eye_d = (_iota((DN_DK, DN_DK), 0) == _iota((DN_DK, DN_DK), 1)).astype(F32)
    for h in range(DN_H):
        uq = u[:, h * DN_DK:(h + 1) * DN_DK]
        q = uq * lax.rsqrt(jnp.sum(uq * uq, axis=-1, keepdims=True) + 1e-6) * (DN_DK ** -0.5)
        uk = u[:, DN_H * DN_DK + h * DN_DK:DN_H * DN_DK + (h + 1) * DN_DK]
        k = uk * lax.rsqrt(jnp.sum(uk * uk, axis=-1, keepdims=True) + 1e-6)
        v = u[:, 2 * DN_H * DN_DK + h * DN_DV:2 * DN_H * DN_DK + (h + 1) * DN_DV]
        eg = jnp.exp(gall[:, SM_DA + h:SM_DA + h + 1])
        beta = ball[:, SM_DB + h:SM_DB + h + 1]
        s0 = sd_ref[h]
        kcol = _col(k, eye_d)
        ks = jnp.sum(kcol * s0, axis=0, keepdims=True)
        qs = jnp.sum(_col(q, eye_d) * s0, axis=0, keepdims=True)
        un = beta * (v - eg * ks)
        o = eg * qs + jnp.sum(q * k, axis=-1, keepdims=True) * un
        sdo_ref[h] = eg * s0 + kcol * un
        y = o * lax.rsqrt(jnp.mean(o * o, axis=-1, keepdims=True) + RMS_EPS) * nrm_ref[...]
        z = d_ref[:, DN_QKV + h * DN_DV:DN_QKV + (h + 1) * DN_DV]
        od_ref[:, h * DN_DV:(h + 1) * DN_DV] = y * jax.nn.silu(z)


def _rec_decode(gla_new, dn_new, small_new, w2p, ba_row, gn_row, conv_prev, conv_w, alog_row, dtb_row, nrm_row,
                s_gla, s_dn, row0):
    b = gla_new.shape[0]
    per_b = lambda n: pl.BlockSpec((None, 1, n), lambda bi: (bi, 0, 0))
    one = lambda a: pl.BlockSpec(a.shape, lambda bi: (0,) * a.ndim)
    sg = lambda r0: pl.BlockSpec((None, GLA_H, GLA_DK, GLA_DV), lambda bi: (r0 + bi, 0, 0, 0))
    sd = lambda r0: pl.BlockSpec((None, DN_H, DN_DK, DN_DV), lambda bi: (r0 + bi, 0, 0, 0))
    return pl.pallas_call(
        _rec_dec_kernel,
        grid=(b,),
        in_specs=[per_b(gla_new.shape[-1]), per_b(dn_new.shape[-1]), per_b(LANES), one(w2p), one(ba_row), one(gn_row),
                  pl.BlockSpec((None, DN_CONV - 1, DN_QKV), lambda bi: (row0 + bi, 0, 0)), one(conv_w), one(alog_row),
                  one(dtb_row), one(nrm_row), sg(row0), sd(row0)],
        out_specs=[per_b(MIX_W), per_b(MIX_W), sg(0), sd(0)],
        out_shape=[jax.ShapeDtypeStruct((b, 1, MIX_W), F32)] * 2
        + [jax.ShapeDtypeStruct((b,) + s_gla.shape[1:], F32), jax.ShapeDtypeStruct((b,) + s_dn.shape[1:], F32)],
        compiler_params=_cp("parallel"),
        name="rec_decode",
    )(gla_new, dn_new, small_new, w2p, ba_row, gn_row, conv_prev, conv_w, alog_row, dtb_row, nrm_row, s_gla, s_dn)


def _heads_to_rows(q_row):
    rows = [q_row[:, h * NSA_HD:(h + 1) * NSA_HD] for h in range(NSA_H)]
    return jnp.concatenate(rows + [jnp.zeros((8 - NSA_H, NSA_HD), F32)], axis=0)


CMP_PAGES = 16


def _nsa_dec_cmp_kernel(pt_ref, q_ref, *refs, nbp):
    page_refs, (kc_ref, vc_ref, oc_ref, sel_ref) = refs[:CMP_PAGES], refs[CMP_PAGES:]
    j = pl.program_id(1)
    per = PAGE_SIZE // NSA_BLOCK
    rows = CMP_PAGES * per
    dst = pl.ds(pl.multiple_of(j * rows, rows), rows)
    for kv, ref in enumerate((kc_ref, vc_ref)):
        means = [jnp.mean(r[pl.ds(kv, PAGE_SIZE, stride=2), :].reshape(per, NSA_BLOCK, NSA_HD), axis=1)
                 for r in page_refs]
        ref[dst, :] = jnp.concatenate(means, axis=0)

    @pl.when(j == pl.num_programs(1) - 1)
    def _():
        q8 = _heads_to_rows(q_ref[...])
        kc, vc = kc_ref[...], vc_ref[...]
        s = lax.dot_general(q8, kc, NT, precision=HI, preferred_element_type=F32)
        p = _masked_softmax(s, jnp.full(s.shape, True))
        oc_ref[...] = jnp.dot(p, vc, precision=HI, preferred_element_type=F32)
        imp = jnp.sum(p[0:NSA_H], axis=0, keepdims=True)
        r_i, c_i = _iota((nbp, nbp), 0), _iota((nbp, nbp), 1)
        imp_col = _col(imp, (r_i == c_i).astype(F32))
        beats = (imp_col > imp) | ((imp_col == imp) & (r_i < c_i))
        rank = jnp.sum(beats.astype(jnp.int32), axis=0, keepdims=True)
        ids = jnp.where(rank == _iota((16, nbp), 0), _iota((16, nbp), 1), 0)
        sel_ref[...] = jnp.broadcast_to(jnp.sum(ids, axis=1, keepdims=True), (16, LANES))


def _nsa_dec_cmp(pt_flat, n_pages, page0, q_new, cmp_cache):
    b = q_new.shape[0]
    nbp = n_pages * (PAGE_SIZE // NSA_BLOCK)
    n_sel = NSA_TOPK - 1
    assert nbp >= n_sel and NSA_TOPK <= 16 and n_pages % CMP_PAGES == 0
    page = lambda r: pl.BlockSpec((None, 2 * PAGE_SIZE, NSA_HD),
                                  lambda bi, j, pt: (page0 + pt[bi * n_pages + j * CMP_PAGES + r], 0, 0))
    mean_spec = pl.BlockSpec((None, nbp, NSA_HD), lambda bi, j, pt: (bi, 0, 0))
    return pl.pallas_call(
        functools.partial(_nsa_dec_cmp_kernel, nbp=nbp),
        grid_spec=pltpu.PrefetchScalarGridSpec(
            num_scalar_prefetch=1,
            grid=(b, n_pages // CMP_PAGES),
            in_specs=[pl.BlockSpec((None, 1, MIX_W), lambda bi, j, pt: (bi, 0, 0))]
            + [page(r) for r in range(CMP_PAGES)],
            out_specs=[mean_spec, mean_spec,
                       pl.BlockSpec((None, 8, NSA_HD), lambda bi, j, pt: (bi, 0, 0)),
                       pl.BlockSpec((None, 16, LANES), lambda bi, j, pt: (bi, 0, 0))],
        ),
        out_shape=[jax.ShapeDtypeStruct((b, nbp, NSA_HD), F32)] * 2
        + [jax.ShapeDtypeStruct((b, 8, NSA_HD), F32), jax.ShapeDtypeStruct((b, 16, LANES), jnp.int32)],
        compiler_params=_cp("parallel", "arbitrary"),
        name="nsa_dec_cmp",
    )(pt_flat, q_new, *([cmp_cache] * CMP_PAGES))


def _nsa_dec_attn_kernel(pt_ref, sel_ref, q_ref, blk_ref, snew_ref, win_ref, wnew_ref, oc_ref, sm_ref, o_ref,
                         m_sc, l_sc, acc_sc, *, sw):
    s_i = pl.program_id(1)
    q8 = _heads_to_rows(q_ref[...])

    @pl.when(s_i == 0)
    def _():
        m_sc[...] = jnp.sum(q8 * snew_ref[:, :NSA_HD], axis=-1, keepdims=True)
        l_sc[...] = jnp.ones_like(l_sc)
        acc_sc[...] = jnp.broadcast_to(snew_ref[:, NSA_HD:], acc_sc.shape)

    k, v = blk_ref[pl.ds(0, NSA_BLOCK, stride=2), :], blk_ref[pl.ds(1, NSA_BLOCK, stride=2), :]
    s = lax.dot_general(q8, k, NT, precision=HI, preferred_element_type=F32)
    m_prev = m_sc[...]
    m_new = jnp.maximum(m_prev, jnp.max(s, axis=-1, keepdims=True))
    alpha = jnp.exp(m_prev - m_new)
    p = jnp.exp(s - m_new)
    l_sc[...] = alpha * l_sc[...] + jnp.sum(p, axis=-1, keepdims=True)
    acc_sc[...] = alpha * acc_sc[...] + jnp.dot(p, v, precision=HI, preferred_element_type=F32)
    m_sc[...] = m_new

    @pl.when(s_i == pl.num_programs(1) - 1)
    def _():
        o_s = acc_sc[...] / l_sc[...]
        kw, vw = win_ref[pl.ds(0, sw, stride=2), :], win_ref[pl.ds(1, sw, stride=2), :]
        sw_ = lax.dot_general(q8, kw, NT, precision=HI, preferred_element_type=F32)
        wmask = (sw - _iota((8, sw), 1)) < NSA_WINDOW
        s_new = jnp.sum(q8 * wnew_ref[:, :NSA_HD], axis=-1, keepdims=True)
        m = jnp.maximum(jnp.max(jnp.where(wmask, sw_, -jnp.inf), axis=-1, keepdims=True), s_new)
        pw = jnp.where(wmask, jnp.exp(sw_ - m), 0.0)
        pn = jnp.exp(s_new - m)
        o_w = (jnp.dot(pw, vw, precision=HI, preferred_element_type=F32) + pn * wnew_ref[:, NSA_HD:]) / (
            jnp.sum(pw, axis=-1, keepdims=True) + pn)
        gate = jax.nn.sigmoid(sm_ref[...])
        o_c = oc_ref[...]
        for h in range(NSA_H):
            g0 = SM_NG + 3 * h
            o_ref[:, h * NSA_HD:(h + 1) * NSA_HD] = (
                gate[:, g0:g0 + 1] * o_c[h:h + 1] + gate[:, g0 + 1:g0 + 2] * o_s[h:h + 1]
                + gate[:, g0 + 2:g0 + 3] * o_w[h:h + 1])


def _nsa_dec_attn(pt_flat, sel_flat, n_pages, n_sel, page0, row0, q_new, slc_cache, slc_new, win_cache, win_new,
                  o_c, small_new):
    b = q_new.shape[0]
    sw = win_cache.shape[1] // 2
    per = PAGE_SIZE // NSA_BLOCK
    half = slc_cache.reshape(slc_cache.shape[0] * per, 2 * NSA_BLOCK, NSA_HD)

    def blk_map(bi, s, pt, sel):
        n = sel[bi * n_sel + s]
        return ((page0 + pt[bi * n_pages + n // per]) * per + n % per, 0, 0)

    per_b = lambda r, n: pl.BlockSpec((None, r, n), lambda bi, s, pt, sel: (bi, 0, 0))
    return pl.pallas_call(
        functools.partial(_nsa_dec_attn_kernel, sw=sw),
        grid_spec=pltpu.PrefetchScalarGridSpec(
            num_scalar_prefetch=2,
            grid=(b, n_sel),
            in_specs=[per_b(1, MIX_W), pl.BlockSpec((None, 2 * NSA_BLOCK, NSA_HD), blk_map), per_b(1, 2 * NSA_HD),
                      pl.BlockSpec((None, 2 * sw, NSA_HD), lambda bi, s, pt, sel: (row0 + bi, 0, 0)),
                      per_b(1, 2 * NSA_HD), per_b(8, NSA_HD), per_b(1, LANES)],
            out_specs=per_b(1, MIX_W),
            scratch_shapes=[pltpu.VMEM((8, 1), F32), pltpu.VMEM((8, 1), F32), pltpu.VMEM((8, NSA_HD), F32)],
        ),
        out_shape=jax.ShapeDtypeStruct((b, 1, MIX_W), F32),
        compiler_params=_cp("parallel", "arbitrary"),
        name="nsa_dec_attn",
    )(pt_flat, sel_flat, q_new, half, slc_new, win_cache, win_new, o_c, small_new)


def _lane_row(vals, at):
    return jnp.zeros((1, LANES), F32).at[0, at:at + vals.shape[0]].set(vals.astype(F32))


SEG_MG, SEG_DN, SEG_FOX, SEG_GLA, SEG_NSA, SEG_SMALL = (0, 8192), (8192, 2048), (10240, 1536), (11776, 1536), \
    (13312, 1280), (14592, 128)
SEGS = dict(mg=SEG_MG, dn=SEG_DN, fox=SEG_FOX, gla=SEG_GLA, nsa=SEG_NSA, small=SEG_SMALL)


def _split_offsets():
    offs = [0]
    for s in SPLIT_SIZES:
        offs.append(offs[-1] + s)
    return offs


def _regroup_pieces():
    o = _split_offsets()
    runs = [(18, 19, SEG_MG[0]), (14, 15, SEG_DN[0]), (17, 18, SEG_DN[0] + DN_QKV), (0, 3, SEG_FOX[0]),
            (4, 7, SEG_GLA[0]), (8, 9, SEG_GLA[0] + 2 * GLA_H * GLA_DK + GLA_H * GLA_DV), (9, 13, SEG_NSA[0])]
    return [(o[a], o[b] - o[a], dst) for a, b, dst in runs]


def _regroup_kernel(x_ref, sm_ref, o_ref):
    step = 4 * LANES
    for src, width, dst in _regroup_pieces():
        for c in range(0, width, step):
            wd = min(step, width - c)
            o_ref[:, dst + c:dst + c + wd] = x_ref[:, src + c:src + c + wd].astype(o_ref.dtype)
    o_ref[:, SEG_SMALL[0]:SEG_SMALL[0] + SEG_SMALL[1]] = sm_ref[...].astype(o_ref.dtype)


def _regroup_w_in(w_in):
    depth, k, d_in = w_in.shape
    o = _split_offsets()
    small = [w_in[:, :, o[i]:o[i + 1]] for i in (3, 7, 13, 15, 16)]
    pad = jnp.zeros((depth, k, LANES - sum(p.shape[2] for p in small)), F32)
    small = jnp.concatenate(small + [pad], axis=2)
    n_out = SEG_SMALL[0] + SEG_SMALL[1]
    tk = 128
    return pl.pallas_call(
        _regroup_kernel,
        grid=(depth, k // tk),
        in_specs=[pl.BlockSpec((None, tk, d_in), lambda l, i: (l, i, 0)),
                  pl.BlockSpec((None, tk, LANES), lambda l, i: (l, i, 0))],
        out_specs=pl.BlockSpec((None, tk, n_out), lambda l, i: (l, i, 0)),
        out_shape=jax.ShapeDtypeStruct((depth, k, n_out), BF16),
        compiler_params=_cp("parallel", "parallel"),
        name="regroup_w_in",
    )(w_in, small)


def _stacked_weights(w_in, w_branch, w_out, w_up, w_down):
    return dict(w_in=_regroup_w_in(w_in), wb=w_branch.astype(BF16), w_out=w_out.astype(BF16),
                w_up=w_up.astype(BF16), w_down=w_down.astype(BF16))


def _layer_rows(fox_b_f, gla_w_a2, gla_b_a, gla_norm, dn_conv_w, dn_a_log, dn_dt_bias, dn_norm):
    return dict(
        bf_row=_lane_row(fox_b_f, SM_FF),
        w2p=jnp.zeros((LANES, GLA_H * GLA_DK), F32).at[SM_GA:SM_GA + GLA_RANK].set(gla_w_a2),
        ba_row=gla_b_a.reshape(1, -1), gn_row=gla_norm.reshape(1, -1), conv_w=dn_conv_w,
        alog_row=_lane_row(dn_a_log, SM_DA), dtb_row=_lane_row(dn_dt_bias, SM_DA), dnn_row=dn_norm.reshape(1, -1))


def _rope_tables(pos):
    half = ROPE_DIM // 2
    inv = ROPE_THETA ** (-jnp.arange(half, dtype=F32) / half)
    ang = pos.astype(F32)[:, None] * inv[None, :]
    cos, sin = jnp.cos(ang), jnp.sin(ang)
    t = pos.shape[0]
    z = lambda n: jnp.zeros((t, n), F32)
    return (jnp.concatenate([cos, cos, jnp.ones((t, LANES - ROPE_DIM), F32)], axis=1),
            jnp.concatenate([-sin, z(LANES - half)], axis=1),
            jnp.concatenate([z(half), sin, z(LANES - ROPE_DIM)], axis=1))


def _project(h, w, l):
    return {name: _mm(h, w["w_in"], l, *SEGS[name]) for name in ("fox", "gla", "nsa", "dn", "small", "mg")}


def _mixer_prompt(h, b, t, w, l):
    p = _project(h, w, l)
    fox, gla, nsa, dn, small = (p[n].reshape(b, t, -1) for n in ("fox", "gla", "nsa", "dn", "small"))
    lf, c, ct, fox_kv = _fox_prep(small, w["bf_row"], fox)
    o_fox = _fox_attn(fox, c, ct)
    o_gla, s_gla = _gla(gla, small, w["w2p"], w["ba_row"], w["gn_row"], jnp.zeros((b, GLA_H, GLA_DK, GLA_DV), F32))
    qr, _, slc_kv, win_kv, cmp_c, slc_c, win_c, cmean = _nsa_prep(nsa, *_rope_tables(jnp.arange(t)), True)
    o_nsa = _nsa_attn(qr, small, cmean, slc_kv, win_kv)
    qd, kd, vd, gsm, btsm = _dn_prep(dn, jnp.zeros((b, 8, DN_QKV), F32), w["conv_w"], small, w["alog_row"],
                                     w["dtb_row"])
    o_dn, s_dn = _gdn(qd, kd, vd, gsm, btsm, dn, w["dnn_row"], jnp.zeros((b, DN_H, DN_DK, DN_DV), F32))
    mix = _branch_gate([o.reshape(b * t, MIX_W) for o in (o_fox, o_gla, o_nsa, o_dn)], w["wb"], l, p["mg"])
    kv5 = lambda a: a.reshape(b, t, 2, 1, NSA_HD)
    wl = min(NSA_WINDOW, t)
    state = (fox_kv.reshape(b, t, 2, FOX_H, FOX_HD), lf[:, :, :FOX_H], kv5(cmp_c), kv5(slc_c),
             kv5(win_c)[:, t - wl:], s_gla, s_dn, dn[:, t - (DN_CONV - 1):, :DN_QKV])
    return mix, state


def _mixer_sample(h, caches, l, pt_flat, n_pages, w):
    fox_kv_c, fox_lf_c, cmp_c, slc_c, win_c, s_gla_c, s_dn_c, conv_c = caches
    b = h.shape[0]
    past_len = n_pages * PAGE_SIZE
    p = _project(h, w, l)
    fox, gla, dn, small = (p[n].reshape(b, 1, -1) for n in ("fox", "gla", "dn", "small"))
    depth, n_pool = fox_kv_c.shape[:2]
    page0, row0 = l * n_pool, l * b
    o_fox, lf_new = _fox_decode(pt_flat, n_pages, page0, fox, small, w["bf_row"],
                                fox_kv_c.reshape(depth * n_pool, PAGE_SIZE * 2 * FOX_H, FOX_HD),
                                jnp.swapaxes(fox_lf_c, 2, 3).reshape(depth * n_pool, FOX_H, PAGE_SIZE))
    o_gla, o_dn, s_gla, s_dn = _rec_decode(
        gla, dn, small, w["w2p"], w["ba_row"], w["gn_row"], conv_c.reshape((depth * b,) + conv_c.shape[2:]),
        w["conv_w"], w["alog_row"], w["dtb_row"], w["dnn_row"], s_gla_c.reshape((depth * b,) + s_gla_c.shape[2:]),
        s_dn_c.reshape((depth * b,) + s_dn_c.shape[2:]), row0)
    tabs = _rope_tables(jnp.full((b,), past_len, jnp.int32))
    qr, cmp_new, slc_new, win_new = _nsa_prep(p["nsa"].reshape(1, b, -1), *tabs, False)[:4]
    qr, cmp_new, slc_new, win_new = (a.reshape(b, 1, -1) for a in (qr, cmp_new, slc_new, win_new))
    paged = lambda c: c.reshape(depth * n_pool, 2 * PAGE_SIZE, NSA_HD)
    _, _, o_c, sel = _nsa_dec_cmp(pt_flat, n_pages, page0, qr, paged(cmp_c))
    n_sel = NSA_TOPK - 1
    sel_flat = sel[:, :n_sel, 0].reshape(-1)
    sw = win_c.shape[2]
    o_nsa = _nsa_dec_attn(pt_flat, sel_flat, n_pages, n_sel, page0, row0, qr, paged(slc_c), slc_new,
                          win_c.reshape(depth * b, 2 * sw, NSA_HD), win_new, o_c, small)
    mix = _branch_gate([o.reshape(b, MIX_W).astype(BF16) for o in (o_fox, o_gla, o_nsa, o_dn)], w["wb"], l, p["mg"])
    kv5 = lambda a: a.reshape(b, 1, 2, 1, NSA_HD)
    wl = min(NSA_WINDOW, sw + 1)
    new_win = jnp.concatenate([win_c[l][:, sw + 1 - wl:], kv5(win_new)], axis=1)
    new_conv = jnp.concatenate([conv_c[l][:, 1:], dn[:, :, :DN_QKV]], axis=1)
    state = (fox[:, :, MIX_W:].reshape(b, 1, 2, FOX_H, FOX_HD), lf_new[:, :, :FOX_H], kv5(cmp_new), kv5(slc_new),
             new_win, s_gla, s_dn, new_conv)
    return mix, state


def _trunk_layer(x, h, mixer, norms, g_next, w, l):
    _, g_post_mix, g_pre_mlp, g_post_mlp = norms
    mix, state = mixer(h)
    x, h_mlp = _mm_norm_res(mix, w["w_out"], l, g_post_mix, x, g_pre_mlp)
    hid = _mm(h_mlp, w["w_up"], l, out_dtype=BF16, act="relu2")
    if g_next is None:
        return _mm_norm_res(hid, w["w_down"], l, g_post_mlp, x), None, state
    x, h_next = _mm_norm_res(hid, w["w_down"], l, g_post_mlp, x, g_next)
    return x, h_next, state


def kernel(x_prompt, x_sample, cache_fox_kv, cache_fox_logf, cache_nsa_cmp_kv, cache_nsa_slc_kv, cache_nsa_win_kv,
           state_gla, state_dn, state_dn_conv, page_table, norm_pre_mix, norm_post_mix, norm_pre_mlp, norm_post_mlp,
           w_in, fox_b_f, gla_w_a2, gla_b_a, gla_norm, dn_conv_w, dn_a_log, dn_dt_bias, dn_norm, w_branch, w_out,
           w_up, w_down):
    bp, tp, _ = x_prompt.shape
    bs, ts, _ = x_sample.shape
    assert ts == 1
    n_pages = page_table.shape[1]
    pt_flat = page_table.reshape(-1).astype(jnp.int32)
    y_p = x_prompt.reshape(bp * tp, D_MODEL)
    y_s = x_sample.reshape(bs, D_MODEL)
    new_p = [[] for _ in range(8)]
    new_s = [[] for _ in range(8)]
    caches = (cache_fox_kv, cache_fox_logf, cache_nsa_cmp_kv, cache_nsa_slc_kv, cache_nsa_win_kv, state_gla, state_dn,
              state_dn_conv)
    big = _stacked_weights(w_in, w_branch, w_out, w_up, w_down)
    h_p = _rmsnorm_cast(y_p, norm_pre_mix[0])
    h_s = _rmsnorm_cast(y_s, norm_pre_mix[0])
    for l in range(DEPTH):
        w = dict(big, **_layer_rows(fox_b_f[l], gla_w_a2[l], gla_b_a[l], gla_norm[l], dn_conv_w[l], dn_a_log[l],
                                    dn_dt_bias[l], dn_norm[l]))
        norms = (norm_pre_mix[l], norm_post_mix[l], norm_pre_mlp[l], norm_post_mlp[l])
        g_next = norm_pre_mix[l + 1] if l + 1 < DEPTH else None
        y_p, h_p, st_p = _trunk_layer(y_p, h_p, lambda h: _mixer_prompt(h, bp, tp, w, l), norms, g_next, w, l)
        y_s, h_s, st_s = _trunk_layer(y_s, h_s, lambda h: _mixer_sample(h, caches, l, pt_flat, n_pages, w), norms,
                                      g_next, w, l)
        for i in range(8):
            new_p[i].append(st_p[i])
            new_s[i].append(st_s[i])
    fox_kv_p, fox_logf_p, cmp_kv_p, slc_kv_p, win_kv_p, gla_p, dn_p, conv_p = [jnp.stack(a) for a in new_p]
    fox_kv_s, fox_logf_s, cmp_kv_s, slc_kv_s, win_kv_s, gla_s, dn_s, conv_s = [jnp.stack(a) for a in new_s]
    return (y_p.reshape(bp, tp, D_MODEL), y_s.reshape(bs, ts, D_MODEL), fox_kv_p, fox_kv_s, fox_logf_p, fox_logf_s,
            cmp_kv_p, cmp_kv_s, slc_kv_p, slc_kv_s, win_kv_p, win_kv_s, gla_p, gla_s, dn_p, dn_s, conv_p, conv_s)
```

```python
import functools

import jax
import jax.numpy as jnp
from jax import lax
from jax.experimental import pallas as pl
from jax.experimental.pallas import tpu as pltpu

F32 = jnp.float32
BF16 = jnp.bfloat16
HI = lax.Precision.HIGHEST
NT = (((1,), (1,)), ((), ()))
TN = (((0,), (0,)), ((), ()))
NEG = -1e30

D_MODEL = 2048
DEPTH = 2
PAGE_SIZE = 128
N_BRANCH = 4
MIX_W = D_MODEL // 4
FOX_H = 4
FOX_HD = MIX_W // FOX_H
GLA_H = 4
GLA_DK = MIX_W // (2 * GLA_H)
GLA_DV = MIX_W // GLA_H
GLA_RANK = 16
GLA_TAU = 16.0
NSA_H = 4
NSA_HD = MIX_W // NSA_H
NSA_BLOCK = 64
NSA_TOPK = 16
NSA_WINDOW = 512
DN_H = 4
DN_DK = MIX_W // DN_H
DN_DV = MIX_W // DN_H
DN_CONV = 4
DN_QKV = DN_H * (2 * DN_DK + DN_DV)
ROPE_DIM = NSA_HD // 4
ROPE_THETA = 500000.0
D_FF = 4 * D_MODEL
CHUNK = 64
SUB = 16
RMS_EPS = 1e-6
LANES = 128
SPLIT_SIZES = (
    FOX_H * FOX_HD, FOX_H * FOX_HD, FOX_H * FOX_HD, FOX_H,
    GLA_H * GLA_DK, GLA_H * GLA_DK, GLA_H * GLA_DV, GLA_RANK, GLA_H * GLA_DV,
    NSA_H * NSA_HD, 2 * NSA_HD, 2 * NSA_HD, 2 * NSA_HD, 3 * NSA_H,
    DN_QKV, DN_H, DN_H, DN_H * DN_DV,
    N_BRANCH * D_MODEL,
)
SM_FF, SM_GA, SM_NG, SM_DA, SM_DB = 0, 4, 20, 32, 36
VMEM_LIMIT = 56 * 1024 * 1024


def _cp(*sem):
    return pltpu.CompilerParams(dimension_semantics=sem, vmem_limit_bytes=VMEM_LIMIT)


def _pick(n, cap):
    if n <= cap:
        return n
    best = None
    for t in range(LANES, cap + 1, LANES):
        if n % t == 0:
            best = t
    assert best is not None, n
    return best


def _log_sigmoid(z):
    return jnp.minimum(z, 0.0) - jnp.log1p(jnp.exp(-jnp.abs(z)))


def _iota(shape, dim):
    return lax.broadcasted_iota(jnp.int32, shape, dim)


def _dot3(a, b, dims=(((1,), (0,)), ((), ()))):
    ah, bh = a.astype(BF16), b.astype(BF16)
    al, bl = (a - ah.astype(F32)).astype(BF16), (b - bh.astype(F32)).astype(BF16)
    f = lambda x, y: lax.dot_general(x, y, dims, preferred_element_type=F32)
    return f(ah, bh) + (f(ah, bl) + f(al, bh))


def _rmsnorm_cast_kernel(x_ref, g_ref, o_ref):
    x = x_ref[...]
    y = x * lax.rsqrt(jnp.mean(x * x, axis=-1, keepdims=True) + RMS_EPS)
    o_ref[...] = (y * g_ref[...]).astype(o_ref.dtype)


def _rmsnorm_cast(x, g):
    m, d = x.shape
    tm = min(m, 512)
    return pl.pallas_call(
        _rmsnorm_cast_kernel,
        grid=(m // tm,),
        in_specs=[pl.BlockSpec((tm, d), lambda i: (i, 0)), pl.BlockSpec((1, d), lambda i: (0, 0))],
        out_specs=pl.BlockSpec((tm, d), lambda i: (i, 0)),
        out_shape=jax.ShapeDtypeStruct((m, d), BF16),
        compiler_params=_cp("parallel"),
        name="rmsnorm_cast",
    )(x, g.reshape(1, d))


def _mm_kernel(a_ref, w_ref, o_ref, *, act):
    y = jnp.dot(a_ref[...], w_ref[...], preferred_element_type=F32)
    if act == "relu2":
        y = jnp.square(jnp.maximum(y, 0.0))
    o_ref[...] = y.astype(o_ref.dtype)


def _mm(a, w, l, col0=0, n=None, out_dtype=F32, act=None):
    m, k = a.shape
    n = w.shape[2] if n is None else n
    tm = min(m, 1024)
    tn = max(t for t in range(LANES, min(n, 2048) + 1, LANES) if n % t == 0 and col0 % t == 0)
    c0 = col0 // tn
    return pl.pallas_call(
        functools.partial(_mm_kernel, act=act),
        grid=(n // tn, m // tm),
        in_specs=[pl.BlockSpec((tm, k), lambda j, i: (i, 0)), pl.BlockSpec((None, k, tn), lambda j, i: (l, 0, c0 + j))],
        out_specs=pl.BlockSpec((tm, tn), lambda j, i: (i, j)),
        out_shape=jax.ShapeDtypeStruct((m, n), out_dtype),
        compiler_params=_cp("parallel", "parallel"),
        name="mm",
    )(a, w)


def _mm_norm_res_kernel(a_ref, w_ref, g_ref, x_ref, *refs):
    (gn_ref, o_ref, h_ref, acc_ref) = refs if len(refs) == 4 else (None, refs[0], None, refs[1])
    k = pl.program_id(1)

    @pl.when(k == 0)
    def _():
        acc_ref[...] = jnp.zeros_like(acc_ref)

    acc_ref[...] += jnp.dot(a_ref[...], w_ref[...], preferred_element_type=F32)

    @pl.when(k == pl.num_programs(1) - 1)
    def _():
        y = acc_ref[...]
        y = y * lax.rsqrt(jnp.mean(y * y, axis=-1, keepdims=True) + RMS_EPS)
        x = x_ref[...] + y * g_ref[...]
        o_ref[...] = x
        if h_ref is not None:
            h = x * lax.rsqrt(jnp.mean(x * x, axis=-1, keepdims=True) + RMS_EPS)
            h_ref[...] = (h * gn_ref[...]).astype(h_ref.dtype)


def _mm_norm_res(a, w, l, g, x, g_next=None):
    m, k = a.shape
    n = w.shape[2]
    tm = min(m, 512)
    tk = _pick(k, 2048)
    row = pl.BlockSpec((1, n), lambda i, kk: (0, 0))
    tile = pl.BlockSpec((tm, n), lambda i, kk: (i, 0))
    fused = g_next is not None
    return pl.pallas_call(
        _mm_norm_res_kernel,
        grid=(m // tm, k // tk),
        in_specs=[pl.BlockSpec((tm, tk), lambda i, kk: (i, kk)), pl.BlockSpec((None, tk, n), lambda i, kk: (l, kk, 0)),
                  row, tile] + ([row] if fused else []),
        out_specs=[tile, tile] if fused else tile,
        out_shape=[jax.ShapeDtypeStruct((m, n), F32), jax.ShapeDtypeStruct((m, n), BF16)] if fused
        else jax.ShapeDtypeStruct((m, n), F32),
        scratch_shapes=[pltpu.VMEM((tm, n), F32)],
        compiler_params=_cp("parallel", "arbitrary"),
        name="mm_norm_res",
    )(a, w, g.reshape(1, n), x, *([g_next.reshape(1, n)] if fused else []))


def _branch_gate_kernel(o0, o1, o2, o3, wb_ref, m0, m1, m2, m3, out_ref):
    acc = None
    for n, (o_n, m_n) in enumerate(((o0, m0), (o1, m1), (o2, m2), (o3, m3))):
        y = jnp.dot(o_n[...], wb_ref[n], preferred_element_type=F32)
        t = jax.nn.sigmoid(m_n[...]) * y
        acc = t if acc is None else acc + t
    out_ref[...] = acc.astype(out_ref.dtype)


def _branch_gate(branches, wb, l, mg):
    m = mg.shape[0]
    tm = min(m, 512)
    tn = 1024
    nj = D_MODEL // tn
    o_spec = pl.BlockSpec((tm, MIX_W), lambda j, i: (i, 0))
    m_specs = [pl.BlockSpec((tm, tn), functools.partial(lambda j, i, n: (i, n * nj + j), n=n)) for n in range(N_BRANCH)]
    return pl.pallas_call(
        _branch_gate_kernel,
        grid=(nj, m // tm),
        in_specs=[o_spec] * 4 + [pl.BlockSpec((None, N_BRANCH, MIX_W, tn), lambda j, i: (l, 0, 0, j))] + m_specs,
        out_specs=pl.BlockSpec((tm, tn), lambda j, i: (i, j)),
        out_shape=jax.ShapeDtypeStruct((m, D_MODEL), BF16),
        compiler_params=_cp("parallel", "parallel"),
        name="branch_gate",
    )(*branches, wb, mg, mg, mg, mg)


def _fox_prep_kernel(s_ref, b_ref, k_ref, v_ref, lf_ref, c_ref, ct_ref, kv_ref, carry_ref, *, tb):
    @pl.when(pl.program_id(1) == 0)
    def _():
        carry_ref[...] = jnp.zeros_like(carry_ref)

    for i, ref in enumerate((k_ref, v_ref)):
        for h in range(FOX_H):
            kv_ref[pl.ds(i * FOX_H + h, tb, stride=2 * FOX_H), :] = ref[:, h * FOX_HD:(h + 1) * FOX_HD]

    lf = _log_sigmoid(s_ref[...] + b_ref[...])
    tri = (_iota((tb, tb), 1) <= _iota((tb, tb), 0)).astype(F32)
    c = jnp.dot(tri, lf, precision=HI, preferred_element_type=F32) + carry_ref[...]
    lf_ref[...] = lf
    c_ref[...] = c
    ct_ref[...] = c.T[:8]
    carry_ref[...] = c[tb - 1:tb]


def _fox_prep(small, bias_row, fox):
    b, t, _ = small.shape
    tb = min(t, 256)
    blk = pl.BlockSpec((None, tb, LANES), lambda bi, i: (bi, i, 0))
    rows = 2 * FOX_H
    return pl.pallas_call(
        functools.partial(_fox_prep_kernel, tb=tb),
        grid=(b, t // tb),
        in_specs=[blk, pl.BlockSpec((1, LANES), lambda bi, i: (0, 0)),
                  pl.BlockSpec((None, tb, MIX_W), lambda bi, i: (bi, i, 1)),
                  pl.BlockSpec((None, tb, MIX_W), lambda bi, i: (bi, i, 2))],
        out_specs=[blk, blk, pl.BlockSpec((None, 8, tb), lambda bi, i: (bi, 0, i)),
                   pl.BlockSpec((None, tb * rows, FOX_HD), lambda bi, i: (bi, i, 0))],
        out_shape=[jax.ShapeDtypeStruct((b, t, LANES), F32)] * 2 + [jax.ShapeDtypeStruct((b, 8, t), F32),
                                                                  jax.ShapeDtypeStruct((b, t * rows, FOX_HD), F32)],
        scratch_shapes=[pltpu.VMEM((1, LANES), F32)],
        compiler_params=_cp("parallel", "arbitrary"),
        name="fox_prep",
    )(small, bias_row, fox, fox)


def _fox_attn_kernel(q_ref, k_ref, v_ref, cq_ref, ck_ref, o_ref, m_sc, l_sc, acc_sc, *, tq, tk):
    qi = pl.program_id(1)
    ki = pl.program_id(2)

    @pl.when(ki == 0)
    def _():
        m_sc[...] = jnp.full_like(m_sc, NEG)
        l_sc[...] = jnp.zeros_like(l_sc)
        acc_sc[...] = jnp.zeros_like(acc_sc)

    def step(diagonal):
        mask = _iota((tk, tq), 0) <= _iota((tk, tq), 1)
        heads = range(FOX_H)
        hsl = [slice(h * FOX_HD, (h + 1) * FOX_HD) for h in heads]
        ss = []
        for h in heads:
            q = (q_ref[:, hsl[h]] * (FOX_HD ** -0.5)).astype(BF16)
            s = lax.dot_general(k_ref[:, hsl[h]].astype(BF16), q, NT, preferred_element_type=F32)
            s = s + cq_ref[h:h + 1, :] - ck_ref[:, h:h + 1]
            ss.append(jnp.where(mask, s, NEG) if diagonal else s)
        m_prev = [m_sc[h] for h in heads]
        m_new = [jnp.maximum(m_prev[h], jnp.max(ss[h], axis=0, keepdims=True)) for h in heads]
        ps = []
        for h in heads:
            p = jnp.exp(ss[h] - m_new[h])
            ps.append(jnp.where(mask, p, 0.0) if diagonal else p)
        pvs = [jnp.dot(v_ref[:, hsl[h]].T.astype(BF16), ps[h].astype(BF16), preferred_element_type=F32)
               for h in heads]
        for h in heads:
            alpha = jnp.exp(m_prev[h] - m_new[h])
            l_sc[h] = alpha * l_sc[h] + jnp.sum(ps[h], axis=0, keepdims=True)
            acc_sc[h] = alpha * acc_sc[h] + pvs[h]
            m_sc[h] = m_new[h]

    pl.when(ki < qi)(functools.partial(step, False))
    pl.when(ki == qi)(functools.partial(step, True))

    @pl.when(ki == pl.num_programs(2) - 1)
    def _():
        for h in range(FOX_H):
            o_ref[:, h * FOX_HD:(h + 1) * FOX_HD] = (acc_sc[h] / l_sc[h]).T.astype(o_ref.dtype)


def _fox_attn(fox, c, ct):
    b, t, _ = fox.shape
    tq = tk = min(t, 512)

    def kmap(col):
        return lambda bi, qi, ki: (bi, jnp.minimum(ki, (qi * tq + tq - 1) // tk), col)

    return pl.pallas_call(
        functools.partial(_fox_attn_kernel, tq=tq, tk=tk),
        grid=(b, t // tq, t // tk),
        in_specs=[
            pl.BlockSpec((None, tq, MIX_W), lambda bi, qi, ki: (bi, qi, 0)),
            pl.BlockSpec((None, tk, MIX_W), kmap(1)),
            pl.BlockSpec((None, tk, MIX_W), kmap(2)),
            pl.BlockSpec((None, 8, tq), lambda bi, qi, ki: (bi, 0, qi)),
            pl.BlockSpec((None, tk, LANES), kmap(0)),
        ],
        out_specs=pl.BlockSpec((None, tq, MIX_W), lambda bi, qi, ki: (bi, qi, 0)),
        out_shape=jax.ShapeDtypeStruct((b, t, MIX_W), BF16),
        scratch_shapes=[pltpu.VMEM((FOX_H, 1, tq), F32), pltpu.VMEM((FOX_H, 1, tq), F32),
                        pltpu.VMEM((FOX_H, FOX_HD, tq), F32)],
        compiler_params=_cp("parallel", "parallel", "arbitrary"),
        name="fox_attn",
    )(fox, fox, fox, ct, c)


def _gla_kernel(g_ref, sm_ref, w2_ref, ba_ref, gn_ref, s0_ref, o_ref, sout_ref, s_sc, *, C):
    c = pl.program_id(1)

    @pl.when(c == 0)
    def _():
        s_sc[...] = s0_ref[...]

    lane = _iota((C, LANES), 1)
    ga = jnp.where((lane >= SM_GA) & (lane < SM_GA + GLA_RANK), sm_ref[...], 0.0)
    pre = jnp.dot(ga, w2_ref[...], precision=HI, preferred_element_type=F32) + ba_ref[...]
    loga = _log_sigmoid(pre) * (1.0 / GLA_TAU)
    tri = (_iota((C, C), 1) <= _iota((C, C), 0)).astype(F32)
    b_all = jnp.dot(tri, loga, precision=HI, preferred_element_type=F32)
    eye = (_iota((GLA_DK, GLA_DK), 0) == _iota((GLA_DK, GLA_DK), 1)).astype(F32)
    heads, blocks = range(GLA_H), range(C // SUB)
    kofs, vofs = GLA_H * GLA_DK, 2 * GLA_H * GLA_DK
    qs = [g_ref[:, h * GLA_DK:(h + 1) * GLA_DK] * (GLA_DK ** -0.5) for h in heads]
    ks = [g_ref[:, kofs + h * GLA_DK:kofs + (h + 1) * GLA_DK] for h in heads]
    vbs = [g_ref[:, vofs + h * GLA_DV:vofs + (h + 1) * GLA_DV].astype(BF16) for h in heads]
    bhs = [b_all[:, h * GLA_DK:(h + 1) * GLA_DK] for h in heads]
    states = [s_sc[h] for h in heads]
    o_inter = [jnp.dot((qs[h] * jnp.exp(bhs[h])).astype(BF16), states[h].astype(BF16), preferred_element_type=F32)
               for h in heads]
    atts = {}
    for h in heads:
        for ib in blocks[1:]:
            a0 = ib * SUB
            bi = bhs[h][a0:a0 + SUB]
            r = bi[0:1]
            qe = (qs[h][a0:a0 + SUB] * jnp.exp(bi - r)).astype(BF16)
            ke = (ks[h][:a0] * jnp.exp(r - bhs[h][:a0])).astype(BF16)
            atts[h, ib] = lax.dot_general(qe, ke, NT, preferred_element_type=F32)
    pair = 2 * GLA_DK
    mask3 = _iota((SUB, SUB, pair), 1) <= _iota((SUB, SUB, pair), 0)
    low = _iota((SUB, SUB, pair), 2) < GLA_DK
    ds = {}
    for hp in range(GLA_H // 2):
        q2 = g_ref[:, hp * pair:(hp + 1) * pair] * (GLA_DK ** -0.5)
        k2 = g_ref[:, kofs + hp * pair:kofs + (hp + 1) * pair]
        b2 = b_all[:, hp * pair:(hp + 1) * pair]
        for ib in blocks:
            rs = slice(ib * SUB, (ib + 1) * SUB)
            bi = b2[rs]
            diff = bi[:, None, :] - bi[None, :, :]
            e = jnp.where(mask3, jnp.exp(jnp.where(mask3, diff, 0.0)), 0.0)
            prod = q2[rs][:, None, :] * k2[rs][None, :, :] * e
            ds[2 * hp, ib] = jnp.sum(jnp.where(low, prod, 0.0), axis=-1)
            ds[2 * hp + 1, ib] = jnp.sum(jnp.where(low, 0.0, prod), axis=-1)
    rows = {}
    for h in heads:
        for ib in blocks:
            a0 = ib * SUB
            o_i = jnp.dot(ds[h, ib].astype(BF16), vbs[h][a0:a0 + SUB], preferred_element_type=F32)
            if ib > 0:
                o_i = o_i + jnp.dot(atts[h, ib].astype(BF16), vbs[h][:a0], preferred_element_type=F32)
            rows[h, ib] = o_i
    for h in heads:
        bend = bhs[h][C - 1:C]
        kdec = (ks[h] * jnp.exp(bend - bhs[h])).astype(BF16)
        dcol = jnp.sum(eye * jnp.exp(bend), axis=1, keepdims=True)
        s_sc[h] = dcol * states[h] + lax.dot_general(kdec, vbs[h], TN, preferred_element_type=F32)
    for h in heads:
        o = jnp.concatenate([rows[h, ib] for ib in blocks], axis=0) + o_inter[h]
        y = o * lax.rsqrt(jnp.mean(o * o, axis=-1, keepdims=True) + RMS_EPS) * gn_ref[...]
        gr = g_ref[:, vofs + GLA_H * GLA_DV + h * GLA_DV:vofs + GLA_H * GLA_DV + (h + 1) * GLA_DV]
        o_ref[:, h * GLA_DV:(h + 1) * GLA_DV] = (y * jax.nn.silu(gr)).astype(o_ref.dtype)

    @pl.when(c == pl.num_programs(1) - 1)
    def _():
        sout_ref[...] = s_sc[...]


def _gla(gla, small, w2p, ba_row, gn_row, s0):
    b, t, _ = gla.shape
    C = CHUNK
    n_gla = gla.shape[-1]
    return pl.pallas_call(
        functools.partial(_gla_kernel, C=C),
        grid=(b, t // C),
        in_specs=[
            pl.BlockSpec((None, C, n_gla), lambda bi, c: (bi, c, 0)),
            pl.BlockSpec((None, C, LANES), lambda bi, c: (bi, c, 0)),
            pl.BlockSpec(w2p.shape, lambda bi, c: (0, 0)),
            pl.BlockSpec(ba_row.shape, lambda bi, c: (0, 0)),
            pl.BlockSpec(gn_row.shape, lambda bi, c: (0, 0)),
            pl.BlockSpec((None, GLA_H, GLA_DK, GLA_DV), lambda bi, c: (bi, 0, 0, 0)),
        ],
        out_specs=[
            pl.BlockSpec((None, C, MIX_W), lambda bi, c: (bi, c, 0)),
            pl.BlockSpec((None, GLA_H, GLA_DK, GLA_DV), lambda bi, c: (bi, 0, 0, 0)),
        ],
        out_shape=[jax.ShapeDtypeStruct((b, t, MIX_W), BF16), jax.ShapeDtypeStruct(s0.shape, F32)],
        scratch_shapes=[pltpu.VMEM((GLA_H, GLA_DK, GLA_DV), F32)],
        compiler_params=_cp("parallel", "arbitrary"),
        name="gla",
    )(gla, small, w2p, ba_row, gn_row, s0)


def _rope(x, cos, sa, sb):
    half = ROPE_DIM // 2
    return x * cos + pltpu.roll(x, LANES - half, 1) * sa + pltpu.roll(x, half, 1) * sb


def _nsa_prep_kernel(x_ref, cos_ref, sa_ref, sb_ref, q_ref, cmp_ref, slc_ref, win_ref, cmp_c, slc_c, win_c,
                     *mean_ref, tb):
    cos, sa, sb = cos_ref[...], sa_ref[...], sb_ref[...]
    for h in range(NSA_H):
        hs = slice(h * NSA_HD, (h + 1) * NSA_HD)
        q_ref[:, hs] = _rope(x_ref[:, hs], cos, sa, sb) * (NSA_HD ** -0.5)
    base = NSA_H * NSA_HD
    for i, (ref, cache_ref) in enumerate(((cmp_ref, cmp_c), (slc_ref, slc_c), (win_ref, win_c))):
        k0 = base + i * 2 * NSA_HD
        k = _rope(x_ref[:, k0:k0 + NSA_HD], cos, sa, sb)
        v = x_ref[:, k0 + NSA_HD:k0 + 2 * NSA_HD]
        ref[:, :NSA_HD] = k
        ref[:, NSA_HD:] = v
        cache_ref[pl.ds(0, tb, stride=2), :] = k
        cache_ref[pl.ds(1, tb, stride=2), :] = v
    if mean_ref:
        kv = cmp_ref[...]
        mean_ref[0][...] = jnp.mean(kv.reshape(tb // NSA_BLOCK, NSA_BLOCK, 2 * NSA_HD), axis=1)


def _nsa_prep(nsa, cos, sa, sb, with_means):
    b, t, n = nsa.shape
    tb = min(t, 512)
    tab = pl.BlockSpec((tb, LANES), lambda bi, i: (i, 0))
    kv = pl.BlockSpec((None, tb, 2 * NSA_HD), lambda bi, i: (bi, i, 0))
    kvc = pl.BlockSpec((None, 2 * tb, NSA_HD), lambda bi, i: (bi, i, 0))
    out_specs = [pl.BlockSpec((None, tb, MIX_W), lambda bi, i: (bi, i, 0)), kv, kv, kv, kvc, kvc, kvc]
    out_shape = ([jax.ShapeDtypeStruct((b, t, MIX_W), F32)] + [jax.ShapeDtypeStruct((b, t, 2 * NSA_HD), F32)] * 3
                 + [jax.ShapeDtypeStruct((b, 2 * t, NSA_HD), F32)] * 3)
    if with_means:
        out_specs.append(pl.BlockSpec((None, tb // NSA_BLOCK, 2 * NSA_HD), lambda bi, i: (bi, i, 0)))
        out_shape.append(jax.ShapeDtypeStruct((b, t // NSA_BLOCK, 2 * NSA_HD), F32))
    return pl.pallas_call(
        functools.partial(_nsa_prep_kernel, tb=tb),
        grid=(b, t // tb),
        in_specs=[pl.BlockSpec((None, tb, n), lambda bi, i: (bi, i, 0)), tab, tab, tab],
        out_specs=out_specs,
        out_shape=out_shape,
        compiler_params=_cp("parallel", "parallel"),
        name="nsa_prep",
    )(nsa, cos, sa, sb)


def _masked_softmax(s, mask):
    s = jnp.where(mask, s, -jnp.inf)
    m = jnp.max(s, axis=-1, keepdims=True)
    m = jnp.where(m == -jnp.inf, 0.0, m)
    p = jnp.where(mask, jnp.exp(s - m), 0.0)
    return p / jnp.maximum(jnp.sum(p, axis=-1, keepdims=True), 1e-30)


def _topk_mask_t(score_t, blk_t, n_sel):
    rank = jnp.zeros(score_t.shape, jnp.int32)
    for m in range(score_t.shape[0]):
        sm = score_t[m:m + 1, :]
        beats = (sm > score_t) | ((sm == score_t) & (blk_t > m))
        rank = rank + beats.astype(jnp.int32)
    return rank < n_sel


def _masked_scores(qs, k, mask):
    bias = jnp.where(mask, 0.0, NEG)
    return [lax.dot_general(k, q, NT, preferred_element_type=F32) + bias for q in qs]


def _softmax_step(carry, ss, v_t):
    ms, ls, accs = carry
    n = range(len(ss))
    m_new = [jnp.maximum(ms[g], jnp.max(ss[g], axis=0, keepdims=True)) for g in n]
    ps = [jnp.exp(ss[g] - m_new[g]) for g in n]
    pvs = [jnp.dot(v_t, ps[g].astype(BF16), preferred_element_type=F32) for g in n]
    alphas = [jnp.exp(ms[g] - m_new[g]) for g in n]
    ls = [alphas[g] * ls[g] + jnp.sum(ps[g], axis=0, keepdims=True) for g in n]
    accs = [alphas[g] * accs[g] + pvs[g] for g in n]
    return m_new, ls, accs


def _nsa_attn_kernel(q_ref, sm_ref, cm_ref, slc_ref, win_ref, o_ref, *, tq, kt, nb):
    qi = pl.program_id(1)
    qpos = qi * tq + _iota((tq, 1), 0)
    blk = _iota((tq, nb), 1)
    cmask = (blk + 1) * NSA_BLOCK <= qpos + 1
    kc, vc = cm_ref[:, :NSA_HD], cm_ref[:, NSA_HD:]
    o_c = []
    imp = jnp.zeros((tq, nb), F32)
    for h in range(NSA_H):
        s = lax.dot_general(q_ref[:, h * NSA_HD:(h + 1) * NSA_HD], kc, NT, precision=HI, preferred_element_type=F32)
        p = _masked_softmax(s, cmask)
        imp = imp + p
        o_c.append(jnp.dot(p, vc, precision=HI, preferred_element_type=F32))
    blk_t = _iota((nb, tq), 0)
    qpos_t = qi * tq + _iota((nb, tq), 1)
    cur_t = qpos_t // NSA_BLOCK
    score_t = jnp.where(blk_t == cur_t, jnp.inf,
                        jnp.where((blk_t + 1) * NSA_BLOCK <= qpos_t + 1, imp.T, -jnp.inf))
    sel_b = (_topk_mask_t(score_t, blk_t, min(NSA_TOPK, nb)) & (blk_t <= cur_t)).astype(BF16)
    qs = [q_ref[:, h * NSA_HD:(h + 1) * NSA_HD].astype(BF16) for h in range(NSA_H)]
    krow = _iota((kt, tq), 0)
    qpos_row = qi * tq + _iota((1, tq), 1)
    blk_of_col = _iota((nb, kt), 1) // NSA_BLOCK - _iota((nb, kt), 0)
    last = (qi * tq + tq - 1) // kt

    def init():
        return ([jnp.full((1, tq), NEG, F32)] * NSA_H, [jnp.zeros((1, tq), F32)] * NSA_H,
                [jnp.zeros((NSA_HD, tq), F32)] * NSA_H)

    def tile_rows(kb):
        return pl.ds(pl.multiple_of(kb * kt, kt), kt)

    def slc_scores(kb):
        expand = (blk_of_col + kb * (kt // NSA_BLOCK) == 0).astype(BF16)
        tok = lax.dot_general(expand, sel_b, TN, preferred_element_type=F32)
        mask = (tok > 0.5) & (kb * kt + krow <= qpos_row)
        return _masked_scores(qs, slc_ref[tile_rows(kb), :NSA_HD].astype(BF16), mask)

    def win_scores(kb):
        dist = qpos_row - (kb * kt + krow)
        mask = (dist >= 0) & (dist < NSA_WINDOW)
        return _masked_scores(qs, win_ref[tile_rows(kb), :NSA_HD].astype(BF16), mask)

    def attend(scores_of, kv_ref, first):
        def body(kb, state):
            v_t = kv_ref[tile_rows(kb), NSA_HD:].T.astype(BF16)
            return _softmax_step(state, scores_of(kb), v_t)

        _, ls, accs = lax.fori_loop(first, last + 1, body, init())
        return ls, accs

    l_s, acc_s = attend(slc_scores, slc_ref, 0)
    l_w, acc_w = attend(win_scores, win_ref, jnp.maximum(qi * tq - (NSA_WINDOW - 1), 0) // kt)
    gate = jax.nn.sigmoid(sm_ref[...])
    for h in range(NSA_H):
        g0 = SM_NG + 3 * h
        o = (gate[:, g0:g0 + 1] * o_c[h] + gate[:, g0 + 1:g0 + 2] * (acc_s[h] / l_s[h]).T
             + gate[:, g0 + 2:g0 + 3] * (acc_w[h] / l_w[h]).T)
        o_ref[:, h * NSA_HD:(h + 1) * NSA_HD] = o.astype(o_ref.dtype)


def _nsa_attn(qr, small, cmean, slc, win):
    b, t, _ = qr.shape
    tq = min(t, 256)
    kt = min(t, 512)
    nb = t // NSA_BLOCK
    whole = lambda n: pl.BlockSpec((None, n, 2 * NSA_HD), lambda bi, qi: (bi, 0, 0))
    return pl.pallas_call(
        functools.partial(_nsa_attn_kernel, tq=tq, kt=kt, nb=nb),
        grid=(b, t // tq),
        in_specs=[
            pl.BlockSpec((None, tq, MIX_W), lambda bi, qi: (bi, qi, 0)),
            pl.BlockSpec((None, tq, LANES), lambda bi, qi: (bi, qi, 0)),
            whole(nb), whole(t), whole(t),
        ],
        out_specs=pl.BlockSpec((None, tq, MIX_W), lambda bi, qi: (bi, qi, 0)),
        out_shape=jax.ShapeDtypeStruct((b, t, MIX_W), BF16),
        compiler_params=_cp("parallel", "parallel"),
        name="nsa_attn",
    )(qr, small, cmean, slc, win)


def _dn_prep_kernel(x_ref, halo_ref, prev_ref, cw_ref, sm_ref, alog_ref, dtb_ref,
                    q_ref, k_ref, v_ref, g_ref, bt_ref, *, tb):
    halo = jnp.where(pl.program_id(1) == 0, prev_ref[...], halo_ref[...])
    xcat = jnp.concatenate([halo, x_ref[...]], axis=0)
    conv = x_ref[...] * cw_ref[DN_CONV - 1:DN_CONV]
    for j in range(DN_CONV - 1):
        sh = DN_CONV - 1 - j
        conv = conv + pltpu.roll(xcat, sh, 0)[8:] * cw_ref[j:j + 1]
    u = jax.nn.silu(conv)
    for h in range(DN_H):
        hs = slice(h * DN_DK, (h + 1) * DN_DK)
        uq = u[:, hs]
        q_ref[:, hs] = uq * lax.rsqrt(jnp.sum(uq * uq, axis=-1, keepdims=True) + 1e-6) * (DN_DK ** -0.5)
        uk = u[:, DN_H * DN_DK + h * DN_DK:DN_H * DN_DK + (h + 1) * DN_DK]
        k_ref[:, hs] = uk * lax.rsqrt(jnp.sum(uk * uk, axis=-1, keepdims=True) + 1e-6)
    v_ref[...] = u[:, 2 * DN_H * DN_DK:]
    sm = sm_ref[...]
    g_ref[...] = -jnp.exp(alog_ref[...]) * jax.nn.softplus(sm + dtb_ref[...])
    bt_ref[...] = jax.nn.sigmoid(sm)


def _dn_prep(dn, prev8, conv_w, small, alog_row, dtb_row):
    b, t, _ = dn.shape
    tb = min(t, 256)
    hb = tb // 8
    row = pl.BlockSpec((None, tb, MIX_W), lambda bi, i: (bi, i, 0))
    sm = pl.BlockSpec((None, tb, LANES), lambda bi, i: (bi, i, 0))
    one = lambda shape: pl.BlockSpec(shape, lambda bi, i: (0,) * len(shape))
    return pl.pallas_call(
        functools.partial(_dn_prep_kernel, tb=tb),
        grid=(b, t // tb),
        in_specs=[
            pl.BlockSpec((None, tb, DN_QKV), lambda bi, i: (bi, i, 0)),
            pl.BlockSpec((None, 8, DN_QKV), lambda bi, i: (bi, jnp.maximum(i * hb - 1, 0), 0)),
            pl.BlockSpec((None, 8, DN_QKV), lambda bi, i: (bi, 0, 0)),
            one(conv_w.shape), sm, one(alog_row.shape), one(dtb_row.shape),
        ],
        out_specs=[row, row, row, sm, sm],
        out_shape=[jax.ShapeDtypeStruct((b, t, MIX_W), F32)] * 3 + [jax.ShapeDtypeStruct((b, t, LANES), F32)] * 2,
        compiler_params=_cp("parallel", "parallel"),
        name="dn_prep",
    )(dn, dn, prev8, conv_w, small, alog_row, dtb_row)


def _gdn_pre_kernel(q_ref, k_ref, v_ref, g_ref, bt_ref, wv_ref, wk_ref, qe_ref, kd_ref, qk_ref, gam_ref, *, C, nck):
    row, col = _iota((C, C), 0), _iota((C, C), 1)
    incl, strict = col <= row, col < row
    eye = (row == col).astype(F32)
    tri = incl.astype(F32)
    n_sq = C.bit_length() - 2
    chains = []
    for c in range(nck):
        rs = slice(c * C, (c + 1) * C)
        gam_all = jnp.dot(tri, g_ref[rs, :], precision=HI, preferred_element_type=F32)
        gam_ref[rs, :] = gam_all
        gam_t = gam_all.T
        for h in range(DN_H):
            hs = slice(h * DN_DK, (h + 1) * DN_DK)
            q, k = q_ref[rs, hs], k_ref[rs, hs]
            gcol = gam_all[:, SM_DA + h:SM_DA + h + 1]
            bcol = bt_ref[rs, SM_DB + h:SM_DB + h + 1]
            diff = gcol - gam_t[SM_DA + h:SM_DA + h + 1, :]
            dec_incl = jnp.where(incl, jnp.exp(jnp.where(incl, diff, 0.0)), 0.0)
            qk = lax.dot_general(q.astype(BF16), k.astype(BF16), NT, preferred_element_type=F32) * dec_incl
            qk_ref[rs, h * C:(h + 1) * C] = qk.astype(BF16)
            qe_ref[rs, hs] = (q * jnp.exp(gcol)).astype(BF16)
            kd_ref[rs, hs] = (k * jnp.exp(gcol[C - 1:C] - gcol)).astype(BF16)
            x = -(bcol * _dot3(k, k, NT) * jnp.where(strict, dec_incl, 0.0))
            chains.append((rs, hs, x, gcol, bcol))
    xs = [ch[2] for ch in chains]
    ps = [eye + x for x in xs]
    for _ in range(n_sq):
        xs = [_dot3(x, x) for x in xs]
        ps = [p + _dot3(p, x) for p, x in zip(ps, xs)]
    for (rs, hs, _, gcol, bcol), p in zip(chains, ps):
        k, v = k_ref[rs, hs], v_ref[rs, hs]
        rhs = jnp.concatenate([bcol * v, (bcol * jnp.exp(gcol)) * k], axis=1)
        w = _dot3(p, rhs)
        wv_ref[rs, hs] = w[:, :DN_DV]
        wk_ref[rs, hs] = w[:, DN_DV:].astype(BF16)


def _gdn_rec_kernel(wv_ref, wk_ref, qe_ref, kd_ref, qk_ref, gam_ref, z_ref, nrm_ref, s0_ref, o_ref, sout_ref, s_sc,
                    *, C, nck):
    i = pl.program_id(1)

    @pl.when(i == 0)
    def _():
        s_sc[...] = s0_ref[...]

    state = [s_sc[h] for h in range(DN_H)]
    heads = range(DN_H)
    hsl = [slice(h * DN_DK, (h + 1) * DN_DK) for h in heads]
    for c in range(nck):
        rs = slice(c * C, (c + 1) * C)
        sbs = [state[h].astype(BF16) for h in heads]
        ubs = [(wv_ref[rs, hsl[h]] - jnp.dot(wk_ref[rs, hsl[h]], sbs[h], preferred_element_type=F32)).astype(BF16)
               for h in heads]
        outs = [jnp.dot(qe_ref[rs, hsl[h]], sbs[h], preferred_element_type=F32)
                + jnp.dot(qk_ref[rs, h * C:(h + 1) * C], ubs[h], preferred_element_type=F32) for h in heads]
        gend = gam_ref[c * C + C - 1:(c + 1) * C, :]
        state = [jnp.exp(gend[:, SM_DA + h:SM_DA + h + 1]) * state[h]
                 + lax.dot_general(kd_ref[rs, hsl[h]], ubs[h], TN, preferred_element_type=F32) for h in heads]
        for h in heads:
            o = outs[h]
            y = o * lax.rsqrt(jnp.mean(o * o, axis=-1, keepdims=True) + RMS_EPS) * nrm_ref[...]
            o_ref[rs, hsl[h]] = (y * jax.nn.silu(z_ref[rs, hsl[h]])).astype(o_ref.dtype)
    for h in range(DN_H):
        s_sc[h] = state[h]

    @pl.when(i == pl.num_programs(1) - 1)
    def _():
        for h in range(DN_H):
            sout_ref[h] = state[h]


GDN_BLOCK = 256


def _gdn(qd, kd, vd, gsm, btsm, dn, nrm_row, s0):
    b, t, _ = qd.shape
    C = CHUNK
    tb = min(t, GDN_BLOCK)
    nck = tb // C
    row = pl.BlockSpec((None, tb, MIX_W), lambda bi, i: (bi, i, 0))
    sm = pl.BlockSpec((None, tb, LANES), lambda bi, i: (bi, i, 0))
    qk_spec = pl.BlockSpec((None, tb, DN_H * C), lambda bi, i: (bi, i, 0))
    st = pl.BlockSpec((None, DN_H, DN_DK, DN_DV), lambda bi, i: (bi, 0, 0, 0))
    wide = lambda dt: jax.ShapeDtypeStruct((b, t, MIX_W), dt)
    wv, wk, qe, kdc, qk, gam = pl.pallas_call(
        functools.partial(_gdn_pre_kernel, C=C, nck=nck),
        grid=(b, t // tb),
        in_specs=[row, row, row, sm, sm],
        out_specs=[row, row, row, row, qk_spec, sm],
        out_shape=[wide(F32), wide(BF16), wide(BF16), wide(BF16), jax.ShapeDtypeStruct((b, t, DN_H * C), BF16),
                   jax.ShapeDtypeStruct((b, t, LANES), F32)],
        compiler_params=_cp("parallel", "parallel"),
        name="gdn_pre",
    )(qd, kd, vd, gsm, btsm)
    return pl.pallas_call(
        functools.partial(_gdn_rec_kernel, C=C, nck=nck),
        grid=(b, t // tb),
        in_specs=[row, row, row, row, qk_spec, sm,
                  pl.BlockSpec((None, tb, MIX_W), lambda bi, i: (bi, i, DN_QKV // MIX_W)),
                  pl.BlockSpec(nrm_row.shape, lambda bi, i: (0, 0)), st],
        out_specs=[row, st],
        out_shape=[wide(BF16), jax.ShapeDtypeStruct(s0.shape, F32)],
        scratch_shapes=[pltpu.VMEM((DN_H, DN_DK, DN_DV), F32)],
        compiler_params=_cp("parallel", "arbitrary"),
        name="gdn_rec",
    )(wv, wk, qe, kdc, qk, gam, dn, nrm_row, s0)


FOX_PAGES = 8


def _fox_dec_kernel(pt_ref, fx_ref, sm_ref, b_ref, *refs):
    kv_refs, lf_refs = refs[:FOX_PAGES], refs[FOX_PAGES:2 * FOX_PAGES]
    o_ref, lfo_ref, m_sc, l_sc, acc_sc, car_sc = refs[2 * FOX_PAGES:]
    j = pl.program_id(1)
    P = PAGE_SIZE
    rows_per_tok = 2 * FOX_H
    qs = [fx_ref[:, h * FOX_HD:(h + 1) * FOX_HD] * (FOX_HD ** -0.5) for h in range(FOX_H)]

    @pl.when(j == 0)
    def _():
        lf_new = _log_sigmoid(sm_ref[...] + b_ref[...])
        lfo_ref[...] = lf_new
        on_diag = _iota((8, LANES), 1) == _iota((8, LANES), 0)
        car_sc[...] = jnp.broadcast_to(jnp.sum(jnp.where(on_diag, lf_new, 0.0), axis=1, keepdims=True), (8, LANES))
        for h in range(FOX_H):
            k_new = fx_ref[:, MIX_W + h * FOX_HD:MIX_W + (h + 1) * FOX_HD]
            m_sc[h] = jnp.sum(qs[h] * k_new, axis=-1, keepdims=True)
            l_sc[h] = jnp.ones((1, 1), F32)
            acc_sc[h] = fx_ref[:, 2 * MIX_W + h * FOX_HD:2 * MIX_W + (h + 1) * FOX_HD]

    upper = (_iota((P, P), 1) > _iota((P, P), 0)).astype(F32)
    carry = car_sc[...]
    pad = jnp.zeros((8 - FOX_H, P), F32)
    scores = []
    for kv_ref, lf_ref in zip(kv_refs, lf_refs):
        lf_t = jnp.concatenate([lf_ref[...], pad], axis=0)
        bias = (lax.dot_general(lf_t, upper, NT, precision=HI, preferred_element_type=F32) + carry).T
        scores.append([jnp.sum(kv_ref[pl.ds(h, P, stride=rows_per_tok), :] * qs[h], axis=-1, keepdims=True)
                       + bias[:, h:h + 1] for h in range(FOX_H)])
        carry = carry + jnp.sum(lf_t, axis=1, keepdims=True)
    car_sc[...] = carry
    outs = []
    for h in range(FOX_H):
        m_step = jnp.max(scores[0][h], axis=0, keepdims=True)
        for g in range(1, FOX_PAGES):
            m_step = jnp.maximum(m_step, jnp.max(scores[g][h], axis=0, keepdims=True))
        m_prev = m_sc[h]
        m_new = jnp.maximum(m_prev, m_step)
        alpha = jnp.exp(m_prev - m_new)
        l_new = alpha * l_sc[h]
        acc = alpha * acc_sc[h]
        for g, kv_ref in enumerate(kv_refs):
            p = jnp.exp(scores[g][h] - m_new)
            l_new = l_new + jnp.sum(p, axis=0, keepdims=True)
            acc = acc + jnp.sum(p * kv_ref[pl.ds(FOX_H + h, P, stride=rows_per_tok), :], axis=0, keepdims=True)
        m_sc[h], l_sc[h], acc_sc[h] = m_new, l_new, acc
        outs.append(acc / l_new)

    @pl.when(j == pl.num_programs(1) - 1)
    def _():
        for h in range(FOX_H):
            o_ref[:, h * FOX_HD:(h + 1) * FOX_HD] = outs[h]


def _fox_decode(pt_flat, n_pages, page0, fox_new, small_new, bias_row, kv_cache, lf_cache):
    b = fox_new.shape[0]
    assert n_pages % FOX_PAGES == 0

    def page(r):
        return lambda bi, j, pt: (page0 + pt[bi * n_pages + n_pages - 1 - (j * FOX_PAGES + r)], 0, 0)

    per_b = lambda n: pl.BlockSpec((None, 1, n), lambda bi, j, pt: (bi, 0, 0))
    return pl.pallas_call(
        _fox_dec_kernel,
        grid_spec=pltpu.PrefetchScalarGridSpec(
            num_scalar_prefetch=1,
            grid=(b, n_pages // FOX_PAGES),
            in_specs=[per_b(3 * MIX_W), per_b(LANES), pl.BlockSpec((1, LANES), lambda bi, j, pt: (0, 0))]
            + [pl.BlockSpec((None, PAGE_SIZE * 2 * FOX_H, FOX_HD), page(r)) for r in range(FOX_PAGES)]
            + [pl.BlockSpec((None, FOX_H, PAGE_SIZE), page(r)) for r in range(FOX_PAGES)],
            out_specs=[per_b(MIX_W), per_b(LANES)],
            scratch_shapes=[pltpu.VMEM((FOX_H, 1, 1), F32), pltpu.VMEM((FOX_H, 1, 1), F32),
                            pltpu.VMEM((FOX_H, 1, FOX_HD), F32), pltpu.VMEM((8, PAGE_SIZE), F32)],
        ),
        out_shape=[jax.ShapeDtypeStruct((b, 1, MIX_W), F32), jax.ShapeDtypeStruct((b, 1, LANES), F32)],
        compiler_params=_cp("parallel", "arbitrary"),
        name="fox_decode",
    )(pt_flat, fox_new, small_new, bias_row, *([kv_cache] * FOX_PAGES), *([lf_cache] * FOX_PAGES))


def _col(row, eye):
    return jnp.sum(eye * row, axis=1, keepdims=True)


def _rec_dec_kernel(g_ref, d_ref, sm_ref, w2_ref, ba_ref, gn_ref, cp_ref, cw_ref, alog_ref, dtb_ref, nrm_ref,
                    sg_ref, sd_ref, og_ref, od_ref, sgo_ref, sdo_ref):
    sm = sm_ref[...]
    lane = _iota((8, LANES), 1)
    ga = jnp.where((lane >= SM_GA) & (lane < SM_GA + GLA_RANK), jnp.broadcast_to(sm, (8, LANES)), 0.0)
    pre = jnp.dot(ga, w2_ref[...], precision=HI, preferred_element_type=F32)[0:1] + ba_ref[...]
    loga = _log_sigmoid(pre) * (1.0 / GLA_TAU)
    eye_k = (_iota((GLA_DK, GLA_DK), 0) == _iota((GLA_DK, GLA_DK), 1)).astype(F32)
    for h in range(GLA_H):
        q = g_ref[:, h * GLA_DK:(h + 1) * GLA_DK] * (GLA_DK ** -0.5)
        k = g_ref[:, GLA_H * GLA_DK + h * GLA_DK:GLA_H * GLA_DK + (h + 1) * GLA_DK]
        v0 = 2 * GLA_H * GLA_DK + h * GLA_DV
        v = g_ref[:, v0:v0 + GLA_DV]
        gr = g_ref[:, v0 + GLA_H * GLA_DV:v0 + GLA_H * GLA_DV + GLA_DV]
        ea = jnp.exp(loga[:, h * GLA_DK:(h + 1) * GLA_DK])
        s0 = sg_ref[h]
        o = jnp.sum(q * k, axis=-1, keepdims=True) * v + jnp.sum(_col(q * ea, eye_k) * s0, axis=0, keepdims=True)
        sgo_ref[h] = _col(ea, eye_k) * s0 + _col(k, eye_k) * v
        y = o * lax.rsqrt(jnp.mean(o * o, axis=-1, keepdims=True) + RMS_EPS) * gn_ref[...]
        og_ref[:, h * GLA_DV:(h + 1) * GLA_DV] = y * jax.nn.silu(gr)
    conv = d_ref[:, :DN_QKV] * cw_ref[DN_CONV - 1:DN_CONV]
    for j in range(DN_CONV - 1):
        conv = conv + cp_ref[j:j + 1] * cw_ref[j:j + 1]
    u = jax.nn.silu(conv)
    gall = -jnp.exp(alog_ref[...]) * jax.nn.softplus(sm + dtb_ref[...])
    ball = jax.nn.sigmoid(sm)
    eye_d = (_iota((DN_DK, DN_DK), 0) == _iota((DN_DK, DN_DK), 1)).astype(F32)
    for h in range(DN_H):
        uq = u[:, h * DN_DK:(h + 1) * DN_DK]
        q = uq * lax.rsqrt(jnp.sum(uq * uq, axis=-1, keepdims=True) + 1e-6) * (DN_DK ** -0.5)
        uk = u[:, DN_H * DN_DK + h * DN_DK:DN_H * DN_DK + (h + 1) * DN_DK]
        k = uk * lax.rsqrt(jnp.sum(uk * uk, axis=-1, keepdims=True) + 1e-6)
        v = u[:, 2 * DN_H * DN_DK + h * DN_DV:2 * DN_H * DN_DK + (h + 1) * DN_DV]
        eg = jnp.exp(gall[:, SM_DA + h:SM_DA + h + 1])
        beta = ball[:, SM_DB + h:SM_DB + h + 1]
        s0 = sd_ref[h]
        kcol = _col(k, eye_d)
        ks = jnp.sum(kcol * s0, axis=0, keepdims=True)
        qs = jnp.sum(_col(q, eye_d) * s0, axis=0, keepdims=True)
        un = beta * (v - eg * ks)
        o = eg * qs + jnp.sum(q * k, axis=-1, keepdims=True) * un
        sdo_ref[h] = eg * s0 + kcol * un
        y = o * lax.rsqrt(jnp.mean(o * o, axis=-1, keepdims=True) + RMS_EPS) * nrm_ref[...]
        z = d_ref[:, DN_QKV + h * DN_DV:DN_QKV + (h + 1) * DN_DV]
        od_ref[:, h * DN_DV:(h + 1) * DN_DV] = y * jax.nn.silu(z)


def _rec_decode(gla_new, dn_new, small_new, w2p, ba_row, gn_row, conv_prev, conv_w, alog_row, dtb_row, nrm_row,
                s_gla, s_dn, row0):
    b = gla_new.shape[0]
    per_b = lambda n: pl.BlockSpec((None, 1, n), lambda bi: (bi, 0, 0))
    one = lambda a: pl.BlockSpec(a.shape, lambda bi: (0,) * a.ndim)
    sg = lambda r0: pl.BlockSpec((None, GLA_H, GLA_DK, GLA_DV), lambda bi: (r0 + bi, 0, 0, 0))
    sd = lambda r0: pl.BlockSpec((None, DN_H, DN_DK, DN_DV), lambda bi: (r0 + bi, 0, 0, 0))
    return pl.pallas_call(
        _rec_dec_kernel,
        grid=(b,),
        in_specs=[per_b(gla_new.shape[-1]), per_b(dn_new.shape[-1]), per_b(LANES), one(w2p), one(ba_row), one(gn_row),
                  pl.BlockSpec((None, DN_CONV - 1, DN_QKV), lambda bi: (row0 + bi, 0, 0)), one(conv_w), one(alog_row),
                  one(dtb_row), one(nrm_row), sg(row0), sd(row0)],
        out_specs=[per_b(MIX_W), per_b(MIX_W), sg(0), sd(0)],
        out_shape=[jax.ShapeDtypeStruct((b, 1, MIX_W), F32)] * 2
        + [jax.ShapeDtypeStruct((b,) + s_gla.shape[1:], F32), jax.ShapeDtypeStruct((b,) + s_dn.shape[1:], F32)],
        compiler_params=_cp("parallel"),
        name="rec_decode",
    )(gla_new, dn_new, small_new, w2p, ba_row, gn_row, conv_prev, conv_w, alog_row, dtb_row, nrm_row, s_gla, s_dn)


def _heads_to_rows(q_row):
    rows = [q_row[:, h * NSA_HD:(h + 1) * NSA_HD] for h in range(NSA_H)]
    return jnp.concatenate(rows + [jnp.zeros((8 - NSA_H, NSA_HD), F32)], axis=0)


CMP_PAGES = 16


def _nsa_dec_cmp_kernel(pt_ref, q_ref, *refs, nbp):
    page_refs, (kc_ref, vc_ref, oc_ref, sel_ref) = refs[:CMP_PAGES], refs[CMP_PAGES:]
    j = pl.program_id(1)
    per = PAGE_SIZE // NSA_BLOCK
    rows = CMP_PAGES * per
    dst = pl.ds(pl.multiple_of(j * rows, rows), rows)
    for kv, ref in enumerate((kc_ref, vc_ref)):
        means = [jnp.mean(r[pl.ds(kv, PAGE_SIZE, stride=2), :].reshape(per, NSA_BLOCK, NSA_HD), axis=1)
                 for r in page_refs]
        ref[dst, :] = jnp.concatenate(means, axis=0)

    @pl.when(j == pl.num_programs(1) - 1)
    def _():
        q8 = _heads_to_rows(q_ref[...])
        kc, vc = kc_ref[...], vc_ref[...]
        s = lax.dot_general(q8, kc, NT, precision=HI, preferred_element_type=F32)
        p = _masked_softmax(s, jnp.full(s.shape, True))
        oc_ref[...] = jnp.dot(p, vc, precision=HI, preferred_element_type=F32)
        imp = jnp.sum(p[0:NSA_H], axis=0, keepdims=True)
        r_i, c_i = _iota((nbp, nbp), 0), _iota((nbp, nbp), 1)
        imp_col = _col(imp, (r_i == c_i).astype(F32))
        beats = (imp_col > imp) | ((imp_col == imp) & (r_i < c_i))
        rank = jnp.sum(beats.astype(jnp.int32), axis=0, keepdims=True)
        ids = jnp.where(rank == _iota((16, nbp), 0), _iota((16, nbp), 1), 0)
        sel_ref[...] = jnp.broadcast_to(jnp.sum(ids, axis=1, keepdims=True), (16, LANES))


def _nsa_dec_cmp(pt_flat, n_pages, page0, q_new, cmp_cache):
    b = q_new.shape[0]
    nbp = n_pages * (PAGE_SIZE // NSA_BLOCK)
    n_sel = NSA_TOPK - 1
    assert nbp >= n_sel and NSA_TOPK <= 16 and n_pages % CMP_PAGES == 0
    page = lambda r: pl.BlockSpec((None, 2 * PAGE_SIZE, NSA_HD),
                                  lambda bi, j, pt: (page0 + pt[bi * n_pages + j * CMP_PAGES + r], 0, 0))
    mean_spec = pl.BlockSpec((None, nbp, NSA_HD), lambda bi, j, pt: (bi, 0, 0))
    return pl.pallas_call(
        functools.partial(_nsa_dec_cmp_kernel, nbp=nbp),
        grid_spec=pltpu.PrefetchScalarGridSpec(
            num_scalar_prefetch=1,
            grid=(b, n_pages // CMP_PAGES),
            in_specs=[pl.BlockSpec((None, 1, MIX_W), lambda bi, j, pt: (bi, 0, 0))]
            + [page(r) for r in range(CMP_PAGES)],
            out_specs=[mean_spec, mean_spec,
                       pl.BlockSpec((None, 8, NSA_HD), lambda bi, j, pt: (bi, 0, 0)),
                       pl.BlockSpec((None, 16, LANES), lambda bi, j, pt: (bi, 0, 0))],
        ),
        out_shape=[jax.ShapeDtypeStruct((b, nbp, NSA_HD), F32)] * 2
        + [jax.ShapeDtypeStruct((b, 8, NSA_HD), F32), jax.ShapeDtypeStruct((b, 16, LANES), jnp.int32)],
        compiler_params=_cp("parallel", "arbitrary"),
        name="nsa_dec_cmp",
    )(pt_flat, q_new, *([cmp_cache] * CMP_PAGES))


def _nsa_dec_attn_kernel(pt_ref, sel_ref, q_ref, blk_ref, snew_ref, win_ref, wnew_ref, oc_ref, sm_ref, o_ref,
                         m_sc, l_sc, acc_sc, *, sw):
    s_i = pl.program_id(1)
    q8 = _heads_to_rows(q_ref[...])

    @pl.when(s_i == 0)
    def _():
        m_sc[...] = jnp.sum(q8 * snew_ref[:, :NSA_HD], axis=-1, keepdims=True)
        l_sc[...] = jnp.ones_like(l_sc)
        acc_sc[...] = jnp.broadcast_to(snew_ref[:, NSA_HD:], acc_sc.shape)

    k, v = blk_ref[pl.ds(0, NSA_BLOCK, stride=2), :], blk_ref[pl.ds(1, NSA_BLOCK, stride=2), :]
    s = lax.dot_general(q8, k, NT, precision=HI, preferred_element_type=F32)
    m_prev = m_sc[...]
    m_new = jnp.maximum(m_prev, jnp.max(s, axis=-1, keepdims=True))
    alpha = jnp.exp(m_prev - m_new)
    p = jnp.exp(s - m_new)
    l_sc[...] = alpha * l_sc[...] + jnp.sum(p, axis=-1, keepdims=True)
    acc_sc[...] = alpha * acc_sc[...] + jnp.dot(p, v, precision=HI, preferred_element_type=F32)
    m_sc[...] = m_new

    @pl.when(s_i == pl.num_programs(1) - 1)
    def _():
        o_s = acc_sc[...] / l_sc[...]
        kw, vw = win_ref[pl.ds(0, sw, stride=2), :], win_ref[pl.ds(1, sw, stride=2), :]
        sw_ = lax.dot_general(q8, kw, NT, precision=HI, preferred_element_type=F32)
        wmask = (sw - _iota((8, sw), 1)) < NSA_WINDOW
        s_new = jnp.sum(q8 * wnew_ref[:, :NSA_HD], axis=-1, keepdims=True)
        m = jnp.maximum(jnp.max(jnp.where(wmask, sw_, -jnp.inf), axis=-1, keepdims=True), s_new)
        pw = jnp.where(wmask, jnp.exp(sw_ - m), 0.0)
        pn = jnp.exp(s_new - m)
        o_w = (jnp.dot(pw, vw, precision=HI, preferred_element_type=F32) + pn * wnew_ref[:, NSA_HD:]) / (
            jnp.sum(pw, axis=-1, keepdims=True) + pn)
        gate = jax.nn.sigmoid(sm_ref[...])
        o_c = oc_ref[...]
        for h in range(NSA_H):
            g0 = SM_NG + 3 * h
            o_ref[:, h * NSA_HD:(h + 1) * NSA_HD] = (
                gate[:, g0:g0 + 1] * o_c[h:h + 1] + gate[:, g0 + 1:g0 + 2] * o_s[h:h + 1]
                + gate[:, g0 + 2:g0 + 3] * o_w[h:h + 1])


def _nsa_dec_attn(pt_flat, sel_flat, n_pages, n_sel, page0, row0, q_new, slc_cache, slc_new, win_cache, win_new,
                  o_c, small_new):
    b = q_new.shape[0]
    sw = win_cache.shape[1] // 2
    per = PAGE_SIZE // NSA_BLOCK
    half = slc_cache.reshape(slc_cache.shape[0] * per, 2 * NSA_BLOCK, NSA_HD)

    def blk_map(bi, s, pt, sel):
        n = sel[bi * n_sel + s]
        return ((page0 + pt[bi * n_pages + n // per]) * per + n % per, 0, 0)

    per_b = lambda r, n: pl.BlockSpec((None, r, n), lambda bi, s, pt, sel: (bi, 0, 0))
    return pl.pallas_call(
        functools.partial(_nsa_dec_attn_kernel, sw=sw),
        grid_spec=pltpu.PrefetchScalarGridSpec(
            num_scalar_prefetch=2,
            grid=(b, n_sel),
            in_specs=[per_b(1, MIX_W), pl.BlockSpec((None, 2 * NSA_BLOCK, NSA_HD), blk_map), per_b(1, 2 * NSA_HD),
                      pl.BlockSpec((None, 2 * sw, NSA_HD), lambda bi, s, pt, sel: (row0 + bi, 0, 0)),
                      per_b(1, 2 * NSA_HD), per_b(8, NSA_HD), per_b(1, LANES)],
            out_specs=per_b(1, MIX_W),
            scratch_shapes=[pltpu.VMEM((8, 1), F32), pltpu.VMEM((8, 1), F32), pltpu.VMEM((8, NSA_HD), F32)],
        ),
        out_shape=jax.ShapeDtypeStruct((b, 1, MIX_W), F32),
        compiler_params=_cp("parallel", "arbitrary"),
        name="nsa_dec_attn",
    )(pt_flat, sel_flat, q_new, half, slc_new, win_cache, win_new, o_c, small_new)


def _lane_row(vals, at):
    return jnp.zeros((1, LANES), F32).at[0, at:at + vals.shape[0]].set(vals.astype(F32))


def _stacked_weights(w_in, w_branch, w_out, w_up, w_down):
    offs = [0]
    for s in SPLIT_SIZES:
        offs.append(offs[-1] + s)
    seg = lambda i: w_in[:, :, offs[i]:offs[i + 1]]
    (fq, fk, fv, ff, gq, gk, gv, ga, gr, nq, nkc, nks, nkw, ng, dqkv, da, dbeta, dz, mg) = [seg(i) for i in range(19)]
    cat = lambda parts: jnp.concatenate(parts, axis=2).astype(BF16)
    small = [ff, ga, ng, da, dbeta]
    pad = jnp.zeros(w_in.shape[:2] + (LANES - sum(p.shape[2] for p in small),), F32)
    return dict(fox=cat([fq, fk, fv]), gla=cat([gq, gk, gv, gr]), nsa=cat([nq, nkc, nks, nkw]), dn=cat([dqkv, dz]),
                small=cat(small + [pad]), mg=mg.astype(BF16), wb=w_branch.astype(BF16), w_out=w_out.astype(BF16),
                w_up=w_up.astype(BF16), w_down=w_down.astype(BF16))


def _layer_rows(fox_b_f, gla_w_a2, gla_b_a, gla_norm, dn_conv_w, dn_a_log, dn_dt_bias, dn_norm):
    return dict(
        bf_row=_lane_row(fox_b_f, SM_FF),
        w2p=jnp.zeros((LANES, GLA_H * GLA_DK), F32).at[SM_GA:SM_GA + GLA_RANK].set(gla_w_a2),
        ba_row=gla_b_a.reshape(1, -1), gn_row=gla_norm.reshape(1, -1), conv_w=dn_conv_w,
        alog_row=_lane_row(dn_a_log, SM_DA), dtb_row=_lane_row(dn_dt_bias, SM_DA), dnn_row=dn_norm.reshape(1, -1))


def _rope_tables(pos):
    half = ROPE_DIM // 2
    inv = ROPE_THETA ** (-jnp.arange(half, dtype=F32) / half)
    ang = pos.astype(F32)[:, None] * inv[None, :]
    cos, sin = jnp.cos(ang), jnp.sin(ang)
    t = pos.shape[0]
    z = lambda n: jnp.zeros((t, n), F32)
    return (jnp.concatenate([cos, cos, jnp.ones((t, LANES - ROPE_DIM), F32)], axis=1),
            jnp.concatenate([-sin, z(LANES - half)], axis=1),
            jnp.concatenate([z(half), sin, z(LANES - ROPE_DIM)], axis=1))


def _project(h, w, l):
    return {name: _mm(h, w[name], l) for name in ("fox", "gla", "nsa", "dn", "small", "mg")}


def _mixer_prompt(h, b, t, w, l):
    p = _project(h, w, l)
    fox, gla, nsa, dn, small = (p[n].reshape(b, t, -1) for n in ("fox", "gla", "nsa", "dn", "small"))
    lf, c, ct, fox_kv = _fox_prep(small, w["bf_row"], fox)
    o_fox = _fox_attn(fox, c, ct)
    o_gla, s_gla = _gla(gla, small, w["w2p"], w["ba_row"], w["gn_row"], jnp.zeros((b, GLA_H, GLA_DK, GLA_DV), F32))
    qr, _, slc_kv, win_kv, cmp_c, slc_c, win_c, cmean = _nsa_prep(nsa, *_rope_tables(jnp.arange(t)), True)
    o_nsa = _nsa_attn(qr, small, cmean, slc_kv, win_kv)
    qd, kd, vd, gsm, btsm = _dn_prep(dn, jnp.zeros((b, 8, DN_QKV), F32), w["conv_w"], small, w["alog_row"],
                                     w["dtb_row"])
    o_dn, s_dn = _gdn(qd, kd, vd, gsm, btsm, dn, w["dnn_row"], jnp.zeros((b, DN_H, DN_DK, DN_DV), F32))
    mix = _branch_gate([o.reshape(b * t, MIX_W) for o in (o_fox, o_gla, o_nsa, o_dn)], w["wb"], l, p["mg"])
    kv5 = lambda a: a.reshape(b, t, 2, 1, NSA_HD)
    wl = min(NSA_WINDOW, t)
    state = (fox_kv.reshape(b, t, 2, FOX_H, FOX_HD), lf[:, :, :FOX_H], kv5(cmp_c), kv5(slc_c),
             kv5(win_c)[:, t - wl:], s_gla, s_dn, dn[:, t - (DN_CONV - 1):, :DN_QKV])
    return mix, state


def _mixer_sample(h, caches, l, pt_flat, n_pages, w):
    fox_kv_c, fox_lf_c, cmp_c, slc_c, win_c, s_gla_c, s_dn_c, conv_c = caches
    b = h.shape[0]
    past_len = n_pages * PAGE_SIZE
    p = _project(h, w, l)
    fox, gla, dn, small = (p[n].reshape(b, 1, -1) for n in ("fox", "gla", "dn", "small"))
    depth, n_pool = fox_kv_c.shape[:2]
    page0, row0 = l * n_pool, l * b
    o_fox, lf_new = _fox_decode(pt_flat, n_pages, page0, fox, small, w["bf_row"],
                                fox_kv_c.reshape(depth * n_pool, PAGE_SIZE * 2 * FOX_H, FOX_HD),
                                jnp.swapaxes(fox_lf_c, 2, 3).reshape(depth * n_pool, FOX_H, PAGE_SIZE))
    o_gla, o_dn, s_gla, s_dn = _rec_decode(
        gla, dn, small, w["w2p"], w["ba_row"], w["gn_row"], conv_c.reshape((depth * b,) + conv_c.shape[2:]),
        w["conv_w"], w["alog_row"], w["dtb_row"], w["dnn_row"], s_gla_c.reshape((depth * b,) + s_gla_c.shape[2:]),
        s_dn_c.reshape((depth * b,) + s_dn_c.shape[2:]), row0)
    tabs = _rope_tables(jnp.full((b,), past_len, jnp.int32))
    qr, cmp_new, slc_new, win_new = _nsa_prep(p["nsa"].reshape(1, b, -1), *tabs, False)[:4]
    qr, cmp_new, slc_new, win_new = (a.reshape(b, 1, -1) for a in (qr, cmp_new, slc_new, win_new))
    paged = lambda c: c.reshape(depth * n_pool, 2 * PAGE_SIZE, NSA_HD)
    _, _, o_c, sel = _nsa_dec_cmp(pt_flat, n_pages, page0, qr, paged(cmp_c))
    n_sel = NSA_TOPK - 1
    sel_flat = sel[:, :n_sel, 0].reshape(-1)
    sw = win_c.shape[2]
    o_nsa = _nsa_dec_attn(pt_flat, sel_flat, n_pages, n_sel, page0, row0, qr, paged(slc_c), slc_new,
                          win_c.reshape(depth * b, 2 * sw, NSA_HD), win_new, o_c, small)
    mix = _branch_gate([o.reshape(b, MIX_W).astype(BF16) for o in (o_fox, o_gla, o_nsa, o_dn)], w["wb"], l, p["mg"])
    kv5 = lambda a: a.reshape(b, 1, 2, 1, NSA_HD)
    wl = min(NSA_WINDOW, sw + 1)
    new_win = jnp.concatenate([win_c[l][:, sw + 1 - wl:], kv5(win_new)], axis=1)
    new_conv = jnp.concatenate([conv_c[l][:, 1:], dn[:, :, :DN_QKV]], axis=1)
    state = (fox[:, :, MIX_W:].reshape(b, 1, 2, FOX_H, FOX_HD), lf_new[:, :, :FOX_H], kv5(cmp_new), kv5(slc_new),
             new_win, s_gla, s_dn, new_conv)
    return mix, state


def _trunk_layer(x, h, mixer, norms, g_next, w, l):
    _, g_post_mix, g_pre_mlp, g_post_mlp = norms
    mix, state = mixer(h)
    x, h_mlp = _mm_norm_res(mix, w["w_out"], l, g_post_mix, x, g_pre_mlp)
    hid = _mm(h_mlp, w["w_up"], l, out_dtype=BF16, act="relu2")
    if g_next is None:
        return _mm_norm_res(hid, w["w_down"], l, g_post_mlp, x), None, state
    x, h_next = _mm_norm_res(hid, w["w_down"], l, g_post_mlp, x, g_next)
    return x, h_next, state


def kernel(x_prompt, x_sample, cache_fox_kv, cache_fox_logf, cache_nsa_cmp_kv, cache_nsa_slc_kv, cache_nsa_win_kv,
           state_gla, state_dn, state_dn_conv, page_table, norm_pre_mix, norm_post_mix, norm_pre_mlp, norm_post_mlp,
           w_in, fox_b_f, gla_w_a2, gla_b_a, gla_norm, dn_conv_w, dn_a_log, dn_dt_bias, dn_norm, w_branch, w_out,
           w_up, w_down):
    bp, tp, _ = x_prompt.shape
    bs, ts, _ = x_sample.shape
    assert ts == 1
    n_pages = page_table.shape[1]
    pt_flat = page_table.reshape(-1).astype(jnp.int32)
    y_p = x_prompt.reshape(bp * tp, D_MODEL)
    y_s = x_sample.reshape(bs, D_MODEL)
    new_p = [[] for _ in range(8)]
    new_s = [[] for _ in range(8)]
    caches = (cache_fox_kv, cache_fox_logf, cache_nsa_cmp_kv, cache_nsa_slc_kv, cache_nsa_win_kv, state_gla, state_dn,
              state_dn_conv)
    big = _stacked_weights(w_in, w_branch, w_out, w_up, w_down)
    h_p = _rmsnorm_cast(y_p, norm_pre_mix[0])
    h_s = _rmsnorm_cast(y_s, norm_pre_mix[0])
    for l in range(DEPTH):
        w = dict(big, **_layer_rows(fox_b_f[l], gla_w_a2[l], gla_b_a[l], gla_norm[l], dn_conv_w[l], dn_a_log[l],
                                    dn_dt_bias[l], dn_norm[l]))
        norms = (norm_pre_mix[l], norm_post_mix[l], norm_pre_mlp[l], norm_post_mlp[l])
        g_next = norm_pre_mix[l + 1] if l + 1 < DEPTH else None
        y_p, h_p, st_p = _trunk_layer(y_p, h_p, lambda h: _mixer_prompt(h, bp, tp, w, l), norms, g_next, w, l)
        y_s, h_s, st_s = _trunk_layer(y_s, h_s, lambda h: _mixer_sample(h, caches, l, pt_flat, n_pages, w), norms,
                                      g_next, w, l)
        for i in range(8):
            new_p[i].append(st_p[i])
            new_s[i].append(st_s[i])
    fox_kv_p, fox_logf_p, cmp_kv_p, slc_kv_p, win_kv_p, gla_p, dn_p, conv_p = [jnp.stack(a) for a in new_p]
    fox_kv_s, fox_logf_s, cmp_kv_s, slc_kv_s, win_kv_s, gla_s, dn_s, conv_s = [jnp.stack(a) for a in new_s]
    return (y_p.reshape(bp, tp, D_MODEL), y_s.reshape(bs, ts, D_MODEL), fox_kv_p, fox_kv_s, fox_logf_p, fox_logf_s,
            cmp_kv_p, cmp_kv_s, slc_kv_p, slc_kv_s, win_kv_p, win_kv_s, gla_p, gla_s, dn_p, dn_s, conv_p, conv_s)
```

```python
import functools

import jax
import jax.numpy as jnp
from jax import lax
from jax.experimental import pallas as pl
from jax.experimental.pallas import tpu as pltpu

F32 = jnp.float32
BF16 = jnp.bfloat16
HI = lax.Precision.HIGHEST
NT = (((1,), (1,)), ((), ()))
TN = (((0,), (0,)), ((), ()))
NEG = -1e30

D_MODEL = 2048
DEPTH = 2
PAGE_SIZE = 128
N_BRANCH = 4
MIX_W = D_MODEL // 4
FOX_H = 4
FOX_HD = MIX_W // FOX_H
GLA_H = 4
GLA_DK = MIX_W // (2 * GLA_H)
GLA_DV = MIX_W // GLA_H
GLA_RANK = 16
GLA_TAU = 16.0
NSA_H = 4
NSA_HD = MIX_W // NSA_H
NSA_BLOCK = 64
NSA_TOPK = 16
NSA_WINDOW = 512
DN_H = 4
DN_DK = MIX_W // DN_H
DN_DV = MIX_W // DN_H
DN_CONV = 4
DN_QKV = DN_H * (2 * DN_DK + DN_DV)
ROPE_DIM = NSA_HD // 4
ROPE_THETA = 500000.0
D_FF = 4 * D_MODEL
CHUNK = 64
SUB = 16
RMS_EPS = 1e-6
LANES = 128
SPLIT_SIZES = (
    FOX_H * FOX_HD, FOX_H * FOX_HD, FOX_H * FOX_HD, FOX_H,
    GLA_H * GLA_DK, GLA_H * GLA_DK, GLA_H * GLA_DV, GLA_RANK, GLA_H * GLA_DV,
    NSA_H * NSA_HD, 2 * NSA_HD, 2 * NSA_HD, 2 * NSA_HD, 3 * NSA_H,
    DN_QKV, DN_H, DN_H, DN_H * DN_DV,
    N_BRANCH * D_MODEL,
)
SM_FF, SM_GA, SM_NG, SM_DA, SM_DB = 0, 4, 20, 32, 36
VMEM_LIMIT = 56 * 1024 * 1024


def _cp(*sem):
    return pltpu.CompilerParams(dimension_semantics=sem, vmem_limit_bytes=VMEM_LIMIT)


def _pick(n, cap):
    if n <= cap:
        return n
    best = None
    for t in range(LANES, cap + 1, LANES):
        if n % t == 0:
            best = t
    assert best is not None, n
    return best


def _log_sigmoid(z):
    return jnp.minimum(z, 0.0) - jnp.log1p(jnp.exp(-jnp.abs(z)))


def _iota(shape, dim):
    return lax.broadcasted_iota(jnp.int32, shape, dim)


def _dot3(a, b, dims=(((1,), (0,)), ((), ()))):
    ah, bh = a.astype(BF16), b.astype(BF16)
    al, bl = (a - ah.astype(F32)).astype(BF16), (b - bh.astype(F32)).astype(BF16)
    f = lambda x, y: lax.dot_general(x, y, dims, preferred_element_type=F32)
    return f(ah, bh) + (f(ah, bl) + f(al, bh))


def _rmsnorm_cast_kernel(x_ref, g_ref, o_ref):
    x = x_ref[...]
    y = x * lax.rsqrt(jnp.mean(x * x, axis=-1, keepdims=True) + RMS_EPS)
    o_ref[...] = (y * g_ref[...]).astype(o_ref.dtype)


def _rmsnorm_cast(x, g):
    m, d = x.shape
    tm = min(m, 512)
    return pl.pallas_call(
        _rmsnorm_cast_kernel,
        grid=(m // tm,),
        in_specs=[pl.BlockSpec((tm, d), lambda i: (i, 0)), pl.BlockSpec((1, d), lambda i: (0, 0))],
        out_specs=pl.BlockSpec((tm, d), lambda i: (i, 0)),
        out_shape=jax.ShapeDtypeStruct((m, d), BF16),
        compiler_params=_cp("parallel"),
        name="rmsnorm_cast",
    )(x, g.reshape(1, d))


def _mm_kernel(a_ref, w_ref, o_ref, *, act):
    y = jnp.dot(a_ref[...], w_ref[...], preferred_element_type=F32)
    if act == "relu2":
        y = jnp.square(jnp.maximum(y, 0.0))
    o_ref[...] = y.astype(o_ref.dtype)


def _mm(a, w, l, col0=0, n=None, out_dtype=F32, act=None):
    m, k = a.shape
    n = w.shape[2] if n is None else n
    tm = min(m, 1024)
    tn = max(t for t in range(LANES, min(n, 2048) + 1, LANES) if n % t == 0 and col0 % t == 0)
    c0 = col0 // tn
    return pl.pallas_call(
        functools.partial(_mm_kernel, act=act),
        grid=(n // tn, m // tm),
        in_specs=[pl.BlockSpec((tm, k), lambda j, i: (i, 0)), pl.BlockSpec((None, k, tn), lambda j, i: (l, 0, c0 + j))],
        out_specs=pl.BlockSpec((tm, tn), lambda j, i: (i, j)),
        out_shape=jax.ShapeDtypeStruct((m, n), out_dtype),
        compiler_params=_cp("parallel", "parallel"),
        name="mm",
    )(a, w)


def _mm_norm_res_kernel(a_ref, w_ref, g_ref, x_ref, *refs):
    (gn_ref, o_ref, h_ref, acc_ref) = refs if len(refs) == 4 else (None, refs[0], None, refs[1])
    k = pl.program_id(1)

    @pl.when(k == 0)
    def _():
        acc_ref[...] = jnp.zeros_like(acc_ref)

    acc_ref[...] += jnp.dot(a_ref[...], w_ref[...], preferred_element_type=F32)

    @pl.when(k == pl.num_programs(1) - 1)
    def _():
        y = acc_ref[...]
        y = y * lax.rsqrt(jnp.mean(y * y, axis=-1, keepdims=True) + RMS_EPS)
        x = x_ref[...] + y * g_ref[...]
        o_ref[...] = x
        if h_ref is not None:
            h = x * lax.rsqrt(jnp.mean(x * x, axis=-1, keepdims=True) + RMS_EPS)
            h_ref[...] = (h * gn_ref[...]).astype(h_ref.dtype)


def _mm_norm_res(a, w, l, g, x, g_next=None):
    m, k = a.shape
    n = w.shape[2]
    tm = min(m, 512)
    tk = _pick(k, 2048)
    row = pl.BlockSpec((1, n), lambda i, kk: (0, 0))
    tile = pl.BlockSpec((tm, n), lambda i, kk: (i, 0))
    fused = g_next is not None
    return pl.pallas_call(
        _mm_norm_res_kernel,
        grid=(m // tm, k // tk),
        in_specs=[pl.BlockSpec((tm, tk), lambda i, kk: (i, kk)), pl.BlockSpec((None, tk, n), lambda i, kk: (l, kk, 0)),
                  row, tile] + ([row] if fused else []),
        out_specs=[tile, tile] if fused else tile,
        out_shape=[jax.ShapeDtypeStruct((m, n), F32), jax.ShapeDtypeStruct((m, n), BF16)] if fused
        else jax.ShapeDtypeStruct((m, n), F32),
        scratch_shapes=[pltpu.VMEM((tm, n), F32)],
        compiler_params=_cp("parallel", "arbitrary"),
        name="mm_norm_res",
    )(a, w, g.reshape(1, n), x, *([g_next.reshape(1, n)] if fused else []))


def _branch_gate_kernel(o0, o1, o2, o3, wb_ref, m0, m1, m2, m3, out_ref):
    acc = None
    for n, (o_n, m_n) in enumerate(((o0, m0), (o1, m1), (o2, m2), (o3, m3))):
        y = jnp.dot(o_n[...], wb_ref[n], preferred_element_type=F32)
        t = jax.nn.sigmoid(m_n[...]) * y
        acc = t if acc is None else acc + t
    out_ref[...] = acc.astype(out_ref.dtype)


def _branch_gate(branches, wb, l, mg):
    m = mg.shape[0]
    tm = min(m, 512)
    tn = 1024
    nj = D_MODEL // tn
    o_spec = pl.BlockSpec((tm, MIX_W), lambda j, i: (i, 0))
    m_specs = [pl.BlockSpec((tm, tn), functools.partial(lambda j, i, n: (i, n * nj + j), n=n)) for n in range(N_BRANCH)]
    return pl.pallas_call(
        _branch_gate_kernel,
        grid=(nj, m // tm),
        in_specs=[o_spec] * 4 + [pl.BlockSpec((None, N_BRANCH, MIX_W, tn), lambda j, i: (l, 0, 0, j))] + m_specs,
        out_specs=pl.BlockSpec((tm, tn), lambda j, i: (i, j)),
        out_shape=jax.ShapeDtypeStruct((m, D_MODEL), BF16),
        compiler_params=_cp("parallel", "parallel"),
        name="branch_gate",
    )(*branches, wb, mg, mg, mg, mg)


def _fox_prep_kernel(s_ref, b_ref, k_ref, v_ref, lf_ref, c_ref, ct_ref, kv_ref, carry_ref, *, tb):
    @pl.when(pl.program_id(1) == 0)
    def _():
        carry_ref[...] = jnp.zeros_like(carry_ref)

    for i, ref in enumerate((k_ref, v_ref)):
        for h in range(FOX_H):
            kv_ref[pl.ds(i * FOX_H + h, tb, stride=2 * FOX_H), :] = ref[:, h * FOX_HD:(h + 1) * FOX_HD]

    lf = _log_sigmoid(s_ref[...] + b_ref[...])
    tri = (_iota((tb, tb), 1) <= _iota((tb, tb), 0)).astype(F32)
    c = jnp.dot(tri, lf, precision=HI, preferred_element_type=F32) + carry_ref[...]
    lf_ref[...] = lf
    c_ref[...] = c
    ct_ref[...] = c.T[:8]
    carry_ref[...] = c[tb - 1:tb]


def _fox_prep(small, bias_row, fox):
    b, t, _ = small.shape
    tb = min(t, 256)
    blk = pl.BlockSpec((None, tb, LANES), lambda bi, i: (bi, i, 0))
    rows = 2 * FOX_H
    return pl.pallas_call(
        functools.partial(_fox_prep_kernel, tb=tb),
        grid=(b, t // tb),
        in_specs=[blk, pl.BlockSpec((1, LANES), lambda bi, i: (0, 0)),
                  pl.BlockSpec((None, tb, MIX_W), lambda bi, i: (bi, i, 1)),
                  pl.BlockSpec((None, tb, MIX_W), lambda bi, i: (bi, i, 2))],
        out_specs=[blk, blk, pl.BlockSpec((None, 8, tb), lambda bi, i: (bi, 0, i)),
                   pl.BlockSpec((None, tb * rows, FOX_HD), lambda bi, i: (bi, i, 0))],
        out_shape=[jax.ShapeDtypeStruct((b, t, LANES), F32)] * 2 + [jax.ShapeDtypeStruct((b, 8, t), F32),
                                                                  jax.ShapeDtypeStruct((b, t * rows, FOX_HD), F32)],
        scratch_shapes=[pltpu.VMEM((1, LANES), F32)],
        compiler_params=_cp("parallel", "arbitrary"),
        name="fox_prep",
    )(small, bias_row, fox, fox)


def _fox_attn_kernel(q_ref, k_ref, v_ref, cq_ref, ck_ref, o_ref, m_sc, l_sc, acc_sc, *, tq, tk):
    qi = pl.program_id(1)
    ki = pl.program_id(2)

    @pl.when(ki == 0)
    def _():
        m_sc[...] = jnp.full_like(m_sc, NEG)
        l_sc[...] = jnp.zeros_like(l_sc)
        acc_sc[...] = jnp.zeros_like(acc_sc)

    def step(diagonal):
        mask = _iota((tk, tq), 0) <= _iota((tk, tq), 1)
        heads = range(FOX_H)
        hsl = [slice(h * FOX_HD, (h + 1) * FOX_HD) for h in heads]
        ss = []
        for h in heads:
            q = (q_ref[:, hsl[h]] * (FOX_HD ** -0.5)).astype(BF16)
            s = lax.dot_general(k_ref[:, hsl[h]].astype(BF16), q, NT, preferred_element_type=F32)
            s = s + cq_ref[h:h + 1, :] - ck_ref[:, h:h + 1]
            ss.append(jnp.where(mask, s, NEG) if diagonal else s)
        m_prev = [m_sc[h] for h in heads]
        m_new = [jnp.maximum(m_prev[h], jnp.max(ss[h], axis=0, keepdims=True)) for h in heads]
        ps = []
        for h in heads:
            p = jnp.exp(ss[h] - m_new[h])
            ps.append(jnp.where(mask, p, 0.0) if diagonal else p)
        pvs = [jnp.dot(v_ref[:, hsl[h]].T.astype(BF16), ps[h].astype(BF16), preferred_element_type=F32)
               for h in heads]
        for h in heads:
            alpha = jnp.exp(m_prev[h] - m_new[h])
            l_sc[h] = alpha * l_sc[h] + jnp.sum(ps[h], axis=0, keepdims=True)
            acc_sc[h] = alpha * acc_sc[h] + pvs[h]
            m_sc[h] = m_new[h]

    pl.when(ki < qi)(functools.partial(step, False))
    pl.when(ki == qi)(functools.partial(step, True))

    @pl.when(ki == pl.num_programs(2) - 1)
    def _():
        for h in range(FOX_H):
            o_ref[:, h * FOX_HD:(h + 1) * FOX_HD] = (acc_sc[h] / l_sc[h]).T.astype(o_ref.dtype)


def _fox_attn(fox, c, ct):
    b, t, _ = fox.shape
    tq = tk = min(t, 512)

    def kmap(col):
        return lambda bi, qi, ki: (bi, jnp.minimum(ki, (qi * tq + tq - 1) // tk), col)

    return pl.pallas_call(
        functools.partial(_fox_attn_kernel, tq=tq, tk=tk),
        grid=(b, t // tq, t // tk),
        in_specs=[
            pl.BlockSpec((None, tq, MIX_W), lambda bi, qi, ki: (bi, qi, 0)),
            pl.BlockSpec((None, tk, MIX_W), kmap(1)),
            pl.BlockSpec((None, tk, MIX_W), kmap(2)),
            pl.BlockSpec((None, 8, tq), lambda bi, qi, ki: (bi, 0, qi)),
            pl.BlockSpec((None, tk, LANES), kmap(0)),
        ],
        out_specs=pl.BlockSpec((None, tq, MIX_W), lambda bi, qi, ki: (bi, qi, 0)),
        out_shape=jax.ShapeDtypeStruct((b, t, MIX_W), BF16),
        scratch_shapes=[pltpu.VMEM((FOX_H, 1, tq), F32), pltpu.VMEM((FOX_H, 1, tq), F32),
                        pltpu.VMEM((FOX_H, FOX_HD, tq), F32)],
        compiler_params=_cp("parallel", "parallel", "arbitrary"),
        name="fox_attn",
    )(fox, fox, fox, ct, c)


def _gla_kernel(g_ref, sm_ref, w2_ref, ba_ref, gn_ref, s0_ref, o_ref, sout_ref, s_sc, *, C):
    c = pl.program_id(1)

    @pl.when(c == 0)
    def _():
        s_sc[...] = s0_ref[...]

    lane = _iota((C, LANES), 1)
    ga = jnp.where((lane >= SM_GA) & (lane < SM_GA + GLA_RANK), sm_ref[...], 0.0)
    pre = jnp.dot(ga, w2_ref[...], precision=HI, preferred_element_type=F32) + ba_ref[...]
    loga = _log_sigmoid(pre) * (1.0 / GLA_TAU)
    tri = (_iota((C, C), 1) <= _iota((C, C), 0)).astype(F32)
    b_all = jnp.dot(tri, loga, precision=HI, preferred_element_type=F32)
    eye = (_iota((GLA_DK, GLA_DK), 0) == _iota((GLA_DK, GLA_DK), 1)).astype(F32)
    heads, blocks = range(GLA_H), range(C // SUB)
    kofs, vofs = GLA_H * GLA_DK, 2 * GLA_H * GLA_DK
    qs = [g_ref[:, h * GLA_DK:(h + 1) * GLA_DK] * (GLA_DK ** -0.5) for h in heads]
    ks = [g_ref[:, kofs + h * GLA_DK:kofs + (h + 1) * GLA_DK] for h in heads]
    vbs = [g_ref[:, vofs + h * GLA_DV:vofs + (h + 1) * GLA_DV].astype(BF16) for h in heads]
    bhs = [b_all[:, h * GLA_DK:(h + 1) * GLA_DK] for h in heads]
    states = [s_sc[h] for h in heads]
    o_inter = [jnp.dot((qs[h] * jnp.exp(bhs[h])).astype(BF16), states[h].astype(BF16), preferred_element_type=F32)
               for h in heads]
    atts = {}
    for h in heads:
        for ib in blocks[1:]:
            a0 = ib * SUB
            bi = bhs[h][a0:a0 + SUB]
            r = bi[0:1]
            qe = (qs[h][a0:a0 + SUB] * jnp.exp(bi - r)).astype(BF16)
            ke = (ks[h][:a0] * jnp.exp(r - bhs[h][:a0])).astype(BF16)
            atts[h, ib] = lax.dot_general(qe, ke, NT, preferred_element_type=F32)
    pair = 2 * GLA_DK
    mask3 = _iota((SUB, SUB, pair), 1) <= _iota((SUB, SUB, pair), 0)
    low = _iota((SUB, SUB, pair), 2) < GLA_DK
    ds = {}
    for hp in range(GLA_H // 2):
        q2 = g_ref[:, hp * pair:(hp + 1) * pair] * (GLA_DK ** -0.5)
        k2 = g_ref[:, kofs + hp * pair:kofs + (hp + 1) * pair]
        b2 = b_all[:, hp * pair:(hp + 1) * pair]
        for ib in blocks:
            rs = slice(ib * SUB, (ib + 1) * SUB)
            bi = b2[rs]
            diff = bi[:, None, :] - bi[None, :, :]
            e = jnp.where(mask3, jnp.exp(jnp.where(mask3, diff, 0.0)), 0.0)
            prod = q2[rs][:, None, :] * k2[rs][None, :, :] * e
            ds[2 * hp, ib] = jnp.sum(jnp.where(low, prod, 0.0), axis=-1)
            ds[2 * hp + 1, ib] = jnp.sum(jnp.where(low, 0.0, prod), axis=-1)
    rows = {}
    for h in heads:
        for ib in blocks:
            a0 = ib * SUB
            o_i = jnp.dot(ds[h, ib].astype(BF16), vbs[h][a0:a0 + SUB], preferred_element_type=F32)
            if ib > 0:
                o_i = o_i + jnp.dot(atts[h, ib].astype(BF16), vbs[h][:a0], preferred_element_type=F32)
            rows[h, ib] = o_i
    for h in heads:
        bend = bhs[h][C - 1:C]
        kdec = (ks[h] * jnp.exp(bend - bhs[h])).astype(BF16)
        dcol = jnp.sum(eye * jnp.exp(bend), axis=1, keepdims=True)
        s_sc[h] = dcol * states[h] + lax.dot_general(kdec, vbs[h], TN, preferred_element_type=F32)
    for h in heads:
        o = jnp.concatenate([rows[h, ib] for ib in blocks], axis=0) + o_inter[h]
        y = o * lax.rsqrt(jnp.mean(o * o, axis=-1, keepdims=True) + RMS_EPS) * gn_ref[...]
        gr = g_ref[:, vofs + GLA_H * GLA_DV + h * GLA_DV:vofs + GLA_H * GLA_DV + (h + 1) * GLA_DV]
        o_ref[:, h * GLA_DV:(h + 1) * GLA_DV] = (y * jax.nn.silu(gr)).astype(o_ref.dtype)

    @pl.when(c == pl.num_programs(1) - 1)
    def _():
        sout_ref[...] = s_sc[...]


def _gla(gla, small, w2p, ba_row, gn_row, s0):
    b, t, _ = gla.shape
    C = CHUNK
    n_gla = gla.shape[-1]
    return pl.pallas_call(
        functools.partial(_gla_kernel, C=C),
        grid=(b, t // C),
        in_specs=[
            pl.BlockSpec((None, C, n_gla), lambda bi, c: (bi, c, 0)),
            pl.BlockSpec((None, C, LANES), lambda bi, c: (bi, c, 0)),
            pl.BlockSpec(w2p.shape, lambda bi, c: (0, 0)),
            pl.BlockSpec(ba_row.shape, lambda bi, c: (0, 0)),
            pl.BlockSpec(gn_row.shape, lambda bi, c: (0, 0)),
            pl.BlockSpec((None, GLA_H, GLA_DK, GLA_DV), lambda bi, c: (bi, 0, 0, 0)),
        ],
        out_specs=[
            pl.BlockSpec((None, C, MIX_W), lambda bi, c: (bi, c, 0)),
            pl.BlockSpec((None, GLA_H, GLA_DK, GLA_DV), lambda bi, c: (bi, 0, 0, 0)),
        ],
        out_shape=[jax.ShapeDtypeStruct((b, t, MIX_W), BF16), jax.ShapeDtypeStruct(s0.shape, F32)],
        scratch_shapes=[pltpu.VMEM((GLA_H, GLA_DK, GLA_DV), F32)],
        compiler_params=_cp("parallel", "arbitrary"),
        name="gla",
    )(gla, small, w2p, ba_row, gn_row, s0)


def _rope(x, cos, sa, sb):
    half = ROPE_DIM // 2
    return x * cos + pltpu.roll(x, LANES - half, 1) * sa + pltpu.roll(x, half, 1) * sb


def _nsa_prep_kernel(x_ref, cos_ref, sa_ref, sb_ref, q_ref, cmp_ref, slc_ref, win_ref, cmp_c, slc_c, win_c,
                     *mean_ref, tb):
    cos, sa, sb = cos_ref[...], sa_ref[...], sb_ref[...]
    for h in range(NSA_H):
        hs = slice(h * NSA_HD, (h + 1) * NSA_HD)
        q_ref[:, hs] = _rope(x_ref[:, hs], cos, sa, sb) * (NSA_HD ** -0.5)
    base = NSA_H * NSA_HD
    for i, (ref, cache_ref) in enumerate(((cmp_ref, cmp_c), (slc_ref, slc_c), (win_ref, win_c))):
        k0 = base + i * 2 * NSA_HD
        k = _rope(x_ref[:, k0:k0 + NSA_HD], cos, sa, sb)
        v = x_ref[:, k0 + NSA_HD:k0 + 2 * NSA_HD]
        ref[:, :NSA_HD] = k
        ref[:, NSA_HD:] = v
        cache_ref[pl.ds(0, tb, stride=2), :] = k
        cache_ref[pl.ds(1, tb, stride=2), :] = v
    if mean_ref:
        kv = cmp_ref[...]
        mean_ref[0][...] = jnp.mean(kv.reshape(tb // NSA_BLOCK, NSA_BLOCK, 2 * NSA_HD), axis=1)


def _nsa_prep(nsa, cos, sa, sb, with_means):
    b, t, n = nsa.shape
    tb = min(t, 512)
    tab = pl.BlockSpec((tb, LANES), lambda bi, i: (i, 0))
    kv = pl.BlockSpec((None, tb, 2 * NSA_HD), lambda bi, i: (bi, i, 0))
    kvc = pl.BlockSpec((None, 2 * tb, NSA_HD), lambda bi, i: (bi, i, 0))
    out_specs = [pl.BlockSpec((None, tb, MIX_W), lambda bi, i: (bi, i, 0)), kv, kv, kv, kvc, kvc, kvc]
    out_shape = ([jax.ShapeDtypeStruct((b, t, MIX_W), F32)] + [jax.ShapeDtypeStruct((b, t, 2 * NSA_HD), F32)] * 3
                 + [jax.ShapeDtypeStruct((b, 2 * t, NSA_HD), F32)] * 3)
    if with_means:
        out_specs.append(pl.BlockSpec((None, tb // NSA_BLOCK, 2 * NSA_HD), lambda bi, i: (bi, i, 0)))
        out_shape.append(jax.ShapeDtypeStruct((b, t // NSA_BLOCK, 2 * NSA_HD), F32))
    return pl.pallas_call(
        functools.partial(_nsa_prep_kernel, tb=tb),
        grid=(b, t // tb),
        in_specs=[pl.BlockSpec((None, tb, n), lambda bi, i: (bi, i, 0)), tab, tab, tab],
        out_specs=out_specs,
        out_shape=out_shape,
        compiler_params=_cp("parallel", "parallel"),
        name="nsa_prep",
    )(nsa, cos, sa, sb)


def _masked_softmax(s, mask):
    s = jnp.where(mask, s, -jnp.inf)
    m = jnp.max(s, axis=-1, keepdims=True)
    m = jnp.where(m == -jnp.inf, 0.0, m)
    p = jnp.where(mask, jnp.exp(s - m), 0.0)
    return p / jnp.maximum(jnp.sum(p, axis=-1, keepdims=True), 1e-30)


def _topk_mask_t(score_t, blk_t, n_sel):
    rank = jnp.zeros(score_t.shape, jnp.int32)
    for m in range(score_t.shape[0]):
        sm = score_t[m:m + 1, :]
        beats = (sm > score_t) | ((sm == score_t) & (blk_t > m))
        rank = rank + beats.astype(jnp.int32)
    return rank < n_sel


def _masked_scores(qs, k, mask):
    bias = jnp.where(mask, 0.0, NEG)
    return [lax.dot_general(k, q, NT, preferred_element_type=F32) + bias for q in qs]


def _softmax_step(carry, ss, v_t):
    ms, ls, accs = carry
    n = range(len(ss))
    m_new = [jnp.maximum(ms[g], jnp.max(ss[g], axis=0, keepdims=True)) for g in n]
    ps = [jnp.exp(ss[g] - m_new[g]) for g in n]
    pvs = [jnp.dot(v_t, ps[g].astype(BF16), preferred_element_type=F32) for g in n]
    alphas = [jnp.exp(ms[g] - m_new[g]) for g in n]
    ls = [alphas[g] * ls[g] + jnp.sum(ps[g], axis=0, keepdims=True) for g in n]
    accs = [alphas[g] * accs[g] + pvs[g] for g in n]
    return m_new, ls, accs


def _nsa_attn_kernel(q_ref, sm_ref, cm_ref, slc_ref, win_ref, o_ref, *, tq, kt, nb):
    qi = pl.program_id(1)
    qpos = qi * tq + _iota((tq, 1), 0)
    blk = _iota((tq, nb), 1)
    cmask = (blk + 1) * NSA_BLOCK <= qpos + 1
    kc, vc = cm_ref[:, :NSA_HD], cm_ref[:, NSA_HD:]
    o_c = []
    imp = jnp.zeros((tq, nb), F32)
    for h in range(NSA_H):
        s = lax.dot_general(q_ref[:, h * NSA_HD:(h + 1) * NSA_HD], kc, NT, precision=HI, preferred_element_type=F32)
        p = _masked_softmax(s, cmask)
        imp = imp + p
        o_c.append(jnp.dot(p, vc, precision=HI, preferred_element_type=F32))
    blk_t = _iota((nb, tq), 0)
    qpos_t = qi * tq + _iota((nb, tq), 1)
    cur_t = qpos_t // NSA_BLOCK
    score_t = jnp.where(blk_t == cur_t, jnp.inf,
                        jnp.where((blk_t + 1) * NSA_BLOCK <= qpos_t + 1, imp.T, -jnp.inf))
    sel_b = (_topk_mask_t(score_t, blk_t, min(NSA_TOPK, nb)) & (blk_t <= cur_t)).astype(BF16)
    qs = [q_ref[:, h * NSA_HD:(h + 1) * NSA_HD].astype(BF16) for h in range(NSA_H)]
    krow = _iota((kt, tq), 0)
    qpos_row = qi * tq + _iota((1, tq), 1)
    blk_of_col = _iota((nb, kt), 1) // NSA_BLOCK - _iota((nb, kt), 0)
    last = (qi * tq + tq - 1) // kt

    def init():
        return ([jnp.full((1, tq), NEG, F32)] * NSA_H, [jnp.zeros((1, tq), F32)] * NSA_H,
                [jnp.zeros((NSA_HD, tq), F32)] * NSA_H)

    def tile_rows(kb):
        return pl.ds(pl.multiple_of(kb * kt, kt), kt)

    def slc_scores(kb):
        expand = (blk_of_col + kb * (kt // NSA_BLOCK) == 0).astype(BF16)
        tok = lax.dot_general(expand, sel_b, TN, preferred_element_type=F32)
        mask = (tok > 0.5) & (kb * kt + krow <= qpos_row)
        return _masked_scores(qs, slc_ref[tile_rows(kb), :NSA_HD].astype(BF16), mask)

    def win_scores(kb):
        dist = qpos_row - (kb * kt + krow)
        mask = (dist >= 0) & (dist < NSA_WINDOW)
        return _masked_scores(qs, win_ref[tile_rows(kb), :NSA_HD].astype(BF16), mask)

    def attend(scores_of, kv_ref, first):
        def body(kb, state):
            v_t = kv_ref[tile_rows(kb), NSA_HD:].T.astype(BF16)
            return _softmax_step(state, scores_of(kb), v_t)

        _, ls, accs = lax.fori_loop(first, last + 1, body, init())
        return ls, accs

    l_s, acc_s = attend(slc_scores, slc_ref, 0)
    l_w, acc_w = attend(win_scores, win_ref, jnp.maximum(qi * tq - (NSA_WINDOW - 1), 0) // kt)
    gate = jax.nn.sigmoid(sm_ref[...])
    for h in range(NSA_H):
        g0 = SM_NG + 3 * h
        o = (gate[:, g0:g0 + 1] * o_c[h] + gate[:, g0 + 1:g0 + 2] * (acc_s[h] / l_s[h]).T
             + gate[:, g0 + 2:g0 + 3] * (acc_w[h] / l_w[h]).T)
        o_ref[:, h * NSA_HD:(h + 1) * NSA_HD] = o.astype(o_ref.dtype)


def _nsa_attn(qr, small, cmean, slc, win):
    b, t, _ = qr.shape
    tq = min(t, 512)
    kt = min(t, 512)
    nb = t // NSA_BLOCK
    whole = lambda n: pl.BlockSpec((None, n, 2 * NSA_HD), lambda bi, qi: (bi, 0, 0))
    return pl.pallas_call(
        functools.partial(_nsa_attn_kernel, tq=tq, kt=kt, nb=nb),
        grid=(b, t // tq),
        in_specs=[
            pl.BlockSpec((None, tq, MIX_W), lambda bi, qi: (bi, qi, 0)),
            pl.BlockSpec((None, tq, LANES), lambda bi, qi: (bi, qi, 0)),
            whole(nb), whole(t), whole(t),
        ],
        out_specs=pl.BlockSpec((None, tq, MIX_W), lambda bi, qi: (bi, qi, 0)),
        out_shape=jax.ShapeDtypeStruct((b, t, MIX_W), BF16),
        compiler_params=_cp("parallel", "parallel"),
        name="nsa_attn",
    )(qr, small, cmean, slc, win)


def _dn_prep_kernel(x_ref, halo_ref, prev_ref, cw_ref, sm_ref, alog_ref, dtb_ref,
                    q_ref, k_ref, v_ref, g_ref, bt_ref, *, tb):
    halo = jnp.where(pl.program_id(1) == 0, prev_ref[...], halo_ref[...])
    xcat = jnp.concatenate([halo, x_ref[...]], axis=0)
    conv = x_ref[...] * cw_ref[DN_CONV - 1:DN_CONV]
    for j in range(DN_CONV - 1):
        sh = DN_CONV - 1 - j
        conv = conv + pltpu.roll(xcat, sh, 0)[8:] * cw_ref[j:j + 1]
    u = jax.nn.silu(conv)
    for h in range(DN_H):
        hs = slice(h * DN_DK, (h + 1) * DN_DK)
        uq = u[:, hs]
        q_ref[:, hs] = uq * lax.rsqrt(jnp.sum(uq * uq, axis=-1, keepdims=True) + 1e-6) * (DN_DK ** -0.5)
        uk = u[:, DN_H * DN_DK + h * DN_DK:DN_H * DN_DK + (h + 1) * DN_DK]
        k_ref[:, hs] = uk * lax.rsqrt(jnp.sum(uk * uk, axis=-1, keepdims=True) + 1e-6)
    v_ref[...] = u[:, 2 * DN_H * DN_DK:]
    sm = sm_ref[...]
    g_ref[...] = -jnp.exp(alog_ref[...]) * jax.nn.softplus(sm + dtb_ref[...])
    bt_ref[...] = jax.nn.sigmoid(sm)


def _dn_prep(dn, prev8, conv_w, small, alog_row, dtb_row):
    b, t, _ = dn.shape
    tb = min(t, 256)
    hb = tb // 8
    row = pl.BlockSpec((None, tb, MIX_W), lambda bi, i: (bi, i, 0))
    sm = pl.BlockSpec((None, tb, LANES), lambda bi, i: (bi, i, 0))
    one = lambda shape: pl.BlockSpec(shape, lambda bi, i: (0,) * len(shape))
    return pl.pallas_call(
        functools.partial(_dn_prep_kernel, tb=tb),
        grid=(b, t // tb),
        in_specs=[
            pl.BlockSpec((None, tb, DN_QKV), lambda bi, i: (bi, i, 0)),
            pl.BlockSpec((None, 8, DN_QKV), lambda bi, i: (bi, jnp.maximum(i * hb - 1, 0), 0)),
            pl.BlockSpec((None, 8, DN_QKV), lambda bi, i: (bi, 0, 0)),
            one(conv_w.shape), sm, one(alog_row.shape), one(dtb_row.shape),
        ],
        out_specs=[row, row, row, sm, sm],
        out_shape=[jax.ShapeDtypeStruct((b, t, MIX_W), F32)] * 3 + [jax.ShapeDtypeStruct((b, t, LANES), F32)] * 2,
        compiler_params=_cp("parallel", "parallel"),
        name="dn_prep",
    )(dn, dn, prev8, conv_w, small, alog_row, dtb_row)


def _gdn_pre_kernel(q_ref, k_ref, v_ref, g_ref, bt_ref, wv_ref, wk_ref, qe_ref, kd_ref, qk_ref, gam_ref, *, C, nck):
    row, col = _iota((C, C), 0), _iota((C, C), 1)
    incl, strict = col <= row, col < row
    eye = (row == col).astype(F32)
    tri = incl.astype(F32)
    n_sq = C.bit_length() - 2
    chains = []
    for c in range(nck):
        rs = slice(c * C, (c + 1) * C)
        gam_all = jnp.dot(tri, g_ref[rs, :], precision=HI, preferred_element_type=F32)
        gam_ref[rs, :] = gam_all
        gam_t = gam_all.T
        for h in range(DN_H):
            hs = slice(h * DN_DK, (h + 1) * DN_DK)
            q, k = q_ref[rs, hs], k_ref[rs, hs]
            gcol = gam_all[:, SM_DA + h:SM_DA + h + 1]
            bcol = bt_ref[rs, SM_DB + h:SM_DB + h + 1]
            diff = gcol - gam_t[SM_DA + h:SM_DA + h + 1, :]
            dec_incl = jnp.where(incl, jnp.exp(jnp.where(incl, diff, 0.0)), 0.0)
            qk = lax.dot_general(q.astype(BF16), k.astype(BF16), NT, preferred_element_type=F32) * dec_incl
            qk_ref[rs, h * C:(h + 1) * C] = qk.astype(BF16)
            qe_ref[rs, hs] = (q * jnp.exp(gcol)).astype(BF16)
            kd_ref[rs, hs] = (k * jnp.exp(gcol[C - 1:C] - gcol)).astype(BF16)
            x = -(bcol * _dot3(k, k, NT) * jnp.where(strict, dec_incl, 0.0))
            chains.append((rs, hs, x, gcol, bcol))
    xs = [ch[2] for ch in chains]
    ps = [eye + x for x in xs]
    for _ in range(n_sq):
        xs = [_dot3(x, x) for x in xs]
        ps = [p + _dot3(p, x) for p, x in zip(ps, xs)]
    for (rs, hs, _, gcol, bcol), p in zip(chains, ps):
        k, v = k_ref[rs, hs], v_ref[rs, hs]
        rhs = jnp.concatenate([bcol * v, (bcol * jnp.exp(gcol)) * k], axis=1)
        w = _dot3(p, rhs)
        wv_ref[rs, hs] = w[:, :DN_DV]
        wk_ref[rs, hs] = w[:, DN_DV:].astype(BF16)


def _gdn_rec_kernel(wv_ref, wk_ref, qe_ref, kd_ref, qk_ref, gam_ref, z_ref, nrm_ref, s0_ref, o_ref, sout_ref, s_sc,
                    *, C, nck):
    i = pl.program_id(1)

    @pl.when(i == 0)
    def _():
        s_sc[...] = s0_ref[...]

    state = [s_sc[h] for h in range(DN_H)]
    heads = range(DN_H)
    hsl = [slice(h * DN_DK, (h + 1) * DN_DK) for h in heads]
    for c in range(nck):
        rs = slice(c * C, (c + 1) * C)
        sbs = [state[h].astype(BF16) for h in heads]
        ubs = [(wv_ref[rs, hsl[h]] - jnp.dot(wk_ref[rs, hsl[h]], sbs[h], preferred_element_type=F32)).astype(BF16)
               for h in heads]
        outs = [jnp.dot(qe_ref[rs, hsl[h]], sbs[h], preferred_element_type=F32)
                + jnp.dot(qk_ref[rs, h * C:(h + 1) * C], ubs[h], preferred_element_type=F32) for h in heads]
        gend = gam_ref[c * C + C - 1:(c + 1) * C, :]
        state = [jnp.exp(gend[:, SM_DA + h:SM_DA + h + 1]) * state[h]
                 + lax.dot_general(kd_ref[rs, hsl[h]], ubs[h], TN, preferred_element_type=F32) for h in heads]
        for h in heads:
            o = outs[h]
            y = o * lax.rsqrt(jnp.mean(o * o, axis=-1, keepdims=True) + RMS_EPS) * nrm_ref[...]
            o_ref[rs, hsl[h]] = (y * jax.nn.silu(z_ref[rs, hsl[h]])).astype(o_ref.dtype)
    for h in range(DN_H):
        s_sc[h] = state[h]

    @pl.when(i == pl.num_programs(1) - 1)
    def _():
        for h in range(DN_H):
            sout_ref[h] = state[h]


GDN_BLOCK = 256


def _gdn(qd, kd, vd, gsm, btsm, dn, nrm_row, s0):
    b, t, _ = qd.shape
    C = CHUNK
    tb = min(t, GDN_BLOCK)
    nck = tb // C
    row = pl.BlockSpec((None, tb, MIX_W), lambda bi, i: (bi, i, 0))
    sm = pl.BlockSpec((None, tb, LANES), lambda bi, i: (bi, i, 0))
    qk_spec = pl.BlockSpec((None, tb, DN_H * C), lambda bi, i: (bi, i, 0))
    st = pl.BlockSpec((None, DN_H, DN_DK, DN_DV), lambda bi, i: (bi, 0, 0, 0))
    wide = lambda dt: jax.ShapeDtypeStruct((b, t, MIX_W), dt)
    wv, wk, qe, kdc, qk, gam = pl.pallas_call(
        functools.partial(_gdn_pre_kernel, C=C, nck=nck),
        grid=(b, t // tb),
        in_specs=[row, row, row, sm, sm],
        out_specs=[row, row, row, row, qk_spec, sm],
        out_shape=[wide(F32), wide(BF16), wide(BF16), wide(BF16), jax.ShapeDtypeStruct((b, t, DN_H * C), BF16),
                   jax.ShapeDtypeStruct((b, t, LANES), F32)],
        compiler_params=_cp("parallel", "parallel"),
        name="gdn_pre",
    )(qd, kd, vd, gsm, btsm)
    return pl.pallas_call(
        functools.partial(_gdn_rec_kernel, C=C, nck=nck),
        grid=(b, t // tb),
        in_specs=[row, row, row, row, qk_spec, sm,
                  pl.BlockSpec((None, tb, MIX_W), lambda bi, i: (bi, i, DN_QKV // MIX_W)),
                  pl.BlockSpec(nrm_row.shape, lambda bi, i: (0, 0)), st],
        out_specs=[row, st],
        out_shape=[wide(BF16), jax.ShapeDtypeStruct(s0.shape, F32)],
        scratch_shapes=[pltpu.VMEM((DN_H, DN_DK, DN_DV), F32)],
        compiler_params=_cp("parallel", "arbitrary"),
        name="gdn_rec",
    )(wv, wk, qe, kdc, qk, gam, dn, nrm_row, s0)


FOX_PAGES = 8


def _fox_dec_kernel(pt_ref, fx_ref, sm_ref, b_ref, *refs):
    kv_refs, lf_refs = refs[:FOX_PAGES], refs[FOX_PAGES:2 * FOX_PAGES]
    o_ref, lfo_ref, m_sc, l_sc, acc_sc, car_sc = refs[2 * FOX_PAGES:]
    j = pl.program_id(1)
    P = PAGE_SIZE
    rows_per_tok = 2 * FOX_H
    qs = [fx_ref[:, h * FOX_HD:(h + 1) * FOX_HD] * (FOX_HD ** -0.5) for h in range(FOX_H)]

    @pl.when(j == 0)
    def _():
        lf_new = _log_sigmoid(sm_ref[...] + b_ref[...])
        lfo_ref[...] = lf_new
        on_diag = _iota((8, LANES), 1) == _iota((8, LANES), 0)
        car_sc[...] = jnp.broadcast_to(jnp.sum(jnp.where(on_diag, lf_new, 0.0), axis=1, keepdims=True), (8, LANES))
        for h in range(FOX_H):
            k_new = fx_ref[:, MIX_W + h * FOX_HD:MIX_W + (h + 1) * FOX_HD]
            m_sc[h] = jnp.sum(qs[h] * k_new, axis=-1, keepdims=True)
            l_sc[h] = jnp.ones((1, 1), F32)
            acc_sc[h] = fx_ref[:, 2 * MIX_W + h * FOX_HD:2 * MIX_W + (h + 1) * FOX_HD]

    upper = (_iota((P, P), 1) > _iota((P, P), 0)).astype(F32)
    carry = car_sc[...]
    pad = jnp.zeros((8 - FOX_H, P), F32)
    scores = []
    for kv_ref, lf_ref in zip(kv_refs, lf_refs):
        lf_t = jnp.concatenate([lf_ref[...], pad], axis=0)
        bias = (lax.dot_general(lf_t, upper, NT, precision=HI, preferred_element_type=F32) + carry).T
        scores.append([jnp.sum(kv_ref[pl.ds(h, P, stride=rows_per_tok), :] * qs[h], axis=-1, keepdims=True)
                       + bias[:, h:h + 1] for h in range(FOX_H)])
        carry = carry + jnp.sum(lf_t, axis=1, keepdims=True)
    car_sc[...] = carry
    outs = []
    for h in range(FOX_H):
        m_step = jnp.max(scores[0][h], axis=0, keepdims=True)
        for g in range(1, FOX_PAGES):
            m_step = jnp.maximum(m_step, jnp.max(scores[g][h], axis=0, keepdims=True))
        m_prev = m_sc[h]
        m_new = jnp.maximum(m_prev, m_step)
        alpha = jnp.exp(m_prev - m_new)
        l_new = alpha * l_sc[h]
        acc = alpha * acc_sc[h]
        for g, kv_ref in enumerate(kv_refs):
            p = jnp.exp(scores[g][h] - m_new)
            l_new = l_new + jnp.sum(p, axis=0, keepdims=True)
            acc = acc + jnp.sum(p * kv_ref[pl.ds(FOX_H + h, P, stride=rows_per_tok), :], axis=0, keepdims=True)
        m_sc[h], l_sc[h], acc_sc[h] = m_new, l_new, acc
        outs.append(acc / l_new)

    @pl.when(j == pl.num_programs(1) - 1)
    def _():
        for h in range(FOX_H):
            o_ref[:, h * FOX_HD:(h + 1) * FOX_HD] = outs[h]


def _fox_decode(pt_flat, n_pages, page0, fox_new, small_new, bias_row, kv_cache, lf_cache):
    b = fox_new.shape[0]
    assert n_pages % FOX_PAGES == 0

    def page(r):
        return lambda bi, j, pt: (page0 + pt[bi * n_pages + n_pages - 1 - (j * FOX_PAGES + r)], 0, 0)

    per_b = lambda n: pl.BlockSpec((None, 1, n), lambda bi, j, pt: (bi, 0, 0))
    return pl.pallas_call(
        _fox_dec_kernel,
        grid_spec=pltpu.PrefetchScalarGridSpec(
            num_scalar_prefetch=1,
            grid=(b, n_pages // FOX_PAGES),
            in_specs=[per_b(3 * MIX_W), per_b(LANES), pl.BlockSpec((1, LANES), lambda bi, j, pt: (0, 0))]
            + [pl.BlockSpec((None, PAGE_SIZE * 2 * FOX_H, FOX_HD), page(r)) for r in range(FOX_PAGES)]
            + [pl.BlockSpec((None, FOX_H, PAGE_SIZE), page(r)) for r in range(FOX_PAGES)],
            out_specs=[per_b(MIX_W), per_b(LANES)],
            scratch_shapes=[pltpu.VMEM((FOX_H, 1, 1), F32), pltpu.VMEM((FOX_H, 1, 1), F32),
                            pltpu.VMEM((FOX_H, 1, FOX_HD), F32), pltpu.VMEM((8, PAGE_SIZE), F32)],
        ),
        out_shape=[jax.ShapeDtypeStruct((b, 1, MIX_W), F32), jax.ShapeDtypeStruct((b, 1, LANES), F32)],
        compiler_params=_cp("parallel", "arbitrary"),
        name="fox_decode",
    )(pt_flat, fox_new, small_new, bias_row, *([kv_cache] * FOX_PAGES), *([lf_cache] * FOX_PAGES))


def _col(row, eye):
    return jnp.sum(eye * row, axis=1, keepdims=True)


def _rec_dec_kernel(g_ref, d_ref, sm_ref, w2_ref, ba_ref, gn_ref, cp_ref, cw_ref, alog_ref, dtb_ref, nrm_ref,
                    sg_ref, sd_ref, og_ref, od_ref, sgo_ref, sdo_ref):
    sm = sm_ref[...]
    lane = _iota((8, LANES), 1)
    ga = jnp.where((lane >= SM_GA) & (lane < SM_GA + GLA_RANK), jnp.broadcast_to(sm, (8, LANES)), 0.0)
    pre = jnp.dot(ga, w2_ref[...], precision=HI, preferred_element_type=F32)[0:1] + ba_ref[...]
    loga = _log_sigmoid(pre) * (1.0 / GLA_TAU)
    eye_k = (_iota((GLA_DK, GLA_DK), 0) == _iota((GLA_DK, GLA_DK), 1)).astype(F32)
    for h in range(GLA_H):
        q = g_ref[:, h * GLA_DK:(h + 1) * GLA_DK] * (GLA_DK ** -0.5)
        k = g_ref[:, GLA_H * GLA_DK + h * GLA_DK:GLA_H * GLA_DK + (h + 1) * GLA_DK]
        v0 = 2 * GLA_H * GLA_DK + h * GLA_DV
        v = g_ref[:, v0:v0 + GLA_DV]
        gr = g_ref[:, v0 + GLA_H * GLA_DV:v0 + GLA_H * GLA_DV + GLA_DV]
        ea = jnp.exp(loga[:, h * GLA_DK:(h + 1) * GLA_DK])
        s0 = sg_ref[h]
        o = jnp.sum(q * k, axis=-1, keepdims=True) * v + jnp.sum(_col(q * ea, eye_k) * s0, axis=0, keepdims=True)
        sgo_ref[h] = _col(ea, eye_k) * s0 + _col(k, eye_k) * v
        y = o * lax.rsqrt(jnp.mean(o * o, axis=-1, keepdims=True) + RMS_EPS) * gn_ref[...]
        og_ref[:, h * GLA_DV:(h + 1) * GLA_DV] = y * jax.nn.silu(gr)
    conv = d_ref[:, :DN_QKV] * cw_ref[DN_CONV - 1:DN_CONV]
    for j in range(DN_CONV - 1):
        conv = conv + cp_ref[j:j + 1] * cw_ref[j:j + 1]
    u = jax.nn.silu(conv)
    gall = -jnp.exp(alog_ref[...]) * jax.nn.softplus(sm + dtb_ref[...])
    ball = jax.nn.sigmoid(sm)
    eye_d = (_iota((DN_DK, DN_DK), 0) == _iota((DN_DK, DN_DK), 1)).astype(F32)
    for h in range(DN_H):
        uq = u[:, h * DN_DK:(h + 1) * DN_DK]
        q = uq * lax.rsqrt(jnp.sum(uq * uq, axis=-1, keepdims=True) + 1e-6) * (DN_DK ** -0.5)
        uk = u[:, DN_H * DN_DK + h * DN_DK:DN_H * DN_DK + (h + 1) * DN_DK]
        k = uk * lax.rsqrt(jnp.sum(uk * uk, axis=-1, keepdims=True) + 1e-6)
        v = u[:, 2 * DN_H * DN_DK + h * DN_DV:2 * DN_H * DN_DK + (h + 1) * DN_DV]
        eg = jnp.exp(gall[:, SM_DA + h:SM_DA + h + 1])
        beta = ball[:, SM_DB + h:SM_DB + h + 1]
        s0 = sd_ref[h]
        kcol = _col(k, eye_d)
        ks = jnp.sum(kcol * s0, axis=0, keepdims=True)
        qs = jnp.sum(_col(q, eye_d) * s0, axis=0, keepdims=True)
        un = beta * (v - eg * ks)
        o = eg * qs + jnp.sum(q * k, axis=-1, keepdims=True) * un
        sdo_ref[h] = eg * s0 + kcol * un
        y = o * lax.rsqrt(jnp.mean(o * o, axis=-1, keepdims=True) + RMS_EPS) * nrm_ref[...]
        z = d_ref[:, DN_QKV + h * DN_DV:DN_QKV + (h + 1) * DN_DV]
        od_ref[:, h * DN_DV:(h + 1) * DN_DV] = y * jax.nn.silu(z)


def _rec_decode(gla_new, dn_new, small_new, w2p, ba_row, gn_row, conv_prev, conv_w, alog_row, dtb_row, nrm_row,
                s_gla, s_dn, row0):
    b = gla_new.shape[0]
    per_b = lambda n: pl.BlockSpec((None, 1, n), lambda bi: (bi, 0, 0))
    one = lambda a: pl.BlockSpec(a.shape, lambda bi: (0,) * a.ndim)
    sg = lambda r0: pl.BlockSpec((None, GLA_H, GLA_DK, GLA_DV), lambda bi: (r0 + bi, 0, 0, 0))
    sd = lambda r0: pl.BlockSpec((None, DN_H, DN_DK, DN_DV), lambda bi: (r0 + bi, 0, 0, 0))
    return pl.pallas_call(
        _rec_dec_kernel,
        grid=(b,),
        in_specs=[per_b(gla_new.shape[-1]), per_b(dn_new.shape[-1]), per_b(LANES), one(w2p), one(ba_row), one(gn_row),
                  pl.BlockSpec((None, DN_CONV - 1, DN_QKV), lambda bi: (row0 + bi, 0, 0)), one(conv_w), one(alog_row),
                  one(dtb_row), one(nrm_row), sg(row0), sd(row0)],
        out_specs=[per_b(MIX_W), per_b(MIX_W), sg(0), sd(0)],
        out_shape=[jax.ShapeDtypeStruct((b, 1, MIX_W), F32)] * 2
        + [jax.ShapeDtypeStruct((b,) + s_gla.shape[1:], F32), jax.ShapeDtypeStruct((b,) + s_dn.shape[1:], F32)],
        compiler_params=_cp("parallel"),
        name="rec_decode",
    )(gla_new, dn_new, small_new, w2p, ba_row, gn_row, conv_prev, conv_w, alog_row, dtb_row, nrm_row, s_gla, s_dn)


def _heads_to_rows(q_row):
    rows = [q_row[:, h * NSA_HD:(h + 1) * NSA_HD] for h in range(NSA_H)]
    return jnp.concatenate(rows + [jnp.zeros((8 - NSA_H, NSA_HD), F32)], axis=0)


CMP_PAGES = 16


def _nsa_dec_cmp_kernel(pt_ref, q_ref, *refs, nbp):
    page_refs, (kc_ref, vc_ref, oc_ref, sel_ref) = refs[:CMP_PAGES], refs[CMP_PAGES:]
    j = pl.program_id(1)
    per = PAGE_SIZE // NSA_BLOCK
    rows = CMP_PAGES * per
    dst = pl.ds(pl.multiple_of(j * rows, rows), rows)
    for kv, ref in enumerate((kc_ref, vc_ref)):
        means = [jnp.mean(r[pl.ds(kv, PAGE_SIZE, stride=2), :].reshape(per, NSA_BLOCK, NSA_HD), axis=1)
                 for r in page_refs]
        ref[dst, :] = jnp.concatenate(means, axis=0)

    @pl.when(j == pl.num_programs(1) - 1)
    def _():
        q8 = _heads_to_rows(q_ref[...])
        kc, vc = kc_ref[...], vc_ref[...]
        s = lax.dot_general(q8, kc, NT, precision=HI, preferred_element_type=F32)
        p = _masked_softmax(s, jnp.full(s.shape, True))
        oc_ref[...] = jnp.dot(p, vc, precision=HI, preferred_element_type=F32)
        imp = jnp.sum(p[0:NSA_H], axis=0, keepdims=True)
        r_i, c_i = _iota((nbp, nbp), 0), _iota((nbp, nbp), 1)
        imp_col = _col(imp, (r_i == c_i).astype(F32))
        beats = (imp_col > imp) | ((imp_col == imp) & (r_i < c_i))
        rank = jnp.sum(beats.astype(jnp.int32), axis=0, keepdims=True)
        ids = jnp.where(rank == _iota((16, nbp), 0), _iota((16, nbp), 1), 0)
        sel_ref[...] = jnp.broadcast_to(jnp.sum(ids, axis=1, keepdims=True), (16, LANES))


def _nsa_dec_cmp(pt_flat, n_pages, page0, q_new, cmp_cache):
    b = q_new.shape[0]
    nbp = n_pages * (PAGE_SIZE // NSA_BLOCK)
    n_sel = NSA_TOPK - 1
    assert nbp >= n_sel and NSA_TOPK <= 16 and n_pages % CMP_PAGES == 0
    page = lambda r: pl.BlockSpec((None, 2 * PAGE_SIZE, NSA_HD),
                                  lambda bi, j, pt: (page0 + pt[bi * n_pages + j * CMP_PAGES + r], 0, 0))
    mean_spec = pl.BlockSpec((None, nbp, NSA_HD), lambda bi, j, pt: (bi, 0, 0))
    return pl.pallas_call(
        functools.partial(_nsa_dec_cmp_kernel, nbp=nbp),
        grid_spec=pltpu.PrefetchScalarGridSpec(
            num_scalar_prefetch=1,
            grid=(b, n_pages // CMP_PAGES),
            in_specs=[pl.BlockSpec((None, 1, MIX_W), lambda bi, j, pt: (bi, 0, 0))]
            + [page(r) for r in range(CMP_PAGES)],
            out_specs=[mean_spec, mean_spec,
                       pl.BlockSpec((None, 8, NSA_HD), lambda bi, j, pt: (bi, 0, 0)),
                       pl.BlockSpec((None, 16, LANES), lambda bi, j, pt: (bi, 0, 0))],
        ),
        out_shape=[jax.ShapeDtypeStruct((b, nbp, NSA_HD), F32)] * 2
        + [jax.ShapeDtypeStruct((b, 8, NSA_HD), F32), jax.ShapeDtypeStruct((b, 16, LANES), jnp.int32)],
        compiler_params=_cp("parallel", "arbitrary"),
        name="nsa_dec_cmp",
    )(pt_flat, q_new, *([cmp_cache] * CMP_PAGES))


def _nsa_dec_attn_kernel(pt_ref, sel_ref, q_ref, blk_ref, snew_ref, win_ref, wnew_ref, oc_ref, sm_ref, o_ref,
                         m_sc, l_sc, acc_sc, *, sw):
    s_i = pl.program_id(1)
    q8 = _heads_to_rows(q_ref[...])

    @pl.when(s_i == 0)
    def _():
        m_sc[...] = jnp.sum(q8 * snew_ref[:, :NSA_HD], axis=-1, keepdims=True)
        l_sc[...] = jnp.ones_like(l_sc)
        acc_sc[...] = jnp.broadcast_to(snew_ref[:, NSA_HD:], acc_sc.shape)

    k, v = blk_ref[pl.ds(0, NSA_BLOCK, stride=2), :], blk_ref[pl.ds(1, NSA_BLOCK, stride=2), :]
    s = lax.dot_general(q8, k, NT, precision=HI, preferred_element_type=F32)
    m_prev = m_sc[...]
    m_new = jnp.maximum(m_prev, jnp.max(s, axis=-1, keepdims=True))
    alpha = jnp.exp(m_prev - m_new)
    p = jnp.exp(s - m_new)
    l_sc[...] = alpha * l_sc[...] + jnp.sum(p, axis=-1, keepdims=True)
    acc_sc[...] = alpha * acc_sc[...] + jnp.dot(p, v, precision=HI, preferred_element_type=F32)
    m_sc[...] = m_new

    @pl.when(s_i == pl.num_programs(1) - 1)
    def _():
        o_s = acc_sc[...] / l_sc[...]
        kw, vw = win_ref[pl.ds(0, sw, stride=2), :], win_ref[pl.ds(1, sw, stride=2), :]
        sw_ = lax.dot_general(q8, kw, NT, precision=HI, preferred_element_type=F32)
        wmask = (sw - _iota((8, sw), 1)) < NSA_WINDOW
        s_new = jnp.sum(q8 * wnew_ref[:, :NSA_HD], axis=-1, keepdims=True)
        m = jnp.maximum(jnp.max(jnp.where(wmask, sw_, -jnp.inf), axis=-1, keepdims=True), s_new)
        pw = jnp.where(wmask, jnp.exp(sw_ - m), 0.0)
        pn = jnp.exp(s_new - m)
        o_w = (jnp.dot(pw, vw, precision=HI, preferred_element_type=F32) + pn * wnew_ref[:, NSA_HD:]) / (
            jnp.sum(pw, axis=-1, keepdims=True) + pn)
        gate = jax.nn.sigmoid(sm_ref[...])
        o_c = oc_ref[...]
        for h in range(NSA_H):
            g0 = SM_NG + 3 * h
            o_ref[:, h * NSA_HD:(h + 1) * NSA_HD] = (
                gate[:, g0:g0 + 1] * o_c[h:h + 1] + gate[:, g0 + 1:g0 + 2] * o_s[h:h + 1]
                + gate[:, g0 + 2:g0 + 3] * o_w[h:h + 1])


def _nsa_dec_attn(pt_flat, sel_flat, n_pages, n_sel, page0, row0, q_new, slc_cache, slc_new, win_cache, win_new,
                  o_c, small_new):
    b = q_new.shape[0]
    sw = win_cache.shape[1] // 2
    per = PAGE_SIZE // NSA_BLOCK
    half = slc_cache.reshape(slc_cache.shape[0] * per, 2 * NSA_BLOCK, NSA_HD)

    def blk_map(bi, s, pt, sel):
        n = sel[bi * n_sel + s]
        return ((page0 + pt[bi * n_pages + n // per]) * per + n % per, 0, 0)

    per_b = lambda r, n: pl.BlockSpec((None, r, n), lambda bi, s, pt, sel: (bi, 0, 0))
    return pl.pallas_call(
        functools.partial(_nsa_dec_attn_kernel, sw=sw),
        grid_spec=pltpu.PrefetchScalarGridSpec(
            num_scalar_prefetch=2,
            grid=(b, n_sel),
            in_specs=[per_b(1, MIX_W), pl.BlockSpec((None, 2 * NSA_BLOCK, NSA_HD), blk_map), per_b(1, 2 * NSA_HD),
                      pl.BlockSpec((None, 2 * sw, NSA_HD), lambda bi, s, pt, sel: (row0 + bi, 0, 0)),
                      per_b(1, 2 * NSA_HD), per_b(8, NSA_HD), per_b(1, LANES)],
            out_specs=per_b(1, MIX_W),
            scratch_shapes=[pltpu.VMEM((8, 1), F32), pltpu.VMEM((8, 1), F32), pltpu.VMEM((8, NSA_HD), F32)],
        ),
        out_shape=jax.ShapeDtypeStruct((b, 1, MIX_W), F32),
        compiler_params=_cp("parallel", "arbitrary"),
        name="nsa_dec_attn",
    )(pt_flat, sel_flat, q_new, half, slc_new, win_cache, win_new, o_c, small_new)


def _lane_row(vals, at):
    return jnp.zeros((1, LANES), F32).at[0, at:at + vals.shape[0]].set(vals.astype(F32))


def _stacked_weights(w_in, w_branch, w_out, w_up, w_down):
    offs = [0]
    for s in SPLIT_SIZES:
        offs.append(offs[-1] + s)
    seg = lambda i: w_in[:, :, offs[i]:offs[i + 1]]
    (fq, fk, fv, ff, gq, gk, gv, ga, gr, nq, nkc, nks, nkw, ng, dqkv, da, dbeta, dz, mg) = [seg(i) for i in range(19)]
    cat = lambda parts: jnp.concatenate(parts, axis=2).astype(BF16)
    small = [ff, ga, ng, da, dbeta]
    pad = jnp.zeros(w_in.shape[:2] + (LANES - sum(p.shape[2] for p in small),), F32)
    return dict(fox=cat([fq, fk, fv]), gla=cat([gq, gk, gv, gr]), nsa=cat([nq, nkc, nks, nkw]), dn=cat([dqkv, dz]),
                small=cat(small + [pad]), mg=mg.astype(BF16), wb=w_branch.astype(BF16), w_out=w_out.astype(BF16),
                w_up=w_up.astype(BF16), w_down=w_down.astype(BF16))


def _layer_rows(fox_b_f, gla_w_a2, gla_b_a, gla_norm, dn_conv_w, dn_a_log, dn_dt_bias, dn_norm):
    return dict(
        bf_row=_lane_row(fox_b_f, SM_FF),
        w2p=jnp.zeros((LANES, GLA_H * GLA_DK), F32).at[SM_GA:SM_GA + GLA_RANK].set(gla_w_a2),
        ba_row=gla_b_a.reshape(1, -1), gn_row=gla_norm.reshape(1, -1), conv_w=dn_conv_w,
        alog_row=_lane_row(dn_a_log, SM_DA), dtb_row=_lane_row(dn_dt_bias, SM_DA), dnn_row=dn_norm.reshape(1, -1))


def _rope_tables(pos):
    half = ROPE_DIM // 2
    inv = ROPE_THETA ** (-jnp.arange(half, dtype=F32) / half)
    ang = pos.astype(F32)[:, None] * inv[None, :]
    cos, sin = jnp.cos(ang), jnp.sin(ang)
    t = pos.shape[0]
    z = lambda n: jnp.zeros((t, n), F32)
    return (jnp.concatenate([cos, cos, jnp.ones((t, LANES - ROPE_DIM), F32)], axis=1),
            jnp.concatenate([-sin, z(LANES - half)], axis=1),
            jnp.concatenate([z(half), sin, z(LANES - ROPE_DIM)], axis=1))


def _project(h, w, l):
    return {name: _mm(h, w[name], l) for name in ("fox", "gla", "nsa", "dn", "small", "mg")}


def _mixer_prompt(h, b, t, w, l):
    p = _project(h, w, l)
    fox, gla, nsa, dn, small = (p[n].reshape(b, t, -1) for n in ("fox", "gla", "nsa", "dn", "small"))
    lf, c, ct, fox_kv = _fox_prep(small, w["bf_row"], fox)
    o_fox = _fox_attn(fox, c, ct)
    o_gla, s_gla = _gla(gla, small, w["w2p"], w["ba_row"], w["gn_row"], jnp.zeros((b, GLA_H, GLA_DK, GLA_DV), F32))
    qr, _, slc_kv, win_kv, cmp_c, slc_c, win_c, cmean = _nsa_prep(nsa, *_rope_tables(jnp.arange(t)), True)
    o_nsa = _nsa_attn(qr, small, cmean, slc_kv, win_kv)
    qd, kd, vd, gsm, btsm = _dn_prep(dn, jnp.zeros((b, 8, DN_QKV), F32), w["conv_w"], small, w["alog_row"],
                                     w["dtb_row"])
    o_dn, s_dn = _gdn(qd, kd, vd, gsm, btsm, dn, w["dnn_row"], jnp.zeros((b, DN_H, DN_DK, DN_DV), F32))
    mix = _branch_gate([o.reshape(b * t, MIX_W) for o in (o_fox, o_gla, o_nsa, o_dn)], w["wb"], l, p["mg"])
    kv5 = lambda a: a.reshape(b, t, 2, 1, NSA_HD)
    wl = min(NSA_WINDOW, t)
    state = (fox_kv.reshape(b, t, 2, FOX_H, FOX_HD), lf[:, :, :FOX_H], kv5(cmp_c), kv5(slc_c),
             kv5(win_c)[:, t - wl:], s_gla, s_dn, dn[:, t - (DN_CONV - 1):, :DN_QKV])
    return mix, state


def _mixer_sample(h, caches, l, pt_flat, n_pages, w):
    fox_kv_c, fox_lf_c, cmp_c, slc_c, win_c, s_gla_c, s_dn_c, conv_c = caches
    b = h.shape[0]
    past_len = n_pages * PAGE_SIZE
    p = _project(h, w, l)
    fox, gla, dn, small = (p[n].reshape(b, 1, -1) for n in ("fox", "gla", "dn", "small"))
    depth, n_pool = fox_kv_c.shape[:2]
    page0, row0 = l * n_pool, l * b
    o_fox, lf_new = _fox_decode(pt_flat, n_pages, page0, fox, small, w["bf_row"],
                                fox_kv_c.reshape(depth * n_pool, PAGE_SIZE * 2 * FOX_H, FOX_HD),
                                jnp.swapaxes(fox_lf_c, 2, 3).reshape(depth * n_pool, FOX_H, PAGE_SIZE))
    o_gla, o_dn, s_gla, s_dn = _rec_decode(
        gla, dn, small, w["w2p"], w["ba_row"], w["gn_row"], conv_c.reshape((depth * b,) + conv_c.shape[2:]),
        w["conv_w"], w["alog_row"], w["dtb_row"], w["dnn_row"], s_gla_c.reshape((depth * b,) + s_gla_c.shape[2:]),
        s_dn_c.reshape((depth * b,) + s_dn_c.shape[2:]), row0)
    tabs = _rope_tables(jnp.full((b,), past_len, jnp.int32))
    qr, cmp_new, slc_new, win_new = _nsa_prep(p["nsa"].reshape(1, b, -1), *tabs, False)[:4]
    qr, cmp_new, slc_new, win_new = (a.reshape(b, 1, -1) for a in (qr, cmp_new, slc_new, win_new))
    paged = lambda c: c.reshape(depth * n_pool, 2 * PAGE_SIZE, NSA_HD)
    _, _, o_c, sel = _nsa_dec_cmp(pt_flat, n_pages, page0, qr, paged(cmp_c))
    n_sel = NSA_TOPK - 1
    sel_flat = sel[:, :n_sel, 0].reshape(-1)
    sw = win_c.shape[2]
    o_nsa = _nsa_dec_attn(pt_flat, sel_flat, n_pages, n_sel, page0, row0, qr, paged(slc_c), slc_new,
                          win_c.reshape(depth * b, 2 * sw, NSA_HD), win_new, o_c, small)
    mix = _branch_gate([o.reshape(b, MIX_W).astype(BF16) for o in (o_fox, o_gla, o_nsa, o_dn)], w["wb"], l, p["mg"])
    kv5 = lambda a: a.reshape(b, 1, 2, 1, NSA_HD)
    wl = min(NSA_WINDOW, sw + 1)
    new_win = jnp.concatenate([win_c[l][:, sw + 1 - wl:], kv5(win_new)], axis=1)
    new_conv = jnp.concatenate([conv_c[l][:, 1:], dn[:, :, :DN_QKV]], axis=1)
    state = (fox[:, :, MIX_W:].reshape(b, 1, 2, FOX_H, FOX_HD), lf_new[:, :, :FOX_H], kv5(cmp_new), kv5(slc_new),
             new_win, s_gla, s_dn, new_conv)
    return mix, state


def _trunk_layer(x, h, mixer, norms, g_next, w, l):
    _, g_post_mix, g_pre_mlp, g_post_mlp = norms
    mix, state = mixer(h)
    x, h_mlp = _mm_norm_res(mix, w["w_out"], l, g_post_mix, x, g_pre_mlp)
    hid = _mm(h_mlp, w["w_up"], l, out_dtype=BF16, act="relu2")
    if g_next is None:
        return _mm_norm_res(hid, w["w_down"], l, g_post_mlp, x), None, state
    x, h_next = _mm_norm_res(hid, w["w_down"], l, g_post_mlp, x, g_next)
    return x, h_next, state


def kernel(x_prompt, x_sample, cache_fox_kv, cache_fox_logf, cache_nsa_cmp_kv, cache_nsa_slc_kv, cache_nsa_win_kv,
           state_gla, state_dn, state_dn_conv, page_table, norm_pre_mix, norm_post_mix, norm_pre_mlp, norm_post_mlp,
           w_in, fox_b_f, gla_w_a2, gla_b_a, gla_norm, dn_conv_w, dn_a_log, dn_dt_bias, dn_norm, w_branch, w_out,
           w_up, w_down):
    bp, tp, _ = x_prompt.shape
    bs, ts, _ = x_sample.shape
    assert ts == 1
    n_pages = page_table.shape[1]
    pt_flat = page_table.reshape(-1).astype(jnp.int32)
    y_p = x_prompt.reshape(bp * tp, D_MODEL)
    y_s = x_sample.reshape(bs, D_MODEL)
    new_p = [[] for _ in range(8)]
    new_s = [[] for _ in range(8)]
    caches = (cache_fox_kv, cache_fox_logf, cache_nsa_cmp_kv, cache_nsa_slc_kv, cache_nsa_win_kv, state_gla, state_dn,
              state_dn_conv)
    big = _stacked_weights(w_in, w_branch, w_out, w_up, w_down)
    h_p = _rmsnorm_cast(y_p, norm_pre_mix[0])
    h_s = _rmsnorm_cast(y_s, norm_pre_mix[0])
    for l in range(DEPTH):
        w = dict(big, **_layer_rows(fox_b_f[l], gla_w_a2[l], gla_b_a[l], gla_norm[l], dn_conv_w[l], dn_a_log[l],
                                    dn_dt_bias[l], dn_norm[l]))
        norms = (norm_pre_mix[l], norm_post_mix[l], norm_pre_mlp[l], norm_post_mlp[l])
        g_next = norm_pre_mix[l + 1] if l + 1 < DEPTH else None
        y_p, h_p, st_p = _trunk_layer(y_p, h_p, lambda h: _mixer_prompt(h, bp, tp, w, l), norms, g_next, w, l)
        y_s, h_s, st_s = _trunk_layer(y_s, h_s, lambda h: _mixer_sample(h, caches, l, pt_flat, n_pages, w), norms,
                                      g_next, w, l)
        for i in range(8):
            new_p[i].append(st_p[i])
            new_s[i].append(st_s[i])
    fox_kv_p, fox_logf_p, cmp_kv_p, slc_kv_p, win_kv_p, gla_p, dn_p, conv_p = [jnp.stack(a) for a in new_p]
    fox_kv_s, fox_logf_s, cmp_kv_s, slc_kv_s, win_kv_s, gla_s, dn_s, conv_s = [jnp.stack(a) for a in new_s]
    return (y_p.reshape(bp, tp, D_MODEL), y_s.reshape(bs, ts, D_MODEL), fox_kv_p, fox_kv_s, fox_logf_p, fox_logf_s,
            cmp_kv_p, cmp_kv_s, slc_kv_p, slc_kv_s, win_kv_p, win_kv_s, gla_p, gla_s, dn_p, dn_s, conv_p, conv_s)
```

```python
import functools

import jax
import jax.numpy as jnp
from jax import lax
from jax.experimental import pallas as pl
from jax.experimental.pallas import tpu as pltpu

F32 = jnp.float32
BF16 = jnp.bfloat16
HI = lax.Precision.HIGHEST
NT = (((1,), (1,)), ((), ()))
TN = (((0,), (0,)), ((), ()))
NEG = -1e30

D_MODEL = 2048
DEPTH = 2
PAGE_SIZE = 128
N_BRANCH = 4
MIX_W = D_MODEL // 4
FOX_H = 4
FOX_HD = MIX_W // FOX_H
GLA_H = 4
GLA_DK = MIX_W // (2 * GLA_H)
GLA_DV = MIX_W // GLA_H
GLA_RANK = 16
GLA_TAU = 16.0
NSA_H = 4
NSA_HD = MIX_W // NSA_H
NSA_BLOCK = 64
NSA_TOPK = 16
NSA_WINDOW = 512
DN_H = 4
DN_DK = MIX_W // DN_H
DN_DV = MIX_W // DN_H
DN_CONV = 4
DN_QKV = DN_H * (2 * DN_DK + DN_DV)
ROPE_DIM = NSA_HD // 4
ROPE_THETA = 500000.0
D_FF = 4 * D_MODEL
CHUNK = 64
SUB = 16
RMS_EPS = 1e-6
LANES = 128
SPLIT_SIZES = (
    FOX_H * FOX_HD, FOX_H * FOX_HD, FOX_H * FOX_HD, FOX_H,
    GLA_H * GLA_DK, GLA_H * GLA_DK, GLA_H * GLA_DV, GLA_RANK, GLA_H * GLA_DV,
    NSA_H * NSA_HD, 2 * NSA_HD, 2 * NSA_HD, 2 * NSA_HD, 3 * NSA_H,
    DN_QKV, DN_H, DN_H, DN_H * DN_DV,
    N_BRANCH * D_MODEL,
)
SM_FF, SM_GA, SM_NG, SM_DA, SM_DB = 0, 4, 20, 32, 36
VMEM_LIMIT = 56 * 1024 * 1024


def _cp(*sem):
    return pltpu.CompilerParams(dimension_semantics=sem, vmem_limit_bytes=VMEM_LIMIT)


def _pick(n, cap):
    if n <= cap:
        return n
    best = None
    for t in range(LANES, cap + 1, LANES):
        if n % t == 0:
            best = t
    assert best is not None, n
    return best


def _log_sigmoid(z):
    return jnp.minimum(z, 0.0) - jnp.log1p(jnp.exp(-jnp.abs(z)))


def _iota(shape, dim):
    return lax.broadcasted_iota(jnp.int32, shape, dim)


def _dot3(a, b, dims=(((1,), (0,)), ((), ()))):
    ah, bh = a.astype(BF16), b.astype(BF16)
    al, bl = (a - ah.astype(F32)).astype(BF16), (b - bh.astype(F32)).astype(BF16)
    f = lambda x, y: lax.dot_general(x, y, dims, preferred_element_type=F32)
    return f(ah, bh) + (f(ah, bl) + f(al, bh))


def _rmsnorm_cast_kernel(x_ref, g_ref, o_ref):
    x = x_ref[...]
    y = x * lax.rsqrt(jnp.mean(x * x, axis=-1, keepdims=True) + RMS_EPS)
    o_ref[...] = (y * g_ref[...]).astype(o_ref.dtype)


def _rmsnorm_cast(x, g):
    m, d = x.shape
    tm = min(m, 512)
    return pl.pallas_call(
        _rmsnorm_cast_kernel,
        grid=(m // tm,),
        in_specs=[pl.BlockSpec((tm, d), lambda i: (i, 0)), pl.BlockSpec((1, d), lambda i: (0, 0))],
        out_specs=pl.BlockSpec((tm, d), lambda i: (i, 0)),
        out_shape=jax.ShapeDtypeStruct((m, d), BF16),
        compiler_params=_cp("parallel"),
        name="rmsnorm_cast",
    )(x, g.reshape(1, d))


def _mm_kernel(a_ref, w_ref, o_ref, *, act):
    y = jnp.dot(a_ref[...], w_ref[...], preferred_element_type=F32)
    if act == "relu2":
        y = jnp.square(jnp.maximum(y, 0.0))
    o_ref[...] = y.astype(o_ref.dtype)


def _mm(a, w, l, col0=0, n=None, out_dtype=F32, act=None):
    m, k = a.shape
    n = w.shape[2] if n is None else n
    tm = min(m, 1024)
    tn = max(t for t in range(LANES, min(n, 2048) + 1, LANES) if n % t == 0 and col0 % t == 0)
    c0 = col0 // tn
    return pl.pallas_call(
        functools.partial(_mm_kernel, act=act),
        grid=(n // tn, m // tm),
        in_specs=[pl.BlockSpec((tm, k), lambda j, i: (i, 0)), pl.BlockSpec((None, k, tn), lambda j, i: (l, 0, c0 + j))],
        out_specs=pl.BlockSpec((tm, tn), lambda j, i: (i, j)),
        out_shape=jax.ShapeDtypeStruct((m, n), out_dtype),
        compiler_params=_cp("parallel", "parallel"),
        name="mm",
    )(a, w)


def _mm_norm_res_kernel(a_ref, w_ref, g_ref, x_ref, *refs):
    (gn_ref, o_ref, h_ref, acc_ref) = refs if len(refs) == 4 else (None, refs[0], None, refs[1])
    k = pl.program_id(1)

    @pl.when(k == 0)
    def _():
        acc_ref[...] = jnp.zeros_like(acc_ref)

    acc_ref[...] += jnp.dot(a_ref[...], w_ref[...], preferred_element_type=F32)

    @pl.when(k == pl.num_programs(1) - 1)
    def _():
        y = acc_ref[...]
        y = y * lax.rsqrt(jnp.mean(y * y, axis=-1, keepdims=True) + RMS_EPS)
        x = x_ref[...] + y * g_ref[...]
        o_ref[...] = x
        if h_ref is not None:
            h = x * lax.rsqrt(jnp.mean(x * x, axis=-1, keepdims=True) + RMS_EPS)
            h_ref[...] = (h * gn_ref[...]).astype(h_ref.dtype)


def _mm_norm_res(a, w, l, g, x, g_next=None):
    m, k = a.shape
    n = w.shape[2]
    tm = min(m, 512)
    tk = _pick(k, 2048)
    row = pl.BlockSpec((1, n), lambda i, kk: (0, 0))
    tile = pl.BlockSpec((tm, n), lambda i, kk: (i, 0))
    fused = g_next is not None
    return pl.pallas_call(
        _mm_norm_res_kernel,
        grid=(m // tm, k // tk),
        in_specs=[pl.BlockSpec((tm, tk), lambda i, kk: (i, kk)), pl.BlockSpec((None, tk, n), lambda i, kk: (l, kk, 0)),
                  row, tile] + ([row] if fused else []),
        out_specs=[tile, tile] if fused else tile,
        out_shape=[jax.ShapeDtypeStruct((m, n), F32), jax.ShapeDtypeStruct((m, n), BF16)] if fused
        else jax.ShapeDtypeStruct((m, n), F32),
        scratch_shapes=[pltpu.VMEM((tm, n), F32)],
        compiler_params=_cp("parallel", "arbitrary"),
        name="mm_norm_res",
    )(a, w, g.reshape(1, n), x, *([g_next.reshape(1, n)] if fused else []))


def _branch_gate_kernel(h_ref, o0, o1, o2, o3, wb_ref, g0, g1, g2, g3, out_ref):
    h = h_ref[...]
    gates = [jnp.dot(h, g[...], preferred_element_type=F32) for g in (g0, g1, g2, g3)]
    ys = [jnp.dot(o_n[...], wb_ref[n], preferred_element_type=F32) for n, o_n in enumerate((o0, o1, o2, o3))]
    acc = None
    for m_n, y in zip(gates, ys):
        t = jax.nn.sigmoid(m_n) * y
        acc = t if acc is None else acc + t
    out_ref[...] = acc.astype(out_ref.dtype)


def _branch_gate(h, branches, wb, w_mg, l):
    m, k = h.shape
    tm = min(m, 512)
    tn = 512
    nj = D_MODEL // tn
    o_spec = pl.BlockSpec((tm, MIX_W), lambda j, i: (i, 0))
    g_specs = [pl.BlockSpec((None, k, tn), functools.partial(lambda j, i, n: (l, 0, n * nj + j), n=n))
               for n in range(N_BRANCH)]
    return pl.pallas_call(
        _branch_gate_kernel,
        grid=(nj, m // tm),
        in_specs=[pl.BlockSpec((tm, k), lambda j, i: (i, 0))] + [o_spec] * 4
        + [pl.BlockSpec((None, N_BRANCH, MIX_W, tn), lambda j, i: (l, 0, 0, j))] + g_specs,
        out_specs=pl.BlockSpec((tm, tn), lambda j, i: (i, j)),
        out_shape=jax.ShapeDtypeStruct((m, D_MODEL), BF16),
        compiler_params=_cp("parallel", "parallel"),
        name="branch_gate",
    )(h, *branches, wb, w_mg, w_mg, w_mg, w_mg)


def _fox_prep_kernel(s_ref, b_ref, k_ref, v_ref, lf_ref, c_ref, ct_ref, kv_ref, carry_ref, *, tb):
    @pl.when(pl.program_id(1) == 0)
    def _():
        carry_ref[...] = jnp.zeros_like(carry_ref)

    for i, ref in enumerate((k_ref, v_ref)):
        for h in range(FOX_H):
            kv_ref[pl.ds(i * FOX_H + h, tb, stride=2 * FOX_H), :] = ref[:, h * FOX_HD:(h + 1) * FOX_HD]

    lf = _log_sigmoid(s_ref[...] + b_ref[...])
    tri = (_iota((tb, tb), 1) <= _iota((tb, tb), 0)).astype(F32)
    c = jnp.dot(tri, lf, precision=HI, preferred_element_type=F32) + carry_ref[...]
    lf_ref[...] = lf
    c_ref[...] = c
    ct_ref[...] = c.T[:8]
    carry_ref[...] = c[tb - 1:tb]


def _fox_prep(small, bias_row, fox):
    b, t, _ = small.shape
    tb = min(t, 256)
    blk = pl.BlockSpec((None, tb, LANES), lambda bi, i: (bi, i, 0))
    rows = 2 * FOX_H
    return pl.pallas_call(
        functools.partial(_fox_prep_kernel, tb=tb),
        grid=(b, t // tb),
        in_specs=[blk, pl.BlockSpec((1, LANES), lambda bi, i: (0, 0)),
                  pl.BlockSpec((None, tb, MIX_W), lambda bi, i: (bi, i, 1)),
                  pl.BlockSpec((None, tb, MIX_W), lambda bi, i: (bi, i, 2))],
        out_specs=[blk, blk, pl.BlockSpec((None, 8, tb), lambda bi, i: (bi, 0, i)),
                   pl.BlockSpec((None, tb * rows, FOX_HD), lambda bi, i: (bi, i, 0))],
        out_shape=[jax.ShapeDtypeStruct((b, t, LANES), F32)] * 2 + [jax.ShapeDtypeStruct((b, 8, t), F32),
                                                                  jax.ShapeDtypeStruct((b, t * rows, FOX_HD), F32)],
        scratch_shapes=[pltpu.VMEM((1, LANES), F32)],
        compiler_params=_cp("parallel", "arbitrary"),
        name="fox_prep",
    )(small, bias_row, fox, fox)


def _fox_attn_kernel(q_ref, k_ref, v_ref, cq_ref, ck_ref, o_ref, m_sc, l_sc, acc_sc, *, tq, tk):
    qi = pl.program_id(1)
    ki = pl.program_id(2)

    @pl.when(ki == 0)
    def _():
        m_sc[...] = jnp.full_like(m_sc, NEG)
        l_sc[...] = jnp.zeros_like(l_sc)
        acc_sc[...] = jnp.zeros_like(acc_sc)

    def step(diagonal):
        mask = _iota((tk, tq), 0) <= _iota((tk, tq), 1)
        heads = range(FOX_H)
        hsl = [slice(h * FOX_HD, (h + 1) * FOX_HD) for h in heads]
        ss = []
        for h in heads:
            q = (q_ref[:, hsl[h]] * (FOX_HD ** -0.5)).astype(BF16)
            s = lax.dot_general(k_ref[:, hsl[h]].astype(BF16), q, NT, preferred_element_type=F32)
            s = s + cq_ref[h:h + 1, :] - ck_ref[:, h:h + 1]
            ss.append(jnp.where(mask, s, NEG) if diagonal else s)
        m_prev = [m_sc[h] for h in heads]
        m_new = [jnp.maximum(m_prev[h], jnp.max(ss[h], axis=0, keepdims=True)) for h in heads]
        ps = []
        for h in heads:
            p = jnp.exp(ss[h] - m_new[h])
            ps.append(jnp.where(mask, p, 0.0) if diagonal else p)
        pvs = [jnp.dot(v_ref[:, hsl[h]].T.astype(BF16), ps[h].astype(BF16), preferred_element_type=F32)
               for h in heads]
        for h in heads:
            alpha = jnp.exp(m_prev[h] - m_new[h])
            l_sc[h] = alpha * l_sc[h] + jnp.sum(ps[h], axis=0, keepdims=True)
            acc_sc[h] = alpha * acc_sc[h] + pvs[h]
            m_sc[h] = m_new[h]

    pl.when(ki < qi)(functools.partial(step, False))
    pl.when(ki == qi)(functools.partial(step, True))

    @pl.when(ki == pl.num_programs(2) - 1)
    def _():
        for h in range(FOX_H):
            o_ref[:, h * FOX_HD:(h + 1) * FOX_HD] = (acc_sc[h] / l_sc[h]).T.astype(o_ref.dtype)


def _fox_attn(fox, c, ct):
    b, t, _ = fox.shape
    tq = tk = min(t, 512)

    def kmap(col):
        return lambda bi, qi, ki: (bi, jnp.minimum(ki, (qi * tq + tq - 1) // tk), col)

    return pl.pallas_call(
        functools.partial(_fox_attn_kernel, tq=tq, tk=tk),
        grid=(b, t // tq, t // tk),
        in_specs=[
            pl.BlockSpec((None, tq, MIX_W), lambda bi, qi, ki: (bi, qi, 0)),
            pl.BlockSpec((None, tk, MIX_W), kmap(1)),
            pl.BlockSpec((None, tk, MIX_W), kmap(2)),
            pl.BlockSpec((None, 8, tq), lambda bi, qi, ki: (bi, 0, qi)),
            pl.BlockSpec((None, tk, LANES), kmap(0)),
        ],
        out_specs=pl.BlockSpec((None, tq, MIX_W), lambda bi, qi, ki: (bi, qi, 0)),
        out_shape=jax.ShapeDtypeStruct((b, t, MIX_W), BF16),
        scratch_shapes=[pltpu.VMEM((FOX_H, 1, tq), F32), pltpu.VMEM((FOX_H, 1, tq), F32),
                        pltpu.VMEM((FOX_H, FOX_HD, tq), F32)],
        compiler_params=_cp("parallel", "parallel", "arbitrary"),
        name="fox_attn",
    )(fox, fox, fox, ct, c)


def _gla_kernel(g_ref, sm_ref, w2_ref, ba_ref, gn_ref, s0_ref, o_ref, sout_ref, s_sc, *, C):
    c = pl.program_id(1)

    @pl.when(c == 0)
    def _():
        s_sc[...] = s0_ref[...]

    lane = _iota((C, LANES), 1)
    ga = jnp.where((lane >= SM_GA) & (lane < SM_GA + GLA_RANK), sm_ref[...], 0.0)
    pre = jnp.dot(ga, w2_ref[...], precision=HI, preferred_element_type=F32) + ba_ref[...]
    loga = _log_sigmoid(pre) * (1.0 / GLA_TAU)
    tri = (_iota((C, C), 1) <= _iota((C, C), 0)).astype(F32)
    b_all = jnp.dot(tri, loga, precision=HI, preferred_element_type=F32)
    eye = (_iota((GLA_DK, GLA_DK), 0) == _iota((GLA_DK, GLA_DK), 1)).astype(F32)
    heads, blocks = range(GLA_H), range(C // SUB)
    kofs, vofs = GLA_H * GLA_DK, 2 * GLA_H * GLA_DK
    qs = [g_ref[:, h * GLA_DK:(h + 1) * GLA_DK] * (GLA_DK ** -0.5) for h in heads]
    ks = [g_ref[:, kofs + h * GLA_DK:kofs + (h + 1) * GLA_DK] for h in heads]
    vbs = [g_ref[:, vofs + h * GLA_DV:vofs + (h + 1) * GLA_DV].astype(BF16) for h in heads]
    bhs = [b_all[:, h * GLA_DK:(h + 1) * GLA_DK] for h in heads]
    states = [s_sc[h] for h in heads]
    o_inter = [jnp.dot((qs[h] * jnp.exp(bhs[h])).astype(BF16), states[h].astype(BF16), preferred_element_type=F32)
               for h in heads]
    atts = {}
    for h in heads:
        for ib in blocks[1:]:
            a0 = ib * SUB
            bi = bhs[h][a0:a0 + SUB]
            r = bi[0:1]
            qe = (qs[h][a0:a0 + SUB] * jnp.exp(bi - r)).astype(BF16)
            ke = (ks[h][:a0] * jnp.exp(r - bhs[h][:a0])).astype(BF16)
            atts[h, ib] = lax.dot_general(qe, ke, NT, preferred_element_type=F32)
    pair = 2 * GLA_DK
    mask3 = _iota((SUB, SUB, pair), 1) <= _iota((SUB, SUB, pair), 0)
    low = _iota((SUB, SUB, pair), 2) < GLA_DK
    ds = {}
    for hp in range(GLA_H // 2):
        q2 = g_ref[:, hp * pair:(hp + 1) * pair] * (GLA_DK ** -0.5)
        k2 = g_ref[:, kofs + hp * pair:kofs + (hp + 1) * pair]
        b2 = b_all[:, hp * pair:(hp + 1) * pair]
        for ib in blocks:
            rs = slice(ib * SUB, (ib + 1) * SUB)
            bi = b2[rs]
            diff = bi[:, None, :] - bi[None, :, :]
            e = jnp.where(mask3, jnp.exp(jnp.where(mask3, diff, 0.0)), 0.0)
            prod = q2[rs][:, None, :] * k2[rs][None, :, :] * e
            ds[2 * hp, ib] = jnp.sum(jnp.where(low, prod, 0.0), axis=-1)
            ds[2 * hp + 1, ib] = jnp.sum(jnp.where(low, 0.0, prod), axis=-1)
    rows = {}
    for h in heads:
        for ib in blocks:
            a0 = ib * SUB
            o_i = jnp.dot(ds[h, ib].astype(BF16), vbs[h][a0:a0 + SUB], preferred_element_type=F32)
            if ib > 0:
                o_i = o_i + jnp.dot(atts[h, ib].astype(BF16), vbs[h][:a0], preferred_element_type=F32)
            rows[h, ib] = o_i
    for h in heads:
        bend = bhs[h][C - 1:C]
        kdec = (ks[h] * jnp.exp(bend - bhs[h])).astype(BF16)
        dcol = jnp.sum(eye * jnp.exp(bend), axis=1, keepdims=True)
        s_sc[h] = dcol * states[h] + lax.dot_general(kdec, vbs[h], TN, preferred_element_type=F32)
    for h in heads:
        o = jnp.concatenate([rows[h, ib] for ib in blocks], axis=0) + o_inter[h]
        y = o * lax.rsqrt(jnp.mean(o * o, axis=-1, keepdims=True) + RMS_EPS) * gn_ref[...]
        gr = g_ref[:, vofs + GLA_H * GLA_DV + h * GLA_DV:vofs + GLA_H * GLA_DV + (h + 1) * GLA_DV]
        o_ref[:, h * GLA_DV:(h + 1) * GLA_DV] = (y * jax.nn.silu(gr)).astype(o_ref.dtype)

    @pl.when(c == pl.num_programs(1) - 1)
    def _():
        sout_ref[...] = s_sc[...]


def _gla(gla, small, w2p, ba_row, gn_row, s0):
    b, t, _ = gla.shape
    C = CHUNK
    n_gla = gla.shape[-1]
    return pl.pallas_call(
        functools.partial(_gla_kernel, C=C),
        grid=(b, t // C),
        in_specs=[
            pl.BlockSpec((None, C, n_gla), lambda bi, c: (bi, c, 0)),
            pl.BlockSpec((None, C, LANES), lambda bi, c: (bi, c, 0)),
            pl.BlockSpec(w2p.shape, lambda bi, c: (0, 0)),
            pl.BlockSpec(ba_row.shape, lambda bi, c: (0, 0)),
            pl.BlockSpec(gn_row.shape, lambda bi, c: (0, 0)),
            pl.BlockSpec((None, GLA_H, GLA_DK, GLA_DV), lambda bi, c: (bi, 0, 0, 0)),
        ],
        out_specs=[
            pl.BlockSpec((None, C, MIX_W), lambda bi, c: (bi, c, 0)),
            pl.BlockSpec((None, GLA_H, GLA_DK, GLA_DV), lambda bi, c: (bi, 0, 0, 0)),
        ],
        out_shape=[jax.ShapeDtypeStruct((b, t, MIX_W), BF16), jax.ShapeDtypeStruct(s0.shape, F32)],
        scratch_shapes=[pltpu.VMEM((GLA_H, GLA_DK, GLA_DV), F32)],
        compiler_params=_cp("parallel", "arbitrary"),
        name="gla",
    )(gla, small, w2p, ba_row, gn_row, s0)


def _rope(x, cos, sa, sb):
    half = ROPE_DIM // 2
    return x * cos + pltpu.roll(x, LANES - half, 1) * sa + pltpu.roll(x, half, 1) * sb


def _nsa_prep_kernel(x_ref, cos_ref, sa_ref, sb_ref, q_ref, cmp_ref, slc_ref, win_ref, cmp_c, slc_c, win_c,
                     *mean_ref, tb):
    cos, sa, sb = cos_ref[...], sa_ref[...], sb_ref[...]
    for h in range(NSA_H):
        hs = slice(h * NSA_HD, (h + 1) * NSA_HD)
        q_ref[:, hs] = _rope(x_ref[:, hs], cos, sa, sb) * (NSA_HD ** -0.5)
    base = NSA_H * NSA_HD
    for i, (ref, cache_ref) in enumerate(((cmp_ref, cmp_c), (slc_ref, slc_c), (win_ref, win_c))):
        k0 = base + i * 2 * NSA_HD
        k = _rope(x_ref[:, k0:k0 + NSA_HD], cos, sa, sb)
        v = x_ref[:, k0 + NSA_HD:k0 + 2 * NSA_HD]
        ref[:, :NSA_HD] = k
        ref[:, NSA_HD:] = v
        cache_ref[pl.ds(0, tb, stride=2), :] = k
        cache_ref[pl.ds(1, tb, stride=2), :] = v
    if mean_ref:
        kv = cmp_ref[...]
        mean_ref[0][...] = jnp.mean(kv.reshape(tb // NSA_BLOCK, NSA_BLOCK, 2 * NSA_HD), axis=1)


def _nsa_prep(nsa, cos, sa, sb, with_means):
    b, t, n = nsa.shape
    tb = min(t, 512)
    tab = pl.BlockSpec((tb, LANES), lambda bi, i: (i, 0))
    kv = pl.BlockSpec((None, tb, 2 * NSA_HD), lambda bi, i: (bi, i, 0))
    kvc = pl.BlockSpec((None, 2 * tb, NSA_HD), lambda bi, i: (bi, i, 0))
    out_specs = [pl.BlockSpec((None, tb, MIX_W), lambda bi, i: (bi, i, 0)), kv, kv, kv, kvc, kvc, kvc]
    out_shape = ([jax.ShapeDtypeStruct((b, t, MIX_W), F32)] + [jax.ShapeDtypeStruct((b, t, 2 * NSA_HD), F32)] * 3
                 + [jax.ShapeDtypeStruct((b, 2 * t, NSA_HD), F32)] * 3)
    if with_means:
        out_specs.append(pl.BlockSpec((None, tb // NSA_BLOCK, 2 * NSA_HD), lambda bi, i: (bi, i, 0)))
        out_shape.append(jax.ShapeDtypeStruct((b, t // NSA_BLOCK, 2 * NSA_HD), F32))
    return pl.pallas_call(
        functools.partial(_nsa_prep_kernel, tb=tb),
        grid=(b, t // tb),
        in_specs=[pl.BlockSpec((None, tb, n), lambda bi, i: (bi, i, 0)), tab, tab, tab],
        out_specs=out_specs,
        out_shape=out_shape,
        compiler_params=_cp("parallel", "parallel"),
        name="nsa_prep",
    )(nsa, cos, sa, sb)


def _masked_softmax(s, mask):
    s = jnp.where(mask, s, -jnp.inf)
    m = jnp.max(s, axis=-1, keepdims=True)
    m = jnp.where(m == -jnp.inf, 0.0, m)
    p = jnp.where(mask, jnp.exp(s - m), 0.0)
    return p / jnp.maximum(jnp.sum(p, axis=-1, keepdims=True), 1e-30)


def _topk_mask_t(score_t, blk_t, n_sel):
    rank = jnp.zeros(score_t.shape, jnp.int32)
    for m in range(score_t.shape[0]):
        sm = score_t[m:m + 1, :]
        beats = (sm > score_t) | ((sm == score_t) & (blk_t > m))
        rank = rank + beats.astype(jnp.int32)
    return rank < n_sel


def _masked_scores(qs, k, mask):
    bias = jnp.where(mask, 0.0, NEG)
    return [lax.dot_general(k, q, NT, preferred_element_type=F32) + bias for q in qs]


def _softmax_step(carry, ss, v_t):
    ms, ls, accs = carry
    n = range(len(ss))
    m_new = [jnp.maximum(ms[g], jnp.max(ss[g], axis=0, keepdims=True)) for g in n]
    ps = [jnp.exp(ss[g] - m_new[g]) for g in n]
    pvs = [jnp.dot(v_t, ps[g].astype(BF16), preferred_element_type=F32) for g in n]
    alphas = [jnp.exp(ms[g] - m_new[g]) for g in n]
    ls = [alphas[g] * ls[g] + jnp.sum(ps[g], axis=0, keepdims=True) for g in n]
    accs = [alphas[g] * accs[g] + pvs[g] for g in n]
    return m_new, ls, accs


def _nsa_attn_kernel(q_ref, sm_ref, cm_ref, slc_ref, win_ref, o_ref, *, tq, kt, nb):
    qi = pl.program_id(1)
    qpos = qi * tq + _iota((tq, 1), 0)
    blk = _iota((tq, nb), 1)
    cmask = (blk + 1) * NSA_BLOCK <= qpos + 1
    kc, vc = cm_ref[:, :NSA_HD], cm_ref[:, NSA_HD:]
    o_c = []
    imp = jnp.zeros((tq, nb), F32)
    for h in range(NSA_H):
        s = lax.dot_general(q_ref[:, h * NSA_HD:(h + 1) * NSA_HD], kc, NT, precision=HI, preferred_element_type=F32)
        p = _masked_softmax(s, cmask)
        imp = imp + p
        o_c.append(jnp.dot(p, vc, precision=HI, preferred_element_type=F32))
    blk_t = _iota((nb, tq), 0)
    qpos_t = qi * tq + _iota((nb, tq), 1)
    cur_t = qpos_t // NSA_BLOCK
    score_t = jnp.where(blk_t == cur_t, jnp.inf,
                        jnp.where((blk_t + 1) * NSA_BLOCK <= qpos_t + 1, imp.T, -jnp.inf))
    sel_b = (_topk_mask_t(score_t, blk_t, min(NSA_TOPK, nb)) & (blk_t <= cur_t)).astype(BF16)
    qs = [q_ref[:, h * NSA_HD:(h + 1) * NSA_HD].astype(BF16) for h in range(NSA_H)]
    krow = _iota((kt, tq), 0)
    qpos_row = qi * tq + _iota((1, tq), 1)
    blk_of_col = _iota((nb, kt), 1) // NSA_BLOCK - _iota((nb, kt), 0)
    last = (qi * tq + tq - 1) // kt

    def init():
        return ([jnp.full((1, tq), NEG, F32)] * NSA_H, [jnp.zeros((1, tq), F32)] * NSA_H,
                [jnp.zeros((NSA_HD, tq), F32)] * NSA_H)

    def tile_rows(kb):
        return pl.ds(pl.multiple_of(kb * kt, kt), kt)

    def slc_scores(kb):
        expand = (blk_of_col + kb * (kt // NSA_BLOCK) == 0).astype(BF16)
        tok = lax.dot_general(expand, sel_b, TN, preferred_element_type=F32)
        mask = (tok > 0.5) & (kb * kt + krow <= qpos_row)
        return _masked_scores(qs, slc_ref[tile_rows(kb), :NSA_HD].astype(BF16), mask)

    def win_scores(kb):
        dist = qpos_row - (kb * kt + krow)
        mask = (dist >= 0) & (dist < NSA_WINDOW)
        return _masked_scores(qs, win_ref[tile_rows(kb), :NSA_HD].astype(BF16), mask)

    def attend(scores_of, kv_ref, first):
        def body(kb, state):
            v_t = kv_ref[tile_rows(kb), NSA_HD:].T.astype(BF16)
            return _softmax_step(state, scores_of(kb), v_t)

        _, ls, accs = lax.fori_loop(first, last + 1, body, init())
        return ls, accs

    l_s, acc_s = attend(slc_scores, slc_ref, 0)
    l_w, acc_w = attend(win_scores, win_ref, jnp.maximum(qi * tq - (NSA_WINDOW - 1), 0) // kt)
    gate = jax.nn.sigmoid(sm_ref[...])
    for h in range(NSA_H):
        g0 = SM_NG + 3 * h
        o = (gate[:, g0:g0 + 1] * o_c[h] + gate[:, g0 + 1:g0 + 2] * (acc_s[h] / l_s[h]).T
             + gate[:, g0 + 2:g0 + 3] * (acc_w[h] / l_w[h]).T)
        o_ref[:, h * NSA_HD:(h + 1) * NSA_HD] = o.astype(o_ref.dtype)


def _nsa_attn(qr, small, cmean, slc, win):
    b, t, _ = qr.shape
    tq = min(t, 512)
    kt = min(t, 512)
    nb = t // NSA_BLOCK
    whole = lambda n: pl.BlockSpec((None, n, 2 * NSA_HD), lambda bi, qi: (bi, 0, 0))
    return pl.pallas_call(
        functools.partial(_nsa_attn_kernel, tq=tq, kt=kt, nb=nb),
        grid=(b, t // tq),
        in_specs=[
            pl.BlockSpec((None, tq, MIX_W), lambda bi, qi: (bi, qi, 0)),
            pl.BlockSpec((None, tq, LANES), lambda bi, qi: (bi, qi, 0)),
            whole(nb), whole(t), whole(t),
        ],
        out_specs=pl.BlockSpec((None, tq, MIX_W), lambda bi, qi: (bi, qi, 0)),
        out_shape=jax.ShapeDtypeStruct((b, t, MIX_W), BF16),
        compiler_params=_cp("parallel", "parallel"),
        name="nsa_attn",
    )(qr, small, cmean, slc, win)


def _dn_prep_kernel(x_ref, halo_ref, prev_ref, cw_ref, sm_ref, alog_ref, dtb_ref,
                    q_ref, k_ref, v_ref, g_ref, bt_ref, *, tb):
    halo = jnp.where(pl.program_id(1) == 0, prev_ref[...], halo_ref[...])
    xcat = jnp.concatenate([halo, x_ref[...]], axis=0)
    conv = x_ref[...] * cw_ref[DN_CONV - 1:DN_CONV]
    for j in range(DN_CONV - 1):
        sh = DN_CONV - 1 - j
        conv = conv + pltpu.roll(xcat, sh, 0)[8:] * cw_ref[j:j + 1]
    u = jax.nn.silu(conv)
    for h in range(DN_H):
        hs = slice(h * DN_DK, (h + 1) * DN_DK)
        uq = u[:, hs]
        q_ref[:, hs] = uq * lax.rsqrt(jnp.sum(uq * uq, axis=-1, keepdims=True) + 1e-6) * (DN_DK ** -0.5)
        uk = u[:, DN_H * DN_DK + h * DN_DK:DN_H * DN_DK + (h + 1) * DN_DK]
        k_ref[:, hs] = uk * lax.rsqrt(jnp.sum(uk * uk, axis=-1, keepdims=True) + 1e-6)
    v_ref[...] = u[:, 2 * DN_H * DN_DK:]
    sm = sm_ref[...]
    g_ref[...] = -jnp.exp(alog_ref[...]) * jax.nn.softplus(sm + dtb_ref[...])
    bt_ref[...] = jax.nn.sigmoid(sm)


def _dn_prep(dn, prev8, conv_w, small, alog_row, dtb_row):
    b, t, _ = dn.shape
    tb = min(t, 256)
    hb = tb // 8
    row = pl.BlockSpec((None, tb, MIX_W), lambda bi, i: (bi, i, 0))
    sm = pl.BlockSpec((None, tb, LANES), lambda bi, i: (bi, i, 0))
    one = lambda shape: pl.BlockSpec(shape, lambda bi, i: (0,) * len(shape))
    return pl.pallas_call(
        functools.partial(_dn_prep_kernel, tb=tb),
        grid=(b, t // tb),
        in_specs=[
            pl.BlockSpec((None, tb, DN_QKV), lambda bi, i: (bi, i, 0)),
            pl.BlockSpec((None, 8, DN_QKV), lambda bi, i: (bi, jnp.maximum(i * hb - 1, 0), 0)),
            pl.BlockSpec((None, 8, DN_QKV), lambda bi, i: (bi, 0, 0)),
            one(conv_w.shape), sm, one(alog_row.shape), one(dtb_row.shape),
        ],
        out_specs=[row, row, row, sm, sm],
        out_shape=[jax.ShapeDtypeStruct((b, t, MIX_W), F32)] * 3 + [jax.ShapeDtypeStruct((b, t, LANES), F32)] * 2,
        compiler_params=_cp("parallel", "parallel"),
        name="dn_prep",
    )(dn, dn, prev8, conv_w, small, alog_row, dtb_row)


def _gdn_pre_kernel(q_ref, k_ref, v_ref, g_ref, bt_ref, wv_ref, wk_ref, qe_ref, kd_ref, qk_ref, gam_ref, *, C, nck):
    row, col = _iota((C, C), 0), _iota((C, C), 1)
    incl, strict = col <= row, col < row
    eye = (row == col).astype(F32)
    tri = incl.astype(F32)
    n_sq = C.bit_length() - 2
    chains = []
    for c in range(nck):
        rs = slice(c * C, (c + 1) * C)
        gam_all = jnp.dot(tri, g_ref[rs, :], precision=HI, preferred_element_type=F32)
        gam_ref[rs, :] = gam_all
        gam_t = gam_all.T
        for h in range(DN_H):
            hs = slice(h * DN_DK, (h + 1) * DN_DK)
            q, k = q_ref[rs, hs], k_ref[rs, hs]
            gcol = gam_all[:, SM_DA + h:SM_DA + h + 1]
            bcol = bt_ref[rs, SM_DB + h:SM_DB + h + 1]
            diff = gcol - gam_t[SM_DA + h:SM_DA + h + 1, :]
            dec_incl = jnp.where(incl, jnp.exp(jnp.where(incl, diff, 0.0)), 0.0)
            qk = lax.dot_general(q.astype(BF16), k.astype(BF16), NT, preferred_element_type=F32) * dec_incl
            qk_ref[rs, h * C:(h + 1) * C] = qk.astype(BF16)
            qe_ref[rs, hs] = (q * jnp.exp(gcol)).astype(BF16)
            kd_ref[rs, hs] = (k * jnp.exp(gcol[C - 1:C] - gcol)).astype(BF16)
            x = -(bcol * _dot3(k, k, NT) * jnp.where(strict, dec_incl, 0.0))
            chains.append((rs, hs, x, gcol, bcol))
    xs = [ch[2] for ch in chains]
    ps = [eye + x for x in xs]
    for _ in range(n_sq):
        xs = [_dot3(x, x) for x in xs]
        ps = [p + _dot3(p, x) for p, x in zip(ps, xs)]
    for (rs, hs, _, gcol, bcol), p in zip(chains, ps):
        k, v = k_ref[rs, hs], v_ref[rs, hs]
        rhs = jnp.concatenate([bcol * v, (bcol * jnp.exp(gcol)) * k], axis=1)
        w = _dot3(p, rhs)
        wv_ref[rs, hs] = w[:, :DN_DV]
        wk_ref[rs, hs] = w[:, DN_DV:].astype(BF16)


def _gdn_rec_kernel(wv_ref, wk_ref, qe_ref, kd_ref, qk_ref, gam_ref, z_ref, nrm_ref, s0_ref, o_ref, sout_ref, s_sc,
                    *, C, nck):
    i = pl.program_id(1)

    @pl.when(i == 0)
    def _():
        s_sc[...] = s0_ref[...]

    state = [s_sc[h] for h in range(DN_H)]
    heads = range(DN_H)
    hsl = [slice(h * DN_DK, (h + 1) * DN_DK) for h in heads]
    for c in range(nck):
        rs = slice(c * C, (c + 1) * C)
        sbs = [state[h].astype(BF16) for h in heads]
        ubs = [(wv_ref[rs, hsl[h]] - jnp.dot(wk_ref[rs, hsl[h]], sbs[h], preferred_element_type=F32)).astype(BF16)
               for h in heads]
        outs = [jnp.dot(qe_ref[rs, hsl[h]], sbs[h], preferred_element_type=F32)
                + jnp.dot(qk_ref[rs, h * C:(h + 1) * C], ubs[h], preferred_element_type=F32) for h in heads]
        gend = gam_ref[c * C + C - 1:(c + 1) * C, :]
        state = [jnp.exp(gend[:, SM_DA + h:SM_DA + h + 1]) * state[h]
                 + lax.dot_general(kd_ref[rs, hsl[h]], ubs[h], TN, preferred_element_type=F32) for h in heads]
        for h in heads:
            o = outs[h]
            y = o * lax.rsqrt(jnp.mean(o * o, axis=-1, keepdims=True) + RMS_EPS) * nrm_ref[...]
            o_ref[rs, hsl[h]] = (y * jax.nn.silu(z_ref[rs, hsl[h]])).astype(o_ref.dtype)
    for h in range(DN_H):
        s_sc[h] = state[h]

    @pl.when(i == pl.num_programs(1) - 1)
    def _():
        for h in range(DN_H):
            sout_ref[h] = state[h]


GDN_BLOCK = 256


def _gdn(qd, kd, vd, gsm, btsm, dn, nrm_row, s0):
    b, t, _ = qd.shape
    C = CHUNK
    tb = min(t, GDN_BLOCK)
    nck = tb // C
    row = pl.BlockSpec((None, tb, MIX_W), lambda bi, i: (bi, i, 0))
    sm = pl.BlockSpec((None, tb, LANES), lambda bi, i: (bi, i, 0))
    qk_spec = pl.BlockSpec((None, tb, DN_H * C), lambda bi, i: (bi, i, 0))
    st = pl.BlockSpec((None, DN_H, DN_DK, DN_DV), lambda bi, i: (bi, 0, 0, 0))
    wide = lambda dt: jax.ShapeDtypeStruct((b, t, MIX_W), dt)
    wv, wk, qe, kdc, qk, gam = pl.pallas_call(
        functools.partial(_gdn_pre_kernel, C=C, nck=nck),
        grid=(b, t // tb),
        in_specs=[row, row, row, sm, sm],
        out_specs=[row, row, row, row, qk_spec, sm],
        out_shape=[wide(F32), wide(BF16), wide(BF16), wide(BF16), jax.ShapeDtypeStruct((b, t, DN_H * C), BF16),
                   jax.ShapeDtypeStruct((b, t, LANES), F32)],
        compiler_params=_cp("parallel", "parallel"),
        name="gdn_pre",
    )(qd, kd, vd, gsm, btsm)
    return pl.pallas_call(
        functools.partial(_gdn_rec_kernel, C=C, nck=nck),
        grid=(b, t // tb),
        in_specs=[row, row, row, row, qk_spec, sm,
                  pl.BlockSpec((None, tb, MIX_W), lambda bi, i: (bi, i, DN_QKV // MIX_W)),
                  pl.BlockSpec(nrm_row.shape, lambda bi, i: (0, 0)), st],
        out_specs=[row, st],
        out_shape=[wide(BF16), jax.ShapeDtypeStruct(s0.shape, F32)],
        scratch_shapes=[pltpu.VMEM((DN_H, DN_DK, DN_DV), F32)],
        compiler_params=_cp("parallel", "arbitrary"),
        name="gdn_rec",
    )(wv, wk, qe, kdc, qk, gam, dn, nrm_row, s0)


FOX_PAGES = 8


def _fox_dec_kernel(pt_ref, fx_ref, sm_ref, b_ref, *refs):
    kv_refs, lf_refs = refs[:FOX_PAGES], refs[FOX_PAGES:2 * FOX_PAGES]
    o_ref, lfo_ref, m_sc, l_sc, acc_sc, car_sc = refs[2 * FOX_PAGES:]
    j = pl.program_id(1)
    P = PAGE_SIZE
    rows_per_tok = 2 * FOX_H
    qs = [fx_ref[:, h * FOX_HD:(h + 1) * FOX_HD] * (FOX_HD ** -0.5) for h in range(FOX_H)]

    @pl.when(j == 0)
    def _():
        lf_new = _log_sigmoid(sm_ref[...] + b_ref[...])
        lfo_ref[...] = lf_new
        on_diag = _iota((8, LANES), 1) == _iota((8, LANES), 0)
        car_sc[...] = jnp.broadcast_to(jnp.sum(jnp.where(on_diag, lf_new, 0.0), axis=1, keepdims=True), (8, LANES))
        for h in range(FOX_H):
            k_new = fx_ref[:, MIX_W + h * FOX_HD:MIX_W + (h + 1) * FOX_HD]
            m_sc[h] = jnp.sum(qs[h] * k_new, axis=-1, keepdims=True)
            l_sc[h] = jnp.ones((1, 1), F32)
            acc_sc[h] = fx_ref[:, 2 * MIX_W + h * FOX_HD:2 * MIX_W + (h + 1) * FOX_HD]

    upper = (_iota((P, P), 1) > _iota((P, P), 0)).astype(F32)
    carry = car_sc[...]
    pad = jnp.zeros((8 - FOX_H, P), F32)
    scores = []
    for kv_ref, lf_ref in zip(kv_refs, lf_refs):
        lf_t = jnp.concatenate([lf_ref[...], pad], axis=0)
        bias = (lax.dot_general(lf_t, upper, NT, precision=HI, preferred_element_type=F32) + carry).T
        scores.append([jnp.sum(kv_ref[pl.ds(h, P, stride=rows_per_tok), :] * qs[h], axis=-1, keepdims=True)
                       + bias[:, h:h + 1] for h in range(FOX_H)])
        carry = carry + jnp.sum(lf_t, axis=1, keepdims=True)
    car_sc[...] = carry
    outs = []
    for h in range(FOX_H):
        m_step = jnp.max(scores[0][h], axis=0, keepdims=True)
        for g in range(1, FOX_PAGES):
            m_step = jnp.maximum(m_step, jnp.max(scores[g][h], axis=0, keepdims=True))
        m_prev = m_sc[h]
        m_new = jnp.maximum(m_prev, m_step)
        alpha = jnp.exp(m_prev - m_new)
        l_new = alpha * l_sc[h]
        acc = alpha * acc_sc[h]
        for g, kv_ref in enumerate(kv_refs):
            p = jnp.exp(scores[g][h] - m_new)
            l_new = l_new + jnp.sum(p, axis=0, keepdims=True)
            acc = acc + jnp.sum(p * kv_ref[pl.ds(FOX_H + h, P, stride=rows_per_tok), :], axis=0, keepdims=True)
        m_sc[h], l_sc[h], acc_sc[h] = m_new, l_new, acc
        outs.append(acc / l_new)

    @pl.when(j == pl.num_programs(1) - 1)
    def _():
        for h in range(FOX_H):
            o_ref[:, h * FOX_HD:(h + 1) * FOX_HD] = outs[h]


def _fox_decode(pt_flat, n_pages, page0, fox_new, small_new, bias_row, kv_cache, lf_cache):
    b = fox_new.shape[0]
    assert n_pages % FOX_PAGES == 0

    def page(r):
        return lambda bi, j, pt: (page0 + pt[bi * n_pages + n_pages - 1 - (j * FOX_PAGES + r)], 0, 0)

    per_b = lambda n: pl.BlockSpec((None, 1, n), lambda bi, j, pt: (bi, 0, 0))
    return pl.pallas_call(
        _fox_dec_kernel,
        grid_spec=pltpu.PrefetchScalarGridSpec(
            num_scalar_prefetch=1,
            grid=(b, n_pages // FOX_PAGES),
            in_specs=[per_b(3 * MIX_W), per_b(LANES), pl.BlockSpec((1, LANES), lambda bi, j, pt: (0, 0))]
            + [pl.BlockSpec((None, PAGE_SIZE * 2 * FOX_H, FOX_HD), page(r)) for r in range(FOX_PAGES)]
            + [pl.BlockSpec((None, FOX_H, PAGE_SIZE), page(r)) for r in range(FOX_PAGES)],
            out_specs=[per_b(MIX_W), per_b(LANES)],
            scratch_shapes=[pltpu.VMEM((FOX_H, 1, 1), F32), pltpu.VMEM((FOX_H, 1, 1), F32),
                            pltpu.VMEM((FOX_H, 1, FOX_HD), F32), pltpu.VMEM((8, PAGE_SIZE), F32)],
        ),
        out_shape=[jax.ShapeDtypeStruct((b, 1, MIX_W), F32), jax.ShapeDtypeStruct((b, 1, LANES), F32)],
        compiler_params=_cp("parallel", "arbitrary"),
        name="fox_decode",
    )(pt_flat, fox_new, small_new, bias_row, *([kv_cache] * FOX_PAGES), *([lf_cache] * FOX_PAGES))


def _col(row, eye):
    return jnp.sum(eye * row, axis=1, keepdims=True)


def _rec_dec_kernel(g_ref, d_ref, sm_ref, w2_ref, ba_ref, gn_ref, cp_ref, cw_ref, alog_ref, dtb_ref, nrm_ref,
                    sg_ref, sd_ref, og_ref, od_ref, sgo_ref, sdo_ref):
    sm = sm_ref[...]
    lane = _iota((8, LANES), 1)
    ga = jnp.where((lane >= SM_GA) & (lane < SM_GA + GLA_RANK), jnp.broadcast_to(sm, (8, LANES)), 0.0)
    pre = jnp.dot(ga, w2_ref[...], precision=HI, preferred_element_type=F32)[0:1] + ba_ref[...]
    loga = _log_sigmoid(pre) * (1.0 / GLA_TAU)
    eye_k = (_iota((GLA_DK, GLA_DK), 0) == _iota((GLA_DK, GLA_DK), 1)).astype(F32)
    for h in range(GLA_H):
        q = g_ref[:, h * GLA_DK:(h + 1) * GLA_DK] * (GLA_DK ** -0.5)
        k = g_ref[:, GLA_H * GLA_DK + h * GLA_DK:GLA_H * GLA_DK + (h + 1) * GLA_DK]
        v0 = 2 * GLA_H * GLA_DK + h * GLA_DV
        v = g_ref[:, v0:v0 + GLA_DV]
        gr = g_ref[:, v0 + GLA_H * GLA_DV:v0 + GLA_H * GLA_DV + GLA_DV]
        ea = jnp.exp(loga[:, h * GLA_DK:(h + 1) * GLA_DK])
        s0 = sg_ref[h]
        o = jnp.sum(q * k, axis=-1, keepdims=True) * v + jnp.sum(_col(q * ea, eye_k) * s0, axis=0, keepdims=True)
        sgo_ref[h] = _col(ea, eye_k) * s0 + _col(k, eye_k) * v
        y = o * lax.rsqrt(jnp.mean(o * o, axis=-1, keepdims=True) + RMS_EPS) * gn_ref[...]
        og_ref[:, h * GLA_DV:(h + 1) * GLA_DV] = y * jax.nn.silu(gr)
    conv = d_ref[:, :DN_QKV] * cw_ref[DN_CONV - 1:DN_CONV]
    for j in range(DN_CONV - 1):
        conv = conv + cp_ref[j:j + 1] * cw_ref[j:j + 1]
    u = jax.nn.silu(conv)
    gall = -jnp.exp(alog_ref[...]) * jax.nn.softplus(sm + dtb_ref[...])
    ball = jax.nn.sigmoid(sm)
    eye_d = (_iota((DN_DK, DN_DK), 0) == _iota((DN_DK, DN_DK), 1)).astype(F32)
    for h in range(DN_H):
        uq = u[:, h * DN_DK:(h + 1) * DN_DK]
        q = uq * lax.rsqrt(jnp.sum(uq * uq, axis=-1, keepdims=True) + 1e-6) * (DN_DK ** -0.5)
        uk = u[:, DN_H * DN_DK + h * DN_DK:DN_H * DN_DK + (h + 1) * DN_DK]
        k = uk * lax.rsqrt(jnp.sum(uk * uk, axis=-1, keepdims=True) + 1e-6)
        v = u[:, 2 * DN_H * DN_DK + h * DN_DV:2 * DN_H * DN_DK + (h + 1) * DN_DV]
        eg = jnp.exp(gall[:, SM_DA + h:SM_DA + h + 1])
        beta = ball[:, SM_DB + h:SM_DB + h + 1]
        s0 = sd_ref[h]
        kcol = _col(k, eye_d)
        ks = jnp.sum(kcol * s0, axis=0, keepdims=True)
        qs = jnp.sum(_col(q, eye_d) * s0, axis=0, keepdims=True)
        un = beta * (v - eg * ks)
        o = eg * qs + jnp.sum(q * k, axis=-1, keepdims=True) * un
        sdo_ref[h] = eg * s0 + kcol * un
        y = o * lax.rsqrt(jnp.mean(o * o, axis=-1, keepdims=True) + RMS_EPS) * nrm_ref[...]
        z = d_ref[:, DN_QKV + h * DN_DV:DN_QKV + (h + 1) * DN_DV]
        od_ref[:, h * DN_DV:(h + 1) * DN_DV] = y * jax.nn.silu(z)


def _rec_decode(gla_new, dn_new, small_new, w2p, ba_row, gn_row, conv_prev, conv_w, alog_row, dtb_row, nrm_row,
                s_gla, s_dn, row0):
    b = gla_new.shape[0]
    per_b = lambda n: pl.BlockSpec((None, 1, n), lambda bi: (bi, 0, 0))
    one = lambda a: pl.BlockSpec(a.shape, lambda bi: (0,) * a.ndim)
    sg = lambda r0: pl.BlockSpec((None, GLA_H, GLA_DK, GLA_DV), lambda bi: (r0 + bi, 0, 0, 0))
    sd = lambda r0: pl.BlockSpec((None, DN_H, DN_DK, DN_DV), lambda bi: (r0 + bi, 0, 0, 0))
    return pl.pallas_call(
        _rec_dec_kernel,
        grid=(b,),
        in_specs=[per_b(gla_new.shape[-1]), per_b(dn_new.shape[-1]), per_b(LANES), one(w2p), one(ba_row), one(gn_row),
                  pl.BlockSpec((None, DN_CONV - 1, DN_QKV), lambda bi: (row0 + bi, 0, 0)), one(conv_w), one(alog_row),
                  one(dtb_row), one(nrm_row), sg(row0), sd(row0)],
        out_specs=[per_b(MIX_W), per_b(MIX_W), sg(0), sd(0)],
        out_shape=[jax.ShapeDtypeStruct((b, 1, MIX_W), F32)] * 2
        + [jax.ShapeDtypeStruct((b,) + s_gla.shape[1:], F32), jax.ShapeDtypeStruct((b,) + s_dn.shape[1:], F32)],
        compiler_params=_cp("parallel"),
        name="rec_decode",
    )(gla_new, dn_new, small_new, w2p, ba_row, gn_row, conv_prev, conv_w, alog_row, dtb_row, nrm_row, s_gla, s_dn)


def _heads_to_rows(q_row):
    rows = [q_row[:, h * NSA_HD:(h + 1) * NSA_HD] for h in range(NSA_H)]
    return jnp.concatenate(rows + [jnp.zeros((8 - NSA_H, NSA_HD), F32)], axis=0)


CMP_PAGES = 16


def _nsa_dec_cmp_kernel(pt_ref, q_ref, *refs, nbp):
    page_refs, (kc_ref, vc_ref, oc_ref, sel_ref) = refs[:CMP_PAGES], refs[CMP_PAGES:]
    j = pl.program_id(1)
    per = PAGE_SIZE // NSA_BLOCK
    rows = CMP_PAGES * per
    dst = pl.ds(pl.multiple_of(j * rows, rows), rows)
    for kv, ref in enumerate((kc_ref, vc_ref)):
        means = [jnp.mean(r[pl.ds(kv, PAGE_SIZE, stride=2), :].reshape(per, NSA_BLOCK, NSA_HD), axis=1)
                 for r in page_refs]
        ref[dst, :] = jnp.concatenate(means, axis=0)

    @pl.when(j == pl.num_programs(1) - 1)
    def _():
        q8 = _heads_to_rows(q_ref[...])
        kc, vc = kc_ref[...], vc_ref[...]
        s = lax.dot_general(q8, kc, NT, precision=HI, preferred_element_type=F32)
        p = _masked_softmax(s, jnp.full(s.shape, True))
        oc_ref[...] = jnp.dot(p, vc, precision=HI, preferred_element_type=F32)
        imp = jnp.sum(p[0:NSA_H], axis=0, keepdims=True)
        r_i, c_i = _iota((nbp, nbp), 0), _iota((nbp, nbp), 1)
        imp_col = _col(imp, (r_i == c_i).astype(F32))
        beats = (imp_col > imp) | ((imp_col == imp) & (r_i < c_i))
        rank = jnp.sum(beats.astype(jnp.int32), axis=0, keepdims=True)
        ids = jnp.where(rank == _iota((16, nbp), 0), _iota((16, nbp), 1), 0)
        sel_ref[...] = jnp.broadcast_to(jnp.sum(ids, axis=1, keepdims=True), (16, LANES))


def _nsa_dec_cmp(pt_flat, n_pages, page0, q_new, cmp_cache):
    b = q_new.shape[0]
    nbp = n_pages * (PAGE_SIZE // NSA_BLOCK)
    n_sel = NSA_TOPK - 1
    assert nbp >= n_sel and NSA_TOPK <= 16 and n_pages % CMP_PAGES == 0
    page = lambda r: pl.BlockSpec((None, 2 * PAGE_SIZE, NSA_HD),
                                  lambda bi, j, pt: (page0 + pt[bi * n_pages + j * CMP_PAGES + r], 0, 0))
    mean_spec = pl.BlockSpec((None, nbp, NSA_HD), lambda bi, j, pt: (bi, 0, 0))
    return pl.pallas_call(
        functools.partial(_nsa_dec_cmp_kernel, nbp=nbp),
        grid_spec=pltpu.PrefetchScalarGridSpec(
            num_scalar_prefetch=1,
            grid=(b, n_pages // CMP_PAGES),
            in_specs=[pl.BlockSpec((None, 1, MIX_W), lambda bi, j, pt: (bi, 0, 0))]
            + [page(r) for r in range(CMP_PAGES)],
            out_specs=[mean_spec, mean_spec,
                       pl.BlockSpec((None, 8, NSA_HD), lambda bi, j, pt: (bi, 0, 0)),
                       pl.BlockSpec((None, 16, LANES), lambda bi, j, pt: (bi, 0, 0))],
        ),
        out_shape=[jax.ShapeDtypeStruct((b, nbp, NSA_HD), F32)] * 2
        + [jax.ShapeDtypeStruct((b, 8, NSA_HD), F32), jax.ShapeDtypeStruct((b, 16, LANES), jnp.int32)],
        compiler_params=_cp("parallel", "arbitrary"),
        name="nsa_dec_cmp",
    )(pt_flat, q_new, *([cmp_cache] * CMP_PAGES))


def _nsa_dec_attn_kernel(pt_ref, sel_ref, q_ref, blk_ref, snew_ref, win_ref, wnew_ref, oc_ref, sm_ref, o_ref,
                         m_sc, l_sc, acc_sc, *, sw):
    s_i = pl.program_id(1)
    q8 = _heads_to_rows(q_ref[...])

    @pl.when(s_i == 0)
    def _():
        m_sc[...] = jnp.sum(q8 * snew_ref[:, :NSA_HD], axis=-1, keepdims=True)
        l_sc[...] = jnp.ones_like(l_sc)
        acc_sc[...] = jnp.broadcast_to(snew_ref[:, NSA_HD:], acc_sc.shape)

    k, v = blk_ref[pl.ds(0, NSA_BLOCK, stride=2), :], blk_ref[pl.ds(1, NSA_BLOCK, stride=2), :]
    s = lax.dot_general(q8, k, NT, precision=HI, preferred_element_type=F32)
    m_prev = m_sc[...]
    m_new = jnp.maximum(m_prev, jnp.max(s, axis=-1, keepdims=True))
    alpha = jnp.exp(m_prev - m_new)
    p = jnp.exp(s - m_new)
    l_sc[...] = alpha * l_sc[...] + jnp.sum(p, axis=-1, keepdims=True)
    acc_sc[...] = alpha * acc_sc[...] + jnp.dot(p, v, precision=HI, preferred_element_type=F32)
    m_sc[...] = m_new

    @pl.when(s_i == pl.num_programs(1) - 1)
    def _():
        o_s = acc_sc[...] / l_sc[...]
        kw, vw = win_ref[pl.ds(0, sw, stride=2), :], win_ref[pl.ds(1, sw, stride=2), :]
        sw_ = lax.dot_general(q8, kw, NT, precision=HI, preferred_element_type=F32)
        wmask = (sw - _iota((8, sw), 1)) < NSA_WINDOW
        s_new = jnp.sum(q8 * wnew_ref[:, :NSA_HD], axis=-1, keepdims=True)
        m = jnp.maximum(jnp.max(jnp.where(wmask, sw_, -jnp.inf), axis=-1, keepdims=True), s_new)
        pw = jnp.where(wmask, jnp.exp(sw_ - m), 0.0)
        pn = jnp.exp(s_new - m)
        o_w = (jnp.dot(pw, vw, precision=HI, preferred_element_type=F32) + pn * wnew_ref[:, NSA_HD:]) / (
            jnp.sum(pw, axis=-1, keepdims=True) + pn)
        gate = jax.nn.sigmoid(sm_ref[...])
        o_c = oc_ref[...]
        for h in range(NSA_H):
            g0 = SM_NG + 3 * h
            o_ref[:, h * NSA_HD:(h + 1) * NSA_HD] = (
                gate[:, g0:g0 + 1] * o_c[h:h + 1] + gate[:, g0 + 1:g0 + 2] * o_s[h:h + 1]
                + gate[:, g0 + 2:g0 + 3] * o_w[h:h + 1])


def _nsa_dec_attn(pt_flat, sel_flat, n_pages, n_sel, page0, row0, q_new, slc_cache, slc_new, win_cache, win_new,
                  o_c, small_new):
    b = q_new.shape[0]
    sw = win_cache.shape[1] // 2
    per = PAGE_SIZE // NSA_BLOCK
    half = slc_cache.reshape(slc_cache.shape[0] * per, 2 * NSA_BLOCK, NSA_HD)

    def blk_map(bi, s, pt, sel):
        n = sel[bi * n_sel + s]
        return ((page0 + pt[bi * n_pages + n // per]) * per + n % per, 0, 0)

    per_b = lambda r, n: pl.BlockSpec((None, r, n), lambda bi, s, pt, sel: (bi, 0, 0))
    return pl.pallas_call(
        functools.partial(_nsa_dec_attn_kernel, sw=sw),
        grid_spec=pltpu.PrefetchScalarGridSpec(
            num_scalar_prefetch=2,
            grid=(b, n_sel),
            in_specs=[per_b(1, MIX_W), pl.BlockSpec((None, 2 * NSA_BLOCK, NSA_HD), blk_map), per_b(1, 2 * NSA_HD),
                      pl.BlockSpec((None, 2 * sw, NSA_HD), lambda bi, s, pt, sel: (row0 + bi, 0, 0)),
                      per_b(1, 2 * NSA_HD), per_b(8, NSA_HD), per_b(1, LANES)],
            out_specs=per_b(1, MIX_W),
            scratch_shapes=[pltpu.VMEM((8, 1), F32), pltpu.VMEM((8, 1), F32), pltpu.VMEM((8, NSA_HD), F32)],
        ),
        out_shape=jax.ShapeDtypeStruct((b, 1, MIX_W), F32),
        compiler_params=_cp("parallel", "arbitrary"),
        name="nsa_dec_attn",
    )(pt_flat, sel_flat, q_new, half, slc_new, win_cache, win_new, o_c, small_new)


def _lane_row(vals, at):
    return jnp.zeros((1, LANES), F32).at[0, at:at + vals.shape[0]].set(vals.astype(F32))


def _stacked_weights(w_in, w_branch, w_out, w_up, w_down):
    offs = [0]
    for s in SPLIT_SIZES:
        offs.append(offs[-1] + s)
    seg = lambda i: w_in[:, :, offs[i]:offs[i + 1]]
    (fq, fk, fv, ff, gq, gk, gv, ga, gr, nq, nkc, nks, nkw, ng, dqkv, da, dbeta, dz, mg) = [seg(i) for i in range(19)]
    cat = lambda parts: jnp.concatenate(parts, axis=2).astype(BF16)
    small = [ff, ga, ng, da, dbeta]
    pad = jnp.zeros(w_in.shape[:2] + (LANES - sum(p.shape[2] for p in small),), F32)
    return dict(fox=cat([fq, fk, fv]), gla=cat([gq, gk, gv, gr]), nsa=cat([nq, nkc, nks, nkw]), dn=cat([dqkv, dz]),
                small=cat(small + [pad]), mg=mg.astype(BF16), wb=w_branch.astype(BF16), w_out=w_out.astype(BF16),
                w_up=w_up.astype(BF16), w_down=w_down.astype(BF16))


def _layer_rows(fox_b_f, gla_w_a2, gla_b_a, gla_norm, dn_conv_w, dn_a_log, dn_dt_bias, dn_norm):
    return dict(
        bf_row=_lane_row(fox_b_f, SM_FF),
        w2p=jnp.zeros((LANES, GLA_H * GLA_DK), F32).at[SM_GA:SM_GA + GLA_RANK].set(gla_w_a2),
        ba_row=gla_b_a.reshape(1, -1), gn_row=gla_norm.reshape(1, -1), conv_w=dn_conv_w,
        alog_row=_lane_row(dn_a_log, SM_DA), dtb_row=_lane_row(dn_dt_bias, SM_DA), dnn_row=dn_norm.reshape(1, -1))


def _rope_tables(pos):
    half = ROPE_DIM // 2
    inv = ROPE_THETA ** (-jnp.arange(half, dtype=F32) / half)
    ang = pos.astype(F32)[:, None] * inv[None, :]
    cos, sin = jnp.cos(ang), jnp.sin(ang)
    t = pos.shape[0]
    z = lambda n: jnp.zeros((t, n), F32)
    return (jnp.concatenate([cos, cos, jnp.ones((t, LANES - ROPE_DIM), F32)], axis=1),
            jnp.concatenate([-sin, z(LANES - half)], axis=1),
            jnp.concatenate([z(half), sin, z(LANES - ROPE_DIM)], axis=1))


def _project(h, w, l):
    return {name: _mm(h, w[name], l) for name in ("fox", "gla", "nsa", "dn", "small")}


def _mixer_prompt(h, b, t, w, l):
    p = _project(h, w, l)
    fox, gla, nsa, dn, small = (p[n].reshape(b, t, -1) for n in ("fox", "gla", "nsa", "dn", "small"))
    lf, c, ct, fox_kv = _fox_prep(small, w["bf_row"], fox)
    o_fox = _fox_attn(fox, c, ct)
    o_gla, s_gla = _gla(gla, small, w["w2p"], w["ba_row"], w["gn_row"], jnp.zeros((b, GLA_H, GLA_DK, GLA_DV), F32))
    qr, _, slc_kv, win_kv, cmp_c, slc_c, win_c, cmean = _nsa_prep(nsa, *_rope_tables(jnp.arange(t)), True)
    o_nsa = _nsa_attn(qr, small, cmean, slc_kv, win_kv)
    qd, kd, vd, gsm, btsm = _dn_prep(dn, jnp.zeros((b, 8, DN_QKV), F32), w["conv_w"], small, w["alog_row"],
                                     w["dtb_row"])
    o_dn, s_dn = _gdn(qd, kd, vd, gsm, btsm, dn, w["dnn_row"], jnp.zeros((b, DN_H, DN_DK, DN_DV), F32))
    mix = _branch_gate(h, [o.reshape(b * t, MIX_W) for o in (o_fox, o_gla, o_nsa, o_dn)], w["wb"], w["mg"], l)
    kv5 = lambda a: a.reshape(b, t, 2, 1, NSA_HD)
    wl = min(NSA_WINDOW, t)
    state = (fox_kv.reshape(b, t, 2, FOX_H, FOX_HD), lf[:, :, :FOX_H], kv5(cmp_c), kv5(slc_c),
             kv5(win_c)[:, t - wl:], s_gla, s_dn, dn[:, t - (DN_CONV - 1):, :DN_QKV])
    return mix, state


def _mixer_sample(h, caches, l, pt_flat, n_pages, w):
    fox_kv_c, fox_lf_c, cmp_c, slc_c, win_c, s_gla_c, s_dn_c, conv_c = caches
    b = h.shape[0]
    past_len = n_pages * PAGE_SIZE
    p = _project(h, w, l)
    fox, gla, dn, small = (p[n].reshape(b, 1, -1) for n in ("fox", "gla", "dn", "small"))
    depth, n_pool = fox_kv_c.shape[:2]
    page0, row0 = l * n_pool, l * b
    o_fox, lf_new = _fox_decode(pt_flat, n_pages, page0, fox, small, w["bf_row"],
                                fox_kv_c.reshape(depth * n_pool, PAGE_SIZE * 2 * FOX_H, FOX_HD),
                                jnp.swapaxes(fox_lf_c, 2, 3).reshape(depth * n_pool, FOX_H, PAGE_SIZE))
    o_gla, o_dn, s_gla, s_dn = _rec_decode(
        gla, dn, small, w["w2p"], w["ba_row"], w["gn_row"], conv_c.reshape((depth * b,) + conv_c.shape[2:]),
        w["conv_w"], w["alog_row"], w["dtb_row"], w["dnn_row"], s_gla_c.reshape((depth * b,) + s_gla_c.shape[2:]),
        s_dn_c.reshape((depth * b,) + s_dn_c.shape[2:]), row0)
    tabs = _rope_tables(jnp.full((b,), past_len, jnp.int32))
    qr, cmp_new, slc_new, win_new = _nsa_prep(p["nsa"].reshape(1, b, -1), *tabs, False)[:4]
    qr, cmp_new, slc_new, win_new = (a.reshape(b, 1, -1) for a in (qr, cmp_new, slc_new, win_new))
    paged = lambda c: c.reshape(depth * n_pool, 2 * PAGE_SIZE, NSA_HD)
    _, _, o_c, sel = _nsa_dec_cmp(pt_flat, n_pages, page0, qr, paged(cmp_c))
    n_sel = NSA_TOPK - 1
    sel_flat = sel[:, :n_sel, 0].reshape(-1)
    sw = win_c.shape[2]
    o_nsa = _nsa_dec_attn(pt_flat, sel_flat, n_pages, n_sel, page0, row0, qr, paged(slc_c), slc_new,
                          win_c.reshape(depth * b, 2 * sw, NSA_HD), win_new, o_c, small)
    mix = _branch_gate(h, [o.reshape(b, MIX_W).astype(BF16) for o in (o_fox, o_gla, o_nsa, o_dn)], w["wb"], w["mg"], l)
    kv5 = lambda a: a.reshape(b, 1, 2, 1, NSA_HD)
    wl = min(NSA_WINDOW, sw + 1)
    new_win = jnp.concatenate([win_c[l][:, sw + 1 - wl:], kv5(win_new)], axis=1)
    new_conv = jnp.concatenate([conv_c[l][:, 1:], dn[:, :, :DN_QKV]], axis=1)
    state = (fox[:, :, MIX_W:].reshape(b, 1, 2, FOX_H, FOX_HD), lf_new[:, :, :FOX_H], kv5(cmp_new), kv5(slc_new),
             new_win, s_gla, s_dn, new_conv)
    return mix, state


def _trunk_layer(x, h, mixer, norms, g_next, w, l):
    _, g_post_mix, g_pre_mlp, g_post_mlp = norms
    mix, state = mixer(h)
    x, h_mlp = _mm_norm_res(mix, w["w_out"], l, g_post_mix, x, g_pre_mlp)
    hid = _mm(h_mlp, w["w_up"], l, out_dtype=BF16, act="relu2")
    if g_next is None:
        return _mm_norm_res(hid, w["w_down"], l, g_post_mlp, x), None, state
    x, h_next = _mm_norm_res(hid, w["w_down"], l, g_post_mlp, x, g_next)
    return x, h_next, state


def kernel(x_prompt, x_sample, cache_fox_kv, cache_fox_logf, cache_nsa_cmp_kv, cache_nsa_slc_kv, cache_nsa_win_kv,
           state_gla, state_dn, state_dn_conv, page_table, norm_pre_mix, norm_post_mix, norm_pre_mlp, norm_post_mlp,
           w_in, fox_b_f, gla_w_a2, gla_b_a, gla_norm, dn_conv_w, dn_a_log, dn_dt_bias, dn_norm, w_branch, w_out,
           w_up, w_down):
    bp, tp, _ = x_prompt.shape
    bs, ts, _ = x_sample.shape
    assert ts == 1
    n_pages = page_table.shape[1]
    pt_flat = page_table.reshape(-1).astype(jnp.int32)
    y_p = x_prompt.reshape(bp * tp, D_MODEL)
    y_s = x_sample.reshape(bs, D_MODEL)
    new_p = [[] for _ in range(8)]
    new_s = [[] for _ in range(8)]
    caches = (cache_fox_kv, cache_fox_logf, cache_nsa_cmp_kv, cache_nsa_slc_kv, cache_nsa_win_kv, state_gla, state_dn,
              state_dn_conv)
    big = _stacked_weights(w_in, w_branch, w_out, w_up, w_down)
    h_p = _rmsnorm_cast(y_p, norm_pre_mix[0])
    h_s = _rmsnorm_cast(y_s, norm_pre_mix[0])
    for l in range(DEPTH):
        w = dict(big, **_layer_rows(fox_b_f[l], gla_w_a2[l], gla_b_a[l], gla_norm[l], dn_conv_w[l], dn_a_log[l],
                                    dn_dt_bias[l], dn_norm[l]))
        norms = (norm_pre_mix[l], norm_post_mix[l], norm_pre_mlp[l], norm_post_mlp[l])
        g_next = norm_pre_mix[l + 1] if l + 1 < DEPTH else None
        y_p, h_p, st_p = _trunk_layer(y_p, h_p, lambda h: _mixer_prompt(h, bp, tp, w, l), norms, g_next, w, l)
        y_s, h_s, st_s = _trunk_layer(y_s, h_s, lambda h: _mixer_sample(h, caches, l, pt_flat, n_pages, w), norms,
                                      g_next, w, l)
        for i in range(8):
            new_p[i].append(st_p[i])
            new_s[i].append(st_s[i])
    fox_kv_p, fox_logf_p, cmp_kv_p, slc_kv_p, win_kv_p, gla_p, dn_p, conv_p = [jnp.stack(a) for a in new_p]
    fox_kv_s, fox_logf_s, cmp_kv_s, slc_kv_s, win_kv_s, gla_s, dn_s, conv_s = [jnp.stack(a) for a in new_s]
    return (y_p.reshape(bp, tp, D_MODEL), y_s.reshape(bs, ts, D_MODEL), fox_kv_p, fox_kv_s, fox_logf_p, fox_logf_s,
            cmp_kv_p, cmp_kv_s, slc_kv_p, slc_kv_s, win_kv_p, win_kv_s, gla_p, gla_s, dn_p, dn_s, conv_p, conv_s)
```
